```python
import math
import jax, jax.numpy as jnp
from jax import lax
import numpy as np

D_MODEL = 1024
BATCH = 2
SEQ = 8192
DEPTH = 1

CONF_DIM = D_MODEL
CONF_KERNEL = 31
SSM_EXPAND = 2
SSM_INNER = SSM_EXPAND * D_MODEL
SSM_HEAD_DIM = 64
SSM_HEADS = SSM_INNER // SSM_HEAD_DIM
SSM_GROUPS = 4
SSM_STATE = 128
SSM_CONV = 5
SSM_CHUNK = 128
SSM_CONV_CH = SSM_INNER + 2 * SSM_GROUPS * SSM_STATE
IN_SPLITS = (2 * CONF_DIM, SSM_INNER, SSM_INNER, SSM_GROUPS * SSM_STATE,
             SSM_GROUPS * SSM_STATE, SSM_HEADS, SSM_HEADS, D_MODEL, D_MODEL)
IN_COLS = sum(IN_SPLITS)
N_EXPERTS = 32
TOP_K = 4
D_FF = D_MODEL
SWIGLU_ALPHA = 1.702
SWIGLU_LIMIT = 7.0
MOE_BLOCK = 128
EPS = 1e-6

kernel_name = "hybrid_conformer_ssd_moe_encoder"


def rms_norm(x, g):
    xf = x.astype(jnp.float32)
    y = xf * lax.rsqrt(jnp.mean(xf * xf, axis=-1, keepdims=True) + EPS)
    return (y * g).astype(x.dtype)


def layer_norm(x, g, b):
    xf = x.astype(jnp.float32)
    mu = jnp.mean(xf, axis=-1, keepdims=True)
    var = jnp.mean(jnp.square(xf - mu), axis=-1, keepdims=True)
    return ((xf - mu) * lax.rsqrt(var + EPS) * g + b).astype(x.dtype)


def modulate(x, g, shift, scale):
    return rms_norm(x, g) * (1.0 + scale[:, None, :]) + shift[:, None, :]


def depthwise_conv(x, w, b):
    k, ch = w.shape
    y = lax.conv_general_dilated(x, w[:, None, :], window_strides=(1,),
                                 padding=[(k // 2, k // 2)],
                                 dimension_numbers=("NWC", "WIO", "NWC"),
                                 feature_group_count=ch)
    return y + b


def ssd_scan(xh, dt, a_coef, bm, cm):
    bsz, seqlen, nh, hp = xh.shape
    ng, ns = bm.shape[2], bm.shape[3]
    r = nh // ng
    nc = seqlen // SSM_CHUNK
    q = SSM_CHUNK
    xdt = (xh * dt[..., None]).reshape(bsz, nc, q, ng, r, hp)
    a = (dt * a_coef).reshape(bsz, nc, q, ng, r)
    a_cs = jnp.cumsum(a, axis=2)
    bc = bm.reshape(bsz, nc, q, ng, ns)
    cc = cm.reshape(bsz, nc, q, ng, ns)
    acs_t = jnp.moveaxis(a_cs, 2, -1)
    seg = acs_t[..., :, None] - acs_t[..., None, :]
    lower = jnp.tril(jnp.ones((q, q), dtype=bool))
    lmat = jnp.exp(jnp.where(lower, seg, -jnp.inf))
    cb = jnp.einsum("bcign,bcjgn->bcgij", cc, bc)
    y_diag = jnp.einsum("bcgij,bcgrij,bcjgrp->bcigrp", cb, lmat, xdt)
    decay_states = jnp.exp(a_cs[:, :, -1:] - a_cs)
    states = jnp.einsum("bcjgn,bcjgr,bcjgrp->bcgrpn", bc, decay_states, xdt)
    chunk_decay = jnp.exp(a_cs[:, :, -1])

    def step(h, inp):
        s, d = inp
        return h * d[..., None, None] + s, h

    h0 = jnp.zeros_like(states[:, 0])
    _, h_enter = lax.scan(step, h0, (jnp.moveaxis(states, 1, 0),
                                     jnp.moveaxis(chunk_decay, 1, 0)))
    h_enter = jnp.moveaxis(h_enter, 0, 1)
    y_off = jnp.einsum("bcign,bcgrpn,bcigr->bcigrp", cc, h_enter, jnp.exp(a_cs))
    return (y_diag + y_off).reshape(bsz, seqlen, nh, hp).astype(xh.dtype)


def moe_ffn(v, router_w, router_b, w_gu, b_gu, w_down, b_down):
    bsz, seqlen, d = v.shape
    t = bsz * seqlen
    vf = v.reshape(t, d)
    logits = (vf @ router_w + router_b).astype(jnp.float32)
    top_v, top_i = lax.top_k(logits, TOP_K)
    top_w = jax.nn.softmax(top_v, axis=-1).astype(v.dtype)
    n_assign = t * TOP_K
    n_blocks = -(-n_assign // MOE_BLOCK) + N_EXPERTS
    flat_e = top_i.reshape(-1)
    flat_tok = jnp.repeat(jnp.arange(t, dtype=jnp.int32), TOP_K)
    flat_w = top_w.reshape(-1)
    order = jnp.argsort(flat_e)
    sorted_e = flat_e[order]
    counts = jnp.bincount(flat_e, length=N_EXPERTS)
    padded = ((counts + MOE_BLOCK - 1) // MOE_BLOCK) * MOE_BLOCK
    pad_end = jnp.cumsum(padded)
    pad_start = pad_end - padded
    grp_start = jnp.cumsum(counts) - counts
    rank = jnp.arange(n_assign, dtype=jnp.int32) - grp_start[sorted_e]
    dest = pad_start[sorted_e] + rank
    buf_tok = jnp.full((n_blocks * MOE_BLOCK,), t, jnp.int32).at[dest].set(flat_tok[order])
    buf_w = jnp.zeros((n_blocks * MOE_BLOCK,), v.dtype).at[dest].set(flat_w[order])
    block_start = jnp.arange(n_blocks, dtype=pad_end.dtype) * MOE_BLOCK
    block_expert = jnp.minimum(jnp.searchsorted(pad_end, block_start, side="right"),
                               N_EXPERTS - 1).astype(jnp.int32)
    vpad = jnp.concatenate([vf, jnp.zeros((1, d), vf.dtype)], axis=0)
    xs = vpad[buf_tok].reshape(n_blocks, MOE_BLOCK, d)

    def expert_block(args):
        xb, e = args
        gu = xb @ w_gu[e] + b_gu[e]
        gate, up = jnp.split(gu, 2, axis=-1)
        gate = jnp.minimum(gate, SWIGLU_LIMIT)
        up = jnp.clip(up, -SWIGLU_LIMIT, SWIGLU_LIMIT)
        glu = gate * jax.nn.sigmoid(SWIGLU_ALPHA * gate)
        return ((up + 1.0) * glu) @ w_down[e] + b_down[e]

    ys = lax.map(expert_block, (xs, block_expert)).reshape(n_blocks * MOE_BLOCK, d)
    out = jax.ops.segment_sum(ys * buf_w[:, None], buf_tok, num_segments=t + 1)[:t]
    return out.reshape(bsz, seqlen, d)


def setup_inputs(seed: int = 0) -> dict:
    key = jax.random.key(seed)
    ks = iter(jax.random.split(key, 40))

    def nrm(shape, scale):
        return jax.random.normal(next(ks), shape, jnp.float32) * scale

    def gain(shape):
        return 1.0 + nrm(shape, 0.02)

    L, D = DEPTH, D_MODEL
    dt0 = jnp.exp(jax.random.uniform(next(ks), (2, L, SSM_HEADS),
                                     minval=math.log(1e-3), maxval=math.log(1e-1)))
    dt_bias = dt0 + jnp.log(-jnp.expm1(-dt0))
    a_log = jnp.log(jax.random.uniform(next(ks), (2, L, SSM_HEADS), minval=1.0, maxval=16.0))
    return {
        "x": nrm((BATCH, SEQ, D), 1.0),
        "c": nrm((BATCH, D), 1.0),
        "ada_w": nrm((L, D, 6 * D), 0.5 * D ** -0.5),
        "ada_b": nrm((L, 6 * D), 0.02),
        "norm_mix_g": gain((L, D)),
        "w_in": nrm((L, D, IN_COLS), D ** -0.5),
        "conf_dw_w": nrm((L, CONF_KERNEL, CONF_DIM), CONF_KERNEL ** -0.5),
        "conf_dw_b": nrm((L, CONF_DIM), 0.02),
        "conf_ln_g": gain((L, CONF_DIM)),
        "conf_ln_b": nrm((L, CONF_DIM), 0.02),
        "conf_out_w": nrm((L, CONF_DIM, D), CONF_DIM ** -0.5),
        "conf_out_b": nrm((L, D), 0.02),
        "ssm_conv_w": nrm((L, SSM_CONV, SSM_CONV_CH), SSM_CONV ** -0.5),
        "ssm_conv_b": nrm((L, SSM_CONV_CH), 0.02),
        "dt_bias_f": dt_bias[0],
        "dt_bias_b": dt_bias[1],
        "a_log_f": a_log[0],
        "a_log_b": a_log[1],
        "ssm_d": gain((L, SSM_HEADS)),
        "ssm_norm_g": gain((L, SSM_INNER)),
        "ssm_out_w": nrm((L, SSM_INNER, D), SSM_INNER ** -0.5),
        "w_o": nrm((L, D, D), D ** -0.5),
        "norm_ffn_g": gain((L, D)),
        "router_w": nrm((L, D, N_EXPERTS), D ** -0.5),
        "router_b": nrm((L, N_EXPERTS), 0.01),
        "w_gu": nrm((L, N_EXPERTS, D, 2 * D_FF), D ** -0.5),
        "b_gu": nrm((L, N_EXPERTS, 2 * D_FF), 0.02),
        "w_down": nrm((L, N_EXPERTS, D_FF, D), D_FF ** -0.5),
        "b_down": nrm((L, N_EXPERTS, D), 0.02),
        "final_ada_w": nrm((D, 2 * D), 0.5 * D ** -0.5),
        "final_ada_b": nrm((2 * D,), 0.02),
        "final_norm_g": gain((D,)),
    }


def reference(x, c, ada_w, ada_b, norm_mix_g, w_in, conf_dw_w, conf_dw_b, conf_ln_g,
              conf_ln_b, conf_out_w, conf_out_b, ssm_conv_w, ssm_conv_b, dt_bias_f,
              dt_bias_b, a_log_f, a_log_b, ssm_d, ssm_norm_g, ssm_out_w, w_o,
              norm_ffn_g, router_w, router_b, w_gu, b_gu, w_down, b_down,
              final_ada_w, final_ada_b, final_norm_g):
    bsz, seqlen, d = x.shape
    c_act = jax.nn.silu(c)
    split_pts = list(np.cumsum(IN_SPLITS)[:-1])
    for l in range(DEPTH):
        ada = c_act @ ada_w[l] + ada_b[l]
        sh1, sc1, g1, sh2, sc2, g2 = jnp.split(ada, 6, axis=-1)
        u = modulate(x, norm_mix_g[l], sh1, sc1)
        proj = u @ w_in[l]
        conf_in, z, xs, bm, cm, dtf, dtb, gate_conf, gate_ssm = jnp.split(proj, split_pts, axis=-1)
        a_half, g_half = jnp.split(conf_in, 2, axis=-1)
        hc = a_half * jax.nn.sigmoid(g_half)
        hc = depthwise_conv(hc, conf_dw_w[l], conf_dw_b[l])
        hc = jax.nn.silu(layer_norm(hc, conf_ln_g[l], conf_ln_b[l]))
        y_conf = hc @ conf_out_w[l] + conf_out_b[l]
        xbc = jnp.concatenate([xs, bm, cm], axis=-1)
        xbc = jax.nn.silu(depthwise_conv(xbc, ssm_conv_w[l], ssm_conv_b[l]))
        xs_c, bm_c, cm_c = jnp.split(xbc, [SSM_INNER, SSM_INNER + SSM_GROUPS * SSM_STATE], axis=-1)
        xh = xs_c.reshape(bsz, seqlen, SSM_HEADS, SSM_HEAD_DIM)
        bg = bm_c.reshape(bsz, seqlen, SSM_GROUPS, SSM_STATE)
        cg = cm_c.reshape(bsz, seqlen, SSM_GROUPS, SSM_STATE)
        dt_f = jax.nn.softplus((dtf + dt_bias_f[l]).astype(jnp.float32))
        dt_b = jax.nn.softplus((dtb + dt_bias_b[l]).astype(jnp.float32))
        a_f = -jnp.exp(a_log_f[l].astype(jnp.float32))
        a_b = -jnp.exp(a_log_b[l].astype(jnp.float32))
        y_fwd = ssd_scan(xh, dt_f, a_f, bg, cg)
        y_bwd = jnp.flip(ssd_scan(jnp.flip(xh, 1), jnp.flip(dt_b, 1), a_b,
                                  jnp.flip(bg, 1), jnp.flip(cg, 1)), 1)
        ys = y_fwd + y_bwd + xh * ssm_d[l][:, None]
        ys = ys.reshape(bsz, seqlen, SSM_INNER) * jax.nn.silu(z)
        ys = rms_norm(ys.reshape(bsz, seqlen, SSM_GROUPS, SSM_INNER // SSM_GROUPS), 1.0)
        ys = ys.reshape(bsz, seqlen, SSM_INNER) * ssm_norm_g[l]
        y_ssm = ys @ ssm_out_w[l]
        merged = jax.nn.sigmoid(gate_conf) * y_conf + jax.nn.sigmoid(gate_ssm) * y_ssm
        x = x + g1[:, None, :] * (merged @ w_o[l])
        v = modulate(x, norm_ffn_g[l], sh2, sc2)
        f = moe_ffn(v, router_w[l], router_b[l], w_gu[l], b_gu[l], w_down[l], b_down[l])
        x = x + g2[:, None, :] * f
    fin = c_act @ final_ada_w + final_ada_b
    sh_f, sc_f = jnp.split(fin, 2, axis=-1)
    return modulate(x, final_norm_g, sh_f, sc_f)
```

```python
import functools

import jax
import jax.numpy as jnp
from jax import lax
from jax.experimental import pallas as pl
from jax.experimental.pallas import tpu as pltpu

F32 = jnp.float32
BF16 = jnp.bfloat16
HIGHEST = lax.Precision.HIGHEST

EPS = 1e-6
CONF_KERNEL = 31
SSM_CONV = 5
SSM_HEAD_DIM = 64
SSM_GROUPS = 4
SSM_STATE = 128
SSM_CHUNK = 128
TOP_K = 4
SWIGLU_ALPHA = 1.702
SWIGLU_LIMIT = 7.0

LANES = 128
SUBLANES = 8
VMEM_LIMIT = 56 * 1024 * 1024

MOE_BLOCK = 256


def _cparams(sem):
    return pltpu.CompilerParams(dimension_semantics=sem, vmem_limit_bytes=VMEM_LIMIT)


def _sigmoid(x):
    return 1.0 / (1.0 + jnp.exp(-x))


def _silu(x):
    return x * _sigmoid(x)


def _ada_kernel(c_ref, w_ref, b_ref, o_ref):
    c = c_ref[...]
    o_ref[...] = jnp.dot(_silu(c), w_ref[...], precision=HIGHEST,
                         preferred_element_type=F32) + b_ref[...]


def _ada(c_pad, w, b):
    d, n = w.shape
    tn = 1024
    return pl.pallas_call(
        _ada_kernel,
        grid=(n // tn,),
        in_specs=[pl.BlockSpec((SUBLANES, d), lambda j: (0, 0)),
                  pl.BlockSpec((d, tn), lambda j: (0, j)),
                  pl.BlockSpec((1, tn), lambda j: (0, j))],
        out_specs=pl.BlockSpec((SUBLANES, tn), lambda j: (0, j)),
        out_shape=jax.ShapeDtypeStruct((SUBLANES, n), F32),
        compiler_params=_cparams(("parallel",)),
        name="ada",
    )(c_pad, w, b.reshape(1, n))


def _modulated_norm(x, scale, shift):
    ms = jnp.mean(x * x, axis=-1, keepdims=True)
    return (x * lax.rsqrt(ms + EPS)) * scale + shift


def _inproj_kernel(x_ref, scale_ref, shift_ref, w_ref, o_ref, *, col_chunk):
    u = _modulated_norm(x_ref[...], scale_ref[0], shift_ref[0]).astype(BF16)
    ncols = o_ref.shape[1]
    for c0 in range(0, ncols, col_chunk):
        c1 = min(c0 + col_chunk, ncols)
        o_ref[:, c0:c1] = jnp.dot(u, w_ref[:, c0:c1], preferred_element_type=F32)


def _inproj(x, scale, shift, w_bf, seq, tm):
    t, d = x.shape
    ncols = w_bf.shape[1]
    tiles_per_seq = seq // tm
    return pl.pallas_call(
        functools.partial(_inproj_kernel, col_chunk=1024),
        grid=(t // tm,),
        in_specs=[pl.BlockSpec((tm, d), lambda i: (i, 0)),
                  pl.BlockSpec((1, 1, d), lambda i: (i // tiles_per_seq, 0, 0)),
                  pl.BlockSpec((1, 1, d), lambda i: (i // tiles_per_seq, 0, 0)),
                  pl.BlockSpec(memory_space=pltpu.VMEM)],
        out_specs=pl.BlockSpec((tm, ncols), lambda i: (i, 0)),
        out_shape=jax.ShapeDtypeStruct((t, ncols), F32),
        compiler_params=_cparams(("parallel",)),
        name="inproj",
    )(x, scale, shift, w_bf)


def _fill_shifted(hbuf_ref, sh_ref, shifts, rows):
    for n, r in enumerate(shifts):
        sh_ref[n] = hbuf_ref[r:r + rows, :]


def _tap_plan(ktaps, halo):
    offs = [k + halo - ktaps // 2 for k in range(ktaps)]
    shifts = sorted({o % SUBLANES for o in offs})
    plan = [(shifts.index(o % SUBLANES), (o // SUBLANES) * SUBLANES) for o in offs]
    return shifts, plan, max(o // SUBLANES for o in offs) * SUBLANES


def _conv_rows(sh_ref, w_ref, plan, r0, rc):
    acc = None
    for k, (n, q8) in enumerate(plan):
        term = sh_ref[n, pl.ds(pl.multiple_of(r0 + q8, SUBLANES), rc), :] * w_ref[k:k + 1, :]
        acc = term if acc is None else acc + term
    return acc


def _cconv_kernel(a_ref, g_ref, ap_ref, gp_ref, an_ref, gn_ref, w_ref, b_ref, lg_ref, lb_ref,
                  o_ref, hbuf_ref, sh_ref, *, tiles_per_seq, halo, shifts, plan, rc):
    i = pl.program_id(0)
    tm = a_ref.shape[0]
    first = (i % tiles_per_seq) == 0
    last = (i % tiles_per_seq) == tiles_per_seq - 1
    glu_p = ap_ref[...] * _sigmoid(gp_ref[...])
    glu_n = an_ref[...] * _sigmoid(gn_ref[...])
    hbuf_ref[0:halo, :] = jnp.where(first, 0.0, glu_p)
    hbuf_ref[halo:halo + tm, :] = a_ref[...] * _sigmoid(g_ref[...])
    hbuf_ref[halo + tm:2 * halo + tm, :] = jnp.where(last, 0.0, glu_n)
    _fill_shifted(hbuf_ref, sh_ref, shifts, sh_ref.shape[1])

    def body(j, carry):
        r0 = j * rc
        h = _conv_rows(sh_ref, w_ref, plan, r0, rc) + b_ref[...]
        mu = jnp.mean(h, axis=-1, keepdims=True)
        hc = h - mu
        var = jnp.mean(hc * hc, axis=-1, keepdims=True)
        y = hc * lax.rsqrt(var + EPS) * lg_ref[...] + lb_ref[...]
        o_ref[pl.ds(pl.multiple_of(r0, rc), rc), :] = _silu(y).astype(o_ref.dtype)
        return carry

    lax.fori_loop(0, tm // rc, body, 0)


def _cconv(proj, w, b, ln_g, ln_b, seq, tm):
    t = proj.shape[0]
    ch = w.shape[1]
    halo = 16
    rc = 16
    shifts, plan, qmax = _tap_plan(CONF_KERNEL, halo)
    tiles_per_seq = seq // tm
    hb = tm // halo
    nhb = t // halo
    cb = ch // ch

    def prev_map(col):
        return lambda i: (jnp.maximum(i * hb - 1, 0), col)

    def next_map(col):
        return lambda i: (jnp.minimum((i + 1) * hb, nhb - 1), col)

    kern = functools.partial(_cconv_kernel, tiles_per_seq=tiles_per_seq, halo=halo,
                             shifts=shifts, plan=plan, rc=rc)
    return pl.pallas_call(
        kern,
        grid=(t // tm,),
        in_specs=[pl.BlockSpec((tm, ch), lambda i: (i, 0)),
                  pl.BlockSpec((tm, ch), lambda i: (i, cb)),
                  pl.BlockSpec((halo, ch), prev_map(0)),
                  pl.BlockSpec((halo, ch), prev_map(cb)),
                  pl.BlockSpec((halo, ch), next_map(0)),
                  pl.BlockSpec((halo, ch), next_map(cb)),
                  pl.BlockSpec((CONF_KERNEL, ch), lambda i: (0, 0)),
                  pl.BlockSpec((1, ch), lambda i: (0, 0)),
                  pl.BlockSpec((1, ch), lambda i: (0, 0)),
                  pl.BlockSpec((1, ch), lambda i: (0, 0))],
        out_specs=pl.BlockSpec((tm, ch), lambda i: (i, 0)),
        out_shape=jax.ShapeDtypeStruct((t, ch), BF16),
        scratch_shapes=[pltpu.VMEM((tm + 2 * halo, ch), F32),
                        pltpu.VMEM((len(shifts), tm + qmax, ch), F32)],
        compiler_params=_cparams(("parallel",)),
        name="cconv",
    )(proj, proj, proj, proj, proj, proj, w, b.reshape(1, ch), ln_g.reshape(1, ch),
      ln_b.reshape(1, ch))


def _sconv_kernel(x_ref, xp_ref, xn_ref, w_ref, b_ref, o_ref, hbuf_ref, sh_ref, *,
                  tiles_per_seq, halo, shifts, plan, rc):
    i = pl.program_id(0)
    tm = x_ref.shape[0]
    first = (i % tiles_per_seq) == 0
    last = (i % tiles_per_seq) == tiles_per_seq - 1
    hbuf_ref[0:halo, :] = jnp.where(first, 0.0, xp_ref[...])
    hbuf_ref[halo:halo + tm, :] = x_ref[...]
    hbuf_ref[halo + tm:2 * halo + tm, :] = jnp.where(last, 0.0, xn_ref[...])
    _fill_shifted(hbuf_ref, sh_ref, shifts, sh_ref.shape[1])

    def body(j, carry):
        r0 = j * rc
        h = _conv_rows(sh_ref, w_ref, plan, r0, rc) + b_ref[...]
        o_ref[pl.ds(pl.multiple_of(r0, rc), rc), :] = _silu(h)
        return carry

    lax.fori_loop(0, tm // rc, body, 0)


def _sconv(proj, col0, w, b, seq, tm):
    t = proj.shape[0]
    ch = w.shape[1]
    cw = 1024
    halo = SUBLANES
    rc = 16
    shifts, plan, qmax = _tap_plan(SSM_CONV, halo)
    tiles_per_seq = seq // tm
    hb = tm // halo
    nhb = t // halo
    c0 = col0 // cw
    kern = functools.partial(_sconv_kernel, tiles_per_seq=tiles_per_seq, halo=halo,
                             shifts=shifts, plan=plan, rc=rc)
    return pl.pallas_call(
        kern,
        grid=(t // tm, ch // cw),
        in_specs=[pl.BlockSpec((tm, cw), lambda i, j: (i, c0 + j)),
                  pl.BlockSpec((halo, cw), lambda i, j: (jnp.maximum(i * hb - 1, 0), c0 + j)),
                  pl.BlockSpec((halo, cw),
                               lambda i, j: (jnp.minimum((i + 1) * hb, nhb - 1), c0 + j)),
                  pl.BlockSpec((SSM_CONV, cw), lambda i, j: (0, j)),
                  pl.BlockSpec((1, cw), lambda i, j: (0, j))],
        out_specs=pl.BlockSpec((tm, cw), lambda i, j: (i, j)),
        out_shape=jax.ShapeDtypeStruct((t, ch), F32),
        scratch_shapes=[pltpu.VMEM((tm + 2 * halo, cw), F32),
                        pltpu.VMEM((len(shifts), tm + qmax, cw), F32)],
        compiler_params=_cparams(("parallel", "parallel")),
        name="sconv",
    )(proj, proj, proj, w, b.reshape(1, ch))


def _split3(f):
    hi = f.astype(BF16)
    r1 = f - hi.astype(F32)
    mid = r1.astype(BF16)
    lo = (r1 - mid.astype(F32)).astype(BF16)
    return hi, mid, lo


def _expand_heads(f, e):
    hi, mid, lo = _split3(f)
    return (jnp.dot(hi, e, preferred_element_type=F32)
            + jnp.dot(mid, e, preferred_element_type=F32)
            + jnp.dot(lo, e, preferred_element_type=F32))


def _ssd_kernel(x_ref, b_ref, c_ref, dt_ref, dtb_ref, alog_ref, e_ref, o_ref, state_ref, *,
                reverse, lane0):
    q = x_ref.shape[0]
    nheads = x_ref.shape[1] // SSM_HEAD_DIM
    hpg = nheads // SSM_GROUPS
    gw = hpg * SSM_HEAD_DIM

    @pl.when(pl.program_id(1) == 0)
    def _():
        state_ref[...] = jnp.zeros_like(state_ref)

    row = lax.broadcasted_iota(jnp.int32, (q, q), 0)
    col = lax.broadcasted_iota(jnp.int32, (q, q), 1)
    keep = (col >= row) if reverse else (col <= row)

    z = dt_ref[...] + dtb_ref[...]
    dt = jnp.maximum(z, 0.0) + jnp.log(1.0 + jnp.exp(-jnp.abs(z)))
    a = dt * (-jnp.exp(alog_ref[...]))
    acs = jnp.dot(keep.astype(F32), a, precision=HIGHEST, preferred_element_type=F32)
    tot = acs[0:1, :] if reverse else acs[q - 1:q, :]
    acs_t = acs.T
    dt_t = dt.T
    e = e_ref[...]
    x = x_ref[...]

    exp_acs = _expand_heads(jnp.exp(acs), e)
    wstate = _expand_heads(dt * jnp.exp(tot - acs), e)
    carry = _expand_heads(jnp.broadcast_to(jnp.exp(tot), (SUBLANES, LANES)), e)[0:1, :]
    wx = (wstate * x).astype(BF16)
    xb = x.astype(BF16)

    for g in range(SSM_GROUPS):
        bg = b_ref[:, g * SSM_STATE:(g + 1) * SSM_STATE].astype(BF16)
        cg = c_ref[:, g * SSM_STATE:(g + 1) * SSM_STATE].astype(BF16)
        cb = lax.dot_general(cg, bg, (((1,), (1,)), ((), ())), preferred_element_type=F32)
        gs = slice(g * gw, (g + 1) * gw)
        s_old = state_ref[:, gs]
        y_off = jnp.dot(cg, s_old.astype(BF16), preferred_element_type=F32) * exp_acs[:, gs]
        for r in range(hpg):
            h = g * hpg + r
            hl = lane0 + h
            seg = acs[:, hl:hl + 1] - acs_t[hl:hl + 1, :]
            lmat = jnp.exp(jnp.where(keep, seg, -jnp.inf))
            m = (cb * lmat * dt_t[hl:hl + 1, :]).astype(BF16)
            hs = slice(h * SSM_HEAD_DIM, (h + 1) * SSM_HEAD_DIM)
            y_h = jnp.dot(m, xb[:, hs], preferred_element_type=F32)
            o_ref[:, hs] = y_h + y_off[:, r * SSM_HEAD_DIM:(r + 1) * SSM_HEAD_DIM]
        upd = lax.dot_general(bg, wx[:, gs], (((0,), (0,)), ((), ())),
                              preferred_element_type=F32)
        state_ref[:, gs] = s_old * carry[:, gs] + upd


def _ssd(xbc, proj, dt_col_block, dt_bias, a_log, expand, bsz, seq, reverse, lane0):
    t = xbc.shape[0]
    q = SSM_CHUNK
    nc = seq // q
    inner = expand.shape[1]
    gn = SSM_GROUPS * SSM_STATE
    b_blk = inner // gn
    c_blk = b_blk + 1

    def tok(b, c):
        return b * nc + (nc - 1 - c if reverse else c)

    kern = functools.partial(_ssd_kernel, reverse=reverse, lane0=lane0)
    return pl.pallas_call(
        kern,
        grid=(bsz, nc),
        in_specs=[pl.BlockSpec((q, inner), lambda b, c: (tok(b, c), 0)),
                  pl.BlockSpec((q, gn), lambda b, c: (tok(b, c), b_blk)),
                  pl.BlockSpec((q, gn), lambda b, c: (tok(b, c), c_blk)),
                  pl.BlockSpec((q, LANES), lambda b, c: (tok(b, c), dt_col_block)),
                  pl.BlockSpec((1, LANES), lambda b, c: (0, 0)),
                  pl.BlockSpec((1, LANES), lambda b, c: (0, 0)),
                  pl.BlockSpec((LANES, inner), lambda b, c: (0, 0))],
        out_specs=pl.BlockSpec((q, inner), lambda b, c: (tok(b, c), 0)),
        out_shape=jax.ShapeDtypeStruct((t, inner), F32),
        scratch_shapes=[pltpu.VMEM((SSM_STATE, inner), F32)],
        compiler_params=_cparams(("arbitrary", "arbitrary")),
        name="ssd_bwd" if reverse else "ssd_fwd",
    )(xbc, xbc, xbc, proj, dt_bias, a_log, expand)


def _merge_kernel(yf_ref, yb_ref, xc_ref, z_ref, hc_ref, gc_ref, gs_ref, x_ref,
                  dexp_ref, ng_ref, wso_ref, wco_ref, bco_ref, wo_ref, g1_ref,
                  scale_ref, shift_ref, wr_ref, br_ref,
                  x1_ref, v_ref, lg_ref, *, ngroups):
    y = (yf_ref[...] + yb_ref[...] + xc_ref[...] * dexp_ref[...]) * _silu(z_ref[...])
    gw = y.shape[1] // ngroups
    parts = []
    for g in range(ngroups):
        yg = y[:, g * gw:(g + 1) * gw]
        ms = jnp.mean(yg * yg, axis=-1, keepdims=True)
        parts.append((yg * lax.rsqrt(ms + EPS)) * ng_ref[:, g * gw:(g + 1) * gw])
    ysn = jnp.concatenate(parts, axis=-1).astype(BF16)
    y_ssm = jnp.dot(ysn, wso_ref[...], preferred_element_type=F32)
    y_conf = jnp.dot(hc_ref[...], wco_ref[...], preferred_element_type=F32) + bco_ref[...]
    merged = _sigmoid(gc_ref[...]) * y_conf + _sigmoid(gs_ref[...]) * y_ssm
    o = jnp.dot(merged.astype(BF16), wo_ref[...], preferred_element_type=F32)
    x1 = x_ref[...] + g1_ref[0] * o
    x1_ref[...] = x1
    v = _modulated_norm(x1, scale_ref[0], shift_ref[0])
    v_ref[...] = v
    lg_ref[...] = jnp.dot(v, wr_ref[...], precision=HIGHEST,
                          preferred_element_type=F32) + br_ref[...]


def _merge(yf, yb, xbc, proj, hc, x, cols, dexp, norm_g, wso, wco, bco, wo, g1, scale, shift,
           wr, br, seq, tm):
    t, d = x.shape
    inner = yf.shape[1]
    tiles_per_seq = seq // tm
    z_blk = cols["z"] // inner
    gc_blk = cols["gate_conf"] // d
    gs_blk = cols["gate_ssm"] // d

    def const(shape):
        return pl.BlockSpec(shape, lambda i: tuple(0 for _ in shape))

    def per_seq():
        return pl.BlockSpec((1, 1, d), lambda i: (i // tiles_per_seq, 0, 0))

    kern = functools.partial(_merge_kernel, ngroups=SSM_GROUPS)
    return pl.pallas_call(
        kern,
        grid=(t // tm,),
        in_specs=[pl.BlockSpec((tm, inner), lambda i: (i, 0)),
                  pl.BlockSpec((tm, inner), lambda i: (i, 0)),
                  pl.BlockSpec((tm, inner), lambda i: (i, 0)),
                  pl.BlockSpec((tm, inner), lambda i: (i, z_blk)),
                  pl.BlockSpec((tm, d), lambda i: (i, 0)),
                  pl.BlockSpec((tm, d), lambda i: (i, gc_blk)),
                  pl.BlockSpec((tm, d), lambda i: (i, gs_blk)),
                  pl.BlockSpec((tm, d), lambda i: (i, 0)),
                  const((1, inner)), const((1, inner)),
                  const((inner, d)), const((d, d)), const((1, d)), const((d, d)),
                  per_seq(), per_seq(), per_seq(),
                  const((d, LANES)), const((1, LANES))],
        out_specs=[pl.BlockSpec((tm, d), lambda i: (i, 0)),
                   pl.BlockSpec((tm, d), lambda i: (i, 0)),
                   pl.BlockSpec((tm, LANES), lambda i: (i, 0))],
        out_shape=[jax.ShapeDtypeStruct((t, d), F32),
                   jax.ShapeDtypeStruct((t, d), F32),
                   jax.ShapeDtypeStruct((t, LANES), F32)],
        compiler_params=_cparams(("parallel",)),
        name="merge",
    )(yf, yb, xbc, proj, hc, proj, proj, x, dexp, norm_g, wso, wco, bco, wo, g1, scale, shift,
      wr, br)


def _route_kernel(lg_ref, idx_ref, w_ref, rank_ref, cnt_ref, run_ref, *, n_experts):
    @pl.when(pl.program_id(0) == 0)
    def _():
        run_ref[...] = jnp.zeros_like(run_ref)

    tm = lg_ref.shape[0]
    lane = lax.broadcasted_iota(jnp.int32, (tm, LANES), 1)
    lg = jnp.where(lane < n_experts, lg_ref[...], -jnp.inf)
    sel = jnp.zeros((tm, LANES), F32)
    vals, idxs = [], []
    for _ in range(TOP_K):
        m = jnp.max(lg, axis=-1, keepdims=True)
        ix = jnp.min(jnp.where(lg == m, lane, LANES), axis=-1, keepdims=True)
        hit = lane == ix
        sel = jnp.where(hit, 1.0, sel)
        lg = jnp.where(hit, -jnp.inf, lg)
        vals.append(m)
        idxs.append(ix)
    ex = [jnp.exp(v - vals[0]) for v in vals]
    den = ex[0] + ex[1] + ex[2] + ex[3]
    row = lax.broadcasted_iota(jnp.int32, (tm, tm), 0)
    col = lax.broadcasted_iota(jnp.int32, (tm, tm), 1)
    before = (col < row).astype(BF16)
    excl = jnp.dot(before, sel.astype(BF16), preferred_element_type=F32) + run_ref[...]
    idx_out = jnp.zeros((tm, LANES), jnp.int32)
    w_out = jnp.zeros((tm, LANES), F32)
    rank_out = jnp.zeros((tm, LANES), jnp.int32)
    for k in range(TOP_K):
        rk = jnp.sum(jnp.where(lane == idxs[k], excl, 0.0), axis=-1, keepdims=True)
        idx_out = jnp.where(lane == k, idxs[k], idx_out)
        w_out = jnp.where(lane == k, ex[k] / den, w_out)
        rank_out = jnp.where(lane == k, rk.astype(jnp.int32), rank_out)
    idx_ref[...] = idx_out
    w_ref[...] = w_out
    rank_ref[...] = rank_out
    run_ref[...] = run_ref[...] + jnp.sum(sel, axis=0, keepdims=True)
    cnt_ref[...] = jnp.broadcast_to(run_ref[...], cnt_ref.shape).astype(jnp.int32)


def _route(logits, n_experts, tm):
    t = logits.shape[0]
    kern = functools.partial(_route_kernel, n_experts=n_experts)
    return pl.pallas_call(
        kern,
        grid=(t // tm,),
        in_specs=[pl.BlockSpec((tm, LANES), lambda i: (i, 0))],
        out_specs=[pl.BlockSpec((tm, LANES), lambda i: (i, 0)),
                   pl.BlockSpec((tm, LANES), lambda i: (i, 0)),
                   pl.BlockSpec((tm, LANES), lambda i: (i, 0)),
                   pl.BlockSpec((SUBLANES, LANES), lambda i: (0, 0))],
        out_shape=[jax.ShapeDtypeStruct((t, LANES), jnp.int32),
                   jax.ShapeDtypeStruct((t, LANES), F32),
                   jax.ShapeDtypeStruct((t, LANES), jnp.int32),
                   jax.ShapeDtypeStruct((SUBLANES, LANES), jnp.int32)],
        scratch_shapes=[pltpu.VMEM((1, LANES), F32)],
        compiler_params=_cparams(("arbitrary",)),
        name="route",
    )(logits)


def _moe_kernel(be_ref, nused_ref, x_ref, wgu_ref, bgu_ref, wd_ref, bd_ref, o_ref,
                wgu_bf, wd_bf):
    b = pl.program_id(0)
    changed = jnp.logical_or(b == 0, be_ref[b] != be_ref[jnp.maximum(b - 1, 0)])

    @pl.when(jnp.logical_and(b < nused_ref[0], changed))
    def _():
        wgu_bf[...] = wgu_ref[0].astype(BF16)
        wd_bf[...] = wd_ref[0].astype(BF16)

    @pl.when(b < nused_ref[0])
    def _():
        ff = wd_bf.shape[0]
        gu = jnp.dot(x_ref[...].astype(BF16), wgu_bf[...], preferred_element_type=F32) + bgu_ref[0]
        gate = jnp.minimum(gu[:, :ff], SWIGLU_LIMIT)
        up = jnp.clip(gu[:, ff:], -SWIGLU_LIMIT, SWIGLU_LIMIT)
        glu = gate * _sigmoid(SWIGLU_ALPHA * gate)
        hid = ((up + 1.0) * glu).astype(BF16)
        o_ref[...] = jnp.dot(hid, wd_bf[...], preferred_element_type=F32) + bd_ref[0]

    @pl.when(b >= nused_ref[0])
    def _():
        o_ref[...] = jnp.zeros_like(o_ref)


def _moe(block_expert, nused, xs, w_gu, b_gu, w_down, b_down):
    n_slots, d = xs.shape
    ne, _, ff2 = w_gu.shape
    ff = w_down.shape[1]
    bm = MOE_BLOCK
    grid_spec = pltpu.PrefetchScalarGridSpec(
        num_scalar_prefetch=2,
        grid=(n_slots // bm,),
        in_specs=[pl.BlockSpec((bm, d), lambda b, be, nu: (b, 0)),
                  pl.BlockSpec((1, d, ff2), lambda b, be, nu: (be[b], 0, 0)),
                  pl.BlockSpec((1, 1, ff2), lambda b, be, nu: (be[b], 0, 0)),
                  pl.BlockSpec((1, ff, d), lambda b, be, nu: (be[b], 0, 0)),
                  pl.BlockSpec((1, 1, d), lambda b, be, nu: (be[b], 0, 0))],
        out_specs=pl.BlockSpec((bm, d), lambda b, be, nu: (b, 0)),
        scratch_shapes=[pltpu.VMEM((d, ff2), BF16), pltpu.VMEM((ff, d), BF16)],
    )
    return pl.pallas_call(
        _moe_kernel,
        grid_spec=grid_spec,
        out_shape=jax.ShapeDtypeStruct((n_slots, d), F32),
        compiler_params=_cparams(("arbitrary",)),
        name="moe",
    )(block_expert, nused, xs, w_gu, b_gu.reshape(ne, 1, ff2), w_down, b_down.reshape(ne, 1, d))


def _final_kernel(x1_ref, yg_ref, w_ref, g2_ref, scale_ref, shift_ref, o_ref):
    w = w_ref[...]
    f = yg_ref[0] * w[:, 0:1]
    for k in range(1, TOP_K):
        f = f + yg_ref[k] * w[:, k:k + 1]
    x2 = x1_ref[...] + g2_ref[0] * f
    o_ref[...] = _modulated_norm(x2, scale_ref[0], shift_ref[0])


def _final(x1, yg, top_w, g2, scale, shift, seq, tm):
    t, d = x1.shape
    tiles_per_seq = seq // tm

    def per_seq():
        return pl.BlockSpec((1, 1, d), lambda i: (i // tiles_per_seq, 0, 0))

    return pl.pallas_call(
        _final_kernel,
        grid=(t // tm,),
        in_specs=[pl.BlockSpec((tm, d), lambda i: (i, 0)),
                  pl.BlockSpec((TOP_K, tm, d), lambda i: (0, i, 0)),
                  pl.BlockSpec((tm, LANES), lambda i: (i, 0)),
                  per_seq(), per_seq(), per_seq()],
        out_specs=pl.BlockSpec((tm, d), lambda i: (i, 0)),
        out_shape=jax.ShapeDtypeStruct((t, d), F32),
        compiler_params=_cparams(("parallel",)),
        name="final",
    )(x1, yg, top_w, g2, scale, shift)


def _pad_cols(w, n):
    return jnp.pad(w, ((0, 0), (0, n - w.shape[1])))


def kernel(x, c, ada_w, ada_b, norm_mix_g, w_in, conf_dw_w, conf_dw_b, conf_ln_g, conf_ln_b,
           conf_out_w, conf_out_b, ssm_conv_w, ssm_conv_b, dt_bias_f, dt_bias_b, a_log_f,
           a_log_b, ssm_d, ssm_norm_g, ssm_out_w, w_o, norm_ffn_g, router_w, router_b, w_gu,
           b_gu, w_down, b_down, final_ada_w, final_ada_b, final_norm_g):
    bsz, seq, d = x.shape
    depth = ada_w.shape[0]
    t = bsz * seq
    nheads = a_log_f.shape[1]
    inner = nheads * SSM_HEAD_DIM
    gn = SSM_GROUPS * SSM_STATE
    conf = conf_dw_w.shape[2]
    n_experts = router_w.shape[2]
    assert 2 * nheads <= LANES and n_experts <= LANES

    c_pad = jnp.zeros((SUBLANES, d), F32).at[:bsz].set(c)
    fin = _ada(c_pad, final_ada_w, final_ada_b)[:bsz]
    xf = x.reshape(t, d)

    sizes = [("conf_a", conf), ("conf_g", conf), ("z", inner), ("xs", inner), ("bm", gn),
             ("cm", gn), ("gate_conf", d), ("gate_ssm", d), ("dt", LANES)]
    cols, off = {}, 0
    for name, n in sizes:
        cols[name] = off
        off += n
    src = {}
    o = 0
    for name, n in [("conf_a", conf), ("conf_g", conf), ("z", inner), ("xs", inner), ("bm", gn),
                    ("cm", gn), ("dtf", nheads), ("dtb", nheads), ("gate_conf", d),
                    ("gate_ssm", d)]:
        src[name] = (o, o + n)
        o += n

    head_of_col = jnp.arange(inner, dtype=jnp.int32) // SSM_HEAD_DIM
    lanes = jnp.arange(LANES, dtype=jnp.int32)[:, None]
    expand_f = (lanes == head_of_col[None, :]).astype(BF16)
    expand_b = (lanes == head_of_col[None, :] + nheads).astype(BF16)

    for l in range(depth):
        ada = _ada(c_pad, ada_w[l], ada_b[l])[:bsz]
        sh1, sc1, g1, sh2, sc2, g2 = [a.reshape(bsz, 1, d) for a in jnp.split(ada, 6, axis=-1)]
        scale1 = norm_mix_g[l][None, None, :] * (1.0 + sc1)
        scale2 = norm_ffn_g[l][None, None, :] * (1.0 + sc2)

        wl = w_in[l]
        w_dt = _pad_cols(jnp.concatenate([wl[:, slice(*src["dtf"])], wl[:, slice(*src["dtb"])]],
                                         axis=1), LANES)
        w_perm = jnp.concatenate(
            [wl[:, slice(*src[n])] for n in ("conf_a", "conf_g", "z", "xs", "bm", "cm",
                                             "gate_conf", "gate_ssm")] + [w_dt],
            axis=1).astype(BF16)
        proj = _inproj(xf, scale1, sh1, w_perm, seq, tm=256)

        hc = _cconv(proj, conf_dw_w[l], conf_dw_b[l], conf_ln_g[l], conf_ln_b[l], seq, tm=256)
        xbc = _sconv(proj, cols["xs"], ssm_conv_w[l], ssm_conv_b[l], seq, tm=256)

        dt_bias = _pad_cols(jnp.concatenate([dt_bias_f[l], dt_bias_b[l]])[None, :], LANES)
        a_log = _pad_cols(jnp.concatenate([a_log_f[l], a_log_b[l]])[None, :], LANES)
        dt_blk = cols["dt"] // LANES
        y_f = _ssd(xbc, proj, dt_blk, dt_bias, a_log, expand_f, bsz, seq, False, 0)
        y_b = _ssd(xbc, proj, dt_blk, dt_bias, a_log, expand_b, bsz, seq, True, nheads)

        dexp = jnp.repeat(ssm_d[l], SSM_HEAD_DIM)[None, :]
        wr = _pad_cols(router_w[l], LANES)
        br = _pad_cols(router_b[l][None, :], LANES)
        x1, v, logits = _merge(
            y_f, y_b, xbc, proj, hc, xf, cols, dexp, ssm_norm_g[l][None, :],
            ssm_out_w[l].astype(BF16), conf_out_w[l].astype(BF16), conf_out_b[l][None, :],
            w_o[l].astype(BF16), g1, scale2, sh2, wr, br, seq, tm=128)

        top_i, top_w, rank, counts = _route(logits, n_experts, tm=512)
        top_i = top_i[:, :TOP_K]
        rank = rank[:, :TOP_K]
        counts = counts[0, :n_experts]
        bm = MOE_BLOCK
        n_blocks = (t * TOP_K) // bm + n_experts
        padded = ((counts + bm - 1) // bm) * bm
        pad_end = jnp.cumsum(padded)
        pad_start = pad_end - padded
        dest = pad_start[top_i] + rank
        block_expert = jnp.minimum(
            jnp.searchsorted(pad_end, jnp.arange(n_blocks, dtype=jnp.int32) * bm, side="right"),
            n_experts - 1).astype(jnp.int32)
        nused = (pad_end[-1] // bm).astype(jnp.int32).reshape(1)

        tok = jnp.broadcast_to(jnp.arange(t, dtype=jnp.int32)[:, None], (t, TOP_K))
        buf_tok = jnp.zeros((n_blocks * bm,), jnp.int32).at[dest.reshape(-1)].set(tok.reshape(-1))
        xs = v[buf_tok]
        ys = _moe(block_expert, nused, xs, w_gu[l], b_gu[l], w_down[l], b_down[l])
        yg = ys[dest.T]

        if l == depth - 1:
            sh_f, sc_f = [a.reshape(bsz, 1, d) for a in jnp.split(fin, 2, axis=-1)]
            scale_f = final_norm_g[None, None, :] * (1.0 + sc_f)
            xf = _final(x1, yg, top_w, g2, scale_f, sh_f, seq, tm=256)
        else:
            raise NotImplementedError("depth > 1 is not wired")
    return xf.reshape(bsz, seq, d)
```

```python
import functools

import jax
import jax.numpy as jnp
from jax import lax
from jax.experimental import pallas as pl
from jax.experimental.pallas import tpu as pltpu

F32 = jnp.float32
BF16 = jnp.bfloat16
HIGHEST = lax.Precision.HIGHEST

EPS = 1e-6
CONF_KERNEL = 31
SSM_CONV = 5
SSM_HEAD_DIM = 64
SSM_GROUPS = 4
SSM_STATE = 128
SSM_CHUNK = 128
TOP_K = 4
SWIGLU_ALPHA = 1.702
SWIGLU_LIMIT = 7.0

LANES = 128
SUBLANES = 8
VMEM_LIMIT = 56 * 1024 * 1024

MOE_BLOCK = 256


def _cparams(sem):
    return pltpu.CompilerParams(dimension_semantics=sem, vmem_limit_bytes=VMEM_LIMIT)


def _sigmoid(x):
    return 1.0 / (1.0 + jnp.exp(-x))


def _silu(x):
    return x * _sigmoid(x)


def _ada_kernel(c_ref, w_ref, b_ref, o_ref):
    c = c_ref[...]
    o_ref[...] = jnp.dot(_silu(c), w_ref[...], precision=HIGHEST,
                         preferred_element_type=F32) + b_ref[...]


def _ada(c_pad, w, b):
    d, n = w.shape
    tn = 1024
    return pl.pallas_call(
        _ada_kernel,
        grid=(n // tn,),
        in_specs=[pl.BlockSpec((SUBLANES, d), lambda j: (0, 0)),
                  pl.BlockSpec((d, tn), lambda j: (0, j)),
                  pl.BlockSpec((1, tn), lambda j: (0, j))],
        out_specs=pl.BlockSpec((SUBLANES, tn), lambda j: (0, j)),
        out_shape=jax.ShapeDtypeStruct((SUBLANES, n), F32),
        compiler_params=_cparams(("parallel",)),
        name="ada",
    )(c_pad, w, b.reshape(1, n))


def _modulated_norm(x, scale, shift):
    ms = jnp.mean(x * x, axis=-1, keepdims=True)
    return (x * lax.rsqrt(ms + EPS)) * scale + shift


def _inproj_kernel(x_ref, scale_ref, shift_ref, w_ref, o_ref, *, col_chunk):
    u = _modulated_norm(x_ref[...], scale_ref[0], shift_ref[0]).astype(BF16)
    ncols = o_ref.shape[1]
    for c0 in range(0, ncols, col_chunk):
        c1 = min(c0 + col_chunk, ncols)
        o_ref[:, c0:c1] = jnp.dot(u, w_ref[:, c0:c1], preferred_element_type=F32)


def _inproj(x, scale, shift, w_bf, seq, tm):
    t, d = x.shape
    ncols = w_bf.shape[1]
    tiles_per_seq = seq // tm
    return pl.pallas_call(
        functools.partial(_inproj_kernel, col_chunk=1024),
        grid=(t // tm,),
        in_specs=[pl.BlockSpec((tm, d), lambda i: (i, 0)),
                  pl.BlockSpec((1, 1, d), lambda i: (i // tiles_per_seq, 0, 0)),
                  pl.BlockSpec((1, 1, d), lambda i: (i // tiles_per_seq, 0, 0)),
                  pl.BlockSpec(memory_space=pltpu.VMEM)],
        out_specs=pl.BlockSpec((tm, ncols), lambda i: (i, 0)),
        out_shape=jax.ShapeDtypeStruct((t, ncols), F32),
        compiler_params=_cparams(("parallel",)),
        name="inproj",
    )(x, scale, shift, w_bf)


def _fill_shifted(hbuf_ref, sh_ref, shifts, rows):
    for n, r in enumerate(shifts):
        sh_ref[n] = hbuf_ref[r:r + rows, :]


def _tap_plan(ktaps, halo):
    offs = [k + halo - ktaps // 2 for k in range(ktaps)]
    shifts = sorted({o % SUBLANES for o in offs})
    plan = [(shifts.index(o % SUBLANES), (o // SUBLANES) * SUBLANES) for o in offs]
    return shifts, plan, max(o // SUBLANES for o in offs) * SUBLANES


def _conv_rows(sh_ref, w_ref, plan, r0, rc):
    acc = None
    for k, (n, q8) in enumerate(plan):
        term = sh_ref[n, pl.ds(pl.multiple_of(r0 + q8, SUBLANES), rc), :] * w_ref[k:k + 1, :]
        acc = term if acc is None else acc + term
    return acc


def _cconv_kernel(a_ref, g_ref, ap_ref, gp_ref, an_ref, gn_ref, w_ref, b_ref, lg_ref, lb_ref,
                  o_ref, hbuf_ref, sh_ref, *, tiles_per_seq, halo, shifts, plan, rc):
    i = pl.program_id(0)
    tm = a_ref.shape[0]
    first = (i % tiles_per_seq) == 0
    last = (i % tiles_per_seq) == tiles_per_seq - 1
    glu_p = ap_ref[...] * _sigmoid(gp_ref[...])
    glu_n = an_ref[...] * _sigmoid(gn_ref[...])
    hbuf_ref[0:halo, :] = jnp.where(first, 0.0, glu_p)
    hbuf_ref[halo:halo + tm, :] = a_ref[...] * _sigmoid(g_ref[...])
    hbuf_ref[halo + tm:2 * halo + tm, :] = jnp.where(last, 0.0, glu_n)
    _fill_shifted(hbuf_ref, sh_ref, shifts, sh_ref.shape[1])

    def body(j, carry):
        r0 = j * rc
        h = _conv_rows(sh_ref, w_ref, plan, r0, rc) + b_ref[...]
        mu = jnp.mean(h, axis=-1, keepdims=True)
        hc = h - mu
        var = jnp.mean(hc * hc, axis=-1, keepdims=True)
        y = hc * lax.rsqrt(var + EPS) * lg_ref[...] + lb_ref[...]
        o_ref[pl.ds(pl.multiple_of(r0, rc), rc), :] = _silu(y).astype(o_ref.dtype)
        return carry

    lax.fori_loop(0, tm // rc, body, 0)


def _cconv(proj, w, b, ln_g, ln_b, seq, tm):
    t = proj.shape[0]
    ch = w.shape[1]
    halo = 16
    rc = 16
    shifts, plan, qmax = _tap_plan(CONF_KERNEL, halo)
    tiles_per_seq = seq // tm
    hb = tm // halo
    nhb = t // halo
    cb = ch // ch

    def prev_map(col):
        return lambda i: (jnp.maximum(i * hb - 1, 0), col)

    def next_map(col):
        return lambda i: (jnp.minimum((i + 1) * hb, nhb - 1), col)

    kern = functools.partial(_cconv_kernel, tiles_per_seq=tiles_per_seq, halo=halo,
                             shifts=shifts, plan=plan, rc=rc)
    return pl.pallas_call(
        kern,
        grid=(t // tm,),
        in_specs=[pl.BlockSpec((tm, ch), lambda i: (i, 0)),
                  pl.BlockSpec((tm, ch), lambda i: (i, cb)),
                  pl.BlockSpec((halo, ch), prev_map(0)),
                  pl.BlockSpec((halo, ch), prev_map(cb)),
                  pl.BlockSpec((halo, ch), next_map(0)),
                  pl.BlockSpec((halo, ch), next_map(cb)),
                  pl.BlockSpec((CONF_KERNEL, ch), lambda i: (0, 0)),
                  pl.BlockSpec((1, ch), lambda i: (0, 0)),
                  pl.BlockSpec((1, ch), lambda i: (0, 0)),
                  pl.BlockSpec((1, ch), lambda i: (0, 0))],
        out_specs=pl.BlockSpec((tm, ch), lambda i: (i, 0)),
        out_shape=jax.ShapeDtypeStruct((t, ch), BF16),
        scratch_shapes=[pltpu.VMEM((tm + 2 * halo, ch), F32),
                        pltpu.VMEM((len(shifts), tm + qmax, ch), F32)],
        compiler_params=_cparams(("parallel",)),
        name="cconv",
    )(proj, proj, proj, proj, proj, proj, w, b.reshape(1, ch), ln_g.reshape(1, ch),
      ln_b.reshape(1, ch))


def _sconv_kernel(x_ref, xp_ref, xn_ref, w_ref, b_ref, o_ref, hbuf_ref, sh_ref, *,
                  tiles_per_seq, halo, shifts, plan, rc):
    i = pl.program_id(0)
    tm = x_ref.shape[0]
    first = (i % tiles_per_seq) == 0
    last = (i % tiles_per_seq) == tiles_per_seq - 1
    hbuf_ref[0:halo, :] = jnp.where(first, 0.0, xp_ref[...])
    hbuf_ref[halo:halo + tm, :] = x_ref[...]
    hbuf_ref[halo + tm:2 * halo + tm, :] = jnp.where(last, 0.0, xn_ref[...])
    _fill_shifted(hbuf_ref, sh_ref, shifts, sh_ref.shape[1])

    def body(j, carry):
        r0 = j * rc
        h = _conv_rows(sh_ref, w_ref, plan, r0, rc) + b_ref[...]
        o_ref[pl.ds(pl.multiple_of(r0, rc), rc), :] = _silu(h)
        return carry

    lax.fori_loop(0, tm // rc, body, 0)


def _sconv(proj, col0, w, b, seq, tm):
    t = proj.shape[0]
    ch = w.shape[1]
    cw = 1024
    halo = SUBLANES
    rc = 16
    shifts, plan, qmax = _tap_plan(SSM_CONV, halo)
    tiles_per_seq = seq // tm
    hb = tm // halo
    nhb = t // halo
    c0 = col0 // cw
    kern = functools.partial(_sconv_kernel, tiles_per_seq=tiles_per_seq, halo=halo,
                             shifts=shifts, plan=plan, rc=rc)
    return pl.pallas_call(
        kern,
        grid=(t // tm, ch // cw),
        in_specs=[pl.BlockSpec((tm, cw), lambda i, j: (i, c0 + j)),
                  pl.BlockSpec((halo, cw), lambda i, j: (jnp.maximum(i * hb - 1, 0), c0 + j)),
                  pl.BlockSpec((halo, cw),
                               lambda i, j: (jnp.minimum((i + 1) * hb, nhb - 1), c0 + j)),
                  pl.BlockSpec((SSM_CONV, cw), lambda i, j: (0, j)),
                  pl.BlockSpec((1, cw), lambda i, j: (0, j))],
        out_specs=pl.BlockSpec((tm, cw), lambda i, j: (i, j)),
        out_shape=jax.ShapeDtypeStruct((t, ch), F32),
        scratch_shapes=[pltpu.VMEM((tm + 2 * halo, cw), F32),
                        pltpu.VMEM((len(shifts), tm + qmax, cw), F32)],
        compiler_params=_cparams(("parallel", "parallel")),
        name="sconv",
    )(proj, proj, proj, w, b.reshape(1, ch))


def _split3(f):
    hi = f.astype(BF16)
    r1 = f - hi.astype(F32)
    mid = r1.astype(BF16)
    lo = (r1 - mid.astype(F32)).astype(BF16)
    return hi, mid, lo


def _expand_heads(f, e):
    hi, mid, lo = _split3(f)
    return (jnp.dot(hi, e, preferred_element_type=F32)
            + jnp.dot(mid, e, preferred_element_type=F32)
            + jnp.dot(lo, e, preferred_element_type=F32))


def _ssd_kernel(x_ref, b_ref, c_ref, dt_ref, dtb_ref, alog_ref, e_ref, o_ref, state_ref, *,
                reverse, lane0):
    q = x_ref.shape[0]
    nheads = x_ref.shape[1] // SSM_HEAD_DIM
    hpg = nheads // SSM_GROUPS
    gw = hpg * SSM_HEAD_DIM

    @pl.when(pl.program_id(1) == 0)
    def _():
        state_ref[...] = jnp.zeros_like(state_ref)

    row = lax.broadcasted_iota(jnp.int32, (q, q), 0)
    col = lax.broadcasted_iota(jnp.int32, (q, q), 1)
    keep = (col >= row) if reverse else (col <= row)

    z = dt_ref[...] + dtb_ref[...]
    dt = jnp.maximum(z, 0.0) + jnp.log(1.0 + jnp.exp(-jnp.abs(z)))
    a = dt * (-jnp.exp(alog_ref[...]))
    acs = jnp.dot(keep.astype(F32), a, precision=HIGHEST, preferred_element_type=F32)
    tot = acs[0:1, :] if reverse else acs[q - 1:q, :]
    acs_t = acs.T
    dt_t = dt.T
    e = e_ref[...]
    x = x_ref[...]

    exp_acs = _expand_heads(jnp.exp(acs), e)
    wstate = _expand_heads(dt * jnp.exp(tot - acs), e)
    carry = _expand_heads(jnp.broadcast_to(jnp.exp(tot), (SUBLANES, LANES)), e)[0:1, :]
    wx = (wstate * x).astype(BF16)
    xb = x.astype(BF16)

    for g in range(SSM_GROUPS):
        bg = b_ref[:, g * SSM_STATE:(g + 1) * SSM_STATE].astype(BF16)
        cg = c_ref[:, g * SSM_STATE:(g + 1) * SSM_STATE].astype(BF16)
        cb = lax.dot_general(cg, bg, (((1,), (1,)), ((), ())), preferred_element_type=F32)
        gs = slice(g * gw, (g + 1) * gw)
        s_old = state_ref[:, gs]
        y_off = jnp.dot(cg, s_old.astype(BF16), preferred_element_type=F32) * exp_acs[:, gs]
        for r in range(hpg):
            h = g * hpg + r
            hl = lane0 + h
            seg = acs[:, hl:hl + 1] - acs_t[hl:hl + 1, :]
            lmat = jnp.exp(jnp.where(keep, seg, -jnp.inf))
            m = (cb * lmat * dt_t[hl:hl + 1, :]).astype(BF16)
            hs = slice(h * SSM_HEAD_DIM, (h + 1) * SSM_HEAD_DIM)
            y_h = jnp.dot(m, xb[:, hs], preferred_element_type=F32)
            o_ref[:, hs] = y_h + y_off[:, r * SSM_HEAD_DIM:(r + 1) * SSM_HEAD_DIM]
        upd = lax.dot_general(bg, wx[:, gs], (((0,), (0,)), ((), ())),
                              preferred_element_type=F32)
        state_ref[:, gs] = s_old * carry[:, gs] + upd


def _ssd(xbc, proj, dt_col_block, dt_bias, a_log, expand, bsz, seq, reverse, lane0):
    t = xbc.shape[0]
    q = SSM_CHUNK
    nc = seq // q
    inner = expand.shape[1]
    gn = SSM_GROUPS * SSM_STATE
    b_blk = inner // gn
    c_blk = b_blk + 1

    def tok(b, c):
        return b * nc + (nc - 1 - c if reverse else c)

    kern = functools.partial(_ssd_kernel, reverse=reverse, lane0=lane0)
    return pl.pallas_call(
        kern,
        grid=(bsz, nc),
        in_specs=[pl.BlockSpec((q, inner), lambda b, c: (tok(b, c), 0)),
                  pl.BlockSpec((q, gn), lambda b, c: (tok(b, c), b_blk)),
                  pl.BlockSpec((q, gn), lambda b, c: (tok(b, c), c_blk)),
                  pl.BlockSpec((q, LANES), lambda b, c: (tok(b, c), dt_col_block)),
                  pl.BlockSpec((1, LANES), lambda b, c: (0, 0)),
                  pl.BlockSpec((1, LANES), lambda b, c: (0, 0)),
                  pl.BlockSpec((LANES, inner), lambda b, c: (0, 0))],
        out_specs=pl.BlockSpec((q, inner), lambda b, c: (tok(b, c), 0)),
        out_shape=jax.ShapeDtypeStruct((t, inner), F32),
        scratch_shapes=[pltpu.VMEM((SSM_STATE, inner), F32)],
        compiler_params=_cparams(("arbitrary", "arbitrary")),
        name="ssd_bwd" if reverse else "ssd_fwd",
    )(xbc, xbc, xbc, proj, dt_bias, a_log, expand)


def _merge_kernel(yf_ref, yb_ref, xc_ref, z_ref, hc_ref, gc_ref, gs_ref, x_ref,
                  dexp_ref, ng_ref, wso_ref, wco_ref, bco_ref, wo_ref, g1_ref,
                  scale_ref, shift_ref, wr_ref, br_ref,
                  x1_ref, v_ref, lg_ref, *, ngroups):
    y = (yf_ref[...] + yb_ref[...] + xc_ref[...] * dexp_ref[...]) * _silu(z_ref[...])
    gw = y.shape[1] // ngroups
    parts = []
    for g in range(ngroups):
        yg = y[:, g * gw:(g + 1) * gw]
        ms = jnp.mean(yg * yg, axis=-1, keepdims=True)
        parts.append((yg * lax.rsqrt(ms + EPS)) * ng_ref[:, g * gw:(g + 1) * gw])
    ysn = jnp.concatenate(parts, axis=-1).astype(BF16)
    y_ssm = jnp.dot(ysn, wso_ref[...], preferred_element_type=F32)
    y_conf = jnp.dot(hc_ref[...], wco_ref[...], preferred_element_type=F32) + bco_ref[...]
    merged = _sigmoid(gc_ref[...]) * y_conf + _sigmoid(gs_ref[...]) * y_ssm
    o = jnp.dot(merged.astype(BF16), wo_ref[...], preferred_element_type=F32)
    x1 = x_ref[...] + g1_ref[0] * o
    x1_ref[...] = x1
    v = _modulated_norm(x1, scale_ref[0], shift_ref[0])
    v_ref[...] = v
    lg_ref[...] = jnp.dot(v, wr_ref[...], precision=HIGHEST,
                          preferred_element_type=F32) + br_ref[...]


def _merge(yf, yb, xbc, proj, hc, x, cols, dexp, norm_g, wso, wco, bco, wo, g1, scale, shift,
           wr, br, seq, tm):
    t, d = x.shape
    inner = yf.shape[1]
    tiles_per_seq = seq // tm
    z_blk = cols["z"] // inner
    gc_blk = cols["gate_conf"] // d
    gs_blk = cols["gate_ssm"] // d

    def const(shape):
        return pl.BlockSpec(shape, lambda i: tuple(0 for _ in shape))

    def per_seq():
        return pl.BlockSpec((1, 1, d), lambda i: (i // tiles_per_seq, 0, 0))

    kern = functools.partial(_merge_kernel, ngroups=SSM_GROUPS)
    return pl.pallas_call(
        kern,
        grid=(t // tm,),
        in_specs=[pl.BlockSpec((tm, inner), lambda i: (i, 0)),
                  pl.BlockSpec((tm, inner), lambda i: (i, 0)),
                  pl.BlockSpec((tm, inner), lambda i: (i, 0)),
                  pl.BlockSpec((tm, inner), lambda i: (i, z_blk)),
                  pl.BlockSpec((tm, d), lambda i: (i, 0)),
                  pl.BlockSpec((tm, d), lambda i: (i, gc_blk)),
                  pl.BlockSpec((tm, d), lambda i: (i, gs_blk)),
                  pl.BlockSpec((tm, d), lambda i: (i, 0)),
                  const((1, inner)), const((1, inner)),
                  const((inner, d)), const((d, d)), const((1, d)), const((d, d)),
                  per_seq(), per_seq(), per_seq(),
                  const((d, LANES)), const((1, LANES))],
        out_specs=[pl.BlockSpec((tm, d), lambda i: (i, 0)),
                   pl.BlockSpec((tm, d), lambda i: (i, 0)),
                   pl.BlockSpec((tm, LANES), lambda i: (i, 0))],
        out_shape=[jax.ShapeDtypeStruct((t, d), F32),
                   jax.ShapeDtypeStruct((t, d), F32),
                   jax.ShapeDtypeStruct((t, LANES), F32)],
        compiler_params=_cparams(("parallel",)),
        name="merge",
    )(yf, yb, xbc, proj, hc, proj, proj, x, dexp, norm_g, wso, wco, bco, wo, g1, scale, shift,
      wr, br)


def _route_kernel(lg_ref, idx_ref, w_ref, rank_ref, cnt_ref, run_ref, *, n_experts):
    @pl.when(pl.program_id(0) == 0)
    def _():
        run_ref[...] = jnp.zeros_like(run_ref)

    tm = lg_ref.shape[0]
    lane = lax.broadcasted_iota(jnp.int32, (tm, LANES), 1)
    lg = jnp.where(lane < n_experts, lg_ref[...], -jnp.inf)
    sel = jnp.zeros((tm, LANES), F32)
    vals, idxs = [], []
    for _ in range(TOP_K):
        m = jnp.max(lg, axis=-1, keepdims=True)
        ix = jnp.min(jnp.where(lg == m, lane, LANES), axis=-1, keepdims=True)
        hit = lane == ix
        sel = jnp.where(hit, 1.0, sel)
        lg = jnp.where(hit, -jnp.inf, lg)
        vals.append(m)
        idxs.append(ix)
    ex = [jnp.exp(v - vals[0]) for v in vals]
    den = ex[0] + ex[1] + ex[2] + ex[3]
    row = lax.broadcasted_iota(jnp.int32, (tm, tm), 0)
    col = lax.broadcasted_iota(jnp.int32, (tm, tm), 1)
    before = (col < row).astype(BF16)
    excl = jnp.dot(before, sel.astype(BF16), preferred_element_type=F32) + run_ref[...]
    idx_out = jnp.zeros((tm, LANES), jnp.int32)
    w_out = jnp.zeros((tm, LANES), F32)
    rank_out = jnp.zeros((tm, LANES), jnp.int32)
    for k in range(TOP_K):
        rk = jnp.sum(jnp.where(lane == idxs[k], excl, 0.0), axis=-1, keepdims=True)
        idx_out = jnp.where(lane == k, idxs[k], idx_out)
        w_out = jnp.where(lane == k, ex[k] / den, w_out)
        rank_out = jnp.where(lane == k, rk.astype(jnp.int32), rank_out)
    idx_ref[...] = idx_out
    w_ref[...] = w_out
    rank_ref[...] = rank_out
    run_ref[...] = run_ref[...] + jnp.sum(sel, axis=0, keepdims=True)
    cnt_ref[...] = jnp.broadcast_to(run_ref[...], cnt_ref.shape).astype(jnp.int32)


def _route(logits, n_experts, tm):
    t = logits.shape[0]
    kern = functools.partial(_route_kernel, n_experts=n_experts)
    return pl.pallas_call(
        kern,
        grid=(t // tm,),
        in_specs=[pl.BlockSpec((tm, LANES), lambda i: (i, 0))],
        out_specs=[pl.BlockSpec((tm, LANES), lambda i: (i, 0)),
                   pl.BlockSpec((tm, LANES), lambda i: (i, 0)),
                   pl.BlockSpec((tm, LANES), lambda i: (i, 0)),
                   pl.BlockSpec((SUBLANES, LANES), lambda i: (0, 0))],
        out_shape=[jax.ShapeDtypeStruct((t, LANES), jnp.int32),
                   jax.ShapeDtypeStruct((t, LANES), F32),
                   jax.ShapeDtypeStruct((t, LANES), jnp.int32),
                   jax.ShapeDtypeStruct((SUBLANES, LANES), jnp.int32)],
        scratch_shapes=[pltpu.VMEM((1, LANES), F32)],
        compiler_params=_cparams(("arbitrary",)),
        name="route",
    )(logits)


def _dest_kernel(idx_ref, rank_ref, pstart_ref, o_ref):
    tm = idx_ref.shape[0]
    lane = lax.broadcasted_iota(jnp.int32, (tm, LANES), 1)
    idx = idx_ref[...]
    out = jnp.zeros((tm, LANES), jnp.int32)
    for k in range(TOP_K):
        base = jnp.sum(jnp.where(lane == idx[:, k:k + 1], pstart_ref[...], 0), axis=-1,
                       keepdims=True)
        out = jnp.where(lane == k, base + rank_ref[:, k:k + 1], out)
    o_ref[...] = out


def _dest(top_i, rank, pad_start_row, tm):
    t = top_i.shape[0]
    return pl.pallas_call(
        _dest_kernel,
        grid=(t // tm,),
        in_specs=[pl.BlockSpec((tm, LANES), lambda i: (i, 0)),
                  pl.BlockSpec((tm, LANES), lambda i: (i, 0)),
                  pl.BlockSpec((1, LANES), lambda i: (0, 0))],
        out_specs=pl.BlockSpec((tm, LANES), lambda i: (i, 0)),
        out_shape=jax.ShapeDtypeStruct((t, LANES), jnp.int32),
        compiler_params=_cparams(("parallel",)),
        name="dest",
    )(top_i, rank, pad_start_row)


def _row_copy(src_hbm, src_row, dst_hbm, dst_row, sem):
    return pltpu.make_async_copy(src_hbm.at[pl.ds(src_row, 1), :],
                                 dst_hbm.at[pl.ds(dst_row, 1), :], sem)


def _dispatch_kernel(dest_ref, fill_ref, end_ref, v_hbm, xs_hbm, sem, fsem, *, tm, n_experts):
    i = pl.program_id(0)

    @pl.when(i == 0)
    def _():
        for e in range(n_experts):
            def fill(s, c):
                _row_copy(v_hbm, 0, xs_hbm, s, fsem).start()
                return c

            def drain(s, c):
                _row_copy(v_hbm, 0, xs_hbm, s, fsem).wait()
                return c

            lax.fori_loop(fill_ref[e], end_ref[e], fill, 0)
            lax.fori_loop(fill_ref[e], end_ref[e], drain, 0)

    t0 = i * tm

    def issue(r, c):
        for k in range(TOP_K):
            _row_copy(v_hbm, t0 + r, xs_hbm, dest_ref[(t0 + r) * TOP_K + k], sem).start()
        return c

    def drain(r, c):
        for k in range(TOP_K):
            _row_copy(v_hbm, 0, xs_hbm, 0, sem).wait()
        return c

    lax.fori_loop(0, tm, issue, 0)
    lax.fori_loop(0, tm, drain, 0)


def _dispatch(dest_flat, fill_start, pad_end, v, n_slots, tm):
    t, d = v.shape
    n_experts = fill_start.shape[0]
    kern = functools.partial(_dispatch_kernel, tm=tm, n_experts=n_experts)
    grid_spec = pltpu.PrefetchScalarGridSpec(
        num_scalar_prefetch=3,
        grid=(t // tm,),
        in_specs=[pl.BlockSpec(memory_space=pl.ANY)],
        out_specs=pl.BlockSpec(memory_space=pl.ANY),
        scratch_shapes=[pltpu.SemaphoreType.DMA, pltpu.SemaphoreType.DMA],
    )
    return pl.pallas_call(
        kern,
        grid_spec=grid_spec,
        out_shape=jax.ShapeDtypeStruct((n_slots, d), v.dtype),
        compiler_params=_cparams(("arbitrary",)),
        name="dispatch",
    )(dest_flat, fill_start, pad_end, v)


def _moe_kernel(be_ref, nused_ref, x_ref, wgu_ref, bgu_ref, wd_ref, bd_ref, o_ref,
                wgu_bf, wd_bf):
    b = pl.program_id(0)
    changed = jnp.logical_or(b == 0, be_ref[b] != be_ref[jnp.maximum(b - 1, 0)])
    active = b < nused_ref[0]

    @pl.when(jnp.logical_and(active, changed))
    def _():
        wgu_bf[...] = wgu_ref[0].astype(BF16)
        wd_bf[...] = wd_ref[0].astype(BF16)

    @pl.when(active)
    def _():
        ff = wd_bf.shape[0]
        gu = jnp.dot(x_ref[...].astype(BF16), wgu_bf[...], preferred_element_type=F32) + bgu_ref[0]
        gate = jnp.minimum(gu[:, :ff], SWIGLU_LIMIT)
        up = jnp.clip(gu[:, ff:], -SWIGLU_LIMIT, SWIGLU_LIMIT)
        glu = gate * _sigmoid(SWIGLU_ALPHA * gate)
        hid = ((up + 1.0) * glu).astype(BF16)
        o_ref[...] = jnp.dot(hid, wd_bf[...], preferred_element_type=F32) + bd_ref[0]


def _moe(block_expert, nused, xs, w_gu, b_gu, w_down, b_down):
    n_slots, d = xs.shape
    ne, _, ff2 = w_gu.shape
    ff = w_down.shape[1]
    bm = MOE_BLOCK

    def blk(b, nu):
        return jnp.minimum(b, nu[0] - 1)

    grid_spec = pltpu.PrefetchScalarGridSpec(
        num_scalar_prefetch=2,
        grid=(n_slots // bm,),
        in_specs=[pl.BlockSpec((bm, d), lambda b, be, nu: (blk(b, nu), 0)),
                  pl.BlockSpec((1, d, ff2), lambda b, be, nu: (be[blk(b, nu)], 0, 0)),
                  pl.BlockSpec((1, 1, ff2), lambda b, be, nu: (be[blk(b, nu)], 0, 0)),
                  pl.BlockSpec((1, ff, d), lambda b, be, nu: (be[blk(b, nu)], 0, 0)),
                  pl.BlockSpec((1, 1, d), lambda b, be, nu: (be[blk(b, nu)], 0, 0))],
        out_specs=pl.BlockSpec((bm, d), lambda b, be, nu: (blk(b, nu), 0)),
        scratch_shapes=[pltpu.VMEM((d, ff2), BF16), pltpu.VMEM((ff, d), BF16)],
    )
    return pl.pallas_call(
        _moe_kernel,
        grid_spec=grid_spec,
        out_shape=jax.ShapeDtypeStruct((n_slots, d), F32),
        compiler_params=_cparams(("arbitrary",)),
        name="moe",
    )(block_expert, nused, xs, w_gu, b_gu.reshape(ne, 1, ff2), w_down, b_down.reshape(ne, 1, d))


def _final_kernel(dest_ref, x1_ref, w_ref, g2_ref, scale_ref, shift_ref, ys_hbm, o_ref,
                  gbuf, sems):
    i = pl.program_id(0)
    n = pl.num_programs(0)
    tm = x1_ref.shape[0]
    slot = i % 2

    def issue(tile, slot_):
        def body(r, c):
            for k in range(TOP_K):
                src = dest_ref[(tile * tm + r) * TOP_K + k]
                pltpu.make_async_copy(ys_hbm.at[pl.ds(src, 1), :],
                                      gbuf.at[slot_, pl.ds(k * tm + r, 1), :],
                                      sems.at[slot_]).start()
            return c
        lax.fori_loop(0, tm, body, 0)

    @pl.when(i == 0)
    def _():
        issue(0, 0)

    @pl.when(i + 1 < n)
    def _():
        issue(i + 1, 1 - slot)

    def drain(r, c):
        for k in range(TOP_K):
            pltpu.make_async_copy(ys_hbm.at[pl.ds(0, 1), :], gbuf.at[slot, pl.ds(0, 1), :],
                                  sems.at[slot]).wait()
        return c
    lax.fori_loop(0, tm, drain, 0)

    w = w_ref[...]
    f = gbuf[slot, 0:tm, :] * w[:, 0:1]
    for k in range(1, TOP_K):
        f = f + gbuf[slot, k * tm:(k + 1) * tm, :] * w[:, k:k + 1]
    x2 = x1_ref[...] + g2_ref[0] * f
    o_ref[...] = _modulated_norm(x2, scale_ref[0], shift_ref[0])


def _final(dest_flat, x1, ys, top_w, g2, scale, shift, seq, tm):
    t, d = x1.shape
    tiles_per_seq = seq // tm

    def per_seq():
        return pl.BlockSpec((1, 1, d), lambda i, dst: (i // tiles_per_seq, 0, 0))

    grid_spec = pltpu.PrefetchScalarGridSpec(
        num_scalar_prefetch=1,
        grid=(t // tm,),
        in_specs=[pl.BlockSpec((tm, d), lambda i, dst: (i, 0)),
                  pl.BlockSpec((tm, LANES), lambda i, dst: (i, 0)),
                  per_seq(), per_seq(), per_seq(),
                  pl.BlockSpec(memory_space=pl.ANY)],
        out_specs=pl.BlockSpec((tm, d), lambda i, dst: (i, 0)),
        scratch_shapes=[pltpu.VMEM((2, TOP_K * tm, d), F32), pltpu.SemaphoreType.DMA((2,))],
    )
    return pl.pallas_call(
        _final_kernel,
        grid_spec=grid_spec,
        out_shape=jax.ShapeDtypeStruct((t, d), F32),
        compiler_params=_cparams(("arbitrary",)),
        name="final",
    )(dest_flat, x1, top_w, g2, scale, shift, ys)


def _pad_cols(w, n):
    return jnp.pad(w, ((0, 0), (0, n - w.shape[1])))


def kernel(x, c, ada_w, ada_b, norm_mix_g, w_in, conf_dw_w, conf_dw_b, conf_ln_g, conf_ln_b,
           conf_out_w, conf_out_b, ssm_conv_w, ssm_conv_b, dt_bias_f, dt_bias_b, a_log_f,
           a_log_b, ssm_d, ssm_norm_g, ssm_out_w, w_o, norm_ffn_g, router_w, router_b, w_gu,
           b_gu, w_down, b_down, final_ada_w, final_ada_b, final_norm_g):
    bsz, seq, d = x.shape
    depth = ada_w.shape[0]
    t = bsz * seq
    nheads = a_log_f.shape[1]
    inner = nheads * SSM_HEAD_DIM
    gn = SSM_GROUPS * SSM_STATE
    conf = conf_dw_w.shape[2]
    n_experts = router_w.shape[2]
    assert 2 * nheads <= LANES and n_experts <= LANES

    c_pad = jnp.zeros((SUBLANES, d), F32).at[:bsz].set(c)
    fin = _ada(c_pad, final_ada_w, final_ada_b)[:bsz]
    xf = x.reshape(t, d)

    sizes = [("conf_a", conf), ("conf_g", conf), ("z", inner), ("xs", inner), ("bm", gn),
             ("cm", gn), ("gate_conf", d), ("gate_ssm", d), ("dt", LANES)]
    cols, off = {}, 0
    for name, n in sizes:
        cols[name] = off
        off += n
    src = {}
    o = 0
    for name, n in [("conf_a", conf), ("conf_g", conf), ("z", inner), ("xs", inner), ("bm", gn),
                    ("cm", gn), ("dtf", nheads), ("dtb", nheads), ("gate_conf", d),
                    ("gate_ssm", d)]:
        src[name] = (o, o + n)
        o += n

    head_of_col = jnp.arange(inner, dtype=jnp.int32) // SSM_HEAD_DIM
    lanes = jnp.arange(LANES, dtype=jnp.int32)[:, None]
    expand_f = (lanes == head_of_col[None, :]).astype(BF16)
    expand_b = (lanes == head_of_col[None, :] + nheads).astype(BF16)

    for l in range(depth):
        ada = _ada(c_pad, ada_w[l], ada_b[l])[:bsz]
        sh1, sc1, g1, sh2, sc2, g2 = [a.reshape(bsz, 1, d) for a in jnp.split(ada, 6, axis=-1)]
        scale1 = norm_mix_g[l][None, None, :] * (1.0 + sc1)
        scale2 = norm_ffn_g[l][None, None, :] * (1.0 + sc2)

        wl = w_in[l]
        w_dt = _pad_cols(jnp.concatenate([wl[:, slice(*src["dtf"])], wl[:, slice(*src["dtb"])]],
                                         axis=1), LANES)
        w_perm = jnp.concatenate(
            [wl[:, slice(*src[n])] for n in ("conf_a", "conf_g", "z", "xs", "bm", "cm",
                                             "gate_conf", "gate_ssm")] + [w_dt],
            axis=1).astype(BF16)
        proj = _inproj(xf, scale1, sh1, w_perm, seq, tm=256)

        hc = _cconv(proj, conf_dw_w[l], conf_dw_b[l], conf_ln_g[l], conf_ln_b[l], seq, tm=256)
        xbc = _sconv(proj, cols["xs"], ssm_conv_w[l], ssm_conv_b[l], seq, tm=256)

        dt_bias = _pad_cols(jnp.concatenate([dt_bias_f[l], dt_bias_b[l]])[None, :], LANES)
        a_log = _pad_cols(jnp.concatenate([a_log_f[l], a_log_b[l]])[None, :], LANES)
        dt_blk = cols["dt"] // LANES
        y_f = _ssd(xbc, proj, dt_blk, dt_bias, a_log, expand_f, bsz, seq, False, 0)
        y_b = _ssd(xbc, proj, dt_blk, dt_bias, a_log, expand_b, bsz, seq, True, nheads)

        dexp = jnp.repeat(ssm_d[l], SSM_HEAD_DIM)[None, :]
        wr = _pad_cols(router_w[l], LANES)
        br = _pad_cols(router_b[l][None, :], LANES)
        x1, v, logits = _merge(
            y_f, y_b, xbc, proj, hc, xf, cols, dexp, ssm_norm_g[l][None, :],
            ssm_out_w[l].astype(BF16), conf_out_w[l].astype(BF16), conf_out_b[l][None, :],
            w_o[l].astype(BF16), g1, scale2, sh2, wr, br, seq, tm=128)

        top_i, top_w, rank, counts = _route(logits, n_experts, tm=512)
        counts = counts[0, :n_experts]
        bm = MOE_BLOCK
        n_blocks = (t * TOP_K) // bm + n_experts
        padded = ((counts + bm - 1) // bm) * bm
        pad_end = jnp.cumsum(padded)
        pad_start = pad_end - padded
        dest = _dest(top_i, rank, _pad_cols(pad_start[None, :], LANES), tm=512)
        dest_flat = dest[:, :TOP_K].reshape(-1)
        block_first = jnp.arange(n_blocks, dtype=jnp.int32) * bm
        block_expert = jnp.minimum(
            jnp.sum((pad_end[None, :] <= block_first[:, None]).astype(jnp.int32), axis=1),
            n_experts - 1).astype(jnp.int32)
        nused = (pad_end[-1] // bm).astype(jnp.int32).reshape(1)
        xs = _dispatch(dest_flat, (pad_start + counts).astype(jnp.int32),
                       pad_end.astype(jnp.int32), v, n_blocks * bm, tm=256)
        ys = _moe(block_expert, nused, xs, w_gu[l], b_gu[l], w_down[l], b_down[l])

        if l == depth - 1:
            sh_f, sc_f = [a.reshape(bsz, 1, d) for a in jnp.split(fin, 2, axis=-1)]
            scale_f = final_norm_g[None, None, :] * (1.0 + sc_f)
            xf = _final(dest_flat, x1, ys, top_w, g2, scale_f, sh_f, seq, tm=256)
        else:
            raise NotImplementedError("depth > 1 is not wired")
    return xf.reshape(bsz, seq, d)
```

```python
import functools

import jax
import jax.numpy as jnp
from jax import lax
from jax.experimental import pallas as pl
from jax.experimental.pallas import tpu as pltpu

F32 = jnp.float32
BF16 = jnp.bfloat16
HIGHEST = lax.Precision.HIGHEST

EPS = 1e-6
CONF_KERNEL = 31
SSM_CONV = 5
SSM_HEAD_DIM = 64
SSM_GROUPS = 4
SSM_STATE = 128
SSM_CHUNK = 128
TOP_K = 4
SWIGLU_ALPHA = 1.702
SWIGLU_LIMIT = 7.0

LANES = 128
SUBLANES = 8
VMEM_LIMIT = 56 * 1024 * 1024

MOE_BLOCK = 256


def _cparams(sem):
    return pltpu.CompilerParams(dimension_semantics=sem, vmem_limit_bytes=VMEM_LIMIT)


def _store_rows(ref, val, row0=0):
    n, d = val.shape
    nch = d // LANES
    for s in range(nch):
        ref[pl.ds(row0 * nch + s, n, stride=nch), :] = val[:, s * LANES:(s + 1) * LANES]


def _load_rows(ref, n, nch, row0=0, lead=()):
    return jnp.concatenate(
        [ref[lead + (pl.ds(row0 * nch + s, n, stride=nch), slice(None))] for s in range(nch)],
        axis=-1)


def _row(ref, r, nch, lead=()):
    return ref.at[lead + (pl.ds(pl.multiple_of(r * nch, nch), nch), slice(None))]


def _sigmoid(x):
    return 1.0 / (1.0 + jnp.exp(-x))


def _silu(x):
    return x * _sigmoid(x)


def _ada_kernel(c_ref, w_ref, b_ref, o_ref):
    c = c_ref[...]
    o_ref[...] = jnp.dot(_silu(c), w_ref[...], precision=HIGHEST,
                         preferred_element_type=F32) + b_ref[...]


def _ada(c_pad, w, b):
    d, n = w.shape
    tn = 1024
    return pl.pallas_call(
        _ada_kernel,
        grid=(n // tn,),
        in_specs=[pl.BlockSpec((SUBLANES, d), lambda j: (0, 0)),
                  pl.BlockSpec((d, tn), lambda j: (0, j)),
                  pl.BlockSpec((1, tn), lambda j: (0, j))],
        out_specs=pl.BlockSpec((SUBLANES, tn), lambda j: (0, j)),
        out_shape=jax.ShapeDtypeStruct((SUBLANES, n), F32),
        compiler_params=_cparams(("parallel",)),
        name="ada",
    )(c_pad, w, b.reshape(1, n))


def _modulated_norm(x, scale, shift):
    ms = jnp.mean(x * x, axis=-1, keepdims=True)
    return (x * lax.rsqrt(ms + EPS)) * scale + shift


def _inproj_kernel(x_ref, scale_ref, shift_ref, w_ref, o_ref, *, col_chunk):
    u = _modulated_norm(x_ref[...], scale_ref[0], shift_ref[0]).astype(BF16)
    ncols = o_ref.shape[1]
    for c0 in range(0, ncols, col_chunk):
        c1 = min(c0 + col_chunk, ncols)
        o_ref[:, c0:c1] = jnp.dot(u, w_ref[:, c0:c1], preferred_element_type=F32)


def _inproj(x, scale, shift, w_bf, seq, tm):
    t, d = x.shape
    ncols = w_bf.shape[1]
    tiles_per_seq = seq // tm
    return pl.pallas_call(
        functools.partial(_inproj_kernel, col_chunk=1024),
        grid=(t // tm,),
        in_specs=[pl.BlockSpec((tm, d), lambda i: (i, 0)),
                  pl.BlockSpec((1, 1, d), lambda i: (i // tiles_per_seq, 0, 0)),
                  pl.BlockSpec((1, 1, d), lambda i: (i // tiles_per_seq, 0, 0)),
                  pl.BlockSpec(memory_space=pltpu.VMEM)],
        out_specs=pl.BlockSpec((tm, ncols), lambda i: (i, 0)),
        out_shape=jax.ShapeDtypeStruct((t, ncols), F32),
        compiler_params=_cparams(("parallel",)),
        name="inproj",
    )(x, scale, shift, w_bf)


def _fill_shifted(hbuf_ref, sh_ref, shifts, rows):
    for n, r in enumerate(shifts):
        sh_ref[n] = hbuf_ref[r:r + rows, :]


def _tap_plan(ktaps, halo):
    offs = [k + halo - ktaps // 2 for k in range(ktaps)]
    shifts = sorted({o % SUBLANES for o in offs} - {0})
    plan = [(shifts.index(o % SUBLANES) if o % SUBLANES else -1, (o // SUBLANES) * SUBLANES)
            for o in offs]
    return shifts, plan, max(o // SUBLANES for o in offs) * SUBLANES


def _conv_rows(hbuf_ref, sh_ref, w8_ref, plan, r0, rc):
    groups = rc // SUBLANES
    accs = [None] * groups
    for k, (n, q8) in enumerate(plan):
        w = w8_ref[k * SUBLANES:(k + 1) * SUBLANES, :]
        for j in range(groups):
            start = pl.multiple_of(r0 + q8 + j * SUBLANES, SUBLANES)
            rows = pl.ds(start, SUBLANES)
            src = hbuf_ref[rows, :] if n < 0 else sh_ref[n, rows, :]
            term = src * w
            accs[j] = term if accs[j] is None else accs[j] + term
    return jnp.concatenate(accs, axis=0)


def _cconv_kernel(a_ref, g_ref, ap_ref, gp_ref, an_ref, gn_ref, w_ref, b_ref, lg_ref, lb_ref,
                  o_ref, hbuf_ref, sh_ref, cbuf_ref, *, tiles_per_seq, halo, shifts, plan, rc,
                  ln_rows):
    i = pl.program_id(0)
    tm = a_ref.shape[0]
    first = (i % tiles_per_seq) == 0
    last = (i % tiles_per_seq) == tiles_per_seq - 1
    glu_p = ap_ref[...] * _sigmoid(gp_ref[...])
    glu_n = an_ref[...] * _sigmoid(gn_ref[...])
    hbuf_ref[0:halo, :] = jnp.where(first, 0.0, glu_p)
    hbuf_ref[halo:halo + tm, :] = a_ref[...] * _sigmoid(g_ref[...])
    hbuf_ref[halo + tm:2 * halo + tm, :] = jnp.where(last, 0.0, glu_n)
    _fill_shifted(hbuf_ref, sh_ref, shifts, sh_ref.shape[1])

    def body(j, carry):
        r0 = j * rc
        h = _conv_rows(hbuf_ref, sh_ref, w_ref, plan, r0, rc) + b_ref[...]
        cbuf_ref[pl.ds(pl.multiple_of(r0, rc), rc), :] = h
        return carry

    lax.fori_loop(0, tm // rc, body, 0)

    for c0 in range(0, tm, ln_rows):
        h = cbuf_ref[c0:c0 + ln_rows, :]
        mu = jnp.mean(h, axis=-1, keepdims=True)
        hc = h - mu
        var = jnp.mean(hc * hc, axis=-1, keepdims=True)
        y = hc * lax.rsqrt(var + EPS) * lg_ref[...] + lb_ref[...]
        o_ref[c0:c0 + ln_rows, :] = _silu(y).astype(o_ref.dtype)


def _cconv(proj, w, b, ln_g, ln_b, seq, tm):
    t = proj.shape[0]
    ch = w.shape[1]
    halo = 16
    rc = 16
    shifts, plan, qmax = _tap_plan(CONF_KERNEL, halo)
    tiles_per_seq = seq // tm
    hb = tm // halo
    nhb = t // halo
    cb = 1

    def prev_map(col):
        return lambda i: (jnp.maximum(i * hb - 1, 0), col)

    def next_map(col):
        return lambda i: (jnp.minimum((i + 1) * hb, nhb - 1), col)

    kern = functools.partial(_cconv_kernel, tiles_per_seq=tiles_per_seq, halo=halo,
                             shifts=shifts, plan=plan, rc=rc, ln_rows=64)
    return pl.pallas_call(
        kern,
        grid=(t // tm,),
        in_specs=[pl.BlockSpec((tm, ch), lambda i: (i, 0)),
                  pl.BlockSpec((tm, ch), lambda i: (i, cb)),
                  pl.BlockSpec((halo, ch), prev_map(0)),
                  pl.BlockSpec((halo, ch), prev_map(cb)),
                  pl.BlockSpec((halo, ch), next_map(0)),
                  pl.BlockSpec((halo, ch), next_map(cb)),
                  pl.BlockSpec((CONF_KERNEL * SUBLANES, ch), lambda i: (0, 0)),
                  pl.BlockSpec((1, ch), lambda i: (0, 0)),
                  pl.BlockSpec((1, ch), lambda i: (0, 0)),
                  pl.BlockSpec((1, ch), lambda i: (0, 0))],
        out_specs=pl.BlockSpec((tm, ch), lambda i: (i, 0)),
        out_shape=jax.ShapeDtypeStruct((t, ch), BF16),
        scratch_shapes=[pltpu.VMEM((tm + 2 * halo, ch), F32),
                        pltpu.VMEM((len(shifts), tm + qmax, ch), F32),
                        pltpu.VMEM((tm, ch), F32)],
        compiler_params=_cparams(("parallel",)),
        name="cconv",
    )(proj, proj, proj, proj, proj, proj, jnp.repeat(w, SUBLANES, axis=0), b.reshape(1, ch),
      ln_g.reshape(1, ch),
      ln_b.reshape(1, ch))


def _sconv_kernel(x_ref, xp_ref, xn_ref, w_ref, b_ref, o_ref, hbuf_ref, sh_ref, *,
                  tiles_per_seq, halo, shifts, plan, rc):
    i = pl.program_id(0)
    tm = x_ref.shape[0]
    first = (i % tiles_per_seq) == 0
    last = (i % tiles_per_seq) == tiles_per_seq - 1
    hbuf_ref[0:halo, :] = jnp.where(first, 0.0, xp_ref[...])
    hbuf_ref[halo:halo + tm, :] = x_ref[...]
    hbuf_ref[halo + tm:2 * halo + tm, :] = jnp.where(last, 0.0, xn_ref[...])
    _fill_shifted(hbuf_ref, sh_ref, shifts, sh_ref.shape[1])

    def body(j, carry):
        r0 = j * rc
        h = _conv_rows(hbuf_ref, sh_ref, w_ref, plan, r0, rc) + b_ref[...]
        o_ref[pl.ds(pl.multiple_of(r0, rc), rc), :] = _silu(h)
        return carry

    lax.fori_loop(0, tm // rc, body, 0)


def _sconv(proj, col0, w, b, seq, tm):
    t = proj.shape[0]
    ch = w.shape[1]
    cw = 1024
    halo = SUBLANES
    rc = 16
    shifts, plan, qmax = _tap_plan(SSM_CONV, halo)
    tiles_per_seq = seq // tm
    hb = tm // halo
    nhb = t // halo
    c0 = col0 // cw
    kern = functools.partial(_sconv_kernel, tiles_per_seq=tiles_per_seq, halo=halo,
                             shifts=shifts, plan=plan, rc=rc)
    return pl.pallas_call(
        kern,
        grid=(t // tm, ch // cw),
        in_specs=[pl.BlockSpec((tm, cw), lambda i, j: (i, c0 + j)),
                  pl.BlockSpec((halo, cw), lambda i, j: (jnp.maximum(i * hb - 1, 0), c0 + j)),
                  pl.BlockSpec((halo, cw),
                               lambda i, j: (jnp.minimum((i + 1) * hb, nhb - 1), c0 + j)),
                  pl.BlockSpec((SSM_CONV * SUBLANES, cw), lambda i, j: (0, j)),
                  pl.BlockSpec((1, cw), lambda i, j: (0, j))],
        out_specs=pl.BlockSpec((tm, cw), lambda i, j: (i, j)),
        out_shape=jax.ShapeDtypeStruct((t, ch), F32),
        scratch_shapes=[pltpu.VMEM((tm + 2 * halo, cw), F32),
                        pltpu.VMEM((len(shifts), tm + qmax, cw), F32)],
        compiler_params=_cparams(("parallel", "parallel")),
        name="sconv",
    )(proj, proj, proj, jnp.repeat(w, SUBLANES, axis=0), b.reshape(1, ch))


def _split3(f):
    hi = f.astype(BF16)
    r1 = f - hi.astype(F32)
    mid = r1.astype(BF16)
    lo = (r1 - mid.astype(F32)).astype(BF16)
    return hi, mid, lo


def _expand_heads(f, e):
    hi, mid, lo = _split3(f)
    return (jnp.dot(hi, e, preferred_element_type=F32)
            + jnp.dot(mid, e, preferred_element_type=F32)
            + jnp.dot(lo, e, preferred_element_type=F32))


def _ssd_kernel(x_ref, b_ref, c_ref, dt_ref, dtb_ref, alog_ref, e_ref, o_ref, state_ref, *,
                reverse, lane0):
    q = x_ref.shape[0]
    nheads = x_ref.shape[1] // SSM_HEAD_DIM
    hpg = nheads // SSM_GROUPS
    gw = hpg * SSM_HEAD_DIM

    @pl.when(pl.program_id(1) == 0)
    def _():
        state_ref[...] = jnp.zeros_like(state_ref)

    row = lax.broadcasted_iota(jnp.int32, (q, q), 0)
    col = lax.broadcasted_iota(jnp.int32, (q, q), 1)
    keep = (col >= row) if reverse else (col <= row)

    z = dt_ref[...] + dtb_ref[...]
    dt = jnp.maximum(z, 0.0) + jnp.log(1.0 + jnp.exp(-jnp.abs(z)))
    a = dt * (-jnp.exp(alog_ref[...]))
    acs = jnp.dot(keep.astype(F32), a, precision=HIGHEST, preferred_element_type=F32)
    tot = acs[0:1, :] if reverse else acs[q - 1:q, :]
    acs_t = acs.T
    dt_t = dt.T
    e = e_ref[...]
    x = x_ref[...]

    exp_acs = _expand_heads(jnp.exp(acs), e)
    wstate = _expand_heads(dt * jnp.exp(tot - acs), e)
    carry = _expand_heads(jnp.broadcast_to(jnp.exp(tot), (SUBLANES, LANES)), e)[0:1, :]
    wx = (wstate * x).astype(BF16)
    xb = x.astype(BF16)

    for g in range(SSM_GROUPS):
        bg = b_ref[:, g * SSM_STATE:(g + 1) * SSM_STATE].astype(BF16)
        cg = c_ref[:, g * SSM_STATE:(g + 1) * SSM_STATE].astype(BF16)
        cb = lax.dot_general(cg, bg, (((1,), (1,)), ((), ())), preferred_element_type=F32)
        gs = slice(g * gw, (g + 1) * gw)
        s_old = state_ref[:, gs]
        y_off = jnp.dot(cg, s_old.astype(BF16), preferred_element_type=F32) * exp_acs[:, gs]
        for r in range(hpg):
            h = g * hpg + r
            hl = lane0 + h
            seg = acs[:, hl:hl + 1] - acs_t[hl:hl + 1, :]
            lmat = jnp.exp(jnp.where(keep, seg, -jnp.inf))
            m = (cb * lmat * dt_t[hl:hl + 1, :]).astype(BF16)
            hs = slice(h * SSM_HEAD_DIM, (h + 1) * SSM_HEAD_DIM)
            y_h = jnp.dot(m, xb[:, hs], preferred_element_type=F32)
            o_ref[:, hs] = y_h + y_off[:, r * SSM_HEAD_DIM:(r + 1) * SSM_HEAD_DIM]
        upd = lax.dot_general(bg, wx[:, gs], (((0,), (0,)), ((), ())),
                              preferred_element_type=F32)
        state_ref[:, gs] = s_old * carry[:, gs] + upd


def _ssd(xbc, proj, dt_col_block, dt_bias, a_log, expand, bsz, seq, reverse, lane0):
    t = xbc.shape[0]
    q = SSM_CHUNK
    nc = seq // q
    inner = expand.shape[1]
    gn = SSM_GROUPS * SSM_STATE
    b_blk = inner // gn
    c_blk = b_blk + 1

    def tok(b, c):
        return b * nc + (nc - 1 - c if reverse else c)

    kern = functools.partial(_ssd_kernel, reverse=reverse, lane0=lane0)
    return pl.pallas_call(
        kern,
        grid=(bsz, nc),
        in_specs=[pl.BlockSpec((q, inner), lambda b, c: (tok(b, c), 0)),
                  pl.BlockSpec((q, gn), lambda b, c: (tok(b, c), b_blk)),
                  pl.BlockSpec((q, gn), lambda b, c: (tok(b, c), c_blk)),
                  pl.BlockSpec((q, LANES), lambda b, c: (tok(b, c), dt_col_block)),
                  pl.BlockSpec((1, LANES), lambda b, c: (0, 0)),
                  pl.BlockSpec((1, LANES), lambda b, c: (0, 0)),
                  pl.BlockSpec((LANES, inner), lambda b, c: (0, 0))],
        out_specs=pl.BlockSpec((q, inner), lambda b, c: (tok(b, c), 0)),
        out_shape=jax.ShapeDtypeStruct((t, inner), F32),
        scratch_shapes=[pltpu.VMEM((SSM_STATE, inner), F32)],
        compiler_params=_cparams(("arbitrary", "arbitrary")),
        name="ssd_bwd" if reverse else "ssd_fwd",
    )(xbc, xbc, xbc, proj, dt_bias, a_log, expand)


def _merge_kernel(yf_ref, yb_ref, xc_ref, z_ref, hc_ref, gc_ref, gs_ref, x_ref,
                  dexp_ref, ng_ref, wso_ref, wco_ref, bco_ref, wo_ref, g1_ref,
                  scale_ref, shift_ref, wr_ref, br_ref,
                  x1_ref, v_ref, lg_ref, *, ngroups, halves):
    tm, inner = yf_ref.shape
    gw = inner // ngroups
    hm = tm // halves
    for h in range(halves):
        r = slice(h * hm, (h + 1) * hm)
        y = (yf_ref[r, :] + yb_ref[r, :] + xc_ref[r, :] * dexp_ref[...]) * _silu(z_ref[r, :])
        parts = []
        for g in range(ngroups):
            yg = y[:, g * gw:(g + 1) * gw]
            ms = jnp.mean(yg * yg, axis=-1, keepdims=True)
            parts.append((yg * lax.rsqrt(ms + EPS)) * ng_ref[:, g * gw:(g + 1) * gw])
        ysn = jnp.concatenate(parts, axis=-1).astype(BF16)
        y_ssm = jnp.dot(ysn, wso_ref[...], preferred_element_type=F32)
        y_conf = jnp.dot(hc_ref[r, :], wco_ref[...], preferred_element_type=F32) + bco_ref[...]
        merged = _sigmoid(gc_ref[r, :]) * y_conf + _sigmoid(gs_ref[r, :]) * y_ssm
        o = jnp.dot(merged.astype(BF16), wo_ref[...], preferred_element_type=F32)
        x1 = x_ref[r, :] + g1_ref[0] * o
        x1_ref[r, :] = x1
        v = _modulated_norm(x1, scale_ref[0], shift_ref[0])
        _store_rows(v_ref, v, h * hm)
        v_hi = v.astype(BF16)
        v_lo = (v - v_hi.astype(F32)).astype(BF16)
        p = (jnp.dot(v_hi, wr_ref[...], preferred_element_type=F32)
             + jnp.dot(v_lo, wr_ref[...], preferred_element_type=F32))
        lg_ref[r, :] = p[:, :LANES] + p[:, LANES:] + br_ref[...]


def _merge(yf, yb, xbc, proj, hc, x, cols, dexp, norm_g, wso, wco, bco, wo, g1, scale, shift,
           wr, br, seq, tm):
    t, d = x.shape
    inner = yf.shape[1]
    tiles_per_seq = seq // tm
    z_blk = cols["z"] // inner
    gc_blk = cols["gate_conf"] // d
    gs_blk = cols["gate_ssm"] // d

    def const(shape):
        return pl.BlockSpec(shape, lambda i: tuple(0 for _ in shape),
                            pipeline_mode=pl.Buffered(1))

    def per_seq():
        return pl.BlockSpec((1, 1, d), lambda i: (i // tiles_per_seq, 0, 0))

    kern = functools.partial(_merge_kernel, ngroups=SSM_GROUPS, halves=2)
    return pl.pallas_call(
        kern,
        grid=(t // tm,),
        in_specs=[pl.BlockSpec((tm, inner), lambda i: (i, 0)),
                  pl.BlockSpec((tm, inner), lambda i: (i, 0)),
                  pl.BlockSpec((tm, inner), lambda i: (i, 0)),
                  pl.BlockSpec((tm, inner), lambda i: (i, z_blk)),
                  pl.BlockSpec((tm, d), lambda i: (i, 0)),
                  pl.BlockSpec((tm, d), lambda i: (i, gc_blk)),
                  pl.BlockSpec((tm, d), lambda i: (i, gs_blk)),
                  pl.BlockSpec((tm, d), lambda i: (i, 0)),
                  const((1, inner)), const((1, inner)),
                  const((inner, d)), const((d, d)), const((1, d)), const((d, d)),
                  per_seq(), per_seq(), per_seq(),
                  const((d, 2 * LANES)), const((1, LANES))],
        out_specs=[pl.BlockSpec((tm, d), lambda i: (i, 0)),
                   pl.BlockSpec((tm * d // LANES, LANES), lambda i: (i, 0)),
                   pl.BlockSpec((tm, LANES), lambda i: (i, 0))],
        out_shape=[jax.ShapeDtypeStruct((t, d), F32),
                   jax.ShapeDtypeStruct((t * d // LANES, LANES), F32),
                   jax.ShapeDtypeStruct((t, LANES), F32)],
        compiler_params=_cparams(("parallel",)),
        name="merge",
    )(yf, yb, xbc, proj, hc, proj, proj, x, dexp, norm_g, wso, wco, bco, wo, g1, scale, shift,
      wr, br)


def _route_kernel(lg_ref, idx_ref, w_ref, rank_ref, cnt_ref, run_ref, *, n_experts):
    @pl.when(pl.program_id(0) == 0)
    def _():
        run_ref[...] = jnp.zeros_like(run_ref)

    tm = lg_ref.shape[0]
    lane = lax.broadcasted_iota(jnp.int32, (tm, LANES), 1)
    lg = jnp.where(lane < n_experts, lg_ref[...], -jnp.inf)
    sel = jnp.zeros((tm, LANES), F32)
    vals, idxs = [], []
    for _ in range(TOP_K):
        m = jnp.max(lg, axis=-1, keepdims=True)
        ix = jnp.min(jnp.where(lg == m, lane, LANES), axis=-1, keepdims=True)
        hit = lane == ix
        sel = jnp.where(hit, 1.0, sel)
        lg = jnp.where(hit, -jnp.inf, lg)
        vals.append(m)
        idxs.append(ix)
    ex = [jnp.exp(v - vals[0]) for v in vals]
    den = ex[0] + ex[1] + ex[2] + ex[3]
    row = lax.broadcasted_iota(jnp.int32, (tm, tm), 0)
    col = lax.broadcasted_iota(jnp.int32, (tm, tm), 1)
    before = (col < row).astype(BF16)
    excl = jnp.dot(before, sel.astype(BF16), preferred_element_type=F32) + run_ref[...]
    idx_out = jnp.zeros((tm, LANES), jnp.int32)
    w_out = jnp.zeros((tm, LANES), F32)
    rank_out = jnp.zeros((tm, LANES), jnp.int32)
    for k in range(TOP_K):
        rk = jnp.sum(jnp.where(lane == idxs[k], excl, 0.0), axis=-1, keepdims=True)
        idx_out = jnp.where(lane == k, idxs[k], idx_out)
        w_out = jnp.where(lane == k, ex[k] / den, w_out)
        rank_out = jnp.where(lane == k, rk.astype(jnp.int32), rank_out)
    idx_ref[...] = idx_out
    w_ref[...] = w_out
    rank_ref[...] = rank_out
    run_ref[...] = run_ref[...] + jnp.sum(sel, axis=0, keepdims=True)
    cnt_ref[...] = jnp.broadcast_to(run_ref[...], cnt_ref.shape).astype(jnp.int32)


def _route(logits, n_experts, tm):
    t = logits.shape[0]
    kern = functools.partial(_route_kernel, n_experts=n_experts)
    return pl.pallas_call(
        kern,
        grid=(t // tm,),
        in_specs=[pl.BlockSpec((tm, LANES), lambda i: (i, 0))],
        out_specs=[pl.BlockSpec((tm, LANES), lambda i: (i, 0)),
                   pl.BlockSpec((tm, LANES), lambda i: (i, 0)),
                   pl.BlockSpec((tm, LANES), lambda i: (i, 0)),
                   pl.BlockSpec((SUBLANES, LANES), lambda i: (0, 0))],
        out_shape=[jax.ShapeDtypeStruct((t, LANES), jnp.int32),
                   jax.ShapeDtypeStruct((t, LANES), F32),
                   jax.ShapeDtypeStruct((t, LANES), jnp.int32),
                   jax.ShapeDtypeStruct((SUBLANES, LANES), jnp.int32)],
        scratch_shapes=[pltpu.VMEM((1, LANES), F32)],
        compiler_params=_cparams(("arbitrary",)),
        name="route",
    )(logits)


def _dest_kernel(idx_ref, rank_ref, pstart_ref, o_ref):
    tm = idx_ref.shape[0]
    lane = lax.broadcasted_iota(jnp.int32, (tm, LANES), 1)
    idx = idx_ref[...]
    out = jnp.zeros((tm, LANES), jnp.int32)
    for k in range(TOP_K):
        base = jnp.sum(jnp.where(lane == idx[:, k:k + 1], pstart_ref[...], 0), axis=-1,
                       keepdims=True)
        out = jnp.where(lane == k, base + rank_ref[:, k:k + 1], out)
    o_ref[...] = out


def _dest(top_i, rank, pad_start_row, tm):
    t = top_i.shape[0]
    return pl.pallas_call(
        _dest_kernel,
        grid=(t // tm,),
        in_specs=[pl.BlockSpec((tm, LANES), lambda i: (i, 0)),
                  pl.BlockSpec((tm, LANES), lambda i: (i, 0)),
                  pl.BlockSpec((1, LANES), lambda i: (0, 0))],
        out_specs=pl.BlockSpec((tm, LANES), lambda i: (i, 0)),
        out_shape=jax.ShapeDtypeStruct((t, LANES), jnp.int32),
        compiler_params=_cparams(("parallel",)),
        name="dest",
    )(top_i, rank, pad_start_row)


def _row_copy(src, src_row, dst, dst_row, sem, nch):
    return pltpu.make_async_copy(_row(src, src_row, nch), _row(dst, dst_row, nch), sem)


def _dispatch_kernel(dest_ref, fill_ref, end_ref, v_ref, xs_hbm, sem, fsem, *, n_experts, nch):
    i = pl.program_id(0)
    tm = v_ref.shape[0] // nch

    @pl.when(i == 0)
    def _():
        for e in range(n_experts):
            def fill(s, c):
                _row_copy(v_ref, 0, xs_hbm, s, fsem, nch).start()
                return c

            def drain(s, c):
                _row_copy(v_ref, 0, xs_hbm, s, fsem, nch).wait()
                return c

            lax.fori_loop(fill_ref[e], end_ref[e], fill, 0)
            lax.fori_loop(fill_ref[e], end_ref[e], drain, 0)

    t0 = i * tm

    def issue(r, c):
        for k in range(TOP_K):
            _row_copy(v_ref, r, xs_hbm, dest_ref[(t0 + r) * TOP_K + k], sem, nch).start()
        return c

    def drain(r, c):
        for k in range(TOP_K):
            _row_copy(v_ref, 0, xs_hbm, 0, sem, nch).wait()
        return c

    lax.fori_loop(0, tm, issue, 0)
    lax.fori_loop(0, tm, drain, 0)


def _dispatch(dest_flat, fill_start, pad_end, v, nch, n_slots, tm):
    t = v.shape[0] // nch
    n_experts = fill_start.shape[0]
    kern = functools.partial(_dispatch_kernel, n_experts=n_experts, nch=nch)
    grid_spec = pltpu.PrefetchScalarGridSpec(
        num_scalar_prefetch=3,
        grid=(t // tm,),
        in_specs=[pl.BlockSpec((tm * nch, LANES), lambda i, dst, fl, en: (i, 0))],
        out_specs=pl.BlockSpec(memory_space=pl.ANY),
        scratch_shapes=[pltpu.SemaphoreType.DMA, pltpu.SemaphoreType.DMA],
    )
    return pl.pallas_call(
        kern,
        grid_spec=grid_spec,
        out_shape=jax.ShapeDtypeStruct((n_slots * nch, LANES), v.dtype),
        compiler_params=_cparams(("arbitrary",)),
        name="dispatch",
    )(dest_flat, fill_start, pad_end, v)


def _moe_kernel(be_ref, nused_ref, x_ref, wgu_ref, bgu_ref, wd_ref, bd_ref, o_ref,
                wgu_bf, wd_bf):
    b = pl.program_id(0)
    changed = jnp.logical_or(b == 0, be_ref[b] != be_ref[jnp.maximum(b - 1, 0)])
    active = b < nused_ref[0]

    @pl.when(jnp.logical_and(active, changed))
    def _():
        wgu_bf[...] = wgu_ref[0].astype(BF16)
        wd_bf[...] = wd_ref[0].astype(BF16)

    @pl.when(active)
    def _():
        ff = wd_bf.shape[0]
        d = wgu_bf.shape[0]
        x = _load_rows(x_ref, x_ref.shape[0] * LANES // d, d // LANES).astype(BF16)
        gu = jnp.dot(x, wgu_bf[...], preferred_element_type=F32) + bgu_ref[0]
        gate = jnp.minimum(gu[:, :ff], SWIGLU_LIMIT)
        up = jnp.clip(gu[:, ff:], -SWIGLU_LIMIT, SWIGLU_LIMIT)
        glu = gate * _sigmoid(SWIGLU_ALPHA * gate)
        hid = ((up + 1.0) * glu).astype(BF16)
        _store_rows(o_ref, jnp.dot(hid, wd_bf[...], preferred_element_type=F32) + bd_ref[0])


def _moe(block_expert, nused, xs, w_gu, b_gu, w_down, b_down):
    ne, d, ff2 = w_gu.shape
    nch = d // LANES
    n_slots = xs.shape[0] // nch
    ff = w_down.shape[1]
    bm = MOE_BLOCK

    def blk(b, nu):
        return jnp.minimum(b, nu[0] - 1)

    grid_spec = pltpu.PrefetchScalarGridSpec(
        num_scalar_prefetch=2,
        grid=(n_slots // bm,),
        in_specs=[pl.BlockSpec((bm * nch, LANES), lambda b, be, nu: (blk(b, nu), 0)),
                  pl.BlockSpec((1, d, ff2), lambda b, be, nu: (be[blk(b, nu)], 0, 0)),
                  pl.BlockSpec((1, 1, ff2), lambda b, be, nu: (be[blk(b, nu)], 0, 0)),
                  pl.BlockSpec((1, ff, d), lambda b, be, nu: (be[blk(b, nu)], 0, 0)),
                  pl.BlockSpec((1, 1, d), lambda b, be, nu: (be[blk(b, nu)], 0, 0))],
        out_specs=pl.BlockSpec((bm * nch, LANES), lambda b, be, nu: (blk(b, nu), 0)),
        scratch_shapes=[pltpu.VMEM((d, ff2), BF16), pltpu.VMEM((ff, d), BF16)],
    )
    return pl.pallas_call(
        _moe_kernel,
        grid_spec=grid_spec,
        out_shape=jax.ShapeDtypeStruct(xs.shape, F32),
        compiler_params=_cparams(("arbitrary",)),
        name="moe",
    )(block_expert, nused, xs, w_gu, b_gu.reshape(ne, 1, ff2), w_down, b_down.reshape(ne, 1, d))


def _final_kernel(dest_ref, x1_ref, w_ref, g2_ref, scale_ref, shift_ref, ys_hbm, o_ref,
                  gbuf, sems, *, nch):
    i = pl.program_id(0)
    n = pl.num_programs(0)
    tm = x1_ref.shape[0]
    slot = i % 2

    def issue(tile, slot_):
        def body(r, c):
            for k in range(TOP_K):
                src = dest_ref[(tile * tm + r) * TOP_K + k]
                pltpu.make_async_copy(_row(ys_hbm, src, nch),
                                      _row(gbuf, k * tm + r, nch, (slot_,)),
                                      sems.at[slot_]).start()
            return c
        lax.fori_loop(0, tm, body, 0)

    @pl.when(i == 0)
    def _():
        issue(0, 0)

    @pl.when(i + 1 < n)
    def _():
        issue(i + 1, 1 - slot)

    def drain(r, c):
        for k in range(TOP_K):
            pltpu.make_async_copy(_row(ys_hbm, 0, nch), _row(gbuf, 0, nch, (slot,)),
                                  sems.at[slot]).wait()
        return c
    lax.fori_loop(0, tm, drain, 0)

    w = w_ref[...]
    f = _load_rows(gbuf, tm, nch, 0, (slot,)) * w[:, 0:1]
    for k in range(1, TOP_K):
        f = f + _load_rows(gbuf, tm, nch, k * tm, (slot,)) * w[:, k:k + 1]
    x2 = x1_ref[...] + g2_ref[0] * f
    o_ref[...] = _modulated_norm(x2, scale_ref[0], shift_ref[0])


def _final(dest_flat, x1, ys, top_w, g2, scale, shift, seq, tm):
    t, d = x1.shape
    nch = d // LANES
    tiles_per_seq = seq // tm

    def per_seq():
        return pl.BlockSpec((1, 1, d), lambda i, dst: (i // tiles_per_seq, 0, 0))

    grid_spec = pltpu.PrefetchScalarGridSpec(
        num_scalar_prefetch=1,
        grid=(t // tm,),
        in_specs=[pl.BlockSpec((tm, d), lambda i, dst: (i, 0)),
                  pl.BlockSpec((tm, LANES), lambda i, dst: (i, 0)),
                  per_seq(), per_seq(), per_seq(),
                  pl.BlockSpec(memory_space=pl.ANY)],
        out_specs=pl.BlockSpec((tm, d), lambda i, dst: (i, 0)),
        scratch_shapes=[pltpu.VMEM((2, TOP_K * tm * nch, LANES), F32),
                        pltpu.SemaphoreType.DMA((2,))],
    )
    return pl.pallas_call(
        functools.partial(_final_kernel, nch=nch),
        grid_spec=grid_spec,
        out_shape=jax.ShapeDtypeStruct((t, d), F32),
        compiler_params=_cparams(("arbitrary",)),
        name="final",
    )(dest_flat, x1, top_w, g2, scale, shift, ys)


def _pad_cols(w, n):
    return jnp.pad(w, ((0, 0), (0, n - w.shape[1])))


def kernel(x, c, ada_w, ada_b, norm_mix_g, w_in, conf_dw_w, conf_dw_b, conf_ln_g, conf_ln_b,
           conf_out_w, conf_out_b, ssm_conv_w, ssm_conv_b, dt_bias_f, dt_bias_b, a_log_f,
           a_log_b, ssm_d, ssm_norm_g, ssm_out_w, w_o, norm_ffn_g, router_w, router_b, w_gu,
           b_gu, w_down, b_down, final_ada_w, final_ada_b, final_norm_g):
    bsz, seq, d = x.shape
    depth = ada_w.shape[0]
    t = bsz * seq
    nheads = a_log_f.shape[1]
    inner = nheads * SSM_HEAD_DIM
    gn = SSM_GROUPS * SSM_STATE
    conf = conf_dw_w.shape[2]
    n_experts = router_w.shape[2]
    assert 2 * nheads <= LANES and n_experts <= LANES

    c_pad = jnp.zeros((SUBLANES, d), F32).at[:bsz].set(c)
    fin = _ada(c_pad, final_ada_w, final_ada_b)[:bsz]
    xf = x.reshape(t, d)

    sizes = [("conf_a", conf), ("conf_g", conf), ("z", inner), ("xs", inner), ("bm", gn),
             ("cm", gn), ("gate_conf", d), ("gate_ssm", d), ("dt", LANES)]
    cols, off = {}, 0
    for name, n in sizes:
        cols[name] = off
        off += n
    src = {}
    o = 0
    for name, n in [("conf_a", conf), ("conf_g", conf), ("z", inner), ("xs", inner), ("bm", gn),
                    ("cm", gn), ("dtf", nheads), ("dtb", nheads), ("gate_conf", d),
                    ("gate_ssm", d)]:
        src[name] = (o, o + n)
        o += n

    head_of_col = jnp.arange(inner, dtype=jnp.int32) // SSM_HEAD_DIM
    lanes = jnp.arange(LANES, dtype=jnp.int32)[:, None]
    expand_f = (lanes == head_of_col[None, :]).astype(BF16)
    expand_b = (lanes == head_of_col[None, :] + nheads).astype(BF16)

    for l in range(depth):
        ada = _ada(c_pad, ada_w[l], ada_b[l])[:bsz]
        sh1, sc1, g1, sh2, sc2, g2 = [a.reshape(bsz, 1, d) for a in jnp.split(ada, 6, axis=-1)]
        scale1 = norm_mix_g[l][None, None, :] * (1.0 + sc1)
        scale2 = norm_ffn_g[l][None, None, :] * (1.0 + sc2)

        wl = w_in[l]
        w_dt = _pad_cols(jnp.concatenate([wl[:, slice(*src["dtf"])], wl[:, slice(*src["dtb"])]],
                                         axis=1), LANES)
        w_perm = jnp.concatenate(
            [wl[:, slice(*src[n])] for n in ("conf_a", "conf_g", "z", "xs", "bm", "cm",
                                             "gate_conf", "gate_ssm")] + [w_dt],
            axis=1).astype(BF16)
        proj = _inproj(xf, scale1, sh1, w_perm, seq, tm=256)

        hc = _cconv(proj, conf_dw_w[l], conf_dw_b[l], conf_ln_g[l], conf_ln_b[l], seq, tm=256)
        xbc = _sconv(proj, cols["xs"], ssm_conv_w[l], ssm_conv_b[l], seq, tm=256)

        dt_bias = _pad_cols(jnp.concatenate([dt_bias_f[l], dt_bias_b[l]])[None, :], LANES)
        a_log = _pad_cols(jnp.concatenate([a_log_f[l], a_log_b[l]])[None, :], LANES)
        dt_blk = cols["dt"] // LANES
        y_f = _ssd(xbc, proj, dt_blk, dt_bias, a_log, expand_f, bsz, seq, False, 0)
        y_b = _ssd(xbc, proj, dt_blk, dt_bias, a_log, expand_b, bsz, seq, True, nheads)

        dexp = jnp.repeat(ssm_d[l], SSM_HEAD_DIM)[None, :]
        wr = _pad_cols(router_w[l], LANES)
        wr_hi = wr.astype(BF16)
        wr = jnp.concatenate([wr_hi, (wr - wr_hi.astype(F32)).astype(BF16)], axis=1)
        br = _pad_cols(router_b[l][None, :], LANES)
        x1, v, logits = _merge(
            y_f, y_b, xbc, proj, hc, xf, cols, dexp, ssm_norm_g[l][None, :],
            ssm_out_w[l].astype(BF16), conf_out_w[l].astype(BF16), conf_out_b[l][None, :],
            w_o[l].astype(BF16), g1, scale2, sh2, wr, br, seq, tm=256)

        top_i, top_w, rank, counts = _route(logits, n_experts, tm=512)
        counts = counts[0, :n_experts]
        bm = MOE_BLOCK
        n_blocks = (t * TOP_K) // bm + n_experts
        padded = ((counts + bm - 1) // bm) * bm
        pad_end = jnp.cumsum(padded)
        pad_start = pad_end - padded
        dest = _dest(top_i, rank, _pad_cols(pad_start[None, :], LANES), tm=512)
        dest_flat = dest[:, :TOP_K].reshape(-1)
        block_first = jnp.arange(n_blocks, dtype=jnp.int32) * bm
        block_expert = jnp.minimum(
            jnp.sum((pad_end[None, :] <= block_first[:, None]).astype(jnp.int32), axis=1),
            n_experts - 1).astype(jnp.int32)
        nused = (pad_end[-1] // bm).astype(jnp.int32).reshape(1)
        xs = _dispatch(dest_flat, (pad_start + counts).astype(jnp.int32),
                       pad_end.astype(jnp.int32), v, d // LANES, n_blocks * bm, tm=256)
        ys = _moe(block_expert, nused, xs, w_gu[l], b_gu[l], w_down[l], b_down[l])

        if l == depth - 1:
            sh_f, sc_f = [a.reshape(bsz, 1, d) for a in jnp.split(fin, 2, axis=-1)]
            scale_f = final_norm_g[None, None, :] * (1.0 + sc_f)
            xf = _final(dest_flat, x1, ys, top_w, g2, scale_f, sh_f, seq, tm=256)
        else:
            raise NotImplementedError("depth > 1 is not wired")
    return xf.reshape(bsz, seq, d)
```

```python
import functools

import jax
import jax.numpy as jnp
from jax import lax
from jax.experimental import pallas as pl
from jax.experimental.pallas import tpu as pltpu

F32 = jnp.float32
BF16 = jnp.bfloat16
HIGHEST = lax.Precision.HIGHEST

EPS = 1e-6
CONF_KERNEL = 31
SSM_CONV = 5
SSM_HEAD_DIM = 64
SSM_GROUPS = 4
SSM_STATE = 128
SSM_CHUNK = 128
TOP_K = 4
SWIGLU_ALPHA = 1.702
SWIGLU_LIMIT = 7.0

LANES = 128
SUBLANES = 8
VMEM_LIMIT = 56 * 1024 * 1024

MOE_BLOCK = 512


def _cparams(sem):
    return pltpu.CompilerParams(dimension_semantics=sem, vmem_limit_bytes=VMEM_LIMIT)


def _store_rows(ref, val, row0=0):
    n, d = val.shape
    nch = d // LANES
    for s in range(nch):
        ref[pl.ds(row0 * nch + s, n, stride=nch), :] = val[:, s * LANES:(s + 1) * LANES]


def _load_rows(ref, n, nch, row0=0, lead=()):
    return jnp.concatenate(
        [ref[lead + (pl.ds(row0 * nch + s, n, stride=nch), slice(None))] for s in range(nch)],
        axis=-1)


def _row(ref, r, nch, lead=()):
    return ref.at[lead + (pl.ds(pl.multiple_of(r * nch, nch), nch), slice(None))]


def _sigmoid(x):
    return 1.0 / (1.0 + jnp.exp(-x))


def _silu(x):
    return x * _sigmoid(x)


def _ada_kernel(c_ref, w_ref, b_ref, o_ref):
    c = c_ref[...]
    o_ref[...] = jnp.dot(_silu(c), w_ref[...], precision=HIGHEST,
                         preferred_element_type=F32) + b_ref[...]


def _ada(c_pad, w, b):
    d, n = w.shape
    tn = 1024
    return pl.pallas_call(
        _ada_kernel,
        grid=(n // tn,),
        in_specs=[pl.BlockSpec((SUBLANES, d), lambda j: (0, 0)),
                  pl.BlockSpec((d, tn), lambda j: (0, j)),
                  pl.BlockSpec((1, tn), lambda j: (0, j))],
        out_specs=pl.BlockSpec((SUBLANES, tn), lambda j: (0, j)),
        out_shape=jax.ShapeDtypeStruct((SUBLANES, n), F32),
        compiler_params=_cparams(("parallel",)),
        name="ada",
    )(c_pad, w, b.reshape(1, n))


def _modulated_norm(x, scale, shift):
    ms = jnp.mean(x * x, axis=-1, keepdims=True)
    return (x * lax.rsqrt(ms + EPS)) * scale + shift


def _inproj_kernel(x_ref, scale_ref, shift_ref, w_ref, o_ref, *, col_chunk):
    u = _modulated_norm(x_ref[...], scale_ref[0], shift_ref[0]).astype(BF16)
    ncols = o_ref.shape[1]
    for c0 in range(0, ncols, col_chunk):
        c1 = min(c0 + col_chunk, ncols)
        o_ref[:, c0:c1] = jnp.dot(u, w_ref[:, c0:c1], preferred_element_type=F32)


def _inproj(x, scale, shift, w_bf, seq, tm, col_splits):
    t, d = x.shape
    ncols = w_bf.shape[1]
    nw = ncols // col_splits
    tiles_per_seq = seq // tm
    return pl.pallas_call(
        functools.partial(_inproj_kernel, col_chunk=1024),
        grid=(col_splits, t // tm),
        in_specs=[pl.BlockSpec((tm, d), lambda j, i: (i, 0)),
                  pl.BlockSpec((1, 1, d), lambda j, i: (i // tiles_per_seq, 0, 0)),
                  pl.BlockSpec((1, 1, d), lambda j, i: (i // tiles_per_seq, 0, 0)),
                  pl.BlockSpec((d, nw), lambda j, i: (0, j), pipeline_mode=pl.Buffered(1))],
        out_specs=pl.BlockSpec((tm, nw), lambda j, i: (i, j)),
        out_shape=jax.ShapeDtypeStruct((t, ncols), F32),
        compiler_params=_cparams(("parallel", "parallel")),
        name="inproj",
    )(x, scale, shift, w_bf)


def _fill_shifted(hbuf_ref, sh_ref, shifts, rows):
    for n, r in enumerate(shifts):
        sh_ref[n] = hbuf_ref[r:r + rows, :]


def _tap_plan(ktaps, halo):
    offs = [k + halo - ktaps // 2 for k in range(ktaps)]
    shifts = sorted({o % SUBLANES for o in offs} - {0})
    plan = [(shifts.index(o % SUBLANES) if o % SUBLANES else -1, (o // SUBLANES) * SUBLANES)
            for o in offs]
    return shifts, plan, max(o // SUBLANES for o in offs) * SUBLANES


def _conv_rows(hbuf_ref, sh_ref, w8_ref, plan, r0, rc):
    groups = rc // SUBLANES
    accs = [None] * groups
    for k, (n, q8) in enumerate(plan):
        w = w8_ref[k * SUBLANES:(k + 1) * SUBLANES, :]
        for j in range(groups):
            start = pl.multiple_of(r0 + q8 + j * SUBLANES, SUBLANES)
            rows = pl.ds(start, SUBLANES)
            src = hbuf_ref[rows, :] if n < 0 else sh_ref[n, rows, :]
            term = src * w
            accs[j] = term if accs[j] is None else accs[j] + term
    return jnp.concatenate(accs, axis=0)


def _cconv_kernel(a_ref, g_ref, ap_ref, gp_ref, an_ref, gn_ref, w_ref, b_ref, lg_ref, lb_ref,
                  o_ref, hbuf_ref, sh_ref, cbuf_ref, *, tiles_per_seq, halo, shifts, plan, rc,
                  ln_rows):
    i = pl.program_id(0)
    tm = a_ref.shape[0]
    first = (i % tiles_per_seq) == 0
    last = (i % tiles_per_seq) == tiles_per_seq - 1
    glu_p = ap_ref[...] * _sigmoid(gp_ref[...])
    glu_n = an_ref[...] * _sigmoid(gn_ref[...])
    hbuf_ref[0:halo, :] = jnp.where(first, 0.0, glu_p)
    hbuf_ref[halo:halo + tm, :] = a_ref[...] * _sigmoid(g_ref[...])
    hbuf_ref[halo + tm:2 * halo + tm, :] = jnp.where(last, 0.0, glu_n)
    _fill_shifted(hbuf_ref, sh_ref, shifts, sh_ref.shape[1])

    def body(j, carry):
        r0 = j * rc
        h = _conv_rows(hbuf_ref, sh_ref, w_ref, plan, r0, rc) + b_ref[...]
        cbuf_ref[pl.ds(pl.multiple_of(r0, rc), rc), :] = h
        return carry

    lax.fori_loop(0, tm // rc, body, 0)

    for c0 in range(0, tm, ln_rows):
        h = cbuf_ref[c0:c0 + ln_rows, :]
        mu = jnp.mean(h, axis=-1, keepdims=True)
        hc = h - mu
        var = jnp.mean(hc * hc, axis=-1, keepdims=True)
        y = hc * lax.rsqrt(var + EPS) * lg_ref[...] + lb_ref[...]
        o_ref[c0:c0 + ln_rows, :] = _silu(y).astype(o_ref.dtype)


def _cconv(proj, w, b, ln_g, ln_b, seq, tm):
    t = proj.shape[0]
    ch = w.shape[1]
    halo = 16
    rc = 16
    shifts, plan, qmax = _tap_plan(CONF_KERNEL, halo)
    tiles_per_seq = seq // tm
    hb = tm // halo
    nhb = t // halo
    cb = 1

    def prev_map(col):
        return lambda i: (jnp.maximum(i * hb - 1, 0), col)

    def next_map(col):
        return lambda i: (jnp.minimum((i + 1) * hb, nhb - 1), col)

    kern = functools.partial(_cconv_kernel, tiles_per_seq=tiles_per_seq, halo=halo,
                             shifts=shifts, plan=plan, rc=rc, ln_rows=64)
    return pl.pallas_call(
        kern,
        grid=(t // tm,),
        in_specs=[pl.BlockSpec((tm, ch), lambda i: (i, 0)),
                  pl.BlockSpec((tm, ch), lambda i: (i, cb)),
                  pl.BlockSpec((halo, ch), prev_map(0)),
                  pl.BlockSpec((halo, ch), prev_map(cb)),
                  pl.BlockSpec((halo, ch), next_map(0)),
                  pl.BlockSpec((halo, ch), next_map(cb)),
                  pl.BlockSpec((CONF_KERNEL * SUBLANES, ch), lambda i: (0, 0)),
                  pl.BlockSpec((1, ch), lambda i: (0, 0)),
                  pl.BlockSpec((1, ch), lambda i: (0, 0)),
                  pl.BlockSpec((1, ch), lambda i: (0, 0))],
        out_specs=pl.BlockSpec((tm, ch), lambda i: (i, 0)),
        out_shape=jax.ShapeDtypeStruct((t, ch), BF16),
        scratch_shapes=[pltpu.VMEM((tm + 2 * halo, ch), F32),
                        pltpu.VMEM((len(shifts), tm + qmax, ch), F32),
                        pltpu.VMEM((tm, ch), F32)],
        compiler_params=_cparams(("parallel",)),
        name="cconv",
    )(proj, proj, proj, proj, proj, proj, jnp.repeat(w, SUBLANES, axis=0), b.reshape(1, ch),
      ln_g.reshape(1, ch),
      ln_b.reshape(1, ch))


def _sconv_kernel(x_ref, xp_ref, xn_ref, w_ref, b_ref, o_ref, hbuf_ref, sh_ref, *,
                  tiles_per_seq, halo, shifts, plan, rc):
    i = pl.program_id(0)
    tm = x_ref.shape[0]
    first = (i % tiles_per_seq) == 0
    last = (i % tiles_per_seq) == tiles_per_seq - 1
    hbuf_ref[0:halo, :] = jnp.where(first, 0.0, xp_ref[...])
    hbuf_ref[halo:halo + tm, :] = x_ref[...]
    hbuf_ref[halo + tm:2 * halo + tm, :] = jnp.where(last, 0.0, xn_ref[...])
    _fill_shifted(hbuf_ref, sh_ref, shifts, sh_ref.shape[1])

    def body(j, carry):
        r0 = j * rc
        h = _conv_rows(hbuf_ref, sh_ref, w_ref, plan, r0, rc) + b_ref[...]
        o_ref[pl.ds(pl.multiple_of(r0, rc), rc), :] = _silu(h)
        return carry

    lax.fori_loop(0, tm // rc, body, 0)


def _sconv(proj, col0, w, b, seq, tm):
    t = proj.shape[0]
    ch = w.shape[1]
    cw = 1024
    halo = SUBLANES
    rc = 16
    shifts, plan, qmax = _tap_plan(SSM_CONV, halo)
    tiles_per_seq = seq // tm
    hb = tm // halo
    nhb = t // halo
    c0 = col0 // cw
    kern = functools.partial(_sconv_kernel, tiles_per_seq=tiles_per_seq, halo=halo,
                             shifts=shifts, plan=plan, rc=rc)
    return pl.pallas_call(
        kern,
        grid=(t // tm, ch // cw),
        in_specs=[pl.BlockSpec((tm, cw), lambda i, j: (i, c0 + j)),
                  pl.BlockSpec((halo, cw), lambda i, j: (jnp.maximum(i * hb - 1, 0), c0 + j)),
                  pl.BlockSpec((halo, cw),
                               lambda i, j: (jnp.minimum((i + 1) * hb, nhb - 1), c0 + j)),
                  pl.BlockSpec((SSM_CONV * SUBLANES, cw), lambda i, j: (0, j)),
                  pl.BlockSpec((1, cw), lambda i, j: (0, j))],
        out_specs=pl.BlockSpec((tm, cw), lambda i, j: (i, j)),
        out_shape=jax.ShapeDtypeStruct((t, ch), F32),
        scratch_shapes=[pltpu.VMEM((tm + 2 * halo, cw), F32),
                        pltpu.VMEM((len(shifts), tm + qmax, cw), F32)],
        compiler_params=_cparams(("parallel", "parallel")),
        name="sconv",
    )(proj, proj, proj, jnp.repeat(w, SUBLANES, axis=0), b.reshape(1, ch))


def _split3(f):
    hi = f.astype(BF16)
    r1 = f - hi.astype(F32)
    mid = r1.astype(BF16)
    lo = (r1 - mid.astype(F32)).astype(BF16)
    return hi, mid, lo


def _expand_heads(f, e):
    hi, mid, lo = _split3(f)
    return (jnp.dot(hi, e, preferred_element_type=F32)
            + jnp.dot(mid, e, preferred_element_type=F32)
            + jnp.dot(lo, e, preferred_element_type=F32))


def _ssd_kernel(x_ref, b_ref, c_ref, dt_ref, dtb_ref, alog_ref, e_ref, o_ref, state_ref,
                acs_s, acst_s, dtt_s, wstt_s, carry_s, cb_s, bgt_s, *, reverse, lane0):
    q = x_ref.shape[0]
    nheads = x_ref.shape[1] // SSM_HEAD_DIM
    hpg = nheads // SSM_GROUPS
    gw = hpg * SSM_HEAD_DIM

    @pl.when(pl.program_id(1) == 0)
    def _():
        state_ref[...] = jnp.zeros_like(state_ref)

    row = lax.broadcasted_iota(jnp.int32, (q, q), 0)
    col = lax.broadcasted_iota(jnp.int32, (q, q), 1)
    keep = (col >= row) if reverse else (col <= row)

    z = dt_ref[...] + dtb_ref[...]
    dt = jnp.maximum(z, 0.0) + jnp.log(1.0 + jnp.exp(-jnp.abs(z)))
    a = dt * (-jnp.exp(alog_ref[...]))
    acs = jnp.dot(keep.astype(F32), a, precision=HIGHEST, preferred_element_type=F32)
    tot = acs[0:1, :] if reverse else acs[q - 1:q, :]
    acs_s[...] = acs
    acst_s[...] = acs.T
    dtt_s[...] = dt.T
    wstt_s[...] = (dt * jnp.exp(tot - acs)).T
    carry_s[...] = _expand_heads(jnp.broadcast_to(jnp.exp(tot), (SUBLANES, LANES)), e_ref[...])

    for g in range(SSM_GROUPS):
        gsl = slice(g * SSM_STATE, (g + 1) * SSM_STATE)
        cb_s[...] = lax.dot_general(c_ref[:, gsl].astype(BF16), b_ref[:, gsl].astype(BF16),
                                    (((1,), (1,)), ((), ())), preferred_element_type=F32)
        bgt_s[...] = b_ref[:, gsl].T
        for pr in range(hpg // 2):
            ps = slice((g * hpg + 2 * pr) * SSM_HEAD_DIM, (g * hpg + 2 * pr + 2) * SSM_HEAD_DIM)
            xp = x_ref[:, ps].astype(BF16)
            rhs = jnp.concatenate([xp, state_ref[:, ps].astype(BF16)], axis=0)
            for sub in range(2):
                hl = lane0 + g * hpg + 2 * pr + sub
                half = slice(sub * SSM_HEAD_DIM, (sub + 1) * SSM_HEAD_DIM)
                hs = slice(ps.start + half.start, ps.start + half.stop)
                colb = jnp.broadcast_to(acs_s[:, hl:hl + 1], (q, q))
                seg = colb - acst_s[hl:hl + 1, :]
                lmat = jnp.exp(jnp.where(keep, seg, -jnp.inf))
                m = (cb_s[...] * lmat * dtt_s[hl:hl + 1, :]).astype(BF16)
                cexp = (c_ref[:, gsl] * jnp.exp(colb)).astype(BF16)
                y = jnp.dot(jnp.concatenate([m, cexp], axis=1), rhs,
                            preferred_element_type=F32)
                o_ref[:, hs] = y[:, half]
                bw = (bgt_s[...] * wstt_s[hl:hl + 1, :]).astype(BF16)
                upd = jnp.dot(bw, xp, preferred_element_type=F32)
                state_ref[:, hs] = state_ref[:, hs] * carry_s[0:1, hs] + upd[:, half]


def _ssd(xbc, proj, dt_col_block, dt_bias, a_log, expand, bsz, seq, reverse, lane0):
    t = xbc.shape[0]
    q = SSM_CHUNK
    nc = seq // q
    inner = expand.shape[1]
    gn = SSM_GROUPS * SSM_STATE
    b_blk = inner // gn
    c_blk = b_blk + 1

    def tok(b, c):
        return b * nc + (nc - 1 - c if reverse else c)

    kern = functools.partial(_ssd_kernel, reverse=reverse, lane0=lane0)
    return pl.pallas_call(
        kern,
        grid=(bsz, nc),
        in_specs=[pl.BlockSpec((q, inner), lambda b, c: (tok(b, c), 0)),
                  pl.BlockSpec((q, gn), lambda b, c: (tok(b, c), b_blk)),
                  pl.BlockSpec((q, gn), lambda b, c: (tok(b, c), c_blk)),
                  pl.BlockSpec((q, LANES), lambda b, c: (tok(b, c), dt_col_block)),
                  pl.BlockSpec((1, LANES), lambda b, c: (0, 0)),
                  pl.BlockSpec((1, LANES), lambda b, c: (0, 0)),
                  pl.BlockSpec((LANES, inner), lambda b, c: (0, 0))],
        out_specs=pl.BlockSpec((q, inner), lambda b, c: (tok(b, c), 0)),
        out_shape=jax.ShapeDtypeStruct((t, inner), F32),
        scratch_shapes=[pltpu.VMEM((SSM_STATE, inner), F32),
                        pltpu.VMEM((q, LANES), F32), pltpu.VMEM((LANES, q), F32),
                        pltpu.VMEM((LANES, q), F32), pltpu.VMEM((LANES, q), F32),
                        pltpu.VMEM((SUBLANES, inner), F32),
                        pltpu.VMEM((q, q), F32), pltpu.VMEM((SSM_STATE, q), F32)],
        compiler_params=_cparams(("arbitrary", "arbitrary")),
        name="ssd_bwd" if reverse else "ssd_fwd",
    )(xbc, xbc, xbc, proj, dt_bias, a_log, expand)


def _merge_kernel(yf_ref, yb_ref, xc_ref, z_ref, hc_ref, gc_ref, gs_ref, x_ref,
                  dexp_ref, ng_ref, wso_ref, wco_ref, bco_ref, wo_ref, g1_ref,
                  scale_ref, shift_ref, wr_ref, br_ref,
                  x1_ref, v_ref, lg_ref, *, ngroups, halves):
    tm, inner = yf_ref.shape
    gw = inner // ngroups
    hm = tm // halves
    for h in range(halves):
        r = slice(h * hm, (h + 1) * hm)
        y = (yf_ref[r, :] + yb_ref[r, :] + xc_ref[r, :] * dexp_ref[...]) * _silu(z_ref[r, :])
        parts = []
        for g in range(ngroups):
            yg = y[:, g * gw:(g + 1) * gw]
            ms = jnp.mean(yg * yg, axis=-1, keepdims=True)
            parts.append((yg * lax.rsqrt(ms + EPS)) * ng_ref[:, g * gw:(g + 1) * gw])
        ysn = jnp.concatenate(parts, axis=-1).astype(BF16)
        y_ssm = jnp.dot(ysn, wso_ref[...], preferred_element_type=F32)
        y_conf = jnp.dot(hc_ref[r, :], wco_ref[...], preferred_element_type=F32) + bco_ref[...]
        merged = _sigmoid(gc_ref[r, :]) * y_conf + _sigmoid(gs_ref[r, :]) * y_ssm
        o = jnp.dot(merged.astype(BF16), wo_ref[...], preferred_element_type=F32)
        x1 = x_ref[r, :] + g1_ref[0] * o
        x1_ref[r, :] = x1
        v = _modulated_norm(x1, scale_ref[0], shift_ref[0])
        _store_rows(v_ref, v, h * hm)
        v_hi = v.astype(BF16)
        v_lo = (v - v_hi.astype(F32)).astype(BF16)
        p = (jnp.dot(v_hi, wr_ref[...], preferred_element_type=F32)
             + jnp.dot(v_lo, wr_ref[...], preferred_element_type=F32))
        lg_ref[r, :] = p[:, :LANES] + p[:, LANES:] + br_ref[...]


def _merge(yf, yb, xbc, proj, hc, x, cols, dexp, norm_g, wso, wco, bco, wo, g1, scale, shift,
           wr, br, seq, tm):
    t, d = x.shape
    inner = yf.shape[1]
    tiles_per_seq = seq // tm
    z_blk = cols["z"] // inner
    gc_blk = cols["gate_conf"] // d
    gs_blk = cols["gate_ssm"] // d

    def const(shape):
        return pl.BlockSpec(shape, lambda i: tuple(0 for _ in shape),
                            pipeline_mode=pl.Buffered(1))

    def per_seq():
        return pl.BlockSpec((1, 1, d), lambda i: (i // tiles_per_seq, 0, 0))

    kern = functools.partial(_merge_kernel, ngroups=SSM_GROUPS, halves=2)
    return pl.pallas_call(
        kern,
        grid=(t // tm,),
        in_specs=[pl.BlockSpec((tm, inner), lambda i: (i, 0)),
                  pl.BlockSpec((tm, inner), lambda i: (i, 0)),
                  pl.BlockSpec((tm, inner), lambda i: (i, 0)),
                  pl.BlockSpec((tm, inner), lambda i: (i, z_blk)),
                  pl.BlockSpec((tm, d), lambda i: (i, 0)),
                  pl.BlockSpec((tm, d), lambda i: (i, gc_blk)),
                  pl.BlockSpec((tm, d), lambda i: (i, gs_blk)),
                  pl.BlockSpec((tm, d), lambda i: (i, 0)),
                  const((1, inner)), const((1, inner)),
                  const((inner, d)), const((d, d)), const((1, d)), const((d, d)),
                  per_seq(), per_seq(), per_seq(),
                  const((d, 2 * LANES)), const((1, LANES))],
        out_specs=[pl.BlockSpec((tm, d), lambda i: (i, 0)),
                   pl.BlockSpec((tm * d // LANES, LANES), lambda i: (i, 0)),
                   pl.BlockSpec((tm, LANES), lambda i: (i, 0))],
        out_shape=[jax.ShapeDtypeStruct((t, d), F32),
                   jax.ShapeDtypeStruct((t * d // LANES, LANES), F32),
                   jax.ShapeDtypeStruct((t, LANES), F32)],
        compiler_params=_cparams(("parallel",)),
        name="merge",
    )(yf, yb, xbc, proj, hc, proj, proj, x, dexp, norm_g, wso, wco, bco, wo, g1, scale, shift,
      wr, br)


def _route_kernel(lg_ref, idx_ref, w_ref, rank_ref, cnt_ref, run_ref, *, n_experts):
    @pl.when(pl.program_id(0) == 0)
    def _():
        run_ref[...] = jnp.zeros_like(run_ref)

    tm = lg_ref.shape[0]
    lane = lax.broadcasted_iota(jnp.int32, (tm, LANES), 1)
    lg = jnp.where(lane < n_experts, lg_ref[...], -jnp.inf)
    sel = jnp.zeros((tm, LANES), F32)
    vals, idxs = [], []
    for _ in range(TOP_K):
        m = jnp.max(lg, axis=-1, keepdims=True)
        ix = jnp.min(jnp.where(lg == m, lane, LANES), axis=-1, keepdims=True)
        hit = lane == ix
        sel = jnp.where(hit, 1.0, sel)
        lg = jnp.where(hit, -jnp.inf, lg)
        vals.append(m)
        idxs.append(ix)
    ex = [jnp.exp(v - vals[0]) for v in vals]
    den = ex[0] + ex[1] + ex[2] + ex[3]
    row = lax.broadcasted_iota(jnp.int32, (tm, tm), 0)
    col = lax.broadcasted_iota(jnp.int32, (tm, tm), 1)
    before = (col < row).astype(BF16)
    excl = jnp.dot(before, sel.astype(BF16), preferred_element_type=F32) + run_ref[...]
    idx_out = jnp.zeros((tm, LANES), jnp.int32)
    w_out = jnp.zeros((tm, LANES), F32)
    rank_out = jnp.zeros((tm, LANES), jnp.int32)
    for k in range(TOP_K):
        rk = jnp.sum(jnp.where(lane == idxs[k], excl, 0.0), axis=-1, keepdims=True)
        idx_out = jnp.where(lane == k, idxs[k], idx_out)
        w_out = jnp.where(lane == k, ex[k] / den, w_out)
        rank_out = jnp.where(lane == k, rk.astype(jnp.int32), rank_out)
    idx_ref[...] = idx_out
    w_ref[...] = w_out
    rank_ref[...] = rank_out
    run_ref[...] = run_ref[...] + jnp.sum(sel, axis=0, keepdims=True)
    cnt_ref[...] = jnp.broadcast_to(run_ref[...], cnt_ref.shape).astype(jnp.int32)


def _route(logits, n_experts, tm):
    t = logits.shape[0]
    kern = functools.partial(_route_kernel, n_experts=n_experts)
    return pl.pallas_call(
        kern,
        grid=(t // tm,),
        in_specs=[pl.BlockSpec((tm, LANES), lambda i: (i, 0))],
        out_specs=[pl.BlockSpec((tm, LANES), lambda i: (i, 0)),
                   pl.BlockSpec((tm, LANES), lambda i: (i, 0)),
                   pl.BlockSpec((tm, LANES), lambda i: (i, 0)),
                   pl.BlockSpec((SUBLANES, LANES), lambda i: (0, 0))],
        out_shape=[jax.ShapeDtypeStruct((t, LANES), jnp.int32),
                   jax.ShapeDtypeStruct((t, LANES), F32),
                   jax.ShapeDtypeStruct((t, LANES), jnp.int32),
                   jax.ShapeDtypeStruct((SUBLANES, LANES), jnp.int32)],
        scratch_shapes=[pltpu.VMEM((1, LANES), F32)],
        compiler_params=_cparams(("arbitrary",)),
        name="route",
    )(logits)


def _dest_kernel(idx_ref, rank_ref, pstart_ref, o_ref):
    tm = idx_ref.shape[0]
    lane = lax.broadcasted_iota(jnp.int32, (tm, LANES), 1)
    idx = idx_ref[...]
    out = jnp.zeros((tm, LANES), jnp.int32)
    for k in range(TOP_K):
        base = jnp.sum(jnp.where(lane == idx[:, k:k + 1], pstart_ref[...], 0), axis=-1,
                       keepdims=True)
        out = jnp.where(lane == k, base + rank_ref[:, k:k + 1], out)
    o_ref[...] = out


def _dest(top_i, rank, pad_start_row, tm):
    t = top_i.shape[0]
    return pl.pallas_call(
        _dest_kernel,
        grid=(t // tm,),
        in_specs=[pl.BlockSpec((tm, LANES), lambda i: (i, 0)),
                  pl.BlockSpec((tm, LANES), lambda i: (i, 0)),
                  pl.BlockSpec((1, LANES), lambda i: (0, 0))],
        out_specs=pl.BlockSpec((tm, LANES), lambda i: (i, 0)),
        out_shape=jax.ShapeDtypeStruct((t, LANES), jnp.int32),
        compiler_params=_cparams(("parallel",)),
        name="dest",
    )(top_i, rank, pad_start_row)


def _row_copy(src, src_row, dst, dst_row, sem, nch):
    return pltpu.make_async_copy(_row(src, src_row, nch), _row(dst, dst_row, nch), sem)


def _dispatch_kernel(dest_ref, fill_ref, end_ref, v_ref, xs_hbm, sem, fsem, *, n_experts, nch):
    i = pl.program_id(0)
    tm = v_ref.shape[0] // nch

    @pl.when(i == 0)
    def _():
        for e in range(n_experts):
            def fill(s, c):
                _row_copy(v_ref, 0, xs_hbm, s, fsem, nch).start()
                return c

            def drain(s, c):
                _row_copy(v_ref, 0, xs_hbm, s, fsem, nch).wait()
                return c

            lax.fori_loop(fill_ref[e], end_ref[e], fill, 0)
            lax.fori_loop(fill_ref[e], end_ref[e], drain, 0)

    t0 = i * tm

    def issue(r, c):
        for k in range(TOP_K):
            _row_copy(v_ref, r, xs_hbm, dest_ref[(t0 + r) * TOP_K + k], sem, nch).start()
        return c

    def drain(r, c):
        for k in range(TOP_K):
            _row_copy(v_ref, 0, xs_hbm, 0, sem, nch).wait()
        return c

    lax.fori_loop(0, tm, issue, 0)
    lax.fori_loop(0, tm, drain, 0)


def _dispatch(dest_flat, fill_start, pad_end, v, nch, n_slots, tm):
    t = v.shape[0] // nch
    n_experts = fill_start.shape[0]
    kern = functools.partial(_dispatch_kernel, n_experts=n_experts, nch=nch)
    grid_spec = pltpu.PrefetchScalarGridSpec(
        num_scalar_prefetch=3,
        grid=(t // tm,),
        in_specs=[pl.BlockSpec((tm * nch, LANES), lambda i, dst, fl, en: (i, 0))],
        out_specs=pl.BlockSpec(memory_space=pl.ANY),
        scratch_shapes=[pltpu.SemaphoreType.DMA, pltpu.SemaphoreType.DMA],
    )
    return pl.pallas_call(
        kern,
        grid_spec=grid_spec,
        out_shape=jax.ShapeDtypeStruct((n_slots * nch, LANES), v.dtype),
        compiler_params=_cparams(("arbitrary",)),
        name="dispatch",
    )(dest_flat, fill_start, pad_end, v)


def _moe_kernel(be_ref, nused_ref, x_ref, wgu_ref, bgu_ref, wd_ref, bd_ref, o_ref,
                wgu_bf, wd_bf):
    b = pl.program_id(0)
    changed = jnp.logical_or(b == 0, be_ref[b] != be_ref[jnp.maximum(b - 1, 0)])
    active = b < nused_ref[0]

    @pl.when(jnp.logical_and(active, changed))
    def _():
        wgu_bf[...] = wgu_ref[0].astype(BF16)
        wd_bf[...] = wd_ref[0].astype(BF16)

    @pl.when(active)
    def _():
        ff = wd_bf.shape[0]
        d = wgu_bf.shape[0]
        x = _load_rows(x_ref, x_ref.shape[0] * LANES // d, d // LANES).astype(BF16)
        gu = jnp.dot(x, wgu_bf[...], preferred_element_type=F32) + bgu_ref[0]
        gate = jnp.minimum(gu[:, :ff], SWIGLU_LIMIT)
        up = jnp.clip(gu[:, ff:], -SWIGLU_LIMIT, SWIGLU_LIMIT)
        glu = gate * _sigmoid(SWIGLU_ALPHA * gate)
        hid = ((up + 1.0) * glu).astype(BF16)
        _store_rows(o_ref, jnp.dot(hid, wd_bf[...], preferred_element_type=F32) + bd_ref[0])


def _moe(block_expert, nused, xs, w_gu, b_gu, w_down, b_down):
    ne, d, ff2 = w_gu.shape
    nch = d // LANES
    n_slots = xs.shape[0] // nch
    ff = w_down.shape[1]
    bm = MOE_BLOCK

    def blk(b, nu):
        return jnp.minimum(b, nu[0] - 1)

    grid_spec = pltpu.PrefetchScalarGridSpec(
        num_scalar_prefetch=2,
        grid=(n_slots // bm,),
        in_specs=[pl.BlockSpec((bm * nch, LANES), lambda b, be, nu: (blk(b, nu), 0)),
                  pl.BlockSpec((1, d, ff2), lambda b, be, nu: (be[blk(b, nu)], 0, 0)),
                  pl.BlockSpec((1, 1, ff2), lambda b, be, nu: (be[blk(b, nu)], 0, 0)),
                  pl.BlockSpec((1, ff, d), lambda b, be, nu: (be[blk(b, nu)], 0, 0)),
                  pl.BlockSpec((1, 1, d), lambda b, be, nu: (be[blk(b, nu)], 0, 0))],
        out_specs=pl.BlockSpec((bm * nch, LANES), lambda b, be, nu: (blk(b, nu), 0)),
        scratch_shapes=[pltpu.VMEM((d, ff2), BF16), pltpu.VMEM((ff, d), BF16)],
    )
    return pl.pallas_call(
        _moe_kernel,
        grid_spec=grid_spec,
        out_shape=jax.ShapeDtypeStruct(xs.shape, F32),
        compiler_params=_cparams(("arbitrary",)),
        name="moe",
    )(block_expert, nused, xs, w_gu, b_gu.reshape(ne, 1, ff2), w_down, b_down.reshape(ne, 1, d))


def _final_kernel(dest_ref, x1_ref, w_ref, g2_ref, scale_ref, shift_ref, ys_hbm, o_ref,
                  gbuf, sems, *, nch):
    i = pl.program_id(0)
    n = pl.num_programs(0)
    tm = x1_ref.shape[0]
    slot = i % 2

    def issue(tile, slot_):
        def body(r, c):
            for k in range(TOP_K):
                src = dest_ref[(tile * tm + r) * TOP_K + k]
                pltpu.make_async_copy(_row(ys_hbm, src, nch),
                                      _row(gbuf, k * tm + r, nch, (slot_,)),
                                      sems.at[slot_]).start()
            return c
        lax.fori_loop(0, tm, body, 0)

    @pl.when(i == 0)
    def _():
        issue(0, 0)

    @pl.when(i + 1 < n)
    def _():
        issue(i + 1, 1 - slot)

    def drain(r, c):
        for k in range(TOP_K):
            pltpu.make_async_copy(_row(ys_hbm, 0, nch), _row(gbuf, 0, nch, (slot,)),
                                  sems.at[slot]).wait()
        return c
    lax.fori_loop(0, tm, drain, 0)

    w = w_ref[...]
    f = _load_rows(gbuf, tm, nch, 0, (slot,)) * w[:, 0:1]
    for k in range(1, TOP_K):
        f = f + _load_rows(gbuf, tm, nch, k * tm, (slot,)) * w[:, k:k + 1]
    x2 = x1_ref[...] + g2_ref[0] * f
    o_ref[...] = _modulated_norm(x2, scale_ref[0], shift_ref[0])


def _final(dest_flat, x1, ys, top_w, g2, scale, shift, seq, tm):
    t, d = x1.shape
    nch = d // LANES
    tiles_per_seq = seq // tm

    def per_seq():
        return pl.BlockSpec((1, 1, d), lambda i, dst: (i // tiles_per_seq, 0, 0))

    grid_spec = pltpu.PrefetchScalarGridSpec(
        num_scalar_prefetch=1,
        grid=(t // tm,),
        in_specs=[pl.BlockSpec((tm, d), lambda i, dst: (i, 0)),
                  pl.BlockSpec((tm, LANES), lambda i, dst: (i, 0)),
                  per_seq(), per_seq(), per_seq(),
                  pl.BlockSpec(memory_space=pl.ANY)],
        out_specs=pl.BlockSpec((tm, d), lambda i, dst: (i, 0)),
        scratch_shapes=[pltpu.VMEM((2, TOP_K * tm * nch, LANES), F32),
                        pltpu.SemaphoreType.DMA((2,))],
    )
    return pl.pallas_call(
        functools.partial(_final_kernel, nch=nch),
        grid_spec=grid_spec,
        out_shape=jax.ShapeDtypeStruct((t, d), F32),
        compiler_params=_cparams(("arbitrary",)),
        name="final",
    )(dest_flat, x1, top_w, g2, scale, shift, ys)


def _pad_cols(w, n):
    return jnp.pad(w, ((0, 0), (0, n - w.shape[1])))


def kernel(x, c, ada_w, ada_b, norm_mix_g, w_in, conf_dw_w, conf_dw_b, conf_ln_g, conf_ln_b,
           conf_out_w, conf_out_b, ssm_conv_w, ssm_conv_b, dt_bias_f, dt_bias_b, a_log_f,
           a_log_b, ssm_d, ssm_norm_g, ssm_out_w, w_o, norm_ffn_g, router_w, router_b, w_gu,
           b_gu, w_down, b_down, final_ada_w, final_ada_b, final_norm_g):
    bsz, seq, d = x.shape
    depth = ada_w.shape[0]
    t = bsz * seq
    nheads = a_log_f.shape[1]
    inner = nheads * SSM_HEAD_DIM
    gn = SSM_GROUPS * SSM_STATE
    conf = conf_dw_w.shape[2]
    n_experts = router_w.shape[2]
    assert 2 * nheads <= LANES and n_experts <= LANES

    c_pad = jnp.zeros((SUBLANES, d), F32).at[:bsz].set(c)
    fin = _ada(c_pad, final_ada_w, final_ada_b)[:bsz]
    xf = x.reshape(t, d)

    sizes = [("conf_a", conf), ("conf_g", conf), ("z", inner), ("xs", inner), ("bm", gn),
             ("cm", gn), ("gate_conf", d), ("gate_ssm", d), ("dt", LANES)]
    cols, off = {}, 0
    for name, n in sizes:
        cols[name] = off
        off += n
    src = {}
    o = 0
    for name, n in [("conf_a", conf), ("conf_g", conf), ("z", inner), ("xs", inner), ("bm", gn),
                    ("cm", gn), ("dtf", nheads), ("dtb", nheads), ("gate_conf", d),
                    ("gate_ssm", d)]:
        src[name] = (o, o + n)
        o += n

    head_of_col = jnp.arange(inner, dtype=jnp.int32) // SSM_HEAD_DIM
    lanes = jnp.arange(LANES, dtype=jnp.int32)[:, None]
    expand_f = (lanes == head_of_col[None, :]).astype(BF16)
    expand_b = (lanes == head_of_col[None, :] + nheads).astype(BF16)

    for l in range(depth):
        ada = _ada(c_pad, ada_w[l], ada_b[l])[:bsz]
        sh1, sc1, g1, sh2, sc2, g2 = [a.reshape(bsz, 1, d) for a in jnp.split(ada, 6, axis=-1)]
        scale1 = norm_mix_g[l][None, None, :] * (1.0 + sc1)
        scale2 = norm_ffn_g[l][None, None, :] * (1.0 + sc2)

        wl = w_in[l]
        w_dt = _pad_cols(jnp.concatenate([wl[:, slice(*src["dtf"])], wl[:, slice(*src["dtb"])]],
                                         axis=1), LANES)
        w_perm = jnp.concatenate(
            [wl[:, slice(*src[n])] for n in ("conf_a", "conf_g", "z", "xs", "bm", "cm",
                                             "gate_conf", "gate_ssm")] + [w_dt],
            axis=1).astype(BF16)
        col_splits = 2
        slab = -(-w_perm.shape[1] // (col_splits * LANES)) * LANES
        proj = _inproj(xf, scale1, sh1, _pad_cols(w_perm, col_splits * slab), seq, tm=512,
                       col_splits=col_splits)

        hc = _cconv(proj, conf_dw_w[l], conf_dw_b[l], conf_ln_g[l], conf_ln_b[l], seq, tm=256)
        xbc = _sconv(proj, cols["xs"], ssm_conv_w[l], ssm_conv_b[l], seq, tm=256)

        dt_bias = _pad_cols(jnp.concatenate([dt_bias_f[l], dt_bias_b[l]])[None, :], LANES)
        a_log = _pad_cols(jnp.concatenate([a_log_f[l], a_log_b[l]])[None, :], LANES)
        dt_blk = cols["dt"] // LANES
        y_f = _ssd(xbc, proj, dt_blk, dt_bias, a_log, expand_f, bsz, seq, False, 0)
        y_b = _ssd(xbc, proj, dt_blk, dt_bias, a_log, expand_b, bsz, seq, True, nheads)

        dexp = jnp.repeat(ssm_d[l], SSM_HEAD_DIM)[None, :]
        wr = _pad_cols(router_w[l], LANES)
        wr_hi = wr.astype(BF16)
        wr = jnp.concatenate([wr_hi, (wr - wr_hi.astype(F32)).astype(BF16)], axis=1)
        br = _pad_cols(router_b[l][None, :], LANES)
        x1, v, logits = _merge(
            y_f, y_b, xbc, proj, hc, xf, cols, dexp, ssm_norm_g[l][None, :],
            ssm_out_w[l].astype(BF16), conf_out_w[l].astype(BF16), conf_out_b[l][None, :],
            w_o[l].astype(BF16), g1, scale2, sh2, wr, br, seq, tm=256)

        top_i, top_w, rank, counts = _route(logits, n_experts, tm=512)
        counts = counts[0, :n_experts]
        bm = MOE_BLOCK
        n_blocks = (t * TOP_K) // bm + n_experts
        padded = ((counts + bm - 1) // bm) * bm
        pad_end = jnp.cumsum(padded)
        pad_start = pad_end - padded
        dest = _dest(top_i, rank, _pad_cols(pad_start[None, :], LANES), tm=512)
        dest_flat = dest[:, :TOP_K].reshape(-1)
        block_first = jnp.arange(n_blocks, dtype=jnp.int32) * bm
        block_expert = jnp.minimum(
            jnp.sum((pad_end[None, :] <= block_first[:, None]).astype(jnp.int32), axis=1),
            n_experts - 1).astype(jnp.int32)
        nused = (pad_end[-1] // bm).astype(jnp.int32).reshape(1)
        xs = _dispatch(dest_flat, (pad_start + counts).astype(jnp.int32),
                       pad_end.astype(jnp.int32), v, d // LANES, n_blocks * bm, tm=256)
        ys = _moe(block_expert, nused, xs, w_gu[l], b_gu[l], w_down[l], b_down[l])

        if l == depth - 1:
            sh_f, sc_f = [a.reshape(bsz, 1, d) for a in jnp.split(fin, 2, axis=-1)]
            scale_f = final_norm_g[None, None, :] * (1.0 + sc_f)
            xf = _final(dest_flat, x1, ys, top_w, g2, scale_f, sh_f, seq, tm=256)
        else:
            raise NotImplementedError("depth > 1 is not wired")
    return xf.reshape(bsz, seq, d)
```

```python
import functools

import jax
import jax.numpy as jnp
from jax import lax
from jax.experimental import pallas as pl
from jax.experimental.pallas import tpu as pltpu

F32 = jnp.float32
BF16 = jnp.bfloat16
HIGHEST = lax.Precision.HIGHEST

EPS = 1e-6
LOG2E = 1.4426950408889634
CONF_KERNEL = 31
SSM_CONV = 5
SSM_HEAD_DIM = 64
SSM_GROUPS = 4
SSM_STATE = 128
SSM_CHUNK = 128
TOP_K = 4
SWIGLU_ALPHA = 1.702
SWIGLU_LIMIT = 7.0

LANES = 128
SUBLANES = 8
VMEM_LIMIT = 56 * 1024 * 1024

MOE_BLOCK = 512
MOE_HALVES = 2


def _cparams(sem):
    return pltpu.CompilerParams(dimension_semantics=sem, vmem_limit_bytes=VMEM_LIMIT)


def _store_rows(ref, val, row0=0):
    n, d = val.shape
    nch = d // LANES
    for s in range(nch):
        ref[pl.ds(row0 * nch + s, n, stride=nch), :] = val[:, s * LANES:(s + 1) * LANES]


def _load_rows(ref, n, nch, row0=0, lead=()):
    return jnp.concatenate(
        [ref[lead + (pl.ds(row0 * nch + s, n, stride=nch), slice(None))] for s in range(nch)],
        axis=-1)


def _row(ref, r, nch, lead=()):
    return ref.at[lead + (pl.ds(pl.multiple_of(r * nch, nch), nch), slice(None))]


def _sigmoid(x):
    return 1.0 / (1.0 + jnp.exp(-x))


def _silu(x):
    return x * _sigmoid(x)


def _ada_kernel(c_ref, w_ref, b_ref, o_ref):
    c = c_ref[...]
    o_ref[...] = jnp.dot(_silu(c), w_ref[...], precision=HIGHEST,
                         preferred_element_type=F32) + b_ref[...]


def _ada(c_pad, w, b):
    d, n = w.shape
    tn = 1024
    return pl.pallas_call(
        _ada_kernel,
        grid=(n // tn,),
        in_specs=[pl.BlockSpec((SUBLANES, d), lambda j: (0, 0)),
                  pl.BlockSpec((d, tn), lambda j: (0, j)),
                  pl.BlockSpec((1, tn), lambda j: (0, j))],
        out_specs=pl.BlockSpec((SUBLANES, tn), lambda j: (0, j)),
        out_shape=jax.ShapeDtypeStruct((SUBLANES, n), F32),
        compiler_params=_cparams(("parallel",)),
        name="ada",
    )(c_pad, w, b.reshape(1, n))


def _modulated_norm(x, scale, shift):
    ms = jnp.mean(x * x, axis=-1, keepdims=True)
    return (x * lax.rsqrt(ms + EPS)) * scale + shift


def _inproj_kernel(x_ref, scale_ref, shift_ref, w_ref, o_ref, *, col_chunk):
    u = _modulated_norm(x_ref[...], scale_ref[0], shift_ref[0]).astype(BF16)
    ncols = o_ref.shape[1]
    for c0 in range(0, ncols, col_chunk):
        c1 = min(c0 + col_chunk, ncols)
        o_ref[:, c0:c1] = jnp.dot(u, w_ref[:, c0:c1], preferred_element_type=F32)


def _inproj(x, scale, shift, w_bf, seq, tm, col_splits):
    t, d = x.shape
    ncols = w_bf.shape[1]
    nw = ncols // col_splits
    tiles_per_seq = seq // tm
    return pl.pallas_call(
        functools.partial(_inproj_kernel, col_chunk=1024),
        grid=(col_splits, t // tm),
        in_specs=[pl.BlockSpec((tm, d), lambda j, i: (i, 0)),
                  pl.BlockSpec((1, 1, d), lambda j, i: (i // tiles_per_seq, 0, 0)),
                  pl.BlockSpec((1, 1, d), lambda j, i: (i // tiles_per_seq, 0, 0)),
                  pl.BlockSpec((d, nw), lambda j, i: (0, j), pipeline_mode=pl.Buffered(1))],
        out_specs=pl.BlockSpec((tm, nw), lambda j, i: (i, j)),
        out_shape=jax.ShapeDtypeStruct((t, ncols), F32),
        compiler_params=_cparams(("parallel", "parallel")),
        name="inproj",
    )(x, scale, shift, w_bf)


def _fill_shifted(hbuf_ref, sh_ref, shifts, rows):
    for n, r in enumerate(shifts):
        sh_ref[n] = hbuf_ref[r:r + rows, :]


def _tap_plan(ktaps, halo):
    offs = [k + halo - ktaps // 2 for k in range(ktaps)]
    shifts = sorted({o % SUBLANES for o in offs} - {0})
    plan = [(shifts.index(o % SUBLANES) if o % SUBLANES else -1, (o // SUBLANES) * SUBLANES)
            for o in offs]
    return shifts, plan, max(o // SUBLANES for o in offs) * SUBLANES


def _conv_rows(hbuf_ref, sh_ref, w8_ref, plan, r0, rc):
    groups = rc // SUBLANES
    accs = [None] * groups
    for k, (n, q8) in enumerate(plan):
        w = w8_ref[k * SUBLANES:(k + 1) * SUBLANES, :]
        for j in range(groups):
            start = pl.multiple_of(r0 + q8 + j * SUBLANES, SUBLANES)
            rows = pl.ds(start, SUBLANES)
            src = hbuf_ref[rows, :] if n < 0 else sh_ref[n, rows, :]
            term = src * w
            accs[j] = term if accs[j] is None else accs[j] + term
    return jnp.concatenate(accs, axis=0)


def _cconv_kernel(a_ref, g_ref, ap_ref, gp_ref, an_ref, gn_ref, w_ref, b_ref, lg_ref, lb_ref,
                  o_ref, hbuf_ref, sh_ref, cbuf_ref, *, tiles_per_seq, halo, shifts, plan, rc,
                  ln_rows):
    i = pl.program_id(0)
    tm = a_ref.shape[0]
    first = (i % tiles_per_seq) == 0
    last = (i % tiles_per_seq) == tiles_per_seq - 1
    glu_p = ap_ref[...] * _sigmoid(gp_ref[...])
    glu_n = an_ref[...] * _sigmoid(gn_ref[...])
    hbuf_ref[0:halo, :] = jnp.where(first, 0.0, glu_p)
    hbuf_ref[halo:halo + tm, :] = a_ref[...] * _sigmoid(g_ref[...])
    hbuf_ref[halo + tm:2 * halo + tm, :] = jnp.where(last, 0.0, glu_n)
    _fill_shifted(hbuf_ref, sh_ref, shifts, sh_ref.shape[1])

    def body(j, carry):
        r0 = j * rc
        h = _conv_rows(hbuf_ref, sh_ref, w_ref, plan, r0, rc) + b_ref[...]
        cbuf_ref[pl.ds(pl.multiple_of(r0, rc), rc), :] = h
        return carry

    lax.fori_loop(0, tm // rc, body, 0)

    for c0 in range(0, tm, ln_rows):
        h = cbuf_ref[c0:c0 + ln_rows, :]
        mu = jnp.mean(h, axis=-1, keepdims=True)
        hc = h - mu
        var = jnp.mean(hc * hc, axis=-1, keepdims=True)
        y = hc * lax.rsqrt(var + EPS) * lg_ref[...] + lb_ref[...]
        o_ref[c0:c0 + ln_rows, :] = _silu(y).astype(o_ref.dtype)


def _cconv(proj, w, b, ln_g, ln_b, seq, tm):
    t = proj.shape[0]
    ch = w.shape[1]
    halo = 16
    rc = 16
    shifts, plan, qmax = _tap_plan(CONF_KERNEL, halo)
    tiles_per_seq = seq // tm
    hb = tm // halo
    nhb = t // halo
    cb = 1

    def prev_map(col):
        return lambda i: (jnp.maximum(i * hb - 1, 0), col)

    def next_map(col):
        return lambda i: (jnp.minimum((i + 1) * hb, nhb - 1), col)

    kern = functools.partial(_cconv_kernel, tiles_per_seq=tiles_per_seq, halo=halo,
                             shifts=shifts, plan=plan, rc=rc, ln_rows=64)
    return pl.pallas_call(
        kern,
        grid=(t // tm,),
        in_specs=[pl.BlockSpec((tm, ch), lambda i: (i, 0)),
                  pl.BlockSpec((tm, ch), lambda i: (i, cb)),
                  pl.BlockSpec((halo, ch), prev_map(0)),
                  pl.BlockSpec((halo, ch), prev_map(cb)),
                  pl.BlockSpec((halo, ch), next_map(0)),
                  pl.BlockSpec((halo, ch), next_map(cb)),
                  pl.BlockSpec((CONF_KERNEL * SUBLANES, ch), lambda i: (0, 0)),
                  pl.BlockSpec((1, ch), lambda i: (0, 0)),
                  pl.BlockSpec((1, ch), lambda i: (0, 0)),
                  pl.BlockSpec((1, ch), lambda i: (0, 0))],
        out_specs=pl.BlockSpec((tm, ch), lambda i: (i, 0)),
        out_shape=jax.ShapeDtypeStruct((t, ch), BF16),
        scratch_shapes=[pltpu.VMEM((tm + 2 * halo, ch), F32),
                        pltpu.VMEM((len(shifts), tm + qmax, ch), F32),
                        pltpu.VMEM((tm, ch), F32)],
        compiler_params=_cparams(("parallel",)),
        name="cconv",
    )(proj, proj, proj, proj, proj, proj, jnp.repeat(w, SUBLANES, axis=0), b.reshape(1, ch),
      ln_g.reshape(1, ch),
      ln_b.reshape(1, ch))


def _sconv_kernel(x_ref, xp_ref, xn_ref, w_ref, b_ref, o_ref, hbuf_ref, sh_ref, *,
                  tiles_per_seq, halo, shifts, plan, rc):
    i = pl.program_id(0)
    tm = x_ref.shape[0]
    first = (i % tiles_per_seq) == 0
    last = (i % tiles_per_seq) == tiles_per_seq - 1
    hbuf_ref[0:halo, :] = jnp.where(first, 0.0, xp_ref[...])
    hbuf_ref[halo:halo + tm, :] = x_ref[...]
    hbuf_ref[halo + tm:2 * halo + tm, :] = jnp.where(last, 0.0, xn_ref[...])
    _fill_shifted(hbuf_ref, sh_ref, shifts, sh_ref.shape[1])

    def body(j, carry):
        r0 = j * rc
        h = _conv_rows(hbuf_ref, sh_ref, w_ref, plan, r0, rc) + b_ref[...]
        o_ref[pl.ds(pl.multiple_of(r0, rc), rc), :] = _silu(h)
        return carry

    lax.fori_loop(0, tm // rc, body, 0)


def _sconv(proj, col0, w, b, seq, tm):
    t = proj.shape[0]
    ch = w.shape[1]
    cw = 1024
    halo = SUBLANES
    rc = 16
    shifts, plan, qmax = _tap_plan(SSM_CONV, halo)
    tiles_per_seq = seq // tm
    hb = tm // halo
    nhb = t // halo
    c0 = col0 // cw
    kern = functools.partial(_sconv_kernel, tiles_per_seq=tiles_per_seq, halo=halo,
                             shifts=shifts, plan=plan, rc=rc)
    return pl.pallas_call(
        kern,
        grid=(t // tm, ch // cw),
        in_specs=[pl.BlockSpec((tm, cw), lambda i, j: (i, c0 + j)),
                  pl.BlockSpec((halo, cw), lambda i, j: (jnp.maximum(i * hb - 1, 0), c0 + j)),
                  pl.BlockSpec((halo, cw),
                               lambda i, j: (jnp.minimum((i + 1) * hb, nhb - 1), c0 + j)),
                  pl.BlockSpec((SSM_CONV * SUBLANES, cw), lambda i, j: (0, j)),
                  pl.BlockSpec((1, cw), lambda i, j: (0, j))],
        out_specs=pl.BlockSpec((tm, cw), lambda i, j: (i, j)),
        out_shape=jax.ShapeDtypeStruct((t, ch), F32),
        scratch_shapes=[pltpu.VMEM((tm + 2 * halo, cw), F32),
                        pltpu.VMEM((len(shifts), tm + qmax, cw), F32)],
        compiler_params=_cparams(("parallel", "parallel")),
        name="sconv",
    )(proj, proj, proj, jnp.repeat(w, SUBLANES, axis=0), b.reshape(1, ch))


def _split3(f):
    hi = f.astype(BF16)
    r1 = f - hi.astype(F32)
    mid = r1.astype(BF16)
    lo = (r1 - mid.astype(F32)).astype(BF16)
    return hi, mid, lo


def _expand_heads(f, e):
    hi, mid, lo = _split3(f)
    return (jnp.dot(hi, e, preferred_element_type=F32)
            + jnp.dot(mid, e, preferred_element_type=F32)
            + jnp.dot(lo, e, preferred_element_type=F32))


def _ssd_kernel(x_ref, b_ref, c_ref, dt_ref, dtb_ref, alog_ref, e_ref, o_ref, state_ref,
                acs_s, acst_s, dtt_s, wstt_s, carry_s, cb_s, bgt_s, *, reverse, lane0):
    q = x_ref.shape[0]
    nheads = x_ref.shape[1] // SSM_HEAD_DIM
    hpg = nheads // SSM_GROUPS

    @pl.when(pl.program_id(1) == 0)
    def _():
        state_ref[...] = jnp.zeros_like(state_ref)

    row = lax.broadcasted_iota(jnp.int32, (q, q), 0)
    col = lax.broadcasted_iota(jnp.int32, (q, q), 1)
    keep = (col >= row) if reverse else (col <= row)

    z = dt_ref[...] + dtb_ref[...]
    dt = jnp.maximum(z, 0.0) + jnp.log(1.0 + jnp.exp(-jnp.abs(z)))
    a = dt * (-jnp.exp(alog_ref[...]))
    acs = jnp.dot(keep.astype(F32), a, precision=HIGHEST, preferred_element_type=F32)
    tot = acs[0:1, :] if reverse else acs[q - 1:q, :]
    acs2 = acs * LOG2E
    acs_s[...] = acs2
    acst_s[...] = acs2.T
    dtt_s[...] = dt.T
    wstt_s[...] = (dt * jnp.exp(tot - acs)).T
    carry_s[...] = _expand_heads(jnp.broadcast_to(jnp.exp(tot), (SUBLANES, LANES)), e_ref[...])

    for g in range(SSM_GROUPS):
        gsl = slice(g * SSM_STATE, (g + 1) * SSM_STATE)
        cb_s[g] = lax.dot_general(c_ref[:, gsl].astype(BF16), b_ref[:, gsl].astype(BF16),
                                  (((1,), (1,)), ((), ())), preferred_element_type=F32)
        bgt_s[g] = b_ref[:, gsl].T
        for pr in range(hpg // 2):
            ps = slice((g * hpg + 2 * pr) * SSM_HEAD_DIM, (g * hpg + 2 * pr + 2) * SSM_HEAD_DIM)
            xp = x_ref[:, ps].astype(BF16)
            rhs = jnp.concatenate([xp, state_ref[:, ps].astype(BF16)], axis=0)
            for sub in range(2):
                hl = lane0 + g * hpg + 2 * pr + sub
                half = slice(sub * SSM_HEAD_DIM, (sub + 1) * SSM_HEAD_DIM)
                hs = slice(ps.start + half.start, ps.start + half.stop)
                colb = jnp.broadcast_to(acs_s[:, hl:hl + 1], (q, q))
                seg = colb - acst_s[hl:hl + 1, :]
                lmat = jnp.exp2(jnp.where(keep, seg, -jnp.inf))
                m = (cb_s[g] * lmat * dtt_s[hl:hl + 1, :]).astype(BF16)
                cexp = (c_ref[:, gsl] * jnp.exp2(colb)).astype(BF16)
                y = jnp.dot(jnp.concatenate([m, cexp], axis=1), rhs,
                            preferred_element_type=F32)
                o_ref[:, hs] = y[:, half]
                bw = (bgt_s[g] * wstt_s[hl:hl + 1, :]).astype(BF16)
                upd = jnp.dot(bw, xp, preferred_element_type=F32)
                state_ref[:, hs] = state_ref[:, hs] * carry_s[0:1, hs] + upd[:, half]


def _ssd(xbc, proj, dt_col_block, dt_bias, a_log, expand, bsz, seq, reverse, lane0):
    t = xbc.shape[0]
    q = SSM_CHUNK
    nc = seq // q
    inner = expand.shape[1]
    gn = SSM_GROUPS * SSM_STATE
    b_blk = inner // gn
    c_blk = b_blk + 1

    def tok(b, c):
        return b * nc + (nc - 1 - c if reverse else c)

    kern = functools.partial(_ssd_kernel, reverse=reverse, lane0=lane0)
    return pl.pallas_call(
        kern,
        grid=(bsz, nc),
        in_specs=[pl.BlockSpec((q, inner), lambda b, c: (tok(b, c), 0)),
                  pl.BlockSpec((q, gn), lambda b, c: (tok(b, c), b_blk)),
                  pl.BlockSpec((q, gn), lambda b, c: (tok(b, c), c_blk)),
                  pl.BlockSpec((q, LANES), lambda b, c: (tok(b, c), dt_col_block)),
                  pl.BlockSpec((1, LANES), lambda b, c: (0, 0)),
                  pl.BlockSpec((1, LANES), lambda b, c: (0, 0)),
                  pl.BlockSpec((LANES, inner), lambda b, c: (0, 0))],
        out_specs=pl.BlockSpec((q, inner), lambda b, c: (tok(b, c), 0)),
        out_shape=jax.ShapeDtypeStruct((t, inner), F32),
        scratch_shapes=[pltpu.VMEM((SSM_STATE, inner), F32),
                        pltpu.VMEM((q, LANES), F32), pltpu.VMEM((LANES, q), F32),
                        pltpu.VMEM((LANES, q), F32), pltpu.VMEM((LANES, q), F32),
                        pltpu.VMEM((SUBLANES, inner), F32),
                        pltpu.VMEM((SSM_GROUPS, q, q), F32),
                        pltpu.VMEM((SSM_GROUPS, SSM_STATE, q), F32)],
        compiler_params=_cparams(("arbitrary", "arbitrary")),
        name="ssd_bwd" if reverse else "ssd_fwd",
    )(xbc, xbc, xbc, proj, dt_bias, a_log, expand)


def _merge_kernel(yf_ref, yb_ref, xc_ref, z_ref, hc_ref, gc_ref, gs_ref, x_ref,
                  dexp_ref, ng_ref, wso_ref, wco_ref, bco_ref, wo_ref, g1_ref,
                  scale_ref, shift_ref, wr_ref, br_ref,
                  x1_ref, v_ref, lg_ref, *, ngroups, halves):
    tm, inner = yf_ref.shape
    gw = inner // ngroups
    hm = tm // halves
    for h in range(halves):
        r = slice(h * hm, (h + 1) * hm)
        y = (yf_ref[r, :] + yb_ref[r, :] + xc_ref[r, :] * dexp_ref[...]) * _silu(z_ref[r, :])
        parts = []
        for g in range(ngroups):
            yg = y[:, g * gw:(g + 1) * gw]
            ms = jnp.mean(yg * yg, axis=-1, keepdims=True)
            parts.append((yg * lax.rsqrt(ms + EPS)) * ng_ref[:, g * gw:(g + 1) * gw])
        ysn = jnp.concatenate(parts, axis=-1).astype(BF16)
        y_ssm = jnp.dot(ysn, wso_ref[...], preferred_element_type=F32)
        y_conf = jnp.dot(hc_ref[r, :], wco_ref[...], preferred_element_type=F32) + bco_ref[...]
        merged = _sigmoid(gc_ref[r, :]) * y_conf + _sigmoid(gs_ref[r, :]) * y_ssm
        o = jnp.dot(merged.astype(BF16), wo_ref[...], preferred_element_type=F32)
        x1 = x_ref[r, :] + g1_ref[0] * o
        x1_ref[r, :] = x1
        v = _modulated_norm(x1, scale_ref[0], shift_ref[0])
        _store_rows(v_ref, v, h * hm)
        v_hi = v.astype(BF16)
        v_lo = (v - v_hi.astype(F32)).astype(BF16)
        p = (jnp.dot(v_hi, wr_ref[...], preferred_element_type=F32)
             + jnp.dot(v_lo, wr_ref[...], preferred_element_type=F32))
        lg_ref[r, :] = p[:, :LANES] + p[:, LANES:] + br_ref[...]


def _merge(yf, yb, xbc, proj, hc, x, cols, dexp, norm_g, wso, wco, bco, wo, g1, scale, shift,
           wr, br, seq, tm):
    t, d = x.shape
    inner = yf.shape[1]
    tiles_per_seq = seq // tm
    z_blk = cols["z"] // inner
    gc_blk = cols["gate_conf"] // d
    gs_blk = cols["gate_ssm"] // d

    def const(shape):
        return pl.BlockSpec(shape, lambda i: tuple(0 for _ in shape),
                            pipeline_mode=pl.Buffered(1))

    def per_seq():
        return pl.BlockSpec((1, 1, d), lambda i: (i // tiles_per_seq, 0, 0))

    kern = functools.partial(_merge_kernel, ngroups=SSM_GROUPS, halves=2)
    return pl.pallas_call(
        kern,
        grid=(t // tm,),
        in_specs=[pl.BlockSpec((tm, inner), lambda i: (i, 0)),
                  pl.BlockSpec((tm, inner), lambda i: (i, 0)),
                  pl.BlockSpec((tm, inner), lambda i: (i, 0)),
                  pl.BlockSpec((tm, inner), lambda i: (i, z_blk)),
                  pl.BlockSpec((tm, d), lambda i: (i, 0)),
                  pl.BlockSpec((tm, d), lambda i: (i, gc_blk)),
                  pl.BlockSpec((tm, d), lambda i: (i, gs_blk)),
                  pl.BlockSpec((tm, d), lambda i: (i, 0)),
                  const((1, inner)), const((1, inner)),
                  const((inner, d)), const((d, d)), const((1, d)), const((d, d)),
                  per_seq(), per_seq(), per_seq(),
                  const((d, 2 * LANES)), const((1, LANES))],
        out_specs=[pl.BlockSpec((tm, d), lambda i: (i, 0)),
                   pl.BlockSpec((tm * d // LANES, LANES), lambda i: (i, 0)),
                   pl.BlockSpec((tm, LANES), lambda i: (i, 0))],
        out_shape=[jax.ShapeDtypeStruct((t, d), F32),
                   jax.ShapeDtypeStruct((t * d // LANES, LANES), F32),
                   jax.ShapeDtypeStruct((t, LANES), F32)],
        compiler_params=_cparams(("parallel",)),
        name="merge",
    )(yf, yb, xbc, proj, hc, proj, proj, x, dexp, norm_g, wso, wco, bco, wo, g1, scale, shift,
      wr, br)


def _route_kernel(lg_ref, idx_ref, w_ref, rank_ref, cnt_ref, run_ref, *, n_experts):
    @pl.when(pl.program_id(0) == 0)
    def _():
        run_ref[...] = jnp.zeros_like(run_ref)

    tm = lg_ref.shape[0]
    lane = lax.broadcasted_iota(jnp.int32, (tm, LANES), 1)
    lg = jnp.where(lane < n_experts, lg_ref[...], -jnp.inf)
    sel = jnp.zeros((tm, LANES), F32)
    vals, idxs = [], []
    for _ in range(TOP_K):
        m = jnp.max(lg, axis=-1, keepdims=True)
        ix = jnp.min(jnp.where(lg == m, lane, LANES), axis=-1, keepdims=True)
        hit = lane == ix
        sel = jnp.where(hit, 1.0, sel)
        lg = jnp.where(hit, -jnp.inf, lg)
        vals.append(m)
        idxs.append(ix)
    ex = [jnp.exp(v - vals[0]) for v in vals]
    den = ex[0] + ex[1] + ex[2] + ex[3]
    row = lax.broadcasted_iota(jnp.int32, (tm, tm), 0)
    col = lax.broadcasted_iota(jnp.int32, (tm, tm), 1)
    before = (col < row).astype(BF16)
    excl = jnp.dot(before, sel.astype(BF16), preferred_element_type=F32) + run_ref[...]
    idx_out = jnp.zeros((tm, LANES), jnp.int32)
    w_out = jnp.zeros((tm, LANES), F32)
    rank_out = jnp.zeros((tm, LANES), jnp.int32)
    for k in range(TOP_K):
        rk = jnp.sum(jnp.where(lane == idxs[k], excl, 0.0), axis=-1, keepdims=True)
        idx_out = jnp.where(lane == k, idxs[k], idx_out)
        w_out = jnp.where(lane == k, ex[k] / den, w_out)
        rank_out = jnp.where(lane == k, rk.astype(jnp.int32), rank_out)
    idx_ref[...] = idx_out
    w_ref[...] = w_out
    rank_ref[...] = rank_out
    run_ref[...] = run_ref[...] + jnp.sum(sel, axis=0, keepdims=True)
    cnt_ref[...] = jnp.broadcast_to(run_ref[...], cnt_ref.shape).astype(jnp.int32)


def _route(logits, n_experts, tm):
    t = logits.shape[0]
    kern = functools.partial(_route_kernel, n_experts=n_experts)
    return pl.pallas_call(
        kern,
        grid=(t // tm,),
        in_specs=[pl.BlockSpec((tm, LANES), lambda i: (i, 0))],
        out_specs=[pl.BlockSpec((tm, LANES), lambda i: (i, 0)),
                   pl.BlockSpec((tm, LANES), lambda i: (i, 0)),
                   pl.BlockSpec((tm, LANES), lambda i: (i, 0)),
                   pl.BlockSpec((SUBLANES, LANES), lambda i: (0, 0))],
        out_shape=[jax.ShapeDtypeStruct((t, LANES), jnp.int32),
                   jax.ShapeDtypeStruct((t, LANES), F32),
                   jax.ShapeDtypeStruct((t, LANES), jnp.int32),
                   jax.ShapeDtypeStruct((SUBLANES, LANES), jnp.int32)],
        scratch_shapes=[pltpu.VMEM((1, LANES), F32)],
        compiler_params=_cparams(("arbitrary",)),
        name="route",
    )(logits)


def _dest_kernel(idx_ref, rank_ref, pstart_ref, o_ref):
    tm = idx_ref.shape[0]
    lane = lax.broadcasted_iota(jnp.int32, (tm, LANES), 1)
    idx = idx_ref[...]
    out = jnp.zeros((tm, LANES), jnp.int32)
    for k in range(TOP_K):
        base = jnp.sum(jnp.where(lane == idx[:, k:k + 1], pstart_ref[...], 0), axis=-1,
                       keepdims=True)
        out = jnp.where(lane == k, base + rank_ref[:, k:k + 1], out)
    o_ref[...] = out


def _dest(top_i, rank, pad_start_row, tm):
    t = top_i.shape[0]
    return pl.pallas_call(
        _dest_kernel,
        grid=(t // tm,),
        in_specs=[pl.BlockSpec((tm, LANES), lambda i: (i, 0)),
                  pl.BlockSpec((tm, LANES), lambda i: (i, 0)),
                  pl.BlockSpec((1, LANES), lambda i: (0, 0))],
        out_specs=pl.BlockSpec((tm, LANES), lambda i: (i, 0)),
        out_shape=jax.ShapeDtypeStruct((t, LANES), jnp.int32),
        compiler_params=_cparams(("parallel",)),
        name="dest",
    )(top_i, rank, pad_start_row)


def _row_copy(src, src_row, dst, dst_row, sem, nch):
    return pltpu.make_async_copy(_row(src, src_row, nch), _row(dst, dst_row, nch), sem)


def _dispatch_kernel(dest_ref, fill_ref, end_ref, v_ref, xs_hbm, sem, fsem, *, n_experts, nch):
    i = pl.program_id(0)
    tm = v_ref.shape[0] // nch

    @pl.when(i == 0)
    def _():
        def fill(s, c):
            _row_copy(v_ref, 0, xs_hbm, s, fsem, nch).start()
            return c

        def drain(s, c):
            _row_copy(v_ref, 0, xs_hbm, s, fsem, nch).wait()
            return c

        group = 8
        for e0 in range(0, n_experts, group):
            for e in range(e0, min(e0 + group, n_experts)):
                lax.fori_loop(fill_ref[e], end_ref[e], fill, 0)
            for e in range(e0, min(e0 + group, n_experts)):
                lax.fori_loop(fill_ref[e], end_ref[e], drain, 0)

    t0 = i * tm

    def issue(r, c):
        for k in range(TOP_K):
            _row_copy(v_ref, r, xs_hbm, dest_ref[(t0 + r) * TOP_K + k], sem, nch).start(
                priority=k % 2)
        return c

    def drain(r, c):
        for k in range(TOP_K):
            _row_copy(v_ref, 0, xs_hbm, 0, sem, nch).wait()
        return c

    lax.fori_loop(0, tm, issue, 0)
    lax.fori_loop(0, tm, drain, 0)


def _dispatch(dest_flat, fill_start, pad_end, v, nch, n_slots, tm):
    t = v.shape[0] // nch
    n_experts = fill_start.shape[0]
    kern = functools.partial(_dispatch_kernel, n_experts=n_experts, nch=nch)
    grid_spec = pltpu.PrefetchScalarGridSpec(
        num_scalar_prefetch=3,
        grid=(t // tm,),
        in_specs=[pl.BlockSpec((tm * nch, LANES), lambda i, dst, fl, en: (i, 0))],
        out_specs=pl.BlockSpec(memory_space=pl.ANY),
        scratch_shapes=[pltpu.SemaphoreType.DMA, pltpu.SemaphoreType.DMA],
    )
    return pl.pallas_call(
        kern,
        grid_spec=grid_spec,
        out_shape=jax.ShapeDtypeStruct((n_slots * nch, LANES), v.dtype),
        compiler_params=_cparams(("arbitrary",)),
        name="dispatch",
    )(dest_flat, fill_start, pad_end, v)


def _moe_kernel(be_ref, nused_ref, x_ref, wgu_ref, bgu_ref, wd_ref, bd_ref, o_ref,
                wgu_bf, wd_bf):
    b = pl.program_id(0)
    changed = jnp.logical_or(b == 0, be_ref[b] != be_ref[jnp.maximum(b - 1, 0)])
    active = b < nused_ref[0]

    @pl.when(jnp.logical_and(active, changed))
    def _():
        wgu_bf[...] = wgu_ref[0].astype(BF16)
        wd_bf[...] = wd_ref[0].astype(BF16)

    @pl.when(active)
    def _():
        ff = wd_bf.shape[0]
        d = wgu_bf.shape[0]
        nch = d // LANES
        hm = x_ref.shape[0] // nch // MOE_HALVES
        for h in range(MOE_HALVES):
            x = _load_rows(x_ref, hm, nch, h * hm).astype(BF16)
            gu = jnp.dot(x, wgu_bf[...], preferred_element_type=F32) + bgu_ref[0]
            gate = jnp.minimum(gu[:, :ff], SWIGLU_LIMIT)
            up = jnp.clip(gu[:, ff:], -SWIGLU_LIMIT, SWIGLU_LIMIT)
            glu = gate * _sigmoid(SWIGLU_ALPHA * gate)
            hid = ((up + 1.0) * glu).astype(BF16)
            y = jnp.dot(hid, wd_bf[...], preferred_element_type=F32) + bd_ref[0]
            _store_rows(o_ref, y, h * hm)


def _moe(block_expert, nused, xs, w_gu, b_gu, w_down, b_down):
    ne, d, ff2 = w_gu.shape
    nch = d // LANES
    n_slots = xs.shape[0] // nch
    ff = w_down.shape[1]
    bm = MOE_BLOCK

    def blk(b, nu):
        return jnp.minimum(b, nu[0] - 1)

    grid_spec = pltpu.PrefetchScalarGridSpec(
        num_scalar_prefetch=2,
        grid=(n_slots // bm,),
        in_specs=[pl.BlockSpec((bm * nch, LANES), lambda b, be, nu: (blk(b, nu), 0)),
                  pl.BlockSpec((1, d, ff2), lambda b, be, nu: (be[blk(b, nu)], 0, 0)),
                  pl.BlockSpec((1, 1, ff2), lambda b, be, nu: (be[blk(b, nu)], 0, 0)),
                  pl.BlockSpec((1, ff, d), lambda b, be, nu: (be[blk(b, nu)], 0, 0)),
                  pl.BlockSpec((1, 1, d), lambda b, be, nu: (be[blk(b, nu)], 0, 0))],
        out_specs=pl.BlockSpec((bm * nch, LANES), lambda b, be, nu: (blk(b, nu), 0)),
        scratch_shapes=[pltpu.VMEM((d, ff2), BF16), pltpu.VMEM((ff, d), BF16)],
    )
    return pl.pallas_call(
        _moe_kernel,
        grid_spec=grid_spec,
        out_shape=jax.ShapeDtypeStruct(xs.shape, F32),
        compiler_params=_cparams(("arbitrary",)),
        name="moe",
    )(block_expert, nused, xs, w_gu, b_gu.reshape(ne, 1, ff2), w_down, b_down.reshape(ne, 1, d))


def _final_kernel(dest_ref, x1_ref, w_ref, g2_ref, scale_ref, shift_ref, ys_hbm, o_ref,
                  gbuf, sems, *, nch):
    i = pl.program_id(0)
    n = pl.num_programs(0)
    tm = x1_ref.shape[0]
    slot = i % 2

    def issue(tile, slot_):
        def body(r, c):
            for k in range(TOP_K):
                src = dest_ref[(tile * tm + r) * TOP_K + k]
                pltpu.make_async_copy(_row(ys_hbm, src, nch),
                                      _row(gbuf, k * tm + r, nch, (slot_,)),
                                      sems.at[slot_]).start(priority=k % 2)
            return c
        lax.fori_loop(0, tm, body, 0)

    @pl.when(i == 0)
    def _():
        issue(0, 0)

    @pl.when(i + 1 < n)
    def _():
        issue(i + 1, 1 - slot)

    def drain(r, c):
        for k in range(TOP_K):
            pltpu.make_async_copy(_row(ys_hbm, 0, nch), _row(gbuf, 0, nch, (slot,)),
                                  sems.at[slot]).wait()
        return c
    lax.fori_loop(0, tm, drain, 0)

    w = w_ref[...]
    f = _load_rows(gbuf, tm, nch, 0, (slot,)) * w[:, 0:1]
    for k in range(1, TOP_K):
        f = f + _load_rows(gbuf, tm, nch, k * tm, (slot,)) * w[:, k:k + 1]
    x2 = x1_ref[...] + g2_ref[0] * f
    o_ref[...] = _modulated_norm(x2, scale_ref[0], shift_ref[0])


def _final(dest_flat, x1, ys, top_w, g2, scale, shift, seq, tm):
    t, d = x1.shape
    nch = d // LANES
    tiles_per_seq = seq // tm

    def per_seq():
        return pl.BlockSpec((1, 1, d), lambda i, dst: (i // tiles_per_seq, 0, 0))

    grid_spec = pltpu.PrefetchScalarGridSpec(
        num_scalar_prefetch=1,
        grid=(t // tm,),
        in_specs=[pl.BlockSpec((tm, d), lambda i, dst: (i, 0)),
                  pl.BlockSpec((tm, LANES), lambda i, dst: (i, 0)),
                  per_seq(), per_seq(), per_seq(),
                  pl.BlockSpec(memory_space=pl.ANY)],
        out_specs=pl.BlockSpec((tm, d), lambda i, dst: (i, 0)),
        scratch_shapes=[pltpu.VMEM((2, TOP_K * tm * nch, LANES), F32),
                        pltpu.SemaphoreType.DMA((2,))],
    )
    return pl.pallas_call(
        functools.partial(_final_kernel, nch=nch),
        grid_spec=grid_spec,
        out_shape=jax.ShapeDtypeStruct((t, d), F32),
        compiler_params=_cparams(("arbitrary",)),
        name="final",
    )(dest_flat, x1, top_w, g2, scale, shift, ys)


def _pad_cols(w, n):
    return jnp.pad(w, ((0, 0), (0, n - w.shape[1])))


def kernel(x, c, ada_w, ada_b, norm_mix_g, w_in, conf_dw_w, conf_dw_b, conf_ln_g, conf_ln_b,
           conf_out_w, conf_out_b, ssm_conv_w, ssm_conv_b, dt_bias_f, dt_bias_b, a_log_f,
           a_log_b, ssm_d, ssm_norm_g, ssm_out_w, w_o, norm_ffn_g, router_w, router_b, w_gu,
           b_gu, w_down, b_down, final_ada_w, final_ada_b, final_norm_g):
    bsz, seq, d = x.shape
    depth = ada_w.shape[0]
    t = bsz * seq
    nheads = a_log_f.shape[1]
    inner = nheads * SSM_HEAD_DIM
    gn = SSM_GROUPS * SSM_STATE
    conf = conf_dw_w.shape[2]
    n_experts = router_w.shape[2]
    assert 2 * nheads <= LANES and n_experts <= LANES

    c_pad = jnp.zeros((SUBLANES, d), F32).at[:bsz].set(c)
    fin = _ada(c_pad, final_ada_w, final_ada_b)[:bsz]
    xf = x.reshape(t, d)

    sizes = [("conf_a", conf), ("conf_g", conf), ("z", inner), ("xs", inner), ("bm", gn),
             ("cm", gn), ("gate_conf", d), ("gate_ssm", d), ("dt", LANES)]
    cols, off = {}, 0
    for name, n in sizes:
        cols[name] = off
        off += n
    src = {}
    o = 0
    for name, n in [("conf_a", conf), ("conf_g", conf), ("z", inner), ("xs", inner), ("bm", gn),
                    ("cm", gn), ("dtf", nheads), ("dtb", nheads), ("gate_conf", d),
                    ("gate_ssm", d)]:
        src[name] = (o, o + n)
        o += n

    head_of_col = jnp.arange(inner, dtype=jnp.int32) // SSM_HEAD_DIM
    lanes = jnp.arange(LANES, dtype=jnp.int32)[:, None]
    expand_f = (lanes == head_of_col[None, :]).astype(BF16)
    expand_b = (lanes == head_of_col[None, :] + nheads).astype(BF16)

    for l in range(depth):
        ada = _ada(c_pad, ada_w[l], ada_b[l])[:bsz]
        sh1, sc1, g1, sh2, sc2, g2 = [a.reshape(bsz, 1, d) for a in jnp.split(ada, 6, axis=-1)]
        scale1 = norm_mix_g[l][None, None, :] * (1.0 + sc1)
        scale2 = norm_ffn_g[l][None, None, :] * (1.0 + sc2)

        wl = w_in[l]
        w_dt = _pad_cols(jnp.concatenate([wl[:, slice(*src["dtf"])], wl[:, slice(*src["dtb"])]],
                                         axis=1), LANES)
        w_perm = jnp.concatenate(
            [wl[:, slice(*src[n])] for n in ("conf_a", "conf_g", "z", "xs", "bm", "cm",
                                             "gate_conf", "gate_ssm")] + [w_dt],
            axis=1).astype(BF16)
        proj = _inproj(xf, scale1, sh1, w_perm, seq, tm=256, col_splits=1)

        hc = _cconv(proj, conf_dw_w[l], conf_dw_b[l], conf_ln_g[l], conf_ln_b[l], seq, tm=256)
        xbc = _sconv(proj, cols["xs"], ssm_conv_w[l], ssm_conv_b[l], seq, tm=256)

        dt_bias = _pad_cols(jnp.concatenate([dt_bias_f[l], dt_bias_b[l]])[None, :], LANES)
        a_log = _pad_cols(jnp.concatenate([a_log_f[l], a_log_b[l]])[None, :], LANES)
        dt_blk = cols["dt"] // LANES
        y_f = _ssd(xbc, proj, dt_blk, dt_bias, a_log, expand_f, bsz, seq, False, 0)
        y_b = _ssd(xbc, proj, dt_blk, dt_bias, a_log, expand_b, bsz, seq, True, nheads)

        dexp = jnp.repeat(ssm_d[l], SSM_HEAD_DIM)[None, :]
        wr = _pad_cols(router_w[l], LANES)
        wr_hi = wr.astype(BF16)
        wr = jnp.concatenate([wr_hi, (wr - wr_hi.astype(F32)).astype(BF16)], axis=1)
        br = _pad_cols(router_b[l][None, :], LANES)
        x1, v, logits = _merge(
            y_f, y_b, xbc, proj, hc, xf, cols, dexp, ssm_norm_g[l][None, :],
            ssm_out_w[l].astype(BF16), conf_out_w[l].astype(BF16), conf_out_b[l][None, :],
            w_o[l].astype(BF16), g1, scale2, sh2, wr, br, seq, tm=256)

        top_i, top_w, rank, counts = _route(logits, n_experts, tm=512)
        counts = counts[0, :n_experts]
        bm = MOE_BLOCK
        n_blocks = (t * TOP_K) // bm + n_experts
        padded = ((counts + bm - 1) // bm) * bm
        pad_end = jnp.cumsum(padded)
        pad_start = pad_end - padded
        dest = _dest(top_i, rank, _pad_cols(pad_start[None, :], LANES), tm=512)
        dest_flat = dest[:, :TOP_K].reshape(-1)
        block_first = jnp.arange(n_blocks, dtype=jnp.int32) * bm
        block_expert = jnp.minimum(
            jnp.sum((pad_end[None, :] <= block_first[:, None]).astype(jnp.int32), axis=1),
            n_experts - 1).astype(jnp.int32)
        nused = (pad_end[-1] // bm).astype(jnp.int32).reshape(1)
        xs = _dispatch(dest_flat, (pad_start + counts).astype(jnp.int32),
                       pad_end.astype(jnp.int32), v, d // LANES, n_blocks * bm, tm=256)
        ys = _moe(block_expert, nused, xs, w_gu[l], b_gu[l], w_down[l], b_down[l])

        if l == depth - 1:
            sh_f, sc_f = [a.reshape(bsz, 1, d) for a in jnp.split(fin, 2, axis=-1)]
            scale_f = final_norm_g[None, None, :] * (1.0 + sc_f)
            xf = _final(dest_flat, x1, ys, top_w, g2, scale_f, sh_f, seq, tm=256)
        else:
            raise NotImplementedError("depth > 1 is not wired")
    return xf.reshape(bsz, seq, d)
```

```python
import functools

import jax
import jax.numpy as jnp
from jax import lax
from jax.experimental import pallas as pl
from jax.experimental.pallas import tpu as pltpu

F32 = jnp.float32
BF16 = jnp.bfloat16
HIGHEST = lax.Precision.HIGHEST

EPS = 1e-6
LOG2E = 1.4426950408889634
CONF_KERNEL = 31
SSM_CONV = 5
SSM_HEAD_DIM = 64
SSM_GROUPS = 4
SSM_STATE = 128
SSM_CHUNK = 128
TOP_K = 4
SWIGLU_ALPHA = 1.702
SWIGLU_LIMIT = 7.0

LANES = 128
SUBLANES = 8
VMEM_LIMIT = 56 * 1024 * 1024

MOE_BLOCK = 512
MOE_HALVES = 2


def _cparams(sem):
    return pltpu.CompilerParams(dimension_semantics=sem, vmem_limit_bytes=VMEM_LIMIT)


def _store_rows(ref, val, row0=0):
    n, d = val.shape
    nch = d // LANES
    for s in range(nch):
        ref[pl.ds(row0 * nch + s, n, stride=nch), :] = val[:, s * LANES:(s + 1) * LANES]


def _load_rows(ref, n, nch, row0=0, lead=()):
    return jnp.concatenate(
        [ref[lead + (pl.ds(row0 * nch + s, n, stride=nch), slice(None))] for s in range(nch)],
        axis=-1)


def _row(ref, r, nch, lead=()):
    return ref.at[lead + (pl.ds(pl.multiple_of(r * nch, nch), nch), slice(None))]


def _sigmoid(x):
    return 1.0 / (1.0 + jnp.exp(-x))


def _silu(x):
    return x * _sigmoid(x)


def _ada_kernel(c_ref, w_ref, b_ref, o_ref):
    c = c_ref[...]
    o_ref[...] = jnp.dot(_silu(c), w_ref[...], precision=HIGHEST,
                         preferred_element_type=F32) + b_ref[...]


def _ada(c_pad, w, b):
    d, n = w.shape
    tn = 1024
    return pl.pallas_call(
        _ada_kernel,
        grid=(n // tn,),
        in_specs=[pl.BlockSpec((SUBLANES, d), lambda j: (0, 0)),
                  pl.BlockSpec((d, tn), lambda j: (0, j)),
                  pl.BlockSpec((1, tn), lambda j: (0, j))],
        out_specs=pl.BlockSpec((SUBLANES, tn), lambda j: (0, j)),
        out_shape=jax.ShapeDtypeStruct((SUBLANES, n), F32),
        compiler_params=_cparams(("parallel",)),
        name="ada",
    )(c_pad, w, b.reshape(1, n))


def _modulated_norm(x, scale, shift):
    ms = jnp.mean(x * x, axis=-1, keepdims=True)
    return (x * lax.rsqrt(ms + EPS)) * scale + shift


def _inproj_kernel(x_ref, scale_ref, shift_ref, w_ref, o_ref, *, col_chunk):
    u = _modulated_norm(x_ref[...], scale_ref[0], shift_ref[0]).astype(BF16)
    ncols = o_ref.shape[1]
    for c0 in range(0, ncols, col_chunk):
        c1 = min(c0 + col_chunk, ncols)
        o_ref[:, c0:c1] = jnp.dot(u, w_ref[:, c0:c1], preferred_element_type=F32)


def _inproj(x, scale, shift, w_bf, seq, tm, col_splits):
    t, d = x.shape
    ncols = w_bf.shape[1]
    nw = ncols // col_splits
    tiles_per_seq = seq // tm
    return pl.pallas_call(
        functools.partial(_inproj_kernel, col_chunk=1024),
        grid=(col_splits, t // tm),
        in_specs=[pl.BlockSpec((tm, d), lambda j, i: (i, 0)),
                  pl.BlockSpec((1, 1, d), lambda j, i: (i // tiles_per_seq, 0, 0)),
                  pl.BlockSpec((1, 1, d), lambda j, i: (i // tiles_per_seq, 0, 0)),
                  pl.BlockSpec((d, nw), lambda j, i: (0, j), pipeline_mode=pl.Buffered(1))],
        out_specs=pl.BlockSpec((tm, nw), lambda j, i: (i, j)),
        out_shape=jax.ShapeDtypeStruct((t, ncols), F32),
        compiler_params=_cparams(("parallel", "parallel")),
        name="inproj",
    )(x, scale, shift, w_bf)


def _fill_shifted(hbuf_ref, sh_ref, shifts, rows):
    for n, r in enumerate(shifts):
        sh_ref[n] = hbuf_ref[r:r + rows, :]


def _tap_plan(ktaps, halo):
    offs = [k + halo - ktaps // 2 for k in range(ktaps)]
    shifts = sorted({o % SUBLANES for o in offs} - {0})
    plan = [(shifts.index(o % SUBLANES) if o % SUBLANES else -1, (o // SUBLANES) * SUBLANES)
            for o in offs]
    return shifts, plan, max(o // SUBLANES for o in offs) * SUBLANES


def _conv_rows(hbuf_ref, sh_ref, w8_ref, plan, r0, rc):
    groups = rc // SUBLANES
    accs = [None] * groups
    for k, (n, q8) in enumerate(plan):
        w = w8_ref[k * SUBLANES:(k + 1) * SUBLANES, :]
        for j in range(groups):
            start = pl.multiple_of(r0 + q8 + j * SUBLANES, SUBLANES)
            rows = pl.ds(start, SUBLANES)
            src = hbuf_ref[rows, :] if n < 0 else sh_ref[n, rows, :]
            term = src * w
            accs[j] = term if accs[j] is None else accs[j] + term
    return jnp.concatenate(accs, axis=0)


def _cconv_kernel(a_ref, g_ref, ap_ref, gp_ref, an_ref, gn_ref, w_ref, b_ref, lg_ref, lb_ref,
                  o_ref, hbuf_ref, sh_ref, cbuf_ref, *, tiles_per_seq, halo, shifts, plan, rc,
                  ln_rows):
    i = pl.program_id(0)
    tm = a_ref.shape[0]
    first = (i % tiles_per_seq) == 0
    last = (i % tiles_per_seq) == tiles_per_seq - 1
    glu_p = ap_ref[...] * _sigmoid(gp_ref[...])
    glu_n = an_ref[...] * _sigmoid(gn_ref[...])
    hbuf_ref[0:halo, :] = jnp.where(first, 0.0, glu_p)
    hbuf_ref[halo:halo + tm, :] = a_ref[...] * _sigmoid(g_ref[...])
    hbuf_ref[halo + tm:2 * halo + tm, :] = jnp.where(last, 0.0, glu_n)
    _fill_shifted(hbuf_ref, sh_ref, shifts, sh_ref.shape[1])

    def body(j, carry):
        r0 = j * rc
        h = _conv_rows(hbuf_ref, sh_ref, w_ref, plan, r0, rc) + b_ref[...]
        cbuf_ref[pl.ds(pl.multiple_of(r0, rc), rc), :] = h
        return carry

    lax.fori_loop(0, tm // rc, body, 0)

    for c0 in range(0, tm, ln_rows):
        h = cbuf_ref[c0:c0 + ln_rows, :]
        mu = jnp.mean(h, axis=-1, keepdims=True)
        hc = h - mu
        var = jnp.mean(hc * hc, axis=-1, keepdims=True)
        y = hc * lax.rsqrt(var + EPS) * lg_ref[...] + lb_ref[...]
        o_ref[c0:c0 + ln_rows, :] = _silu(y).astype(o_ref.dtype)


def _cconv(proj, w, b, ln_g, ln_b, seq, tm):
    t = proj.shape[0]
    ch = w.shape[1]
    halo = 16
    rc = 16
    shifts, plan, qmax = _tap_plan(CONF_KERNEL, halo)
    tiles_per_seq = seq // tm
    hb = tm // halo
    nhb = t // halo
    cb = 1

    def prev_map(col):
        return lambda i: (jnp.maximum(i * hb - 1, 0), col)

    def next_map(col):
        return lambda i: (jnp.minimum((i + 1) * hb, nhb - 1), col)

    kern = functools.partial(_cconv_kernel, tiles_per_seq=tiles_per_seq, halo=halo,
                             shifts=shifts, plan=plan, rc=rc, ln_rows=64)
    return pl.pallas_call(
        kern,
        grid=(t // tm,),
        in_specs=[pl.BlockSpec((tm, ch), lambda i: (i, 0)),
                  pl.BlockSpec((tm, ch), lambda i: (i, cb)),
                  pl.BlockSpec((halo, ch), prev_map(0)),
                  pl.BlockSpec((halo, ch), prev_map(cb)),
                  pl.BlockSpec((halo, ch), next_map(0)),
                  pl.BlockSpec((halo, ch), next_map(cb)),
                  pl.BlockSpec((CONF_KERNEL * SUBLANES, ch), lambda i: (0, 0)),
                  pl.BlockSpec((1, ch), lambda i: (0, 0)),
                  pl.BlockSpec((1, ch), lambda i: (0, 0)),
                  pl.BlockSpec((1, ch), lambda i: (0, 0))],
        out_specs=pl.BlockSpec((tm, ch), lambda i: (i, 0)),
        out_shape=jax.ShapeDtypeStruct((t, ch), BF16),
        scratch_shapes=[pltpu.VMEM((tm + 2 * halo, ch), F32),
                        pltpu.VMEM((len(shifts), tm + qmax, ch), F32),
                        pltpu.VMEM((tm, ch), F32)],
        compiler_params=_cparams(("parallel",)),
        name="cconv",
    )(proj, proj, proj, proj, proj, proj, jnp.repeat(w, SUBLANES, axis=0), b.reshape(1, ch),
      ln_g.reshape(1, ch),
      ln_b.reshape(1, ch))


def _sconv_kernel(x_ref, xp_ref, xn_ref, w_ref, b_ref, o_ref, hbuf_ref, sh_ref, *,
                  tiles_per_seq, halo, shifts, plan, rc):
    i = pl.program_id(0)
    tm = x_ref.shape[0]
    first = (i % tiles_per_seq) == 0
    last = (i % tiles_per_seq) == tiles_per_seq - 1
    hbuf_ref[0:halo, :] = jnp.where(first, 0.0, xp_ref[...])
    hbuf_ref[halo:halo + tm, :] = x_ref[...]
    hbuf_ref[halo + tm:2 * halo + tm, :] = jnp.where(last, 0.0, xn_ref[...])
    _fill_shifted(hbuf_ref, sh_ref, shifts, sh_ref.shape[1])

    def body(j, carry):
        r0 = j * rc
        h = _conv_rows(hbuf_ref, sh_ref, w_ref, plan, r0, rc) + b_ref[...]
        o_ref[pl.ds(pl.multiple_of(r0, rc), rc), :] = _silu(h)
        return carry

    lax.fori_loop(0, tm // rc, body, 0)


def _sconv(proj, col0, w, b, seq, tm):
    t = proj.shape[0]
    ch = w.shape[1]
    cw = 1024
    halo = SUBLANES
    rc = 16
    shifts, plan, qmax = _tap_plan(SSM_CONV, halo)
    tiles_per_seq = seq // tm
    hb = tm // halo
    nhb = t // halo
    c0 = col0 // cw
    kern = functools.partial(_sconv_kernel, tiles_per_seq=tiles_per_seq, halo=halo,
                             shifts=shifts, plan=plan, rc=rc)
    return pl.pallas_call(
        kern,
        grid=(t // tm, ch // cw),
        in_specs=[pl.BlockSpec((tm, cw), lambda i, j: (i, c0 + j)),
                  pl.BlockSpec((halo, cw), lambda i, j: (jnp.maximum(i * hb - 1, 0), c0 + j)),
                  pl.BlockSpec((halo, cw),
                               lambda i, j: (jnp.minimum((i + 1) * hb, nhb - 1), c0 + j)),
                  pl.BlockSpec((SSM_CONV * SUBLANES, cw), lambda i, j: (0, j)),
                  pl.BlockSpec((1, cw), lambda i, j: (0, j))],
        out_specs=pl.BlockSpec((tm, cw), lambda i, j: (i, j)),
        out_shape=jax.ShapeDtypeStruct((t, ch), F32),
        scratch_shapes=[pltpu.VMEM((tm + 2 * halo, cw), F32),
                        pltpu.VMEM((len(shifts), tm + qmax, cw), F32)],
        compiler_params=_cparams(("parallel", "parallel")),
        name="sconv",
    )(proj, proj, proj, jnp.repeat(w, SUBLANES, axis=0), b.reshape(1, ch))


def _split3(f):
    hi = f.astype(BF16)
    r1 = f - hi.astype(F32)
    mid = r1.astype(BF16)
    lo = (r1 - mid.astype(F32)).astype(BF16)
    return hi, mid, lo


def _expand_heads(f, e):
    hi, mid, lo = _split3(f)
    return (jnp.dot(hi, e, preferred_element_type=F32)
            + jnp.dot(mid, e, preferred_element_type=F32)
            + jnp.dot(lo, e, preferred_element_type=F32))


def _ssd_kernel(x_ref, b_ref, c_ref, dt_ref, dtb_ref, alog_ref, e_ref, o_ref, state_ref,
                acs_s, acst_s, dtt_s, wstt_s, carry_s, cb_s, bgt_s, *, reverse, lane0):
    q = x_ref.shape[0]
    nheads = x_ref.shape[1] // SSM_HEAD_DIM
    hpg = nheads // SSM_GROUPS

    @pl.when(pl.program_id(1) == 0)
    def _():
        state_ref[...] = jnp.zeros_like(state_ref)

    row = lax.broadcasted_iota(jnp.int32, (q, q), 0)
    col = lax.broadcasted_iota(jnp.int32, (q, q), 1)
    keep = (col >= row) if reverse else (col <= row)

    z = dt_ref[...] + dtb_ref[...]
    dt = jnp.maximum(z, 0.0) + jnp.log(1.0 + jnp.exp(-jnp.abs(z)))
    a = dt * (-jnp.exp(alog_ref[...]))
    acs = jnp.dot(keep.astype(F32), a, precision=HIGHEST, preferred_element_type=F32)
    tot = acs[0:1, :] if reverse else acs[q - 1:q, :]
    acs2 = acs * LOG2E
    acs_s[...] = acs2
    acst_s[...] = acs2.T
    dtt_s[...] = dt.T
    wstt_s[...] = (dt * jnp.exp(tot - acs)).T
    carry_s[...] = _expand_heads(jnp.broadcast_to(jnp.exp(tot), (SUBLANES, LANES)), e_ref[...])

    for g in range(SSM_GROUPS):
        gsl = slice(g * SSM_STATE, (g + 1) * SSM_STATE)
        cb_s[g] = lax.dot_general(c_ref[:, gsl].astype(BF16), b_ref[:, gsl].astype(BF16),
                                  (((1,), (1,)), ((), ())), preferred_element_type=F32)
        bgt_s[g] = b_ref[:, gsl].T
        for pr in range(hpg // 2):
            ps = slice((g * hpg + 2 * pr) * SSM_HEAD_DIM, (g * hpg + 2 * pr + 2) * SSM_HEAD_DIM)
            xp = x_ref[:, ps].astype(BF16)
            rhs = jnp.concatenate([xp, state_ref[:, ps].astype(BF16)], axis=0)
            for sub in range(2):
                hl = lane0 + g * hpg + 2 * pr + sub
                half = slice(sub * SSM_HEAD_DIM, (sub + 1) * SSM_HEAD_DIM)
                hs = slice(ps.start + half.start, ps.start + half.stop)
                colb = jnp.broadcast_to(acs_s[:, hl:hl + 1], (q, q))
                seg = colb - acst_s[hl:hl + 1, :]
                lmat = jnp.exp2(jnp.where(keep, seg, -jnp.inf))
                m = (cb_s[g] * lmat * dtt_s[hl:hl + 1, :]).astype(BF16)
                cexp = (c_ref[:, gsl] * jnp.exp2(colb)).astype(BF16)
                y = jnp.dot(jnp.concatenate([m, cexp], axis=1), rhs,
                            preferred_element_type=F32)
                o_ref[:, hs] = y[:, half]
                bw = (bgt_s[g] * wstt_s[hl:hl + 1, :]).astype(BF16)
                upd = jnp.dot(bw, xp, preferred_element_type=F32)
                state_ref[:, hs] = state_ref[:, hs] * carry_s[0:1, hs] + upd[:, half]


def _ssd(xbc, proj, dt_col_block, dt_bias, a_log, expand, bsz, seq, reverse, lane0):
    t = xbc.shape[0]
    q = SSM_CHUNK
    nc = seq // q
    inner = expand.shape[1]
    gn = SSM_GROUPS * SSM_STATE
    b_blk = inner // gn
    c_blk = b_blk + 1

    def tok(b, c):
        return b * nc + (nc - 1 - c if reverse else c)

    kern = functools.partial(_ssd_kernel, reverse=reverse, lane0=lane0)
    return pl.pallas_call(
        kern,
        grid=(bsz, nc),
        in_specs=[pl.BlockSpec((q, inner), lambda b, c: (tok(b, c), 0)),
                  pl.BlockSpec((q, gn), lambda b, c: (tok(b, c), b_blk)),
                  pl.BlockSpec((q, gn), lambda b, c: (tok(b, c), c_blk)),
                  pl.BlockSpec((q, LANES), lambda b, c: (tok(b, c), dt_col_block)),
                  pl.BlockSpec((1, LANES), lambda b, c: (0, 0)),
                  pl.BlockSpec((1, LANES), lambda b, c: (0, 0)),
                  pl.BlockSpec((LANES, inner), lambda b, c: (0, 0))],
        out_specs=pl.BlockSpec((q, inner), lambda b, c: (tok(b, c), 0)),
        out_shape=jax.ShapeDtypeStruct((t, inner), F32),
        scratch_shapes=[pltpu.VMEM((SSM_STATE, inner), F32),
                        pltpu.VMEM((q, LANES), F32), pltpu.VMEM((LANES, q), F32),
                        pltpu.VMEM((LANES, q), F32), pltpu.VMEM((LANES, q), F32),
                        pltpu.VMEM((SUBLANES, inner), F32),
                        pltpu.VMEM((SSM_GROUPS, q, q), F32),
                        pltpu.VMEM((SSM_GROUPS, SSM_STATE, q), F32)],
        compiler_params=_cparams(("arbitrary", "arbitrary")),
        name="ssd_bwd" if reverse else "ssd_fwd",
    )(xbc, xbc, xbc, proj, dt_bias, a_log, expand)


def _merge_kernel(yf_ref, yb_ref, xc_ref, z_ref, hc_ref, gc_ref, gs_ref, x_ref,
                  dexp_ref, ng_ref, wso_ref, wco_ref, bco_ref, wo_ref, g1_ref,
                  scale_ref, shift_ref, wr_ref, br_ref,
                  x1_ref, v_ref, lg_ref, *, ngroups, halves):
    tm, inner = yf_ref.shape
    gw = inner // ngroups
    hm = tm // halves
    for h in range(halves):
        r = slice(h * hm, (h + 1) * hm)
        y = (yf_ref[r, :] + yb_ref[r, :] + xc_ref[r, :] * dexp_ref[...]) * _silu(z_ref[r, :])
        parts = []
        for g in range(ngroups):
            yg = y[:, g * gw:(g + 1) * gw]
            ms = jnp.mean(yg * yg, axis=-1, keepdims=True)
            parts.append((yg * lax.rsqrt(ms + EPS)) * ng_ref[:, g * gw:(g + 1) * gw])
        ysn = jnp.concatenate(parts, axis=-1).astype(BF16)
        y_ssm = jnp.dot(ysn, wso_ref[...], preferred_element_type=F32)
        y_conf = jnp.dot(hc_ref[r, :], wco_ref[...], preferred_element_type=F32) + bco_ref[...]
        merged = _sigmoid(gc_ref[r, :]) * y_conf + _sigmoid(gs_ref[r, :]) * y_ssm
        o = jnp.dot(merged.astype(BF16), wo_ref[...], preferred_element_type=F32)
        x1 = x_ref[r, :] + g1_ref[0] * o
        x1_ref[r, :] = x1
        v = _modulated_norm(x1, scale_ref[0], shift_ref[0])
        v_ref[r, :] = v
        v_hi = v.astype(BF16)
        v_lo = (v - v_hi.astype(F32)).astype(BF16)
        p = (jnp.dot(v_hi, wr_ref[...], preferred_element_type=F32)
             + jnp.dot(v_lo, wr_ref[...], preferred_element_type=F32))
        lg_ref[r, :] = p[:, :LANES] + p[:, LANES:] + br_ref[...]


def _merge(yf, yb, xbc, proj, hc, x, cols, dexp, norm_g, wso, wco, bco, wo, g1, scale, shift,
           wr, br, seq, tm):
    t, d = x.shape
    inner = yf.shape[1]
    tiles_per_seq = seq // tm
    z_blk = cols["z"] // inner
    gc_blk = cols["gate_conf"] // d
    gs_blk = cols["gate_ssm"] // d

    def const(shape):
        return pl.BlockSpec(shape, lambda i: tuple(0 for _ in shape),
                            pipeline_mode=pl.Buffered(1))

    def per_seq():
        return pl.BlockSpec((1, 1, d), lambda i: (i // tiles_per_seq, 0, 0))

    kern = functools.partial(_merge_kernel, ngroups=SSM_GROUPS, halves=2)
    return pl.pallas_call(
        kern,
        grid=(t // tm,),
        in_specs=[pl.BlockSpec((tm, inner), lambda i: (i, 0)),
                  pl.BlockSpec((tm, inner), lambda i: (i, 0)),
                  pl.BlockSpec((tm, inner), lambda i: (i, 0)),
                  pl.BlockSpec((tm, inner), lambda i: (i, z_blk)),
                  pl.BlockSpec((tm, d), lambda i: (i, 0)),
                  pl.BlockSpec((tm, d), lambda i: (i, gc_blk)),
                  pl.BlockSpec((tm, d), lambda i: (i, gs_blk)),
                  pl.BlockSpec((tm, d), lambda i: (i, 0)),
                  const((1, inner)), const((1, inner)),
                  const((inner, d)), const((d, d)), const((1, d)), const((d, d)),
                  per_seq(), per_seq(), per_seq(),
                  const((d, 2 * LANES)), const((1, LANES))],
        out_specs=[pl.BlockSpec((tm, d), lambda i: (i, 0)),
                   pl.BlockSpec((tm, d), lambda i: (i, 0)),
                   pl.BlockSpec((tm, LANES), lambda i: (i, 0))],
        out_shape=[jax.ShapeDtypeStruct((t, d), F32),
                   jax.ShapeDtypeStruct((t, d), F32),
                   jax.ShapeDtypeStruct((t, LANES), F32)],
        compiler_params=_cparams(("parallel",)),
        name="merge",
    )(yf, yb, xbc, proj, hc, proj, proj, x, dexp, norm_g, wso, wco, bco, wo, g1, scale, shift,
      wr, br)


ROUTE_TILE = 256


def _route_kernel(lg_ref, idx_ref, w_ref, lslot_ref, lslot_t_ref, cnt_ref, *, n_experts):
    tm = lg_ref.shape[0]
    lane = lax.broadcasted_iota(jnp.int32, (tm, LANES), 1)
    lg = jnp.where(lane < n_experts, lg_ref[...], -jnp.inf)
    sel = jnp.zeros((tm, LANES), F32)
    vals, idxs = [], []
    for _ in range(TOP_K):
        m = jnp.max(lg, axis=-1, keepdims=True)
        ix = jnp.min(jnp.where(lg == m, lane, LANES), axis=-1, keepdims=True)
        hit = lane == ix
        sel = jnp.where(hit, 1.0, sel)
        lg = jnp.where(hit, -jnp.inf, lg)
        vals.append(m)
        idxs.append(ix)
    ex = [jnp.exp(v - vals[0]) for v in vals]
    den = ex[0] + ex[1] + ex[2] + ex[3]
    row = lax.broadcasted_iota(jnp.int32, (tm, tm), 0)
    col = lax.broadcasted_iota(jnp.int32, (tm, tm), 1)
    rank = jnp.dot((col < row).astype(BF16), sel.astype(BF16), preferred_element_type=F32)
    cnt = jnp.sum(sel, axis=0, keepdims=True)
    erow = lax.broadcasted_iota(jnp.int32, (LANES, LANES), 0)
    ecol = lax.broadcasted_iota(jnp.int32, (LANES, LANES), 1)
    off = jnp.dot(jnp.broadcast_to(cnt, (SUBLANES, LANES)).astype(BF16),
                  (erow < ecol).astype(BF16), preferred_element_type=F32)[0:1, :]
    slot_of = rank + off
    idx_out = jnp.zeros((tm, LANES), jnp.int32)
    w_out = jnp.zeros((tm, LANES), F32)
    slot_out = jnp.zeros((tm, LANES), F32)
    for k in range(TOP_K):
        sk = jnp.sum(jnp.where(lane == idxs[k], slot_of, 0.0), axis=-1, keepdims=True)
        idx_out = jnp.where(lane == k, idxs[k], idx_out)
        w_out = jnp.where(lane == k, ex[k] / den, w_out)
        slot_out = jnp.where(lane == k, sk, slot_out)
    idx_ref[...] = idx_out
    w_ref[...] = w_out
    lslot_ref[...] = slot_out.astype(jnp.int32)
    lslot_t_ref[...] = slot_out.T[0:SUBLANES, :].astype(jnp.int32)
    cnt_ref[0] = jnp.broadcast_to(cnt, (SUBLANES, LANES)).astype(jnp.int32)


def _route(logits, n_experts):
    t = logits.shape[0]
    tm = ROUTE_TILE
    nt = t // tm
    kern = functools.partial(_route_kernel, n_experts=n_experts)
    return pl.pallas_call(
        kern,
        grid=(nt,),
        in_specs=[pl.BlockSpec((tm, LANES), lambda i: (i, 0))],
        out_specs=[pl.BlockSpec((tm, LANES), lambda i: (i, 0)),
                   pl.BlockSpec((tm, LANES), lambda i: (i, 0)),
                   pl.BlockSpec((tm, LANES), lambda i: (i, 0)),
                   pl.BlockSpec((SUBLANES, tm), lambda i: (i, 0)),
                   pl.BlockSpec((1, SUBLANES, LANES), lambda i: (i, 0, 0))],
        out_shape=[jax.ShapeDtypeStruct((t, LANES), jnp.int32),
                   jax.ShapeDtypeStruct((t, LANES), F32),
                   jax.ShapeDtypeStruct((t, LANES), jnp.int32),
                   jax.ShapeDtypeStruct((nt * SUBLANES, tm), jnp.int32),
                   jax.ShapeDtypeStruct((nt, SUBLANES, LANES), jnp.int32)],
        compiler_params=_cparams(("parallel",)),
        name="route",
    )(logits)


def _dest_kernel(idx_ref, lslot_ref, delta_ref, o_ref):
    tm = idx_ref.shape[0]
    lane = lax.broadcasted_iota(jnp.int32, (tm, LANES), 1)
    idx = idx_ref[...]
    out = jnp.zeros((tm, LANES), jnp.int32)
    for k in range(TOP_K):
        base = jnp.sum(jnp.where(lane == idx[:, k:k + 1], delta_ref[0], 0), axis=-1,
                       keepdims=True)
        out = jnp.where(lane == k, base + lslot_ref[:, k:k + 1], out)
    o_ref[...] = out


def _dest(top_i, lslot, delta):
    t = top_i.shape[0]
    tm = ROUTE_TILE
    return pl.pallas_call(
        _dest_kernel,
        grid=(t // tm,),
        in_specs=[pl.BlockSpec((tm, LANES), lambda i: (i, 0)),
                  pl.BlockSpec((tm, LANES), lambda i: (i, 0)),
                  pl.BlockSpec((1, 1, LANES), lambda i: (i, 0, 0))],
        out_specs=pl.BlockSpec((tm, LANES), lambda i: (i, 0)),
        out_shape=jax.ShapeDtypeStruct((t, LANES), jnp.int32),
        compiler_params=_cparams(("parallel",)),
        name="dest",
    )(top_i, lslot, delta)


def _row_copy(src, src_row, dst, dst_row, sem, nch):
    return pltpu.make_async_copy(_row(src, src_row, nch), _row(dst, dst_row, nch), sem)


def _segment_copies(cnt_ref, loc_ref, glob_ref, tile, n_experts, local_buf, hbm, sem, nch,
                    to_hbm, wait):
    max_bit = (ROUTE_TILE).bit_length() - 1

    def body(e, c):
        n = cnt_ref[tile * n_experts + e]
        lo = loc_ref[tile * n_experts + e]
        go = glob_ref[tile * n_experts + e]
        for bit in range(max_bit, -1, -1):
            size = 1 << bit
            done = (n >> (bit + 1)) << (bit + 1)

            @pl.when((n & size) != 0)
            def _():
                loc = local_buf.at[pl.ds(pl.multiple_of((lo + done) * nch, nch), size * nch), :]
                glb = hbm.at[pl.ds(pl.multiple_of((go + done) * nch, nch), size * nch), :]
                cp = pltpu.make_async_copy(loc, glb, sem) if to_hbm else \
                    pltpu.make_async_copy(glb, loc, sem)
                if wait:
                    cp.wait()
                else:
                    cp.start()
        return c

    lax.fori_loop(0, n_experts, body, 0)


def _dispatch_kernel(cnt_ref, loc_ref, glob_ref, fill_ref, end_ref, lslot_t_ref, v_ref, xs_hbm,
                     buf, sems, fsem, *, n_experts, nch):
    i = pl.program_id(0)
    n = pl.num_programs(0)
    tm = v_ref.shape[0]
    nslots = tm * TOP_K
    slot = i % 2
    srow = lax.broadcasted_iota(jnp.int32, (nslots, tm), 0)
    hit = srow == lslot_t_ref[0:1, :]
    for k in range(1, TOP_K):
        hit = jnp.logical_or(hit, srow == lslot_t_ref[k:k + 1, :])
    p = jnp.where(hit, 1.0, 0.0).astype(BF16)
    v = v_ref[...]
    v_hi = v.astype(BF16)
    v_lo = (v - v_hi.astype(F32)).astype(BF16)
    xl = (jnp.dot(p, v_hi, preferred_element_type=F32)
          + jnp.dot(p, v_lo, preferred_element_type=F32))
    for s_ in range(nch):
        buf[slot, pl.ds(s_, nslots, stride=nch), :] = xl[:, s_ * LANES:(s_ + 1) * LANES]

    @pl.when(i == 0)
    def _():
        def fill(s, c):
            _row_copy(buf.at[0], 0, xs_hbm, s, fsem, nch).start()
            return c

        def drain(s, c):
            _row_copy(buf.at[0], 0, xs_hbm, s, fsem, nch).wait()
            return c

        group = 8
        for e0 in range(0, n_experts, group):
            for e in range(e0, min(e0 + group, n_experts)):
                lax.fori_loop(fill_ref[e], end_ref[e], fill, 0)
            for e in range(e0, min(e0 + group, n_experts)):
                lax.fori_loop(fill_ref[e], end_ref[e], drain, 0)

    args = (cnt_ref, loc_ref, glob_ref)
    _segment_copies(*args, i, n_experts, buf.at[slot], xs_hbm, sems.at[slot], nch, True, False)

    @pl.when(i > 0)
    def _():
        _segment_copies(*args, i - 1, n_experts, buf.at[1 - slot], xs_hbm, sems.at[1 - slot],
                        nch, True, True)

    @pl.when(i == n - 1)
    def _():
        _segment_copies(*args, i, n_experts, buf.at[slot], xs_hbm, sems.at[slot], nch, True,
                        True)


def _dispatch(cnt_flat, loc_flat, glob_flat, fill_start, pad_end, lslot_t, v, n_slots):
    t, d = v.shape
    nch = d // LANES
    tm = ROUTE_TILE
    n_experts = fill_start.shape[0]
    kern = functools.partial(_dispatch_kernel, n_experts=n_experts, nch=nch)
    grid_spec = pltpu.PrefetchScalarGridSpec(
        num_scalar_prefetch=5,
        grid=(t // tm,),
        in_specs=[pl.BlockSpec((SUBLANES, tm), lambda i, *_: (i, 0)),
                  pl.BlockSpec((tm, d), lambda i, *_: (i, 0))],
        out_specs=pl.BlockSpec(memory_space=pl.ANY),
        scratch_shapes=[pltpu.VMEM((2, tm * TOP_K * nch, LANES), F32),
                        pltpu.SemaphoreType.DMA((2,)), pltpu.SemaphoreType.DMA],
    )
    return pl.pallas_call(
        kern,
        grid_spec=grid_spec,
        out_shape=jax.ShapeDtypeStruct((n_slots * nch, LANES), v.dtype),
        compiler_params=_cparams(("arbitrary",)),
        name="dispatch",
    )(cnt_flat, loc_flat, glob_flat, fill_start, pad_end, lslot_t, v)


def _moe_kernel(be_ref, nused_ref, x_ref, wgu_ref, bgu_ref, wd_ref, bd_ref, o_ref,
                wgu_bf, wd_bf):
    b = pl.program_id(0)
    changed = jnp.logical_or(b == 0, be_ref[b] != be_ref[jnp.maximum(b - 1, 0)])
    active = b < nused_ref[0]

    @pl.when(jnp.logical_and(active, changed))
    def _():
        wgu_bf[...] = wgu_ref[0].astype(BF16)
        wd_bf[...] = wd_ref[0].astype(BF16)

    @pl.when(active)
    def _():
        ff = wd_bf.shape[0]
        d = wgu_bf.shape[0]
        nch = d // LANES
        hm = x_ref.shape[0] // nch // MOE_HALVES
        for h in range(MOE_HALVES):
            x = _load_rows(x_ref, hm, nch, h * hm).astype(BF16)
            gu = jnp.dot(x, wgu_bf[...], preferred_element_type=F32) + bgu_ref[0]
            gate = jnp.minimum(gu[:, :ff], SWIGLU_LIMIT)
            up = jnp.clip(gu[:, ff:], -SWIGLU_LIMIT, SWIGLU_LIMIT)
            glu = gate * _sigmoid(SWIGLU_ALPHA * gate)
            hid = ((up + 1.0) * glu).astype(BF16)
            y = jnp.dot(hid, wd_bf[...], preferred_element_type=F32) + bd_ref[0]
            _store_rows(o_ref, y, h * hm)


def _moe(block_expert, nused, xs, w_gu, b_gu, w_down, b_down):
    ne, d, ff2 = w_gu.shape
    nch = d // LANES
    n_slots = xs.shape[0] // nch
    ff = w_down.shape[1]
    bm = MOE_BLOCK

    def blk(b, nu):
        return jnp.minimum(b, nu[0] - 1)

    grid_spec = pltpu.PrefetchScalarGridSpec(
        num_scalar_prefetch=2,
        grid=(n_slots // bm,),
        in_specs=[pl.BlockSpec((bm * nch, LANES), lambda b, be, nu: (blk(b, nu), 0)),
                  pl.BlockSpec((1, d, ff2), lambda b, be, nu: (be[blk(b, nu)], 0, 0)),
                  pl.BlockSpec((1, 1, ff2), lambda b, be, nu: (be[blk(b, nu)], 0, 0)),
                  pl.BlockSpec((1, ff, d), lambda b, be, nu: (be[blk(b, nu)], 0, 0)),
                  pl.BlockSpec((1, 1, d), lambda b, be, nu: (be[blk(b, nu)], 0, 0))],
        out_specs=pl.BlockSpec((bm * nch, LANES), lambda b, be, nu: (blk(b, nu), 0)),
        scratch_shapes=[pltpu.VMEM((d, ff2), BF16), pltpu.VMEM((ff, d), BF16)],
    )
    return pl.pallas_call(
        _moe_kernel,
        grid_spec=grid_spec,
        out_shape=jax.ShapeDtypeStruct(xs.shape, F32),
        compiler_params=_cparams(("arbitrary",)),
        name="moe",
    )(block_expert, nused, xs, w_gu, b_gu.reshape(ne, 1, ff2), w_down, b_down.reshape(ne, 1, d))


def _final_kernel(dest_ref, x1_ref, w_ref, g2_ref, scale_ref, shift_ref, ys_hbm, o_ref,
                  gbuf, sems, *, nch):
    i = pl.program_id(0)
    n = pl.num_programs(0)
    tm = x1_ref.shape[0]
    slot = i % 2

    def issue(tile, slot_):
        def body(r, c):
            for k in range(TOP_K):
                src = dest_ref[(tile * tm + r) * TOP_K + k]
                pltpu.make_async_copy(_row(ys_hbm, src, nch),
                                      _row(gbuf, k * tm + r, nch, (slot_,)),
                                      sems.at[slot_]).start(priority=k % 2)
            return c
        lax.fori_loop(0, tm, body, 0)

    @pl.when(i == 0)
    def _():
        issue(0, 0)

    @pl.when(i + 1 < n)
    def _():
        issue(i + 1, 1 - slot)

    def drain(r, c):
        for k in range(TOP_K):
            pltpu.make_async_copy(_row(ys_hbm, 0, nch), _row(gbuf, 0, nch, (slot,)),
                                  sems.at[slot]).wait()
        return c
    lax.fori_loop(0, tm, drain, 0)

    w = w_ref[...]
    f = _load_rows(gbuf, tm, nch, 0, (slot,)) * w[:, 0:1]
    for k in range(1, TOP_K):
        f = f + _load_rows(gbuf, tm, nch, k * tm, (slot,)) * w[:, k:k + 1]
    x2 = x1_ref[...] + g2_ref[0] * f
    o_ref[...] = _modulated_norm(x2, scale_ref[0], shift_ref[0])


def _final(dest_flat, x1, ys, top_w, g2, scale, shift, seq, tm):
    t, d = x1.shape
    nch = d // LANES
    tiles_per_seq = seq // tm

    def per_seq():
        return pl.BlockSpec((1, 1, d), lambda i, dst: (i // tiles_per_seq, 0, 0))

    grid_spec = pltpu.PrefetchScalarGridSpec(
        num_scalar_prefetch=1,
        grid=(t // tm,),
        in_specs=[pl.BlockSpec((tm, d), lambda i, dst: (i, 0)),
                  pl.BlockSpec((tm, LANES), lambda i, dst: (i, 0)),
                  per_seq(), per_seq(), per_seq(),
                  pl.BlockSpec(memory_space=pl.ANY)],
        out_specs=pl.BlockSpec((tm, d), lambda i, dst: (i, 0)),
        scratch_shapes=[pltpu.VMEM((2, TOP_K * tm * nch, LANES), F32),
                        pltpu.SemaphoreType.DMA((2,))],
    )
    return pl.pallas_call(
        functools.partial(_final_kernel, nch=nch),
        grid_spec=grid_spec,
        out_shape=jax.ShapeDtypeStruct((t, d), F32),
        compiler_params=_cparams(("arbitrary",)),
        name="final",
    )(dest_flat, x1, top_w, g2, scale, shift, ys)


def _pad_cols(w, n):
    return jnp.pad(w, ((0, 0), (0, n - w.shape[1])))


def kernel(x, c, ada_w, ada_b, norm_mix_g, w_in, conf_dw_w, conf_dw_b, conf_ln_g, conf_ln_b,
           conf_out_w, conf_out_b, ssm_conv_w, ssm_conv_b, dt_bias_f, dt_bias_b, a_log_f,
           a_log_b, ssm_d, ssm_norm_g, ssm_out_w, w_o, norm_ffn_g, router_w, router_b, w_gu,
           b_gu, w_down, b_down, final_ada_w, final_ada_b, final_norm_g):
    bsz, seq, d = x.shape
    depth = ada_w.shape[0]
    t = bsz * seq
    nheads = a_log_f.shape[1]
    inner = nheads * SSM_HEAD_DIM
    gn = SSM_GROUPS * SSM_STATE
    conf = conf_dw_w.shape[2]
    n_experts = router_w.shape[2]
    assert 2 * nheads <= LANES and n_experts <= LANES

    c_pad = jnp.zeros((SUBLANES, d), F32).at[:bsz].set(c)
    fin = _ada(c_pad, final_ada_w, final_ada_b)[:bsz]
    xf = x.reshape(t, d)

    sizes = [("conf_a", conf), ("conf_g", conf), ("z", inner), ("xs", inner), ("bm", gn),
             ("cm", gn), ("gate_conf", d), ("gate_ssm", d), ("dt", LANES)]
    cols, off = {}, 0
    for name, n in sizes:
        cols[name] = off
        off += n
    src = {}
    o = 0
    for name, n in [("conf_a", conf), ("conf_g", conf), ("z", inner), ("xs", inner), ("bm", gn),
                    ("cm", gn), ("dtf", nheads), ("dtb", nheads), ("gate_conf", d),
                    ("gate_ssm", d)]:
        src[name] = (o, o + n)
        o += n

    head_of_col = jnp.arange(inner, dtype=jnp.int32) // SSM_HEAD_DIM
    lanes = jnp.arange(LANES, dtype=jnp.int32)[:, None]
    expand_f = (lanes == head_of_col[None, :]).astype(BF16)
    expand_b = (lanes == head_of_col[None, :] + nheads).astype(BF16)

    for l in range(depth):
        ada = _ada(c_pad, ada_w[l], ada_b[l])[:bsz]
        sh1, sc1, g1, sh2, sc2, g2 = [a.reshape(bsz, 1, d) for a in jnp.split(ada, 6, axis=-1)]
        scale1 = norm_mix_g[l][None, None, :] * (1.0 + sc1)
        scale2 = norm_ffn_g[l][None, None, :] * (1.0 + sc2)

        wl = w_in[l]
        w_dt = _pad_cols(jnp.concatenate([wl[:, slice(*src["dtf"])], wl[:, slice(*src["dtb"])]],
                                         axis=1), LANES)
        w_perm = jnp.concatenate(
            [wl[:, slice(*src[n])] for n in ("conf_a", "conf_g", "z", "xs", "bm", "cm",
                                             "gate_conf", "gate_ssm")] + [w_dt],
            axis=1).astype(BF16)
        proj = _inproj(xf, scale1, sh1, w_perm, seq, tm=256, col_splits=1)

        hc = _cconv(proj, conf_dw_w[l], conf_dw_b[l], conf_ln_g[l], conf_ln_b[l], seq, tm=256)
        xbc = _sconv(proj, cols["xs"], ssm_conv_w[l], ssm_conv_b[l], seq, tm=256)

        dt_bias = _pad_cols(jnp.concatenate([dt_bias_f[l], dt_bias_b[l]])[None, :], LANES)
        a_log = _pad_cols(jnp.concatenate([a_log_f[l], a_log_b[l]])[None, :], LANES)
        dt_blk = cols["dt"] // LANES
        y_f = _ssd(xbc, proj, dt_blk, dt_bias, a_log, expand_f, bsz, seq, False, 0)
        y_b = _ssd(xbc, proj, dt_blk, dt_bias, a_log, expand_b, bsz, seq, True, nheads)

        dexp = jnp.repeat(ssm_d[l], SSM_HEAD_DIM)[None, :]
        wr = _pad_cols(router_w[l], LANES)
        wr_hi = wr.astype(BF16)
        wr = jnp.concatenate([wr_hi, (wr - wr_hi.astype(F32)).astype(BF16)], axis=1)
        br = _pad_cols(router_b[l][None, :], LANES)
        x1, v, logits = _merge(
            y_f, y_b, xbc, proj, hc, xf, cols, dexp, ssm_norm_g[l][None, :],
            ssm_out_w[l].astype(BF16), conf_out_w[l].astype(BF16), conf_out_b[l][None, :],
            w_o[l].astype(BF16), g1, scale2, sh2, wr, br, seq, tm=256)

        top_i, top_w, lslot, lslot_t, tile_cnt = _route(logits, n_experts)
        cnt2 = tile_cnt[:, 0, :n_experts]
        counts = jnp.sum(cnt2, axis=0)
        bm = MOE_BLOCK
        n_blocks = (t * TOP_K) // bm + n_experts
        padded = ((counts + bm - 1) // bm) * bm
        pad_end = jnp.cumsum(padded)
        pad_start = pad_end - padded
        loc2 = jnp.cumsum(cnt2, axis=1) - cnt2
        glob2 = pad_start[None, :] + jnp.cumsum(cnt2, axis=0) - cnt2
        delta = _pad_cols(glob2 - loc2, LANES)[:, None, :]
        dest = _dest(top_i, lslot, delta)
        dest_flat = dest[:, :TOP_K].reshape(-1)
        block_first = jnp.arange(n_blocks, dtype=jnp.int32) * bm
        block_expert = jnp.minimum(
            jnp.sum((pad_end[None, :] <= block_first[:, None]).astype(jnp.int32), axis=1),
            n_experts - 1).astype(jnp.int32)
        nused = (pad_end[-1] // bm).astype(jnp.int32).reshape(1)
        xs = _dispatch(cnt2.reshape(-1), loc2.reshape(-1), glob2.reshape(-1),
                       (pad_start + counts).astype(jnp.int32), pad_end.astype(jnp.int32),
                       lslot_t, v, n_blocks * bm)
        ys = _moe(block_expert, nused, xs, w_gu[l], b_gu[l], w_down[l], b_down[l])

        if l == depth - 1:
            sh_f, sc_f = [a.reshape(bsz, 1, d) for a in jnp.split(fin, 2, axis=-1)]
            scale_f = final_norm_g[None, None, :] * (1.0 + sc_f)
            xf = _final(dest_flat, x1, ys, top_w, g2, scale_f, sh_f, seq, tm=256)
        else:
            raise NotImplementedError("depth > 1 is not wired")
    return xf.reshape(bsz, seq, d)
```

```python
import functools

import jax
import jax.numpy as jnp
from jax import lax
from jax.experimental import pallas as pl
from jax.experimental.pallas import tpu as pltpu

F32 = jnp.float32
BF16 = jnp.bfloat16
HIGHEST = lax.Precision.HIGHEST

EPS = 1e-6
LOG2E = 1.4426950408889634
CONF_KERNEL = 31
SSM_CONV = 5
SSM_HEAD_DIM = 64
SSM_GROUPS = 4
SSM_STATE = 128
SSM_CHUNK = 128
TOP_K = 4
SWIGLU_ALPHA = 1.702
SWIGLU_LIMIT = 7.0

LANES = 128
SUBLANES = 8
VMEM_LIMIT = 56 * 1024 * 1024

MOE_BLOCK = 512
MOE_HALVES = 2


def _cparams(sem):
    return pltpu.CompilerParams(dimension_semantics=sem, vmem_limit_bytes=VMEM_LIMIT)


def _store_rows(ref, val, row0=0):
    n, d = val.shape
    nch = d // LANES
    for s in range(nch):
        ref[pl.ds(row0 * nch + s, n, stride=nch), :] = val[:, s * LANES:(s + 1) * LANES]


def _load_rows(ref, n, nch, row0=0, lead=()):
    return jnp.concatenate(
        [ref[lead + (pl.ds(row0 * nch + s, n, stride=nch), slice(None))] for s in range(nch)],
        axis=-1)


def _row(ref, r, nch, lead=()):
    return ref.at[lead + (pl.ds(pl.multiple_of(r * nch, nch), nch), slice(None))]


def _sigmoid(x):
    return 1.0 / (1.0 + jnp.exp(-x))


def _silu(x):
    return x * _sigmoid(x)


def _ada_kernel(c_ref, w_ref, b_ref, o_ref):
    c = c_ref[...]
    o_ref[...] = jnp.dot(_silu(c), w_ref[...], precision=HIGHEST,
                         preferred_element_type=F32) + b_ref[...]


def _ada(c_pad, w, b):
    d, n = w.shape
    tn = 1024
    return pl.pallas_call(
        _ada_kernel,
        grid=(n // tn,),
        in_specs=[pl.BlockSpec((SUBLANES, d), lambda j: (0, 0)),
                  pl.BlockSpec((d, tn), lambda j: (0, j)),
                  pl.BlockSpec((1, tn), lambda j: (0, j))],
        out_specs=pl.BlockSpec((SUBLANES, tn), lambda j: (0, j)),
        out_shape=jax.ShapeDtypeStruct((SUBLANES, n), F32),
        compiler_params=_cparams(("parallel",)),
        name="ada",
    )(c_pad, w, b.reshape(1, n))


def _modulated_norm(x, scale, shift):
    ms = jnp.mean(x * x, axis=-1, keepdims=True)
    return (x * lax.rsqrt(ms + EPS)) * scale + shift


def _inproj_kernel(x_ref, scale_ref, shift_ref, w_ref, o_ref, *, col_chunk):
    u = _modulated_norm(x_ref[...], scale_ref[0], shift_ref[0]).astype(BF16)
    ncols = o_ref.shape[1]
    for c0 in range(0, ncols, col_chunk):
        c1 = min(c0 + col_chunk, ncols)
        o_ref[:, c0:c1] = jnp.dot(u, w_ref[:, c0:c1], preferred_element_type=F32)


def _inproj(x, scale, shift, w_bf, seq, tm, col_splits):
    t, d = x.shape
    ncols = w_bf.shape[1]
    nw = ncols // col_splits
    tiles_per_seq = seq // tm
    return pl.pallas_call(
        functools.partial(_inproj_kernel, col_chunk=1024),
        grid=(col_splits, t // tm),
        in_specs=[pl.BlockSpec((tm, d), lambda j, i: (i, 0)),
                  pl.BlockSpec((1, 1, d), lambda j, i: (i // tiles_per_seq, 0, 0)),
                  pl.BlockSpec((1, 1, d), lambda j, i: (i // tiles_per_seq, 0, 0)),
                  pl.BlockSpec((d, nw), lambda j, i: (0, j), pipeline_mode=pl.Buffered(1))],
        out_specs=pl.BlockSpec((tm, nw), lambda j, i: (i, j)),
        out_shape=jax.ShapeDtypeStruct((t, ncols), F32),
        compiler_params=_cparams(("parallel", "parallel")),
        name="inproj",
    )(x, scale, shift, w_bf)


def _fill_shifted(hbuf_ref, sh_ref, shifts, rows):
    for n, r in enumerate(shifts):
        sh_ref[n] = hbuf_ref[r:r + rows, :]


def _tap_plan(ktaps, halo):
    offs = [k + halo - ktaps // 2 for k in range(ktaps)]
    shifts = sorted({o % SUBLANES for o in offs} - {0})
    plan = [(shifts.index(o % SUBLANES) if o % SUBLANES else -1, (o // SUBLANES) * SUBLANES)
            for o in offs]
    return shifts, plan, max(o // SUBLANES for o in offs) * SUBLANES


def _conv_rows(hbuf_ref, sh_ref, w8_ref, plan, r0, rc):
    groups = rc // SUBLANES
    accs = [None] * groups
    for k, (n, q8) in enumerate(plan):
        w = w8_ref[k * SUBLANES:(k + 1) * SUBLANES, :]
        for j in range(groups):
            start = pl.multiple_of(r0 + q8 + j * SUBLANES, SUBLANES)
            rows = pl.ds(start, SUBLANES)
            src = hbuf_ref[rows, :] if n < 0 else sh_ref[n, rows, :]
            term = src * w
            accs[j] = term if accs[j] is None else accs[j] + term
    return jnp.concatenate(accs, axis=0)


def _cconv_kernel(a_ref, g_ref, ap_ref, gp_ref, an_ref, gn_ref, w_ref, b_ref, lg_ref, lb_ref,
                  o_ref, hbuf_ref, sh_ref, cbuf_ref, *, tiles_per_seq, halo, shifts, plan, rc,
                  ln_rows):
    i = pl.program_id(0)
    tm = a_ref.shape[0]
    first = (i % tiles_per_seq) == 0
    last = (i % tiles_per_seq) == tiles_per_seq - 1
    glu_p = ap_ref[...] * _sigmoid(gp_ref[...])
    glu_n = an_ref[...] * _sigmoid(gn_ref[...])
    hbuf_ref[0:halo, :] = jnp.where(first, 0.0, glu_p)
    hbuf_ref[halo:halo + tm, :] = a_ref[...] * _sigmoid(g_ref[...])
    hbuf_ref[halo + tm:2 * halo + tm, :] = jnp.where(last, 0.0, glu_n)
    _fill_shifted(hbuf_ref, sh_ref, shifts, sh_ref.shape[1])

    def body(j, carry):
        r0 = j * rc
        h = _conv_rows(hbuf_ref, sh_ref, w_ref, plan, r0, rc) + b_ref[...]
        cbuf_ref[pl.ds(pl.multiple_of(r0, rc), rc), :] = h
        return carry

    lax.fori_loop(0, tm // rc, body, 0)

    for c0 in range(0, tm, ln_rows):
        h = cbuf_ref[c0:c0 + ln_rows, :]
        mu = jnp.mean(h, axis=-1, keepdims=True)
        hc = h - mu
        var = jnp.mean(hc * hc, axis=-1, keepdims=True)
        y = hc * lax.rsqrt(var + EPS) * lg_ref[...] + lb_ref[...]
        o_ref[c0:c0 + ln_rows, :] = _silu(y).astype(o_ref.dtype)


def _cconv(proj, w, b, ln_g, ln_b, seq, tm):
    t = proj.shape[0]
    ch = w.shape[1]
    halo = 16
    rc = 16
    shifts, plan, qmax = _tap_plan(CONF_KERNEL, halo)
    tiles_per_seq = seq // tm
    hb = tm // halo
    nhb = t // halo
    cb = 1

    def prev_map(col):
        return lambda i: (jnp.maximum(i * hb - 1, 0), col)

    def next_map(col):
        return lambda i: (jnp.minimum((i + 1) * hb, nhb - 1), col)

    kern = functools.partial(_cconv_kernel, tiles_per_seq=tiles_per_seq, halo=halo,
                             shifts=shifts, plan=plan, rc=rc, ln_rows=64)
    return pl.pallas_call(
        kern,
        grid=(t // tm,),
        in_specs=[pl.BlockSpec((tm, ch), lambda i: (i, 0)),
                  pl.BlockSpec((tm, ch), lambda i: (i, cb)),
                  pl.BlockSpec((halo, ch), prev_map(0)),
                  pl.BlockSpec((halo, ch), prev_map(cb)),
                  pl.BlockSpec((halo, ch), next_map(0)),
                  pl.BlockSpec((halo, ch), next_map(cb)),
                  pl.BlockSpec((CONF_KERNEL * SUBLANES, ch), lambda i: (0, 0)),
                  pl.BlockSpec((1, ch), lambda i: (0, 0)),
                  pl.BlockSpec((1, ch), lambda i: (0, 0)),
                  pl.BlockSpec((1, ch), lambda i: (0, 0))],
        out_specs=pl.BlockSpec((tm, ch), lambda i: (i, 0)),
        out_shape=jax.ShapeDtypeStruct((t, ch), BF16),
        scratch_shapes=[pltpu.VMEM((tm + 2 * halo, ch), F32),
                        pltpu.VMEM((len(shifts), tm + qmax, ch), F32),
                        pltpu.VMEM((tm, ch), F32)],
        compiler_params=_cparams(("parallel",)),
        name="cconv",
    )(proj, proj, proj, proj, proj, proj, jnp.repeat(w, SUBLANES, axis=0), b.reshape(1, ch),
      ln_g.reshape(1, ch),
      ln_b.reshape(1, ch))


def _sconv_kernel(x_ref, xp_ref, xn_ref, w_ref, b_ref, o_ref, hbuf_ref, sh_ref, *,
                  tiles_per_seq, halo, shifts, plan, rc):
    i = pl.program_id(0)
    tm = x_ref.shape[0]
    first = (i % tiles_per_seq) == 0
    last = (i % tiles_per_seq) == tiles_per_seq - 1
    hbuf_ref[0:halo, :] = jnp.where(first, 0.0, xp_ref[...])
    hbuf_ref[halo:halo + tm, :] = x_ref[...]
    hbuf_ref[halo + tm:2 * halo + tm, :] = jnp.where(last, 0.0, xn_ref[...])
    _fill_shifted(hbuf_ref, sh_ref, shifts, sh_ref.shape[1])

    def body(j, carry):
        r0 = j * rc
        h = _conv_rows(hbuf_ref, sh_ref, w_ref, plan, r0, rc) + b_ref[...]
        o_ref[pl.ds(pl.multiple_of(r0, rc), rc), :] = _silu(h)
        return carry

    lax.fori_loop(0, tm // rc, body, 0)


def _sconv(proj, col0, w, b, seq, tm):
    t = proj.shape[0]
    ch = w.shape[1]
    cw = 1024
    halo = SUBLANES
    rc = 16
    shifts, plan, qmax = _tap_plan(SSM_CONV, halo)
    tiles_per_seq = seq // tm
    hb = tm // halo
    nhb = t // halo
    c0 = col0 // cw
    kern = functools.partial(_sconv_kernel, tiles_per_seq=tiles_per_seq, halo=halo,
                             shifts=shifts, plan=plan, rc=rc)
    return pl.pallas_call(
        kern,
        grid=(t // tm, ch // cw),
        in_specs=[pl.BlockSpec((tm, cw), lambda i, j: (i, c0 + j)),
                  pl.BlockSpec((halo, cw), lambda i, j: (jnp.maximum(i * hb - 1, 0), c0 + j)),
                  pl.BlockSpec((halo, cw),
                               lambda i, j: (jnp.minimum((i + 1) * hb, nhb - 1), c0 + j)),
                  pl.BlockSpec((SSM_CONV * SUBLANES, cw), lambda i, j: (0, j)),
                  pl.BlockSpec((1, cw), lambda i, j: (0, j))],
        out_specs=pl.BlockSpec((tm, cw), lambda i, j: (i, j)),
        out_shape=jax.ShapeDtypeStruct((t, ch), F32),
        scratch_shapes=[pltpu.VMEM((tm + 2 * halo, cw), F32),
                        pltpu.VMEM((len(shifts), tm + qmax, cw), F32)],
        compiler_params=_cparams(("parallel", "parallel")),
        name="sconv",
    )(proj, proj, proj, jnp.repeat(w, SUBLANES, axis=0), b.reshape(1, ch))


def _split3(f):
    hi = f.astype(BF16)
    r1 = f - hi.astype(F32)
    mid = r1.astype(BF16)
    lo = (r1 - mid.astype(F32)).astype(BF16)
    return hi, mid, lo


def _expand_heads(f, e):
    hi, mid, lo = _split3(f)
    return (jnp.dot(hi, e, preferred_element_type=F32)
            + jnp.dot(mid, e, preferred_element_type=F32)
            + jnp.dot(lo, e, preferred_element_type=F32))


def _ssd_kernel(x_ref, b_ref, c_ref, dt_ref, dtb_ref, alog_ref, e_ref, o_ref, state_ref,
                acs_s, acst_s, dtt_s, wstt_s, carry_s, cb_s, bgt_s, *, reverse, lane0):
    q = x_ref.shape[0]
    nheads = x_ref.shape[1] // SSM_HEAD_DIM
    hpg = nheads // SSM_GROUPS

    @pl.when(pl.program_id(1) == 0)
    def _():
        state_ref[...] = jnp.zeros_like(state_ref)

    row = lax.broadcasted_iota(jnp.int32, (q, q), 0)
    col = lax.broadcasted_iota(jnp.int32, (q, q), 1)
    keep = (col >= row) if reverse else (col <= row)

    z = dt_ref[...] + dtb_ref[...]
    dt = jnp.maximum(z, 0.0) + jnp.log(1.0 + jnp.exp(-jnp.abs(z)))
    a = dt * (-jnp.exp(alog_ref[...]))
    acs = jnp.dot(keep.astype(F32), a, precision=HIGHEST, preferred_element_type=F32)
    tot = acs[0:1, :] if reverse else acs[q - 1:q, :]
    acs2 = acs * LOG2E
    acs_s[...] = acs2
    acst_s[...] = acs2.T
    dtt_s[...] = dt.T
    wstt_s[...] = (dt * jnp.exp(tot - acs)).T
    carry_s[...] = _expand_heads(jnp.broadcast_to(jnp.exp(tot), (SUBLANES, LANES)), e_ref[...])

    for g in range(SSM_GROUPS):
        gsl = slice(g * SSM_STATE, (g + 1) * SSM_STATE)
        cb_s[g] = lax.dot_general(c_ref[:, gsl].astype(BF16), b_ref[:, gsl].astype(BF16),
                                  (((1,), (1,)), ((), ())), preferred_element_type=F32)
        bgt_s[g] = b_ref[:, gsl].T
        for pr in range(hpg // 2):
            ps = slice((g * hpg + 2 * pr) * SSM_HEAD_DIM, (g * hpg + 2 * pr + 2) * SSM_HEAD_DIM)
            xp = x_ref[:, ps].astype(BF16)
            rhs = jnp.concatenate([xp, state_ref[:, ps].astype(BF16)], axis=0)
            for sub in range(2):
                hl = lane0 + g * hpg + 2 * pr + sub
                half = slice(sub * SSM_HEAD_DIM, (sub + 1) * SSM_HEAD_DIM)
                hs = slice(ps.start + half.start, ps.start + half.stop)
                colb = jnp.broadcast_to(acs_s[:, hl:hl + 1], (q, q))
                seg = colb - acst_s[hl:hl + 1, :]
                lmat = jnp.exp2(jnp.where(keep, seg, -jnp.inf))
                m = (cb_s[g] * lmat * dtt_s[hl:hl + 1, :]).astype(BF16)
                cexp = (c_ref[:, gsl] * jnp.exp2(colb)).astype(BF16)
                y = jnp.dot(jnp.concatenate([m, cexp], axis=1), rhs,
                            preferred_element_type=F32)
                o_ref[:, hs] = y[:, half]
                bw = (bgt_s[g] * wstt_s[hl:hl + 1, :]).astype(BF16)
                upd = jnp.dot(bw, xp, preferred_element_type=F32)
                state_ref[:, hs] = state_ref[:, hs] * carry_s[0:1, hs] + upd[:, half]


def _ssd(xbc, proj, dt_col_block, dt_bias, a_log, expand, bsz, seq, reverse, lane0):
    t = xbc.shape[0]
    q = SSM_CHUNK
    nc = seq // q
    inner = expand.shape[1]
    gn = SSM_GROUPS * SSM_STATE
    b_blk = inner // gn
    c_blk = b_blk + 1

    def tok(b, c):
        return b * nc + (nc - 1 - c if reverse else c)

    kern = functools.partial(_ssd_kernel, reverse=reverse, lane0=lane0)
    return pl.pallas_call(
        kern,
        grid=(bsz, nc),
        in_specs=[pl.BlockSpec((q, inner), lambda b, c: (tok(b, c), 0)),
                  pl.BlockSpec((q, gn), lambda b, c: (tok(b, c), b_blk)),
                  pl.BlockSpec((q, gn), lambda b, c: (tok(b, c), c_blk)),
                  pl.BlockSpec((q, LANES), lambda b, c: (tok(b, c), dt_col_block)),
                  pl.BlockSpec((1, LANES), lambda b, c: (0, 0)),
                  pl.BlockSpec((1, LANES), lambda b, c: (0, 0)),
                  pl.BlockSpec((LANES, inner), lambda b, c: (0, 0))],
        out_specs=pl.BlockSpec((q, inner), lambda b, c: (tok(b, c), 0)),
        out_shape=jax.ShapeDtypeStruct((t, inner), F32),
        scratch_shapes=[pltpu.VMEM((SSM_STATE, inner), F32),
                        pltpu.VMEM((q, LANES), F32), pltpu.VMEM((LANES, q), F32),
                        pltpu.VMEM((LANES, q), F32), pltpu.VMEM((LANES, q), F32),
                        pltpu.VMEM((SUBLANES, inner), F32),
                        pltpu.VMEM((SSM_GROUPS, q, q), F32),
                        pltpu.VMEM((SSM_GROUPS, SSM_STATE, q), F32)],
        compiler_params=_cparams(("arbitrary", "arbitrary")),
        name="ssd_bwd" if reverse else "ssd_fwd",
    )(xbc, xbc, xbc, proj, dt_bias, a_log, expand)


def _merge_kernel(yf_ref, yb_ref, xc_ref, z_ref, hc_ref, gc_ref, gs_ref, x_ref,
                  dexp_ref, ng_ref, wso_ref, wco_ref, bco_ref, wo_ref, g1_ref,
                  scale_ref, shift_ref, wr_ref, br_ref,
                  x1_ref, v_ref, lg_ref, *, ngroups, halves):
    tm, inner = yf_ref.shape
    gw = inner // ngroups
    hm = tm // halves
    for h in range(halves):
        r = slice(h * hm, (h + 1) * hm)
        y = (yf_ref[r, :] + yb_ref[r, :] + xc_ref[r, :] * dexp_ref[...]) * _silu(z_ref[r, :])
        parts = []
        for g in range(ngroups):
            yg = y[:, g * gw:(g + 1) * gw]
            ms = jnp.mean(yg * yg, axis=-1, keepdims=True)
            parts.append((yg * lax.rsqrt(ms + EPS)) * ng_ref[:, g * gw:(g + 1) * gw])
        ysn = jnp.concatenate(parts, axis=-1).astype(BF16)
        y_ssm = jnp.dot(ysn, wso_ref[...], preferred_element_type=F32)
        y_conf = jnp.dot(hc_ref[r, :], wco_ref[...], preferred_element_type=F32) + bco_ref[...]
        merged = _sigmoid(gc_ref[r, :]) * y_conf + _sigmoid(gs_ref[r, :]) * y_ssm
        o = jnp.dot(merged.astype(BF16), wo_ref[...], preferred_element_type=F32)
        x1 = x_ref[r, :] + g1_ref[0] * o
        x1_ref[r, :] = x1
        v = _modulated_norm(x1, scale_ref[0], shift_ref[0])
        v_ref[r, :] = v
        v_hi = v.astype(BF16)
        v_lo = (v - v_hi.astype(F32)).astype(BF16)
        p = (jnp.dot(v_hi, wr_ref[...], preferred_element_type=F32)
             + jnp.dot(v_lo, wr_ref[...], preferred_element_type=F32))
        lg_ref[r, :] = p[:, :LANES] + p[:, LANES:] + br_ref[...]


def _merge(yf, yb, xbc, proj, hc, x, cols, dexp, norm_g, wso, wco, bco, wo, g1, scale, shift,
           wr, br, seq, tm):
    t, d = x.shape
    inner = yf.shape[1]
    tiles_per_seq = seq // tm
    z_blk = cols["z"] // inner
    gc_blk = cols["gate_conf"] // d
    gs_blk = cols["gate_ssm"] // d

    def const(shape):
        return pl.BlockSpec(shape, lambda i: tuple(0 for _ in shape),
                            pipeline_mode=pl.Buffered(1))

    def per_seq():
        return pl.BlockSpec((1, 1, d), lambda i: (i // tiles_per_seq, 0, 0))

    kern = functools.partial(_merge_kernel, ngroups=SSM_GROUPS, halves=2)
    return pl.pallas_call(
        kern,
        grid=(t // tm,),
        in_specs=[pl.BlockSpec((tm, inner), lambda i: (i, 0)),
                  pl.BlockSpec((tm, inner), lambda i: (i, 0)),
                  pl.BlockSpec((tm, inner), lambda i: (i, 0)),
                  pl.BlockSpec((tm, inner), lambda i: (i, z_blk)),
                  pl.BlockSpec((tm, d), lambda i: (i, 0)),
                  pl.BlockSpec((tm, d), lambda i: (i, gc_blk)),
                  pl.BlockSpec((tm, d), lambda i: (i, gs_blk)),
                  pl.BlockSpec((tm, d), lambda i: (i, 0)),
                  const((1, inner)), const((1, inner)),
                  const((inner, d)), const((d, d)), const((1, d)), const((d, d)),
                  per_seq(), per_seq(), per_seq(),
                  const((d, 2 * LANES)), const((1, LANES))],
        out_specs=[pl.BlockSpec((tm, d), lambda i: (i, 0)),
                   pl.BlockSpec((tm, d), lambda i: (i, 0)),
                   pl.BlockSpec((tm, LANES), lambda i: (i, 0))],
        out_shape=[jax.ShapeDtypeStruct((t, d), F32),
                   jax.ShapeDtypeStruct((t, d), F32),
                   jax.ShapeDtypeStruct((t, LANES), F32)],
        compiler_params=_cparams(("parallel",)),
        name="merge",
    )(yf, yb, xbc, proj, hc, proj, proj, x, dexp, norm_g, wso, wco, bco, wo, g1, scale, shift,
      wr, br)


ROUTE_TILE = 256


def _route_kernel(lg_ref, idx_ref, w_ref, lslot_ref, lslot_t_ref, cnt_ref, *, n_experts):
    tm = lg_ref.shape[0]
    lane = lax.broadcasted_iota(jnp.int32, (tm, LANES), 1)
    lg = jnp.where(lane < n_experts, lg_ref[...], -jnp.inf)
    sel = jnp.zeros((tm, LANES), F32)
    vals, idxs = [], []
    for _ in range(TOP_K):
        m = jnp.max(lg, axis=-1, keepdims=True)
        ix = jnp.min(jnp.where(lg == m, lane, LANES), axis=-1, keepdims=True)
        hit = lane == ix
        sel = jnp.where(hit, 1.0, sel)
        lg = jnp.where(hit, -jnp.inf, lg)
        vals.append(m)
        idxs.append(ix)
    ex = [jnp.exp(v - vals[0]) for v in vals]
    den = ex[0] + ex[1] + ex[2] + ex[3]
    row = lax.broadcasted_iota(jnp.int32, (tm, tm), 0)
    col = lax.broadcasted_iota(jnp.int32, (tm, tm), 1)
    rank = jnp.dot((col < row).astype(BF16), sel.astype(BF16), preferred_element_type=F32)
    cnt = jnp.sum(sel, axis=0, keepdims=True)
    erow = lax.broadcasted_iota(jnp.int32, (LANES, LANES), 0)
    ecol = lax.broadcasted_iota(jnp.int32, (LANES, LANES), 1)
    off = jnp.dot(jnp.broadcast_to(cnt, (SUBLANES, LANES)).astype(BF16),
                  (erow < ecol).astype(BF16), preferred_element_type=F32)[0:1, :]
    slot_of = rank + off
    idx_out = jnp.zeros((tm, LANES), jnp.int32)
    w_out = jnp.zeros((tm, LANES), F32)
    slot_out = jnp.zeros((tm, LANES), F32)
    for k in range(TOP_K):
        sk = jnp.sum(jnp.where(lane == idxs[k], slot_of, 0.0), axis=-1, keepdims=True)
        idx_out = jnp.where(lane == k, idxs[k], idx_out)
        w_out = jnp.where(lane == k, ex[k] / den, w_out)
        slot_out = jnp.where(lane == k, sk, slot_out)
    idx_ref[...] = idx_out
    w_ref[...] = w_out
    lslot_ref[...] = slot_out.astype(jnp.int32)
    lslot_t_ref[...] = slot_out.T[0:SUBLANES, :].astype(jnp.int32)
    cnt_ref[0] = jnp.broadcast_to(cnt, (SUBLANES, LANES)).astype(jnp.int32)


def _route(logits, n_experts):
    t = logits.shape[0]
    tm = ROUTE_TILE
    nt = t // tm
    kern = functools.partial(_route_kernel, n_experts=n_experts)
    return pl.pallas_call(
        kern,
        grid=(nt,),
        in_specs=[pl.BlockSpec((tm, LANES), lambda i: (i, 0))],
        out_specs=[pl.BlockSpec((tm, LANES), lambda i: (i, 0)),
                   pl.BlockSpec((tm, LANES), lambda i: (i, 0)),
                   pl.BlockSpec((tm, LANES), lambda i: (i, 0)),
                   pl.BlockSpec((SUBLANES, tm), lambda i: (i, 0)),
                   pl.BlockSpec((1, SUBLANES, LANES), lambda i: (i, 0, 0))],
        out_shape=[jax.ShapeDtypeStruct((t, LANES), jnp.int32),
                   jax.ShapeDtypeStruct((t, LANES), F32),
                   jax.ShapeDtypeStruct((t, LANES), jnp.int32),
                   jax.ShapeDtypeStruct((nt * SUBLANES, tm), jnp.int32),
                   jax.ShapeDtypeStruct((nt, SUBLANES, LANES), jnp.int32)],
        compiler_params=_cparams(("parallel",)),
        name="route",
    )(logits)


def _row_copy(src, src_row, dst, dst_row, sem, nch):
    return pltpu.make_async_copy(_row(src, src_row, nch), _row(dst, dst_row, nch), sem)


def _segment_copies(cnt_ref, loc_ref, glob_ref, tile, n_experts, local_buf, hbm, sem, nch,
                    to_hbm, wait):
    max_bit = (ROUTE_TILE).bit_length() - 1
    split_bit = max_bit - 2

    def body(e, c):
        n = cnt_ref[tile * n_experts + e]
        lo = loc_ref[tile * n_experts + e]
        go = glob_ref[tile * n_experts + e]
        def copy_bits(bits):
            for bit in bits:
                size = 1 << bit
                done = (n >> (bit + 1)) << (bit + 1)

                @pl.when((n & size) != 0)
                def _():
                    loc = local_buf.at[pl.ds(pl.multiple_of((lo + done) * nch, nch), size * nch),
                                       :]
                    glb = hbm.at[pl.ds(pl.multiple_of((go + done) * nch, nch), size * nch), :]
                    cp = pltpu.make_async_copy(loc, glb, sem) if to_hbm else \
                        pltpu.make_async_copy(glb, loc, sem)
                    if wait:
                        cp.wait()
                    else:
                        cp.start()

        pl.when(n >= (1 << split_bit))(lambda: copy_bits(range(max_bit, split_bit - 1, -1)))
        copy_bits(range(split_bit - 1, -1, -1))
        return c

    lax.fori_loop(0, n_experts, body, 0)


def _dispatch_kernel(cnt_ref, loc_ref, glob_ref, fill_ref, end_ref, lslot_t_ref, v_ref, xs_hbm,
                     buf, sems, fsem, *, n_experts, nch):
    i = pl.program_id(0)
    n = pl.num_programs(0)
    tm = v_ref.shape[0]
    nslots = tm * TOP_K
    slot = i % 2
    srow = lax.broadcasted_iota(jnp.int32, (nslots, tm), 0)
    hit = srow == lslot_t_ref[0:1, :]
    for k in range(1, TOP_K):
        hit = jnp.logical_or(hit, srow == lslot_t_ref[k:k + 1, :])
    p = jnp.where(hit, 1.0, 0.0).astype(BF16)
    v = v_ref[...]
    v_hi = v.astype(BF16)
    v_lo = (v - v_hi.astype(F32)).astype(BF16)
    xl = (jnp.dot(p, v_hi, preferred_element_type=F32)
          + jnp.dot(p, v_lo, preferred_element_type=F32))
    for s_ in range(nch):
        buf[slot, pl.ds(s_, nslots, stride=nch), :] = xl[:, s_ * LANES:(s_ + 1) * LANES]

    @pl.when(i == 0)
    def _():
        def fill(s, c):
            _row_copy(buf.at[0], 0, xs_hbm, s, fsem, nch).start()
            return c

        def drain(s, c):
            _row_copy(buf.at[0], 0, xs_hbm, s, fsem, nch).wait()
            return c

        group = 8
        for e0 in range(0, n_experts, group):
            for e in range(e0, min(e0 + group, n_experts)):
                lax.fori_loop(fill_ref[e], end_ref[e], fill, 0)
            for e in range(e0, min(e0 + group, n_experts)):
                lax.fori_loop(fill_ref[e], end_ref[e], drain, 0)

    args = (cnt_ref, loc_ref, glob_ref)
    _segment_copies(*args, i, n_experts, buf.at[slot], xs_hbm, sems.at[slot], nch, True, False)

    @pl.when(i > 0)
    def _():
        _segment_copies(*args, i - 1, n_experts, buf.at[1 - slot], xs_hbm, sems.at[1 - slot],
                        nch, True, True)

    @pl.when(i == n - 1)
    def _():
        _segment_copies(*args, i, n_experts, buf.at[slot], xs_hbm, sems.at[slot], nch, True,
                        True)


def _dispatch(cnt_flat, loc_flat, glob_flat, fill_start, pad_end, lslot_t, v, n_slots):
    t, d = v.shape
    nch = d // LANES
    tm = ROUTE_TILE
    n_experts = fill_start.shape[0]
    kern = functools.partial(_dispatch_kernel, n_experts=n_experts, nch=nch)
    grid_spec = pltpu.PrefetchScalarGridSpec(
        num_scalar_prefetch=5,
        grid=(t // tm,),
        in_specs=[pl.BlockSpec((SUBLANES, tm), lambda i, *_: (i, 0)),
                  pl.BlockSpec((tm, d), lambda i, *_: (i, 0))],
        out_specs=pl.BlockSpec(memory_space=pl.ANY),
        scratch_shapes=[pltpu.VMEM((2, tm * TOP_K * nch, LANES), F32),
                        pltpu.SemaphoreType.DMA((2,)), pltpu.SemaphoreType.DMA],
    )
    return pl.pallas_call(
        kern,
        grid_spec=grid_spec,
        out_shape=jax.ShapeDtypeStruct((n_slots * nch, LANES), v.dtype),
        compiler_params=_cparams(("arbitrary",)),
        name="dispatch",
    )(cnt_flat, loc_flat, glob_flat, fill_start, pad_end, lslot_t, v)


def _moe_kernel(be_ref, nused_ref, x_ref, wgu_ref, bgu_ref, wd_ref, bd_ref, o_ref,
                wgu_bf, wd_bf):
    b = pl.program_id(0)
    changed = jnp.logical_or(b == 0, be_ref[b] != be_ref[jnp.maximum(b - 1, 0)])
    active = b < nused_ref[0]

    @pl.when(jnp.logical_and(active, changed))
    def _():
        wgu_bf[...] = wgu_ref[0].astype(BF16)
        wd_bf[...] = wd_ref[0].astype(BF16)

    @pl.when(active)
    def _():
        ff = wd_bf.shape[0]
        d = wgu_bf.shape[0]
        nch = d // LANES
        hm = x_ref.shape[0] // nch // MOE_HALVES
        for h in range(MOE_HALVES):
            x = _load_rows(x_ref, hm, nch, h * hm).astype(BF16)
            gu = jnp.dot(x, wgu_bf[...], preferred_element_type=F32) + bgu_ref[0]
            gate = jnp.minimum(gu[:, :ff], SWIGLU_LIMIT)
            up = jnp.clip(gu[:, ff:], -SWIGLU_LIMIT, SWIGLU_LIMIT)
            glu = gate * _sigmoid(SWIGLU_ALPHA * gate)
            hid = ((up + 1.0) * glu).astype(BF16)
            y = jnp.dot(hid, wd_bf[...], preferred_element_type=F32) + bd_ref[0]
            _store_rows(o_ref, y, h * hm)


def _moe(block_expert, nused, xs, w_gu, b_gu, w_down, b_down):
    ne, d, ff2 = w_gu.shape
    nch = d // LANES
    n_slots = xs.shape[0] // nch
    ff = w_down.shape[1]
    bm = MOE_BLOCK

    def blk(b, nu):
        return jnp.minimum(b, nu[0] - 1)

    grid_spec = pltpu.PrefetchScalarGridSpec(
        num_scalar_prefetch=2,
        grid=(n_slots // bm,),
        in_specs=[pl.BlockSpec((bm * nch, LANES), lambda b, be, nu: (blk(b, nu), 0)),
                  pl.BlockSpec((1, d, ff2), lambda b, be, nu: (be[blk(b, nu)], 0, 0)),
                  pl.BlockSpec((1, 1, ff2), lambda b, be, nu: (be[blk(b, nu)], 0, 0)),
                  pl.BlockSpec((1, ff, d), lambda b, be, nu: (be[blk(b, nu)], 0, 0)),
                  pl.BlockSpec((1, 1, d), lambda b, be, nu: (be[blk(b, nu)], 0, 0))],
        out_specs=pl.BlockSpec((bm * nch, LANES), lambda b, be, nu: (blk(b, nu), 0)),
        scratch_shapes=[pltpu.VMEM((d, ff2), BF16), pltpu.VMEM((ff, d), BF16)],
    )
    return pl.pallas_call(
        _moe_kernel,
        grid_spec=grid_spec,
        out_shape=jax.ShapeDtypeStruct(xs.shape, F32),
        compiler_params=_cparams(("arbitrary",)),
        name="moe",
    )(block_expert, nused, xs, w_gu, b_gu.reshape(ne, 1, ff2), w_down, b_down.reshape(ne, 1, d))


def _final_kernel(cnt_ref, loc_ref, glob_ref, x1_ref, lslot_ref, w_ref, g2_ref, scale_ref,
                  shift_ref, ys_hbm, o_ref, buf, sems, *, n_experts, nch):
    i = pl.program_id(0)
    n = pl.num_programs(0)
    tm = x1_ref.shape[0]
    nslots = tm * TOP_K
    slot = i % 2
    args = (cnt_ref, loc_ref, glob_ref)

    @pl.when(i == 0)
    def _():
        _segment_copies(*args, 0, n_experts, buf.at[0], ys_hbm, sems.at[0], nch, False, False)

    @pl.when(i + 1 < n)
    def _():
        _segment_copies(*args, i + 1, n_experts, buf.at[1 - slot], ys_hbm, sems.at[1 - slot], nch,
                        False, False)

    _segment_copies(*args, i, n_experts, buf.at[slot], ys_hbm, sems.at[slot], nch, False, True)

    lane_slot = lax.broadcasted_iota(jnp.int32, (tm, nslots), 1)
    pw = jnp.zeros((tm, nslots), F32)
    for k in range(TOP_K):
        pw = jnp.where(lane_slot == lslot_ref[:, k:k + 1], w_ref[:, k:k + 1], pw)
    pw_hi = pw.astype(BF16)
    pw_lo = (pw - pw_hi.astype(F32)).astype(BF16)
    y = _load_rows(buf, nslots, nch, 0, (slot,))
    y_hi = y.astype(BF16)
    y_lo = (y - y_hi.astype(F32)).astype(BF16)
    f = (jnp.dot(pw_hi, y_hi, preferred_element_type=F32)
         + jnp.dot(pw_hi, y_lo, preferred_element_type=F32)
         + jnp.dot(pw_lo, y_hi, preferred_element_type=F32))
    x2 = x1_ref[...] + g2_ref[0] * f
    o_ref[...] = _modulated_norm(x2, scale_ref[0], shift_ref[0])


def _final(cnt_flat, loc_flat, glob_flat, n_experts, x1, ys, lslot, top_w, g2, scale, shift, seq):
    t, d = x1.shape
    nch = d // LANES
    tm = ROUTE_TILE
    tiles_per_seq = seq // tm

    def per_seq():
        return pl.BlockSpec((1, 1, d), lambda i, *_: (i // tiles_per_seq, 0, 0))

    grid_spec = pltpu.PrefetchScalarGridSpec(
        num_scalar_prefetch=3,
        grid=(t // tm,),
        in_specs=[pl.BlockSpec((tm, d), lambda i, *_: (i, 0)),
                  pl.BlockSpec((tm, LANES), lambda i, *_: (i, 0)),
                  pl.BlockSpec((tm, LANES), lambda i, *_: (i, 0)),
                  per_seq(), per_seq(), per_seq(),
                  pl.BlockSpec(memory_space=pl.ANY)],
        out_specs=pl.BlockSpec((tm, d), lambda i, *_: (i, 0)),
        scratch_shapes=[pltpu.VMEM((2, TOP_K * tm * nch, LANES), F32),
                        pltpu.SemaphoreType.DMA((2,))],
    )
    return pl.pallas_call(
        functools.partial(_final_kernel, n_experts=n_experts, nch=nch),
        grid_spec=grid_spec,
        out_shape=jax.ShapeDtypeStruct((t, d), F32),
        compiler_params=_cparams(("arbitrary",)),
        name="final",
    )(cnt_flat, loc_flat, glob_flat, x1, lslot, top_w, g2, scale, shift, ys)


def _pad_cols(w, n):
    return jnp.pad(w, ((0, 0), (0, n - w.shape[1])))


def kernel(x, c, ada_w, ada_b, norm_mix_g, w_in, conf_dw_w, conf_dw_b, conf_ln_g, conf_ln_b,
           conf_out_w, conf_out_b, ssm_conv_w, ssm_conv_b, dt_bias_f, dt_bias_b, a_log_f,
           a_log_b, ssm_d, ssm_norm_g, ssm_out_w, w_o, norm_ffn_g, router_w, router_b, w_gu,
           b_gu, w_down, b_down, final_ada_w, final_ada_b, final_norm_g):
    bsz, seq, d = x.shape
    depth = ada_w.shape[0]
    t = bsz * seq
    nheads = a_log_f.shape[1]
    inner = nheads * SSM_HEAD_DIM
    gn = SSM_GROUPS * SSM_STATE
    conf = conf_dw_w.shape[2]
    n_experts = router_w.shape[2]
    assert 2 * nheads <= LANES and n_experts <= LANES

    c_pad = jnp.zeros((SUBLANES, d), F32).at[:bsz].set(c)
    fin = _ada(c_pad, final_ada_w, final_ada_b)[:bsz]
    xf = x.reshape(t, d)

    sizes = [("conf_a", conf), ("conf_g", conf), ("z", inner), ("xs", inner), ("bm", gn),
             ("cm", gn), ("gate_conf", d), ("gate_ssm", d), ("dt", LANES)]
    cols, off = {}, 0
    for name, n in sizes:
        cols[name] = off
        off += n
    src = {}
    o = 0
    for name, n in [("conf_a", conf), ("conf_g", conf), ("z", inner), ("xs", inner), ("bm", gn),
                    ("cm", gn), ("dtf", nheads), ("dtb", nheads), ("gate_conf", d),
                    ("gate_ssm", d)]:
        src[name] = (o, o + n)
        o += n

    head_of_col = jnp.arange(inner, dtype=jnp.int32) // SSM_HEAD_DIM
    lanes = jnp.arange(LANES, dtype=jnp.int32)[:, None]
    expand_f = (lanes == head_of_col[None, :]).astype(BF16)
    expand_b = (lanes == head_of_col[None, :] + nheads).astype(BF16)

    for l in range(depth):
        ada = _ada(c_pad, ada_w[l], ada_b[l])[:bsz]
        sh1, sc1, g1, sh2, sc2, g2 = [a.reshape(bsz, 1, d) for a in jnp.split(ada, 6, axis=-1)]
        scale1 = norm_mix_g[l][None, None, :] * (1.0 + sc1)
        scale2 = norm_ffn_g[l][None, None, :] * (1.0 + sc2)

        wl = w_in[l]
        w_dt = _pad_cols(jnp.concatenate([wl[:, slice(*src["dtf"])], wl[:, slice(*src["dtb"])]],
                                         axis=1), LANES)
        w_perm = jnp.concatenate(
            [wl[:, slice(*src[n])] for n in ("conf_a", "conf_g", "z", "xs", "bm", "cm",
                                             "gate_conf", "gate_ssm")] + [w_dt],
            axis=1).astype(BF16)
        proj = _inproj(xf, scale1, sh1, w_perm, seq, tm=256, col_splits=1)

        hc = _cconv(proj, conf_dw_w[l], conf_dw_b[l], conf_ln_g[l], conf_ln_b[l], seq, tm=256)
        xbc = _sconv(proj, cols["xs"], ssm_conv_w[l], ssm_conv_b[l], seq, tm=256)

        dt_bias = _pad_cols(jnp.concatenate([dt_bias_f[l], dt_bias_b[l]])[None, :], LANES)
        a_log = _pad_cols(jnp.concatenate([a_log_f[l], a_log_b[l]])[None, :], LANES)
        dt_blk = cols["dt"] // LANES
        y_f = _ssd(xbc, proj, dt_blk, dt_bias, a_log, expand_f, bsz, seq, False, 0)
        y_b = _ssd(xbc, proj, dt_blk, dt_bias, a_log, expand_b, bsz, seq, True, nheads)

        dexp = jnp.repeat(ssm_d[l], SSM_HEAD_DIM)[None, :]
        wr = _pad_cols(router_w[l], LANES)
        wr_hi = wr.astype(BF16)
        wr = jnp.concatenate([wr_hi, (wr - wr_hi.astype(F32)).astype(BF16)], axis=1)
        br = _pad_cols(router_b[l][None, :], LANES)
        x1, v, logits = _merge(
            y_f, y_b, xbc, proj, hc, xf, cols, dexp, ssm_norm_g[l][None, :],
            ssm_out_w[l].astype(BF16), conf_out_w[l].astype(BF16), conf_out_b[l][None, :],
            w_o[l].astype(BF16), g1, scale2, sh2, wr, br, seq, tm=256)

        top_i, top_w, lslot, lslot_t, tile_cnt = _route(logits, n_experts)
        cnt2 = tile_cnt[:, 0, :n_experts]
        counts = jnp.sum(cnt2, axis=0)
        bm = MOE_BLOCK
        n_blocks = (t * TOP_K) // bm + n_experts
        padded = ((counts + bm - 1) // bm) * bm
        pad_end = jnp.cumsum(padded)
        pad_start = pad_end - padded
        loc2 = jnp.cumsum(cnt2, axis=1) - cnt2
        glob2 = pad_start[None, :] + jnp.cumsum(cnt2, axis=0) - cnt2
        tables = (cnt2.reshape(-1), loc2.reshape(-1), glob2.reshape(-1))
        block_first = jnp.arange(n_blocks, dtype=jnp.int32) * bm
        block_expert = jnp.minimum(
            jnp.sum((pad_end[None, :] <= block_first[:, None]).astype(jnp.int32), axis=1),
            n_experts - 1).astype(jnp.int32)
        nused = (pad_end[-1] // bm).astype(jnp.int32).reshape(1)
        xs = _dispatch(*tables, (pad_start + counts).astype(jnp.int32),
                       pad_end.astype(jnp.int32), lslot_t, v, n_blocks * bm)
        ys = _moe(block_expert, nused, xs, w_gu[l], b_gu[l], w_down[l], b_down[l])

        if l == depth - 1:
            sh_f, sc_f = [a.reshape(bsz, 1, d) for a in jnp.split(fin, 2, axis=-1)]
            scale_f = final_norm_g[None, None, :] * (1.0 + sc_f)
            xf = _final(*tables, n_experts, x1, ys, lslot, top_w, g2, scale_f, sh_f, seq)
        else:
            raise NotImplementedError("depth > 1 is not wired")
    return xf.reshape(bsz, seq, d)
```

```python
import functools

import jax
import jax.numpy as jnp
from jax import lax
from jax.experimental import pallas as pl
from jax.experimental.pallas import tpu as pltpu

F32 = jnp.float32
BF16 = jnp.bfloat16
HIGHEST = lax.Precision.HIGHEST

EPS = 1e-6
LOG2E = 1.4426950408889634
CONF_KERNEL = 31
SSM_CONV = 5
SSM_HEAD_DIM = 64
SSM_GROUPS = 4
SSM_STATE = 128
SSM_CHUNK = 128
TOP_K = 4
SWIGLU_ALPHA = 1.702
SWIGLU_LIMIT = 7.0

LANES = 128
SUBLANES = 8
VMEM_LIMIT = 56 * 1024 * 1024

MOE_BLOCK = 512
MOE_HALVES = 2


def _cparams(sem):
    return pltpu.CompilerParams(dimension_semantics=sem, vmem_limit_bytes=VMEM_LIMIT)


def _store_rows(ref, val, row0=0):
    n, d = val.shape
    nch = d // LANES
    for s in range(nch):
        ref[pl.ds(row0 * nch + s, n, stride=nch), :] = val[:, s * LANES:(s + 1) * LANES]


def _load_rows(ref, n, nch, row0=0, lead=()):
    return jnp.concatenate(
        [ref[lead + (pl.ds(row0 * nch + s, n, stride=nch), slice(None))] for s in range(nch)],
        axis=-1)


def _row(ref, r, nch, lead=()):
    return ref.at[lead + (pl.ds(pl.multiple_of(r * nch, nch), nch), slice(None))]


def _sigmoid(x):
    return 1.0 / (1.0 + jnp.exp(-x))


def _silu(x):
    return x * _sigmoid(x)


def _ada_kernel(c_ref, w_ref, b_ref, o_ref):
    c = c_ref[...]
    o_ref[...] = jnp.dot(_silu(c), w_ref[...], precision=HIGHEST,
                         preferred_element_type=F32) + b_ref[...]


def _ada(c_pad, w, b):
    d, n = w.shape
    tn = 1024
    return pl.pallas_call(
        _ada_kernel,
        grid=(n // tn,),
        in_specs=[pl.BlockSpec((SUBLANES, d), lambda j: (0, 0)),
                  pl.BlockSpec((d, tn), lambda j: (0, j)),
                  pl.BlockSpec((1, tn), lambda j: (0, j))],
        out_specs=pl.BlockSpec((SUBLANES, tn), lambda j: (0, j)),
        out_shape=jax.ShapeDtypeStruct((SUBLANES, n), F32),
        compiler_params=_cparams(("parallel",)),
        name="ada",
    )(c_pad, w, b.reshape(1, n))


def _modulated_norm(x, scale, shift):
    ms = jnp.mean(x * x, axis=-1, keepdims=True)
    return (x * lax.rsqrt(ms + EPS)) * scale + shift


def _inproj_kernel(x_ref, scale_ref, shift_ref, w_ref, o_ref, *, col_chunk):
    u = _modulated_norm(x_ref[...], scale_ref[0], shift_ref[0]).astype(BF16)
    ncols = o_ref.shape[1]
    for c0 in range(0, ncols, col_chunk):
        c1 = min(c0 + col_chunk, ncols)
        o_ref[:, c0:c1] = jnp.dot(u, w_ref[:, c0:c1], preferred_element_type=F32)


def _inproj(x, scale, shift, w_bf, seq, tm, col_splits):
    t, d = x.shape
    ncols = w_bf.shape[1]
    nw = ncols // col_splits
    tiles_per_seq = seq // tm
    return pl.pallas_call(
        functools.partial(_inproj_kernel, col_chunk=1024),
        grid=(col_splits, t // tm),
        in_specs=[pl.BlockSpec((tm, d), lambda j, i: (i, 0)),
                  pl.BlockSpec((1, 1, d), lambda j, i: (i // tiles_per_seq, 0, 0)),
                  pl.BlockSpec((1, 1, d), lambda j, i: (i // tiles_per_seq, 0, 0)),
                  pl.BlockSpec((d, nw), lambda j, i: (0, j), pipeline_mode=pl.Buffered(1))],
        out_specs=pl.BlockSpec((tm, nw), lambda j, i: (i, j)),
        out_shape=jax.ShapeDtypeStruct((t, ncols), F32),
        compiler_params=_cparams(("parallel", "parallel")),
        name="inproj",
    )(x, scale, shift, w_bf)


def _fill_shifted(hbuf_ref, sh_ref, shifts, rows):
    for n, r in enumerate(shifts):
        sh_ref[n] = hbuf_ref[r:r + rows, :]


def _tap_plan(ktaps, halo):
    offs = [k + halo - ktaps // 2 for k in range(ktaps)]
    shifts = sorted({o % SUBLANES for o in offs} - {0})
    plan = [(shifts.index(o % SUBLANES) if o % SUBLANES else -1, (o // SUBLANES) * SUBLANES)
            for o in offs]
    return shifts, plan, max(o // SUBLANES for o in offs) * SUBLANES


def _conv_rows(hbuf_ref, sh_ref, w8_ref, plan, r0, rc):
    groups = rc // SUBLANES
    accs = [None] * groups
    for k, (n, q8) in enumerate(plan):
        w = w8_ref[k * SUBLANES:(k + 1) * SUBLANES, :]
        for j in range(groups):
            start = pl.multiple_of(r0 + q8 + j * SUBLANES, SUBLANES)
            rows = pl.ds(start, SUBLANES)
            src = hbuf_ref[rows, :] if n < 0 else sh_ref[n, rows, :]
            term = src * w
            accs[j] = term if accs[j] is None else accs[j] + term
    return jnp.concatenate(accs, axis=0)


def _cconv_kernel(a_ref, g_ref, ap_ref, gp_ref, an_ref, gn_ref, w_ref, b_ref, lg_ref, lb_ref,
                  o_ref, hbuf_ref, sh_ref, cbuf_ref, *, tiles_per_seq, halo, shifts, plan, rc,
                  ln_rows):
    i = pl.program_id(0)
    tm = a_ref.shape[0]
    first = (i % tiles_per_seq) == 0
    last = (i % tiles_per_seq) == tiles_per_seq - 1
    glu_p = ap_ref[...] * _sigmoid(gp_ref[...])
    glu_n = an_ref[...] * _sigmoid(gn_ref[...])
    hbuf_ref[0:halo, :] = jnp.where(first, 0.0, glu_p)
    hbuf_ref[halo:halo + tm, :] = a_ref[...] * _sigmoid(g_ref[...])
    hbuf_ref[halo + tm:2 * halo + tm, :] = jnp.where(last, 0.0, glu_n)
    _fill_shifted(hbuf_ref, sh_ref, shifts, sh_ref.shape[1])

    def body(j, carry):
        r0 = j * rc
        h = _conv_rows(hbuf_ref, sh_ref, w_ref, plan, r0, rc) + b_ref[...]
        cbuf_ref[pl.ds(pl.multiple_of(r0, rc), rc), :] = h
        return carry

    lax.fori_loop(0, tm // rc, body, 0)

    for c0 in range(0, tm, ln_rows):
        h = cbuf_ref[c0:c0 + ln_rows, :]
        mu = jnp.mean(h, axis=-1, keepdims=True)
        hc = h - mu
        var = jnp.mean(hc * hc, axis=-1, keepdims=True)
        y = hc * lax.rsqrt(var + EPS) * lg_ref[...] + lb_ref[...]
        o_ref[c0:c0 + ln_rows, :] = _silu(y).astype(o_ref.dtype)


def _cconv(proj, w, b, ln_g, ln_b, seq, tm):
    t = proj.shape[0]
    ch = w.shape[1]
    halo = 16
    rc = 16
    shifts, plan, qmax = _tap_plan(CONF_KERNEL, halo)
    tiles_per_seq = seq // tm
    hb = tm // halo
    nhb = t // halo
    cb = 1

    def prev_map(col):
        return lambda i: (jnp.maximum(i * hb - 1, 0), col)

    def next_map(col):
        return lambda i: (jnp.minimum((i + 1) * hb, nhb - 1), col)

    kern = functools.partial(_cconv_kernel, tiles_per_seq=tiles_per_seq, halo=halo,
                             shifts=shifts, plan=plan, rc=rc, ln_rows=64)
    return pl.pallas_call(
        kern,
        grid=(t // tm,),
        in_specs=[pl.BlockSpec((tm, ch), lambda i: (i, 0)),
                  pl.BlockSpec((tm, ch), lambda i: (i, cb)),
                  pl.BlockSpec((halo, ch), prev_map(0)),
                  pl.BlockSpec((halo, ch), prev_map(cb)),
                  pl.BlockSpec((halo, ch), next_map(0)),
                  pl.BlockSpec((halo, ch), next_map(cb)),
                  pl.BlockSpec((CONF_KERNEL * SUBLANES, ch), lambda i: (0, 0)),
                  pl.BlockSpec((1, ch), lambda i: (0, 0)),
                  pl.BlockSpec((1, ch), lambda i: (0, 0)),
                  pl.BlockSpec((1, ch), lambda i: (0, 0))],
        out_specs=pl.BlockSpec((tm, ch), lambda i: (i, 0)),
        out_shape=jax.ShapeDtypeStruct((t, ch), BF16),
        scratch_shapes=[pltpu.VMEM((tm + 2 * halo, ch), F32),
                        pltpu.VMEM((len(shifts), tm + qmax, ch), F32),
                        pltpu.VMEM((tm, ch), F32)],
        compiler_params=_cparams(("parallel",)),
        name="cconv",
    )(proj, proj, proj, proj, proj, proj, jnp.repeat(w, SUBLANES, axis=0), b.reshape(1, ch),
      ln_g.reshape(1, ch),
      ln_b.reshape(1, ch))


def _sconv_kernel(x_ref, xp_ref, xn_ref, w_ref, b_ref, o_ref, hbuf_ref, sh_ref, *,
                  tiles_per_seq, halo, shifts, plan, rc):
    i = pl.program_id(0)
    tm = x_ref.shape[0]
    first = (i % tiles_per_seq) == 0
    last = (i % tiles_per_seq) == tiles_per_seq - 1
    hbuf_ref[0:halo, :] = jnp.where(first, 0.0, xp_ref[...])
    hbuf_ref[halo:halo + tm, :] = x_ref[...]
    hbuf_ref[halo + tm:2 * halo + tm, :] = jnp.where(last, 0.0, xn_ref[...])
    _fill_shifted(hbuf_ref, sh_ref, shifts, sh_ref.shape[1])

    def body(j, carry):
        r0 = j * rc
        h = _conv_rows(hbuf_ref, sh_ref, w_ref, plan, r0, rc) + b_ref[...]
        o_ref[pl.ds(pl.multiple_of(r0, rc), rc), :] = _silu(h)
        return carry

    lax.fori_loop(0, tm // rc, body, 0)


def _sconv(proj, col0, w, b, seq, tm):
    t = proj.shape[0]
    ch = w.shape[1]
    cw = 1024
    halo = SUBLANES
    rc = 16
    shifts, plan, qmax = _tap_plan(SSM_CONV, halo)
    tiles_per_seq = seq // tm
    hb = tm // halo
    nhb = t // halo
    c0 = col0 // cw
    kern = functools.partial(_sconv_kernel, tiles_per_seq=tiles_per_seq, halo=halo,
                             shifts=shifts, plan=plan, rc=rc)
    return pl.pallas_call(
        kern,
        grid=(t // tm, ch // cw),
        in_specs=[pl.BlockSpec((tm, cw), lambda i, j: (i, c0 + j)),
                  pl.BlockSpec((halo, cw), lambda i, j: (jnp.maximum(i * hb - 1, 0), c0 + j)),
                  pl.BlockSpec((halo, cw),
                               lambda i, j: (jnp.minimum((i + 1) * hb, nhb - 1), c0 + j)),
                  pl.BlockSpec((SSM_CONV * SUBLANES, cw), lambda i, j: (0, j)),
                  pl.BlockSpec((1, cw), lambda i, j: (0, j))],
        out_specs=pl.BlockSpec((tm, cw), lambda i, j: (i, j)),
        out_shape=jax.ShapeDtypeStruct((t, ch), F32),
        scratch_shapes=[pltpu.VMEM((tm + 2 * halo, cw), F32),
                        pltpu.VMEM((len(shifts), tm + qmax, cw), F32)],
        compiler_params=_cparams(("parallel", "parallel")),
        name="sconv",
    )(proj, proj, proj, jnp.repeat(w, SUBLANES, axis=0), b.reshape(1, ch))


def _split3(f):
    hi = f.astype(BF16)
    r1 = f - hi.astype(F32)
    mid = r1.astype(BF16)
    lo = (r1 - mid.astype(F32)).astype(BF16)
    return hi, mid, lo


def _expand_heads(f, e):
    hi, mid, lo = _split3(f)
    return (jnp.dot(hi, e, preferred_element_type=F32)
            + jnp.dot(mid, e, preferred_element_type=F32)
            + jnp.dot(lo, e, preferred_element_type=F32))


def _ssd_both_kernel(xf_ref, bf_ref, cf_ref, dtf_ref, xb_ref, bb_ref, cb_ref, dtb_ref, bias_ref,
                     alog_ref, ef_ref, eb_ref, of_ref, ob_ref, *scratch):
    nscr = len(scratch) // 2
    fwd, bwd = scratch[:nscr], scratch[nscr:]
    nheads = xf_ref.shape[1] // SSM_HEAD_DIM

    @pl.when(pl.program_id(1) == 0)
    def _():
        fwd[0][...] = jnp.zeros_like(fwd[0])
        bwd[0][...] = jnp.zeros_like(bwd[0])

    _ssd_direction(xf_ref, bf_ref, cf_ref, dtf_ref, bias_ref, alog_ref, ef_ref, of_ref, *fwd,
                   reverse=False, lane0=0)
    _ssd_direction(xb_ref, bb_ref, cb_ref, dtb_ref, bias_ref, alog_ref, eb_ref, ob_ref, *bwd,
                   reverse=True, lane0=nheads)


def _ssd_direction(x_ref, b_ref, c_ref, dt_ref, dtb_ref, alog_ref, e_ref, o_ref, state_ref,
                   acs_s, acst_s, dtt_s, wstt_s, carry_s, cb_s, bgt_s, *, reverse, lane0):
    q = x_ref.shape[0]
    nheads = x_ref.shape[1] // SSM_HEAD_DIM
    hpg = nheads // SSM_GROUPS

    row = lax.broadcasted_iota(jnp.int32, (q, q), 0)
    col = lax.broadcasted_iota(jnp.int32, (q, q), 1)
    keep = (col >= row) if reverse else (col <= row)

    z = dt_ref[...] + dtb_ref[...]
    dt = jnp.maximum(z, 0.0) + jnp.log(1.0 + jnp.exp(-jnp.abs(z)))
    a = dt * (-jnp.exp(alog_ref[...]))
    acs = jnp.dot(keep.astype(F32), a, precision=HIGHEST, preferred_element_type=F32)
    tot = acs[0:1, :] if reverse else acs[q - 1:q, :]
    acs2 = acs * LOG2E
    acs_s[...] = acs2
    acst_s[...] = acs2.T
    dtt_s[...] = dt.T
    wstt_s[...] = (dt * jnp.exp(tot - acs)).T
    carry_s[...] = _expand_heads(jnp.broadcast_to(jnp.exp(tot), (SUBLANES, LANES)), e_ref[...])

    for g in range(SSM_GROUPS):
        gsl = slice(g * SSM_STATE, (g + 1) * SSM_STATE)
        cb_s[g] = lax.dot_general(c_ref[:, gsl].astype(BF16), b_ref[:, gsl].astype(BF16),
                                  (((1,), (1,)), ((), ())), preferred_element_type=F32)
        bgt_s[g] = b_ref[:, gsl].T
        for pr in range(hpg // 2):
            ps = slice((g * hpg + 2 * pr) * SSM_HEAD_DIM, (g * hpg + 2 * pr + 2) * SSM_HEAD_DIM)
            xp = x_ref[:, ps].astype(BF16)
            rhs = jnp.concatenate([xp, state_ref[:, ps].astype(BF16)], axis=0)
            for sub in range(2):
                hl = lane0 + g * hpg + 2 * pr + sub
                half = slice(sub * SSM_HEAD_DIM, (sub + 1) * SSM_HEAD_DIM)
                hs = slice(ps.start + half.start, ps.start + half.stop)
                colb = jnp.broadcast_to(acs_s[:, hl:hl + 1], (q, q))
                seg = colb - acst_s[hl:hl + 1, :]
                lmat = jnp.exp2(jnp.where(keep, seg, -jnp.inf))
                m = (cb_s[g] * lmat * dtt_s[hl:hl + 1, :]).astype(BF16)
                cexp = (c_ref[:, gsl] * jnp.exp2(colb)).astype(BF16)
                y = jnp.dot(jnp.concatenate([m, cexp], axis=1), rhs,
                            preferred_element_type=F32)
                o_ref[:, hs] = y[:, half]
                bw = (bgt_s[g] * wstt_s[hl:hl + 1, :]).astype(BF16)
                upd = jnp.dot(bw, xp, preferred_element_type=F32)
                state_ref[:, hs] = state_ref[:, hs] * carry_s[0:1, hs] + upd[:, half]


def _ssd(xbc, proj, dt_col_block, dt_bias, a_log, expand_f, expand_b, bsz, seq):
    t = xbc.shape[0]
    q = SSM_CHUNK
    nc = seq // q
    inner = expand_f.shape[1]
    gn = SSM_GROUPS * SSM_STATE
    b_blk = inner // gn
    c_blk = b_blk + 1

    def fwd(b, c):
        return b * nc + c

    def bwd(b, c):
        return b * nc + nc - 1 - c

    def chunk_specs(tok):
        return [pl.BlockSpec((q, inner), lambda b, c: (tok(b, c), 0)),
                pl.BlockSpec((q, gn), lambda b, c: (tok(b, c), b_blk)),
                pl.BlockSpec((q, gn), lambda b, c: (tok(b, c), c_blk)),
                pl.BlockSpec((q, LANES), lambda b, c: (tok(b, c), dt_col_block))]

    def const(shape):
        return pl.BlockSpec(shape, lambda b, c: (0, 0))

    direction_scratch = [pltpu.VMEM((SSM_STATE, inner), F32),
                         pltpu.VMEM((q, LANES), F32), pltpu.VMEM((LANES, q), F32),
                         pltpu.VMEM((LANES, q), F32), pltpu.VMEM((LANES, q), F32),
                         pltpu.VMEM((SUBLANES, inner), F32),
                         pltpu.VMEM((SSM_GROUPS, q, q), F32),
                         pltpu.VMEM((SSM_GROUPS, SSM_STATE, q), F32)]
    return pl.pallas_call(
        _ssd_both_kernel,
        grid=(bsz, nc),
        in_specs=chunk_specs(fwd) + chunk_specs(bwd) + [
            const((1, LANES)), const((1, LANES)), const((LANES, inner)), const((LANES, inner))],
        out_specs=[pl.BlockSpec((q, inner), lambda b, c: (fwd(b, c), 0)),
                   pl.BlockSpec((q, inner), lambda b, c: (bwd(b, c), 0))],
        out_shape=[jax.ShapeDtypeStruct((t, inner), F32), jax.ShapeDtypeStruct((t, inner), F32)],
        scratch_shapes=direction_scratch + direction_scratch,
        compiler_params=_cparams(("arbitrary", "arbitrary")),
        name="ssd",
    )(xbc, xbc, xbc, proj, xbc, xbc, xbc, proj, dt_bias, a_log, expand_f, expand_b)


def _merge_kernel(yf_ref, yb_ref, xc_ref, z_ref, hc_ref, gc_ref, gs_ref, x_ref,
                  dexp_ref, ng_ref, wso_ref, wco_ref, bco_ref, wo_ref, g1_ref,
                  scale_ref, shift_ref, wr_ref, br_ref,
                  x1_ref, v_ref, lg_ref, *, ngroups, halves):
    tm, inner = yf_ref.shape
    gw = inner // ngroups
    hm = tm // halves
    for h in range(halves):
        r = slice(h * hm, (h + 1) * hm)
        y = (yf_ref[r, :] + yb_ref[r, :] + xc_ref[r, :] * dexp_ref[...]) * _silu(z_ref[r, :])
        parts = []
        for g in range(ngroups):
            yg = y[:, g * gw:(g + 1) * gw]
            ms = jnp.mean(yg * yg, axis=-1, keepdims=True)
            parts.append((yg * lax.rsqrt(ms + EPS)) * ng_ref[:, g * gw:(g + 1) * gw])
        ysn = jnp.concatenate(parts, axis=-1).astype(BF16)
        y_ssm = jnp.dot(ysn, wso_ref[...], preferred_element_type=F32)
        y_conf = jnp.dot(hc_ref[r, :], wco_ref[...], preferred_element_type=F32) + bco_ref[...]
        merged = _sigmoid(gc_ref[r, :]) * y_conf + _sigmoid(gs_ref[r, :]) * y_ssm
        o = jnp.dot(merged.astype(BF16), wo_ref[...], preferred_element_type=F32)
        x1 = x_ref[r, :] + g1_ref[0] * o
        x1_ref[r, :] = x1
        v = _modulated_norm(x1, scale_ref[0], shift_ref[0])
        v_ref[r, :] = v
        v_hi = v.astype(BF16)
        v_lo = (v - v_hi.astype(F32)).astype(BF16)
        p = (jnp.dot(v_hi, wr_ref[...], preferred_element_type=F32)
             + jnp.dot(v_lo, wr_ref[...], preferred_element_type=F32))
        lg_ref[r, :] = p[:, :LANES] + p[:, LANES:] + br_ref[...]


def _merge(yf, yb, xbc, proj, hc, x, cols, dexp, norm_g, wso, wco, bco, wo, g1, scale, shift,
           wr, br, seq, tm):
    t, d = x.shape
    inner = yf.shape[1]
    tiles_per_seq = seq // tm
    z_blk = cols["z"] // inner
    gc_blk = cols["gate_conf"] // d
    gs_blk = cols["gate_ssm"] // d

    def const(shape):
        return pl.BlockSpec(shape, lambda i: tuple(0 for _ in shape),
                            pipeline_mode=pl.Buffered(1))

    def per_seq():
        return pl.BlockSpec((1, 1, d), lambda i: (i // tiles_per_seq, 0, 0))

    kern = functools.partial(_merge_kernel, ngroups=SSM_GROUPS, halves=2)
    return pl.pallas_call(
        kern,
        grid=(t // tm,),
        in_specs=[pl.BlockSpec((tm, inner), lambda i: (i, 0)),
                  pl.BlockSpec((tm, inner), lambda i: (i, 0)),
                  pl.BlockSpec((tm, inner), lambda i: (i, 0)),
                  pl.BlockSpec((tm, inner), lambda i: (i, z_blk)),
                  pl.BlockSpec((tm, d), lambda i: (i, 0)),
                  pl.BlockSpec((tm, d), lambda i: (i, gc_blk)),
                  pl.BlockSpec((tm, d), lambda i: (i, gs_blk)),
                  pl.BlockSpec((tm, d), lambda i: (i, 0)),
                  const((1, inner)), const((1, inner)),
                  const((inner, d)), const((d, d)), const((1, d)), const((d, d)),
                  per_seq(), per_seq(), per_seq(),
                  const((d, 2 * LANES)), const((1, LANES))],
        out_specs=[pl.BlockSpec((tm, d), lambda i: (i, 0)),
                   pl.BlockSpec((tm, d), lambda i: (i, 0)),
                   pl.BlockSpec((tm, LANES), lambda i: (i, 0))],
        out_shape=[jax.ShapeDtypeStruct((t, d), F32),
                   jax.ShapeDtypeStruct((t, d), F32),
                   jax.ShapeDtypeStruct((t, LANES), F32)],
        compiler_params=_cparams(("parallel",)),
        name="merge",
    )(yf, yb, xbc, proj, hc, proj, proj, x, dexp, norm_g, wso, wco, bco, wo, g1, scale, shift,
      wr, br)


ROUTE_TILE = 256


def _route_kernel(lg_ref, idx_ref, w_ref, lslot_ref, lslot_t_ref, cnt_ref, *, n_experts):
    tm = lg_ref.shape[0]
    lane = lax.broadcasted_iota(jnp.int32, (tm, LANES), 1)
    lg = jnp.where(lane < n_experts, lg_ref[...], -jnp.inf)
    sel = jnp.zeros((tm, LANES), F32)
    vals, idxs = [], []
    for _ in range(TOP_K):
        m = jnp.max(lg, axis=-1, keepdims=True)
        ix = jnp.min(jnp.where(lg == m, lane, LANES), axis=-1, keepdims=True)
        hit = lane == ix
        sel = jnp.where(hit, 1.0, sel)
        lg = jnp.where(hit, -jnp.inf, lg)
        vals.append(m)
        idxs.append(ix)
    ex = [jnp.exp(v - vals[0]) for v in vals]
    den = ex[0] + ex[1] + ex[2] + ex[3]
    row = lax.broadcasted_iota(jnp.int32, (tm, tm), 0)
    col = lax.broadcasted_iota(jnp.int32, (tm, tm), 1)
    rank = jnp.dot((col < row).astype(BF16), sel.astype(BF16), preferred_element_type=F32)
    cnt = jnp.sum(sel, axis=0, keepdims=True)
    erow = lax.broadcasted_iota(jnp.int32, (LANES, LANES), 0)
    ecol = lax.broadcasted_iota(jnp.int32, (LANES, LANES), 1)
    off = jnp.dot(jnp.broadcast_to(cnt, (SUBLANES, LANES)).astype(BF16),
                  (erow < ecol).astype(BF16), preferred_element_type=F32)[0:1, :]
    slot_of = rank + off
    idx_out = jnp.zeros((tm, LANES), jnp.int32)
    w_out = jnp.zeros((tm, LANES), F32)
    slot_out = jnp.zeros((tm, LANES), F32)
    for k in range(TOP_K):
        sk = jnp.sum(jnp.where(lane == idxs[k], slot_of, 0.0), axis=-1, keepdims=True)
        idx_out = jnp.where(lane == k, idxs[k], idx_out)
        w_out = jnp.where(lane == k, ex[k] / den, w_out)
        slot_out = jnp.where(lane == k, sk, slot_out)
    idx_ref[...] = idx_out
    w_ref[...] = w_out
    lslot_ref[...] = slot_out.astype(jnp.int32)
    lslot_t_ref[...] = slot_out.T[0:SUBLANES, :].astype(jnp.int32)
    cnt_ref[0] = jnp.broadcast_to(cnt, (SUBLANES, LANES)).astype(jnp.int32)


def _route(logits, n_experts):
    t = logits.shape[0]
    tm = ROUTE_TILE
    nt = t // tm
    kern = functools.partial(_route_kernel, n_experts=n_experts)
    return pl.pallas_call(
        kern,
        grid=(nt,),
        in_specs=[pl.BlockSpec((tm, LANES), lambda i: (i, 0))],
        out_specs=[pl.BlockSpec((tm, LANES), lambda i: (i, 0)),
                   pl.BlockSpec((tm, LANES), lambda i: (i, 0)),
                   pl.BlockSpec((tm, LANES), lambda i: (i, 0)),
                   pl.BlockSpec((SUBLANES, tm), lambda i: (i, 0)),
                   pl.BlockSpec((1, SUBLANES, LANES), lambda i: (i, 0, 0))],
        out_shape=[jax.ShapeDtypeStruct((t, LANES), jnp.int32),
                   jax.ShapeDtypeStruct((t, LANES), F32),
                   jax.ShapeDtypeStruct((t, LANES), jnp.int32),
                   jax.ShapeDtypeStruct((nt * SUBLANES, tm), jnp.int32),
                   jax.ShapeDtypeStruct((nt, SUBLANES, LANES), jnp.int32)],
        compiler_params=_cparams(("parallel",)),
        name="route",
    )(logits)


def _row_copy(src, src_row, dst, dst_row, sem, nch):
    return pltpu.make_async_copy(_row(src, src_row, nch), _row(dst, dst_row, nch), sem)


def _segment_copies(cnt_ref, loc_ref, glob_ref, tile, n_experts, local_buf, hbm, sem, nch,
                    to_hbm, wait):
    max_bit = (ROUTE_TILE).bit_length() - 1
    split_bit = max_bit - 2

    def body(e, c):
        n = cnt_ref[tile * n_experts + e]
        lo = loc_ref[tile * n_experts + e]
        go = glob_ref[tile * n_experts + e]
        def copy_bits(bits):
            for bit in bits:
                size = 1 << bit
                done = (n >> (bit + 1)) << (bit + 1)

                @pl.when((n & size) != 0)
                def _():
                    loc = local_buf.at[pl.ds(pl.multiple_of((lo + done) * nch, nch), size * nch),
                                       :]
                    glb = hbm.at[pl.ds(pl.multiple_of((go + done) * nch, nch), size * nch), :]
                    cp = pltpu.make_async_copy(loc, glb, sem) if to_hbm else \
                        pltpu.make_async_copy(glb, loc, sem)
                    if wait:
                        cp.wait()
                    else:
                        cp.start()

        pl.when(n >= (1 << split_bit))(lambda: copy_bits(range(max_bit, split_bit - 1, -1)))
        copy_bits(range(split_bit - 1, -1, -1))
        return c

    lax.fori_loop(0, n_experts, body, 0)


def _wait_segments(local_buf, hbm, sem, to_hbm):
    whole = hbm.at[pl.ds(0, local_buf.shape[0]), :]
    if to_hbm:
        pltpu.make_async_copy(local_buf, whole, sem).wait()
    else:
        pltpu.make_async_copy(whole, local_buf, sem).wait()


def _dispatch_kernel(cnt_ref, loc_ref, glob_ref, fill_ref, end_ref, lslot_t_ref, v_ref, xs_hbm,
                     buf, sems, fsem, *, n_experts, nch):
    i = pl.program_id(0)
    n = pl.num_programs(0)
    tm = v_ref.shape[0]
    nslots = tm * TOP_K
    slot = i % 2
    srow = lax.broadcasted_iota(jnp.int32, (nslots, tm), 0)
    hit = srow == lslot_t_ref[0:1, :]
    for k in range(1, TOP_K):
        hit = jnp.logical_or(hit, srow == lslot_t_ref[k:k + 1, :])
    p = jnp.where(hit, 1.0, 0.0).astype(BF16)
    v = v_ref[...]
    v_hi = v.astype(BF16)
    v_lo = (v - v_hi.astype(F32)).astype(BF16)
    xl = (jnp.dot(p, v_hi, preferred_element_type=F32)
          + jnp.dot(p, v_lo, preferred_element_type=F32))
    for s_ in range(nch):
        buf[slot, pl.ds(s_, nslots, stride=nch), :] = xl[:, s_ * LANES:(s_ + 1) * LANES]

    @pl.when(i == 0)
    def _():
        def fill(s, c):
            _row_copy(buf.at[0], 0, xs_hbm, s, fsem, nch).start()
            return c

        def drain(s, c):
            _row_copy(buf.at[0], 0, xs_hbm, s, fsem, nch).wait()
            return c

        group = 8
        for e0 in range(0, n_experts, group):
            for e in range(e0, min(e0 + group, n_experts)):
                lax.fori_loop(fill_ref[e], end_ref[e], fill, 0)
            for e in range(e0, min(e0 + group, n_experts)):
                lax.fori_loop(fill_ref[e], end_ref[e], drain, 0)

    args = (cnt_ref, loc_ref, glob_ref)
    _segment_copies(*args, i, n_experts, buf.at[slot], xs_hbm, sems.at[slot], nch, True, False)

    @pl.when(i > 0)
    def _():
        _wait_segments(buf.at[1 - slot], xs_hbm, sems.at[1 - slot], True)

    @pl.when(i == n - 1)
    def _():
        _wait_segments(buf.at[slot], xs_hbm, sems.at[slot], True)


def _dispatch(cnt_flat, loc_flat, glob_flat, fill_start, pad_end, lslot_t, v, n_slots):
    t, d = v.shape
    nch = d // LANES
    tm = ROUTE_TILE
    n_experts = fill_start.shape[0]
    kern = functools.partial(_dispatch_kernel, n_experts=n_experts, nch=nch)
    grid_spec = pltpu.PrefetchScalarGridSpec(
        num_scalar_prefetch=5,
        grid=(t // tm,),
        in_specs=[pl.BlockSpec((SUBLANES, tm), lambda i, *_: (i, 0)),
                  pl.BlockSpec((tm, d), lambda i, *_: (i, 0))],
        out_specs=pl.BlockSpec(memory_space=pl.ANY),
        scratch_shapes=[pltpu.VMEM((2, tm * TOP_K * nch, LANES), F32),
                        pltpu.SemaphoreType.DMA((2,)), pltpu.SemaphoreType.DMA],
    )
    return pl.pallas_call(
        kern,
        grid_spec=grid_spec,
        out_shape=jax.ShapeDtypeStruct((n_slots * nch, LANES), v.dtype),
        compiler_params=_cparams(("arbitrary",)),
        name="dispatch",
    )(cnt_flat, loc_flat, glob_flat, fill_start, pad_end, lslot_t, v)


def _moe_kernel(be_ref, nused_ref, x_ref, wgu_ref, bgu_ref, wd_ref, bd_ref, o_ref,
                wgu_bf, wd_bf):
    b = pl.program_id(0)
    changed = jnp.logical_or(b == 0, be_ref[b] != be_ref[jnp.maximum(b - 1, 0)])
    active = b < nused_ref[0]

    @pl.when(jnp.logical_and(active, changed))
    def _():
        wgu_bf[...] = wgu_ref[0].astype(BF16)
        wd_bf[...] = wd_ref[0].astype(BF16)

    @pl.when(active)
    def _():
        ff = wd_bf.shape[0]
        d = wgu_bf.shape[0]
        nch = d // LANES
        hm = x_ref.shape[0] // nch // MOE_HALVES
        for h in range(MOE_HALVES):
            x = _load_rows(x_ref, hm, nch, h * hm).astype(BF16)
            gu = jnp.dot(x, wgu_bf[...], preferred_element_type=F32) + bgu_ref[0]
            gate = jnp.minimum(gu[:, :ff], SWIGLU_LIMIT)
            up = jnp.clip(gu[:, ff:], -SWIGLU_LIMIT, SWIGLU_LIMIT)
            glu = gate * _sigmoid(SWIGLU_ALPHA * gate)
            hid = ((up + 1.0) * glu).astype(BF16)
            y = jnp.dot(hid, wd_bf[...], preferred_element_type=F32) + bd_ref[0]
            _store_rows(o_ref, y, h * hm)


def _moe(block_expert, nused, xs, w_gu, b_gu, w_down, b_down):
    ne, d, ff2 = w_gu.shape
    nch = d // LANES
    n_slots = xs.shape[0] // nch
    ff = w_down.shape[1]
    bm = MOE_BLOCK

    def blk(b, nu):
        return jnp.minimum(b, nu[0] - 1)

    grid_spec = pltpu.PrefetchScalarGridSpec(
        num_scalar_prefetch=2,
        grid=(n_slots // bm,),
        in_specs=[pl.BlockSpec((bm * nch, LANES), lambda b, be, nu: (blk(b, nu), 0)),
                  pl.BlockSpec((1, d, ff2), lambda b, be, nu: (be[blk(b, nu)], 0, 0)),
                  pl.BlockSpec((1, 1, ff2), lambda b, be, nu: (be[blk(b, nu)], 0, 0)),
                  pl.BlockSpec((1, ff, d), lambda b, be, nu: (be[blk(b, nu)], 0, 0)),
                  pl.BlockSpec((1, 1, d), lambda b, be, nu: (be[blk(b, nu)], 0, 0))],
        out_specs=pl.BlockSpec((bm * nch, LANES), lambda b, be, nu: (blk(b, nu), 0)),
        scratch_shapes=[pltpu.VMEM((d, ff2), BF16), pltpu.VMEM((ff, d), BF16)],
    )
    return pl.pallas_call(
        _moe_kernel,
        grid_spec=grid_spec,
        out_shape=jax.ShapeDtypeStruct(xs.shape, F32),
        compiler_params=_cparams(("arbitrary",)),
        name="moe",
    )(block_expert, nused, xs, w_gu, b_gu.reshape(ne, 1, ff2), w_down, b_down.reshape(ne, 1, d))


def _final_kernel(cnt_ref, loc_ref, glob_ref, x1_ref, lslot_ref, w_ref, g2_ref, scale_ref,
                  shift_ref, ys_hbm, o_ref, buf, sems, *, n_experts, nch):
    i = pl.program_id(0)
    n = pl.num_programs(0)
    tm = x1_ref.shape[0]
    nslots = tm * TOP_K
    slot = i % 2
    args = (cnt_ref, loc_ref, glob_ref)

    @pl.when(i == 0)
    def _():
        _segment_copies(*args, 0, n_experts, buf.at[0], ys_hbm, sems.at[0], nch, False, False)

    @pl.when(i + 1 < n)
    def _():
        _segment_copies(*args, i + 1, n_experts, buf.at[1 - slot], ys_hbm, sems.at[1 - slot], nch,
                        False, False)

    _wait_segments(buf.at[slot], ys_hbm, sems.at[slot], False)

    lane_slot = lax.broadcasted_iota(jnp.int32, (tm, nslots), 1)
    pw = jnp.zeros((tm, nslots), F32)
    for k in range(TOP_K):
        pw = jnp.where(lane_slot == lslot_ref[:, k:k + 1], w_ref[:, k:k + 1], pw)
    pw_hi = pw.astype(BF16)
    pw_lo = (pw - pw_hi.astype(F32)).astype(BF16)
    y = _load_rows(buf, nslots, nch, 0, (slot,))
    y_hi = y.astype(BF16)
    y_lo = (y - y_hi.astype(F32)).astype(BF16)
    f = (jnp.dot(pw_hi, y_hi, preferred_element_type=F32)
         + jnp.dot(pw_hi, y_lo, preferred_element_type=F32)
         + jnp.dot(pw_lo, y_hi, preferred_element_type=F32))
    x2 = x1_ref[...] + g2_ref[0] * f
    o_ref[...] = _modulated_norm(x2, scale_ref[0], shift_ref[0])


def _final(cnt_flat, loc_flat, glob_flat, n_experts, x1, ys, lslot, top_w, g2, scale, shift, seq):
    t, d = x1.shape
    nch = d // LANES
    tm = ROUTE_TILE
    tiles_per_seq = seq // tm

    def per_seq():
        return pl.BlockSpec((1, 1, d), lambda i, *_: (i // tiles_per_seq, 0, 0))

    grid_spec = pltpu.PrefetchScalarGridSpec(
        num_scalar_prefetch=3,
        grid=(t // tm,),
        in_specs=[pl.BlockSpec((tm, d), lambda i, *_: (i, 0)),
                  pl.BlockSpec((tm, LANES), lambda i, *_: (i, 0)),
                  pl.BlockSpec((tm, LANES), lambda i, *_: (i, 0)),
                  per_seq(), per_seq(), per_seq(),
                  pl.BlockSpec(memory_space=pl.ANY)],
        out_specs=pl.BlockSpec((tm, d), lambda i, *_: (i, 0)),
        scratch_shapes=[pltpu.VMEM((2, TOP_K * tm * nch, LANES), F32),
                        pltpu.SemaphoreType.DMA((2,))],
    )
    return pl.pallas_call(
        functools.partial(_final_kernel, n_experts=n_experts, nch=nch),
        grid_spec=grid_spec,
        out_shape=jax.ShapeDtypeStruct((t, d), F32),
        compiler_params=_cparams(("arbitrary",)),
        name="final",
    )(cnt_flat, loc_flat, glob_flat, x1, lslot, top_w, g2, scale, shift, ys)


def _pad_cols(w, n):
    return jnp.pad(w, ((0, 0), (0, n - w.shape[1])))


def kernel(x, c, ada_w, ada_b, norm_mix_g, w_in, conf_dw_w, conf_dw_b, conf_ln_g, conf_ln_b,
           conf_out_w, conf_out_b, ssm_conv_w, ssm_conv_b, dt_bias_f, dt_bias_b, a_log_f,
           a_log_b, ssm_d, ssm_norm_g, ssm_out_w, w_o, norm_ffn_g, router_w, router_b, w_gu,
           b_gu, w_down, b_down, final_ada_w, final_ada_b, final_norm_g):
    bsz, seq, d = x.shape
    depth = ada_w.shape[0]
    t = bsz * seq
    nheads = a_log_f.shape[1]
    inner = nheads * SSM_HEAD_DIM
    gn = SSM_GROUPS * SSM_STATE
    conf = conf_dw_w.shape[2]
    n_experts = router_w.shape[2]
    assert 2 * nheads <= LANES and n_experts <= LANES

    c_pad = jnp.zeros((SUBLANES, d), F32).at[:bsz].set(c)
    fin = _ada(c_pad, final_ada_w, final_ada_b)[:bsz]
    xf = x.reshape(t, d)

    sizes = [("conf_a", conf), ("conf_g", conf), ("z", inner), ("xs", inner), ("bm", gn),
             ("cm", gn), ("gate_conf", d), ("gate_ssm", d), ("dt", LANES)]
    cols, off = {}, 0
    for name, n in sizes:
        cols[name] = off
        off += n
    src = {}
    o = 0
    for name, n in [("conf_a", conf), ("conf_g", conf), ("z", inner), ("xs", inner), ("bm", gn),
                    ("cm", gn), ("dtf", nheads), ("dtb", nheads), ("gate_conf", d),
                    ("gate_ssm", d)]:
        src[name] = (o, o + n)
        o += n

    head_of_col = jnp.arange(inner, dtype=jnp.int32) // SSM_HEAD_DIM
    lanes = jnp.arange(LANES, dtype=jnp.int32)[:, None]
    expand_f = (lanes == head_of_col[None, :]).astype(BF16)
    expand_b = (lanes == head_of_col[None, :] + nheads).astype(BF16)

    for l in range(depth):
        ada = _ada(c_pad, ada_w[l], ada_b[l])[:bsz]
        sh1, sc1, g1, sh2, sc2, g2 = [a.reshape(bsz, 1, d) for a in jnp.split(ada, 6, axis=-1)]
        scale1 = norm_mix_g[l][None, None, :] * (1.0 + sc1)
        scale2 = norm_ffn_g[l][None, None, :] * (1.0 + sc2)

        wl = w_in[l]
        w_dt = _pad_cols(jnp.concatenate([wl[:, slice(*src["dtf"])], wl[:, slice(*src["dtb"])]],
                                         axis=1), LANES)
        w_perm = jnp.concatenate(
            [wl[:, slice(*src[n])] for n in ("conf_a", "conf_g", "z", "xs", "bm", "cm",
                                             "gate_conf", "gate_ssm")] + [w_dt],
            axis=1).astype(BF16)
        proj = _inproj(xf, scale1, sh1, w_perm, seq, tm=256, col_splits=1)

        hc = _cconv(proj, conf_dw_w[l], conf_dw_b[l], conf_ln_g[l], conf_ln_b[l], seq, tm=256)
        xbc = _sconv(proj, cols["xs"], ssm_conv_w[l], ssm_conv_b[l], seq, tm=256)

        dt_bias = _pad_cols(jnp.concatenate([dt_bias_f[l], dt_bias_b[l]])[None, :], LANES)
        a_log = _pad_cols(jnp.concatenate([a_log_f[l], a_log_b[l]])[None, :], LANES)
        dt_blk = cols["dt"] // LANES
        y_f, y_b = _ssd(xbc, proj, dt_blk, dt_bias, a_log, expand_f, expand_b, bsz, seq)

        dexp = jnp.repeat(ssm_d[l], SSM_HEAD_DIM)[None, :]
        wr = _pad_cols(router_w[l], LANES)
        wr_hi = wr.astype(BF16)
        wr = jnp.concatenate([wr_hi, (wr - wr_hi.astype(F32)).astype(BF16)], axis=1)
        br = _pad_cols(router_b[l][None, :], LANES)
        x1, v, logits = _merge(
            y_f, y_b, xbc, proj, hc, xf, cols, dexp, ssm_norm_g[l][None, :],
            ssm_out_w[l].astype(BF16), conf_out_w[l].astype(BF16), conf_out_b[l][None, :],
            w_o[l].astype(BF16), g1, scale2, sh2, wr, br, seq, tm=256)

        top_i, top_w, lslot, lslot_t, tile_cnt = _route(logits, n_experts)
        cnt2 = tile_cnt[:, 0, :n_experts]
        counts = jnp.sum(cnt2, axis=0)
        bm = MOE_BLOCK
        n_blocks = (t * TOP_K) // bm + n_experts
        padded = ((counts + bm - 1) // bm) * bm
        pad_end = jnp.cumsum(padded)
        pad_start = pad_end - padded
        loc2 = jnp.cumsum(cnt2, axis=1) - cnt2
        glob2 = pad_start[None, :] + jnp.cumsum(cnt2, axis=0) - cnt2
        tables = (cnt2.reshape(-1), loc2.reshape(-1), glob2.reshape(-1))
        block_first = jnp.arange(n_blocks, dtype=jnp.int32) * bm
        block_expert = jnp.minimum(
            jnp.sum((pad_end[None, :] <= block_first[:, None]).astype(jnp.int32), axis=1),
            n_experts - 1).astype(jnp.int32)
        nused = (pad_end[-1] // bm).astype(jnp.int32).reshape(1)
        xs = _dispatch(*tables, (pad_start + counts).astype(jnp.int32),
                       pad_end.astype(jnp.int32), lslot_t, v, n_blocks * bm)
        ys = _moe(block_expert, nused, xs, w_gu[l], b_gu[l], w_down[l], b_down[l])

        if l == depth - 1:
            sh_f, sc_f = [a.reshape(bsz, 1, d) for a in jnp.split(fin, 2, axis=-1)]
            scale_f = final_norm_g[None, None, :] * (1.0 + sc_f)
            xf = _final(*tables, n_experts, x1, ys, lslot, top_w, g2, scale_f, sh_f, seq)
        else:
            raise NotImplementedError("depth > 1 is not wired")
    return xf.reshape(bsz, seq, d)
```

```python
import functools

import jax
import jax.numpy as jnp
from jax import lax
from jax.experimental import pallas as pl
from jax.experimental.pallas import tpu as pltpu

F32 = jnp.float32
BF16 = jnp.bfloat16
HIGHEST = lax.Precision.HIGHEST

EPS = 1e-6
LOG2E = 1.4426950408889634
CONF_KERNEL = 31
SSM_CONV = 5
SSM_HEAD_DIM = 64
SSM_GROUPS = 4
SSM_STATE = 128
SSM_CHUNK = 128
TOP_K = 4
SWIGLU_ALPHA = 1.702
SWIGLU_LIMIT = 7.0

LANES = 128
SUBLANES = 8
VMEM_LIMIT = 56 * 1024 * 1024

MOE_BLOCK = 512
MOE_HALVES = 2


def _cparams(sem):
    return pltpu.CompilerParams(dimension_semantics=sem, vmem_limit_bytes=VMEM_LIMIT)


def _store_rows(ref, val, row0=0):
    n, d = val.shape
    nch = d // LANES
    for s in range(nch):
        ref[pl.ds(row0 * nch + s, n, stride=nch), :] = val[:, s * LANES:(s + 1) * LANES]


def _load_rows(ref, n, nch, row0=0, lead=()):
    return jnp.concatenate(
        [ref[lead + (pl.ds(row0 * nch + s, n, stride=nch), slice(None))] for s in range(nch)],
        axis=-1)


def _row(ref, r, nch, lead=()):
    return ref.at[lead + (pl.ds(pl.multiple_of(r * nch, nch), nch), slice(None))]


def _sigmoid(x):
    return 1.0 / (1.0 + jnp.exp(-x))


def _silu(x):
    return x * _sigmoid(x)


def _ada_kernel(c_ref, w_ref, b_ref, o_ref):
    c = c_ref[...]
    o_ref[...] = jnp.dot(_silu(c), w_ref[...], precision=HIGHEST,
                         preferred_element_type=F32) + b_ref[...]


def _ada(c_pad, w, b):
    d, n = w.shape
    tn = 1024
    return pl.pallas_call(
        _ada_kernel,
        grid=(n // tn,),
        in_specs=[pl.BlockSpec((SUBLANES, d), lambda j: (0, 0)),
                  pl.BlockSpec((d, tn), lambda j: (0, j)),
                  pl.BlockSpec((1, tn), lambda j: (0, j))],
        out_specs=pl.BlockSpec((SUBLANES, tn), lambda j: (0, j)),
        out_shape=jax.ShapeDtypeStruct((SUBLANES, n), F32),
        compiler_params=_cparams(("parallel",)),
        name="ada",
    )(c_pad, w, b.reshape(1, n))


def _modulated_norm(x, scale, shift):
    ms = jnp.mean(x * x, axis=-1, keepdims=True)
    return (x * lax.rsqrt(ms + EPS)) * scale + shift


def _inproj_kernel(x_ref, scale_ref, shift_ref, w_ref, o_ref, *, col_chunk):
    u = _modulated_norm(x_ref[...], scale_ref[0], shift_ref[0]).astype(BF16)
    ncols = o_ref.shape[1]
    for c0 in range(0, ncols, col_chunk):
        c1 = min(c0 + col_chunk, ncols)
        o_ref[:, c0:c1] = jnp.dot(u, w_ref[:, c0:c1], preferred_element_type=F32)


def _inproj(x, scale, shift, w_bf, seq, tm, col_splits):
    t, d = x.shape
    ncols = w_bf.shape[1]
    nw = ncols // col_splits
    tiles_per_seq = seq // tm
    return pl.pallas_call(
        functools.partial(_inproj_kernel, col_chunk=1024),
        grid=(col_splits, t // tm),
        in_specs=[pl.BlockSpec((tm, d), lambda j, i: (i, 0)),
                  pl.BlockSpec((1, 1, d), lambda j, i: (i // tiles_per_seq, 0, 0)),
                  pl.BlockSpec((1, 1, d), lambda j, i: (i // tiles_per_seq, 0, 0)),
                  pl.BlockSpec((d, nw), lambda j, i: (0, j), pipeline_mode=pl.Buffered(1))],
        out_specs=pl.BlockSpec((tm, nw), lambda j, i: (i, j)),
        out_shape=jax.ShapeDtypeStruct((t, ncols), F32),
        compiler_params=_cparams(("parallel", "parallel")),
        name="inproj",
    )(x, scale, shift, w_bf)


def _fill_shifted(hbuf_ref, sh_ref, shifts, rows):
    for n, r in enumerate(shifts):
        sh_ref[n] = hbuf_ref[r:r + rows, :]


def _tap_plan(ktaps, halo):
    offs = [k + halo - ktaps // 2 for k in range(ktaps)]
    shifts = sorted({o % SUBLANES for o in offs} - {0})
    plan = [(shifts.index(o % SUBLANES) if o % SUBLANES else -1, (o // SUBLANES) * SUBLANES)
            for o in offs]
    return shifts, plan, max(o // SUBLANES for o in offs) * SUBLANES


def _conv_rows(hbuf_ref, sh_ref, w8_ref, plan, r0, rc):
    groups = rc // SUBLANES
    accs = [None] * groups
    for k, (n, q8) in enumerate(plan):
        w = w8_ref[k * SUBLANES:(k + 1) * SUBLANES, :]
        for j in range(groups):
            start = pl.multiple_of(r0 + q8 + j * SUBLANES, SUBLANES)
            rows = pl.ds(start, SUBLANES)
            src = hbuf_ref[rows, :] if n < 0 else sh_ref[n, rows, :]
            term = src * w
            accs[j] = term if accs[j] is None else accs[j] + term
    return jnp.concatenate(accs, axis=0)


def _cconv_kernel(a_ref, g_ref, ap_ref, gp_ref, an_ref, gn_ref, w_ref, b_ref, lg_ref, lb_ref,
                  o_ref, hbuf_ref, sh_ref, cbuf_ref, *, tiles_per_seq, halo, shifts, plan, rc,
                  ln_rows):
    i = pl.program_id(0)
    tm = a_ref.shape[0]
    first = (i % tiles_per_seq) == 0
    last = (i % tiles_per_seq) == tiles_per_seq - 1
    glu_p = ap_ref[...] * _sigmoid(gp_ref[...])
    glu_n = an_ref[...] * _sigmoid(gn_ref[...])
    hbuf_ref[0:halo, :] = jnp.where(first, 0.0, glu_p)
    hbuf_ref[halo:halo + tm, :] = a_ref[...] * _sigmoid(g_ref[...])
    hbuf_ref[halo + tm:2 * halo + tm, :] = jnp.where(last, 0.0, glu_n)
    _fill_shifted(hbuf_ref, sh_ref, shifts, sh_ref.shape[1])

    def body(j, carry):
        r0 = j * rc
        h = _conv_rows(hbuf_ref, sh_ref, w_ref, plan, r0, rc) + b_ref[...]
        cbuf_ref[pl.ds(pl.multiple_of(r0, rc), rc), :] = h
        return carry

    lax.fori_loop(0, tm // rc, body, 0)

    for c0 in range(0, tm, ln_rows):
        h = cbuf_ref[c0:c0 + ln_rows, :]
        mu = jnp.mean(h, axis=-1, keepdims=True)
        hc = h - mu
        var = jnp.mean(hc * hc, axis=-1, keepdims=True)
        y = hc * lax.rsqrt(var + EPS) * lg_ref[...] + lb_ref[...]
        o_ref[c0:c0 + ln_rows, :] = _silu(y).astype(o_ref.dtype)


def _cconv(proj, w, b, ln_g, ln_b, seq, tm):
    t = proj.shape[0]
    ch = w.shape[1]
    halo = 16
    rc = 16
    shifts, plan, qmax = _tap_plan(CONF_KERNEL, halo)
    tiles_per_seq = seq // tm
    hb = tm // halo
    nhb = t // halo
    cb = 1

    def prev_map(col):
        return lambda i: (jnp.maximum(i * hb - 1, 0), col)

    def next_map(col):
        return lambda i: (jnp.minimum((i + 1) * hb, nhb - 1), col)

    kern = functools.partial(_cconv_kernel, tiles_per_seq=tiles_per_seq, halo=halo,
                             shifts=shifts, plan=plan, rc=rc, ln_rows=64)
    return pl.pallas_call(
        kern,
        grid=(t // tm,),
        in_specs=[pl.BlockSpec((tm, ch), lambda i: (i, 0)),
                  pl.BlockSpec((tm, ch), lambda i: (i, cb)),
                  pl.BlockSpec((halo, ch), prev_map(0)),
                  pl.BlockSpec((halo, ch), prev_map(cb)),
                  pl.BlockSpec((halo, ch), next_map(0)),
                  pl.BlockSpec((halo, ch), next_map(cb)),
                  pl.BlockSpec((CONF_KERNEL * SUBLANES, ch), lambda i: (0, 0)),
                  pl.BlockSpec((1, ch), lambda i: (0, 0)),
                  pl.BlockSpec((1, ch), lambda i: (0, 0)),
                  pl.BlockSpec((1, ch), lambda i: (0, 0))],
        out_specs=pl.BlockSpec((tm, ch), lambda i: (i, 0)),
        out_shape=jax.ShapeDtypeStruct((t, ch), BF16),
        scratch_shapes=[pltpu.VMEM((tm + 2 * halo, ch), F32),
                        pltpu.VMEM((len(shifts), tm + qmax, ch), F32),
                        pltpu.VMEM((tm, ch), F32)],
        compiler_params=_cparams(("parallel",)),
        name="cconv",
    )(proj, proj, proj, proj, proj, proj, jnp.repeat(w, SUBLANES, axis=0), b.reshape(1, ch),
      ln_g.reshape(1, ch),
      ln_b.reshape(1, ch))


def _sconv_kernel(x_ref, xp_ref, xn_ref, w_ref, b_ref, o_ref, hbuf_ref, sh_ref, *,
                  tiles_per_seq, halo, shifts, plan, rc):
    i = pl.program_id(0)
    tm = x_ref.shape[0]
    first = (i % tiles_per_seq) == 0
    last = (i % tiles_per_seq) == tiles_per_seq - 1
    hbuf_ref[0:halo, :] = jnp.where(first, 0.0, xp_ref[...])
    hbuf_ref[halo:halo + tm, :] = x_ref[...]
    hbuf_ref[halo + tm:2 * halo + tm, :] = jnp.where(last, 0.0, xn_ref[...])
    _fill_shifted(hbuf_ref, sh_ref, shifts, sh_ref.shape[1])

    def body(j, carry):
        r0 = j * rc
        h = _conv_rows(hbuf_ref, sh_ref, w_ref, plan, r0, rc) + b_ref[...]
        o_ref[pl.ds(pl.multiple_of(r0, rc), rc), :] = _silu(h)
        return carry

    lax.fori_loop(0, tm // rc, body, 0)


def _sconv(proj, col0, w, b, seq, tm):
    t = proj.shape[0]
    ch = w.shape[1]
    cw = 1024
    halo = SUBLANES
    rc = 16
    shifts, plan, qmax = _tap_plan(SSM_CONV, halo)
    tiles_per_seq = seq // tm
    hb = tm // halo
    nhb = t // halo
    c0 = col0 // cw
    kern = functools.partial(_sconv_kernel, tiles_per_seq=tiles_per_seq, halo=halo,
                             shifts=shifts, plan=plan, rc=rc)
    return pl.pallas_call(
        kern,
        grid=(t // tm, ch // cw),
        in_specs=[pl.BlockSpec((tm, cw), lambda i, j: (i, c0 + j)),
                  pl.BlockSpec((halo, cw), lambda i, j: (jnp.maximum(i * hb - 1, 0), c0 + j)),
                  pl.BlockSpec((halo, cw),
                               lambda i, j: (jnp.minimum((i + 1) * hb, nhb - 1), c0 + j)),
                  pl.BlockSpec((SSM_CONV * SUBLANES, cw), lambda i, j: (0, j)),
                  pl.BlockSpec((1, cw), lambda i, j: (0, j))],
        out_specs=pl.BlockSpec((tm, cw), lambda i, j: (i, j)),
        out_shape=jax.ShapeDtypeStruct((t, ch), F32),
        scratch_shapes=[pltpu.VMEM((tm + 2 * halo, cw), F32),
                        pltpu.VMEM((len(shifts), tm + qmax, cw), F32)],
        compiler_params=_cparams(("parallel", "parallel")),
        name="sconv",
    )(proj, proj, proj, jnp.repeat(w, SUBLANES, axis=0), b.reshape(1, ch))


def _split3(f):
    hi = f.astype(BF16)
    r1 = f - hi.astype(F32)
    mid = r1.astype(BF16)
    lo = (r1 - mid.astype(F32)).astype(BF16)
    return hi, mid, lo


def _expand_heads(f, e):
    hi, mid, lo = _split3(f)
    return (jnp.dot(hi, e, preferred_element_type=F32)
            + jnp.dot(mid, e, preferred_element_type=F32)
            + jnp.dot(lo, e, preferred_element_type=F32))


def _ssd_both_kernel(xf_ref, bf_ref, cf_ref, dtf_ref, xb_ref, bb_ref, cb_ref, dtb_ref, bias_ref,
                     alog_ref, ef_ref, eb_ref, of_ref, ob_ref, *scratch):
    nscr = len(scratch) // 2
    fwd, bwd = scratch[:nscr], scratch[nscr:]
    nheads = xf_ref.shape[1] // SSM_HEAD_DIM

    @pl.when(pl.program_id(1) == 0)
    def _():
        fwd[0][...] = jnp.zeros_like(fwd[0])
        bwd[0][...] = jnp.zeros_like(bwd[0])

    _ssd_direction(xf_ref, bf_ref, cf_ref, dtf_ref, bias_ref, alog_ref, ef_ref, of_ref, *fwd,
                   reverse=False, lane0=0)
    _ssd_direction(xb_ref, bb_ref, cb_ref, dtb_ref, bias_ref, alog_ref, eb_ref, ob_ref, *bwd,
                   reverse=True, lane0=nheads)


def _ssd_direction(x_ref, b_ref, c_ref, dt_ref, dtb_ref, alog_ref, e_ref, o_ref, state_ref,
                   acs_s, acst_s, dtt_s, wstt_s, carry_s, cb_s, bgt_s, *, reverse, lane0):
    q = x_ref.shape[0]
    nheads = x_ref.shape[1] // SSM_HEAD_DIM
    hpg = nheads // SSM_GROUPS

    row = lax.broadcasted_iota(jnp.int32, (q, q), 0)
    col = lax.broadcasted_iota(jnp.int32, (q, q), 1)
    keep = (col >= row) if reverse else (col <= row)

    z = dt_ref[...] + dtb_ref[...]
    dt = jnp.maximum(z, 0.0) + jnp.log(1.0 + jnp.exp(-jnp.abs(z)))
    a = dt * (-jnp.exp(alog_ref[...]))
    acs = jnp.dot(keep.astype(F32), a, precision=HIGHEST, preferred_element_type=F32)
    tot = acs[0:1, :] if reverse else acs[q - 1:q, :]
    acs2 = acs * LOG2E
    acs_s[...] = acs2
    acst_s[...] = acs2.T
    dtt_s[...] = dt.T
    wstt_s[...] = (dt * jnp.exp(tot - acs)).T
    carry_s[...] = _expand_heads(jnp.broadcast_to(jnp.exp(tot), (SUBLANES, LANES)), e_ref[...])

    for g in range(SSM_GROUPS):
        gsl = slice(g * SSM_STATE, (g + 1) * SSM_STATE)
        cb_s[g] = lax.dot_general(c_ref[:, gsl].astype(BF16), b_ref[:, gsl].astype(BF16),
                                  (((1,), (1,)), ((), ())), preferred_element_type=F32)
        bgt_s[g] = b_ref[:, gsl].T
        for pr in range(hpg // 2):
            ps = slice((g * hpg + 2 * pr) * SSM_HEAD_DIM, (g * hpg + 2 * pr + 2) * SSM_HEAD_DIM)
            xp = x_ref[:, ps].astype(BF16)
            rhs = jnp.concatenate([xp, state_ref[:, ps].astype(BF16)], axis=0)
            for sub in range(2):
                hl = lane0 + g * hpg + 2 * pr + sub
                half = slice(sub * SSM_HEAD_DIM, (sub + 1) * SSM_HEAD_DIM)
                hs = slice(ps.start + half.start, ps.start + half.stop)
                colb = jnp.broadcast_to(acs_s[:, hl:hl + 1], (q, q))
                seg = colb - acst_s[hl:hl + 1, :]
                lmat = jnp.exp2(jnp.where(keep, seg, -jnp.inf))
                m = (cb_s[g] * lmat * dtt_s[hl:hl + 1, :]).astype(BF16)
                cexp = (c_ref[:, gsl] * jnp.exp2(colb)).astype(BF16)
                y = jnp.dot(jnp.concatenate([m, cexp], axis=1), rhs,
                            preferred_element_type=F32)
                o_ref[:, hs] = y[:, half]
                bw = (bgt_s[g] * wstt_s[hl:hl + 1, :]).astype(BF16)
                upd = jnp.dot(bw, xp, preferred_element_type=F32)
                state_ref[:, hs] = state_ref[:, hs] * carry_s[0:1, hs] + upd[:, half]


def _ssd(xbc, proj, dt_col_block, dt_bias, a_log, expand_f, expand_b, bsz, seq):
    t = xbc.shape[0]
    q = SSM_CHUNK
    nc = seq // q
    inner = expand_f.shape[1]
    gn = SSM_GROUPS * SSM_STATE
    b_blk = inner // gn
    c_blk = b_blk + 1

    def fwd(b, c):
        return b * nc + c

    def bwd(b, c):
        return b * nc + nc - 1 - c

    def chunk_specs(tok):
        return [pl.BlockSpec((q, inner), lambda b, c: (tok(b, c), 0)),
                pl.BlockSpec((q, gn), lambda b, c: (tok(b, c), b_blk)),
                pl.BlockSpec((q, gn), lambda b, c: (tok(b, c), c_blk)),
                pl.BlockSpec((q, LANES), lambda b, c: (tok(b, c), dt_col_block))]

    def const(shape):
        return pl.BlockSpec(shape, lambda b, c: (0, 0))

    direction_scratch = [pltpu.VMEM((SSM_STATE, inner), F32),
                         pltpu.VMEM((q, LANES), F32), pltpu.VMEM((LANES, q), F32),
                         pltpu.VMEM((LANES, q), F32), pltpu.VMEM((LANES, q), F32),
                         pltpu.VMEM((SUBLANES, inner), F32),
                         pltpu.VMEM((SSM_GROUPS, q, q), F32),
                         pltpu.VMEM((SSM_GROUPS, SSM_STATE, q), F32)]
    return pl.pallas_call(
        _ssd_both_kernel,
        grid=(bsz, nc),
        in_specs=chunk_specs(fwd) + chunk_specs(bwd) + [
            const((1, LANES)), const((1, LANES)), const((LANES, inner)), const((LANES, inner))],
        out_specs=[pl.BlockSpec((q, inner), lambda b, c: (fwd(b, c), 0)),
                   pl.BlockSpec((q, inner), lambda b, c: (bwd(b, c), 0))],
        out_shape=[jax.ShapeDtypeStruct((t, inner), F32), jax.ShapeDtypeStruct((t, inner), F32)],
        scratch_shapes=direction_scratch + direction_scratch,
        compiler_params=_cparams(("arbitrary", "arbitrary")),
        name="ssd",
    )(xbc, xbc, xbc, proj, xbc, xbc, xbc, proj, dt_bias, a_log, expand_f, expand_b)


def _merge_kernel(yf_ref, yb_ref, xc_ref, z_ref, hc_ref, gc_ref, gs_ref, x_ref,
                  dexp_ref, ng_ref, wso_ref, wco_ref, bco_ref, wo_ref, g1_ref,
                  scale_ref, shift_ref, wr_ref, br_ref,
                  x1_ref, v_ref, lg_ref, *, ngroups, halves):
    tm, inner = yf_ref.shape
    gw = inner // ngroups
    hm = tm // halves
    for h in range(halves):
        r = slice(h * hm, (h + 1) * hm)
        y = (yf_ref[r, :] + yb_ref[r, :] + xc_ref[r, :] * dexp_ref[...]) * _silu(z_ref[r, :])
        parts = []
        for g in range(ngroups):
            yg = y[:, g * gw:(g + 1) * gw]
            ms = jnp.mean(yg * yg, axis=-1, keepdims=True)
            parts.append((yg * lax.rsqrt(ms + EPS)) * ng_ref[:, g * gw:(g + 1) * gw])
        ysn = jnp.concatenate(parts, axis=-1).astype(BF16)
        y_ssm = jnp.dot(ysn, wso_ref[...], preferred_element_type=F32)
        y_conf = jnp.dot(hc_ref[r, :], wco_ref[...], preferred_element_type=F32) + bco_ref[...]
        merged = _sigmoid(gc_ref[r, :]) * y_conf + _sigmoid(gs_ref[r, :]) * y_ssm
        o = jnp.dot(merged.astype(BF16), wo_ref[...], preferred_element_type=F32)
        x1 = x_ref[r, :] + g1_ref[0] * o
        x1_ref[r, :] = x1
        v = _modulated_norm(x1, scale_ref[0], shift_ref[0])
        v_ref[r, :] = v
        v_hi = v.astype(BF16)
        v_lo = (v - v_hi.astype(F32)).astype(BF16)
        p = (jnp.dot(v_hi, wr_ref[...], preferred_element_type=F32)
             + jnp.dot(v_lo, wr_ref[...], preferred_element_type=F32))
        lg_ref[r, :] = p[:, :LANES] + p[:, LANES:] + br_ref[...]


def _merge(yf, yb, xbc, proj, hc, x, cols, dexp, norm_g, wso, wco, bco, wo, g1, scale, shift,
           wr, br, seq, tm):
    t, d = x.shape
    inner = yf.shape[1]
    tiles_per_seq = seq // tm
    z_blk = cols["z"] // inner
    gc_blk = cols["gate_conf"] // d
    gs_blk = cols["gate_ssm"] // d

    def const(shape):
        return pl.BlockSpec(shape, lambda i: tuple(0 for _ in shape),
                            pipeline_mode=pl.Buffered(1))

    def per_seq():
        return pl.BlockSpec((1, 1, d), lambda i: (i // tiles_per_seq, 0, 0))

    kern = functools.partial(_merge_kernel, ngroups=SSM_GROUPS, halves=2)
    return pl.pallas_call(
        kern,
        grid=(t // tm,),
        in_specs=[pl.BlockSpec((tm, inner), lambda i: (i, 0)),
                  pl.BlockSpec((tm, inner), lambda i: (i, 0)),
                  pl.BlockSpec((tm, inner), lambda i: (i, 0)),
                  pl.BlockSpec((tm, inner), lambda i: (i, z_blk)),
                  pl.BlockSpec((tm, d), lambda i: (i, 0)),
                  pl.BlockSpec((tm, d), lambda i: (i, gc_blk)),
                  pl.BlockSpec((tm, d), lambda i: (i, gs_blk)),
                  pl.BlockSpec((tm, d), lambda i: (i, 0)),
                  const((1, inner)), const((1, inner)),
                  const((inner, d)), const((d, d)), const((1, d)), const((d, d)),
                  per_seq(), per_seq(), per_seq(),
                  const((d, 2 * LANES)), const((1, LANES))],
        out_specs=[pl.BlockSpec((tm, d), lambda i: (i, 0)),
                   pl.BlockSpec((tm, d), lambda i: (i, 0)),
                   pl.BlockSpec((tm, LANES), lambda i: (i, 0))],
        out_shape=[jax.ShapeDtypeStruct((t, d), F32),
                   jax.ShapeDtypeStruct((t, d), F32),
                   jax.ShapeDtypeStruct((t, LANES), F32)],
        compiler_params=_cparams(("parallel",)),
        name="merge",
    )(yf, yb, xbc, proj, hc, proj, proj, x, dexp, norm_g, wso, wco, bco, wo, g1, scale, shift,
      wr, br)


ROUTE_TILE = 256


def _route_kernel(lg_ref, idx_ref, w_ref, lslot_ref, lslot_t_ref, cnt_ref, *, n_experts):
    tm = lg_ref.shape[0]
    lane = lax.broadcasted_iota(jnp.int32, (tm, LANES), 1)
    lg = jnp.where(lane < n_experts, lg_ref[...], -jnp.inf)
    sel = jnp.zeros((tm, LANES), F32)
    vals, idxs = [], []
    for _ in range(TOP_K):
        m = jnp.max(lg, axis=-1, keepdims=True)
        ix = jnp.min(jnp.where(lg == m, lane, LANES), axis=-1, keepdims=True)
        hit = lane == ix
        sel = jnp.where(hit, 1.0, sel)
        lg = jnp.where(hit, -jnp.inf, lg)
        vals.append(m)
        idxs.append(ix)
    ex = [jnp.exp(v - vals[0]) for v in vals]
    den = ex[0] + ex[1] + ex[2] + ex[3]
    row = lax.broadcasted_iota(jnp.int32, (tm, tm), 0)
    col = lax.broadcasted_iota(jnp.int32, (tm, tm), 1)
    rank = jnp.dot((col < row).astype(BF16), sel.astype(BF16), preferred_element_type=F32)
    cnt = jnp.sum(sel, axis=0, keepdims=True)
    erow = lax.broadcasted_iota(jnp.int32, (LANES, LANES), 0)
    ecol = lax.broadcasted_iota(jnp.int32, (LANES, LANES), 1)
    off = jnp.dot(jnp.broadcast_to(cnt, (SUBLANES, LANES)).astype(BF16),
                  (erow < ecol).astype(BF16), preferred_element_type=F32)[0:1, :]
    slot_of = rank + off
    idx_out = jnp.zeros((tm, LANES), jnp.int32)
    w_out = jnp.zeros((tm, LANES), F32)
    slot_out = jnp.zeros((tm, LANES), F32)
    for k in range(TOP_K):
        sk = jnp.sum(jnp.where(lane == idxs[k], slot_of, 0.0), axis=-1, keepdims=True)
        idx_out = jnp.where(lane == k, idxs[k], idx_out)
        w_out = jnp.where(lane == k, ex[k] / den, w_out)
        slot_out = jnp.where(lane == k, sk, slot_out)
    idx_ref[...] = idx_out
    w_ref[...] = w_out
    lslot_ref[...] = slot_out.astype(jnp.int32)
    lslot_t_ref[...] = slot_out.T[0:SUBLANES, :].astype(jnp.int32)
    cnt_ref[0] = jnp.broadcast_to(cnt, (SUBLANES, LANES)).astype(jnp.int32)


def _route(logits, n_experts):
    t = logits.shape[0]
    tm = ROUTE_TILE
    nt = t // tm
    kern = functools.partial(_route_kernel, n_experts=n_experts)
    return pl.pallas_call(
        kern,
        grid=(nt,),
        in_specs=[pl.BlockSpec((tm, LANES), lambda i: (i, 0))],
        out_specs=[pl.BlockSpec((tm, LANES), lambda i: (i, 0)),
                   pl.BlockSpec((tm, LANES), lambda i: (i, 0)),
                   pl.BlockSpec((tm, LANES), lambda i: (i, 0)),
                   pl.BlockSpec((SUBLANES, tm), lambda i: (i, 0)),
                   pl.BlockSpec((1, SUBLANES, LANES), lambda i: (i, 0, 0))],
        out_shape=[jax.ShapeDtypeStruct((t, LANES), jnp.int32),
                   jax.ShapeDtypeStruct((t, LANES), F32),
                   jax.ShapeDtypeStruct((t, LANES), jnp.int32),
                   jax.ShapeDtypeStruct((nt * SUBLANES, tm), jnp.int32),
                   jax.ShapeDtypeStruct((nt, SUBLANES, LANES), jnp.int32)],
        compiler_params=_cparams(("parallel",)),
        name="route",
    )(logits)


def _row_copy(src, src_row, dst, dst_row, sem, nch):
    return pltpu.make_async_copy(_row(src, src_row, nch), _row(dst, dst_row, nch), sem)


RUN_BITS = ROUTE_TILE.bit_length()


def _run_piece_tables(cnt2, loc2, glob2):
    nt, ne = cnt2.shape
    bits = jnp.arange(RUN_BITS, dtype=jnp.int32)[None, :, None]
    n = cnt2[:, None, :]
    valid = ((n >> bits) & 1) == 1
    done = (n >> (bits + 1)) << (bits + 1)
    pos = jnp.cumsum(valid.astype(jnp.int32), axis=-1) - 1
    hit = valid[..., None] & (pos[..., None] == jnp.arange(ne, dtype=jnp.int32))
    loc = jnp.sum(jnp.where(hit, (loc2[:, None, :] + done)[..., None], 0), axis=2)
    glob = jnp.sum(jnp.where(hit, (glob2[:, None, :] + done)[..., None], 0), axis=2)
    npieces = jnp.sum(valid.astype(jnp.int32), axis=-1)
    return (npieces.reshape(-1).astype(jnp.int32), loc.reshape(-1).astype(jnp.int32),
            glob.reshape(-1).astype(jnp.int32))


def _segment_copies(np_ref, loc_ref, glob_ref, tile, n_experts, local_buf, hbm, sem, nch, to_hbm):
    for bit in range(RUN_BITS):
        rows = (1 << bit) * nch
        base = tile * RUN_BITS + bit

        def body(j, c, rows=rows, base=base):
            lo = loc_ref[base * n_experts + j]
            go = glob_ref[base * n_experts + j]
            loc = local_buf.at[pl.ds(pl.multiple_of(lo * nch, nch), rows), :]
            glb = hbm.at[pl.ds(pl.multiple_of(go * nch, nch), rows), :]
            if to_hbm:
                pltpu.make_async_copy(loc, glb, sem).start()
            else:
                pltpu.make_async_copy(glb, loc, sem).start()
            return c

        lax.fori_loop(0, np_ref[base], body, 0)


def _wait_segments(local_buf, hbm, sem, to_hbm):
    whole = hbm.at[pl.ds(0, local_buf.shape[0]), :]
    if to_hbm:
        pltpu.make_async_copy(local_buf, whole, sem).wait()
    else:
        pltpu.make_async_copy(whole, local_buf, sem).wait()


def _dispatch_kernel(cnt_ref, loc_ref, glob_ref, fill_ref, end_ref, lslot_t_ref, v_ref, xs_hbm,
                     buf, sems, fsem, *, n_experts, nch):
    i = pl.program_id(0)
    n = pl.num_programs(0)
    tm = v_ref.shape[0]
    nslots = tm * TOP_K
    slot = i % 2
    srow = lax.broadcasted_iota(jnp.int32, (nslots, tm), 0)
    hit = srow == lslot_t_ref[0:1, :]
    for k in range(1, TOP_K):
        hit = jnp.logical_or(hit, srow == lslot_t_ref[k:k + 1, :])
    p = jnp.where(hit, 1.0, 0.0).astype(BF16)
    v = v_ref[...]
    v_hi = v.astype(BF16)
    v_lo = (v - v_hi.astype(F32)).astype(BF16)
    xl = (jnp.dot(p, v_hi, preferred_element_type=F32)
          + jnp.dot(p, v_lo, preferred_element_type=F32))
    for s_ in range(nch):
        buf[slot, pl.ds(s_, nslots, stride=nch), :] = xl[:, s_ * LANES:(s_ + 1) * LANES]

    @pl.when(i == 0)
    def _():
        def fill(s, c):
            _row_copy(buf.at[0], 0, xs_hbm, s, fsem, nch).start()
            return c

        def drain(s, c):
            _row_copy(buf.at[0], 0, xs_hbm, s, fsem, nch).wait()
            return c

        group = 8
        for e0 in range(0, n_experts, group):
            for e in range(e0, min(e0 + group, n_experts)):
                lax.fori_loop(fill_ref[e], end_ref[e], fill, 0)
            for e in range(e0, min(e0 + group, n_experts)):
                lax.fori_loop(fill_ref[e], end_ref[e], drain, 0)

    args = (cnt_ref, loc_ref, glob_ref)
    _segment_copies(*args, i, n_experts, buf.at[slot], xs_hbm, sems.at[slot], nch, True)

    @pl.when(i > 0)
    def _():
        _wait_segments(buf.at[1 - slot], xs_hbm, sems.at[1 - slot], True)

    @pl.when(i == n - 1)
    def _():
        _wait_segments(buf.at[slot], xs_hbm, sems.at[slot], True)


def _dispatch(cnt_flat, loc_flat, glob_flat, fill_start, pad_end, lslot_t, v, n_slots):
    t, d = v.shape
    nch = d // LANES
    tm = ROUTE_TILE
    n_experts = fill_start.shape[0]
    kern = functools.partial(_dispatch_kernel, n_experts=n_experts, nch=nch)
    grid_spec = pltpu.PrefetchScalarGridSpec(
        num_scalar_prefetch=5,
        grid=(t // tm,),
        in_specs=[pl.BlockSpec((SUBLANES, tm), lambda i, *_: (i, 0)),
                  pl.BlockSpec((tm, d), lambda i, *_: (i, 0))],
        out_specs=pl.BlockSpec(memory_space=pl.ANY),
        scratch_shapes=[pltpu.VMEM((2, tm * TOP_K * nch, LANES), F32),
                        pltpu.SemaphoreType.DMA((2,)), pltpu.SemaphoreType.DMA],
    )
    return pl.pallas_call(
        kern,
        grid_spec=grid_spec,
        out_shape=jax.ShapeDtypeStruct((n_slots * nch, LANES), v.dtype),
        compiler_params=_cparams(("arbitrary",)),
        name="dispatch",
    )(cnt_flat, loc_flat, glob_flat, fill_start, pad_end, lslot_t, v)


def _moe_kernel(be_ref, nused_ref, rows_ref, x_ref, wgu_ref, bgu_ref, wd_ref, bd_ref, o_ref,
                wgu_bf, wd_bf):
    b = pl.program_id(0)
    changed = jnp.logical_or(b == 0, be_ref[b] != be_ref[jnp.maximum(b - 1, 0)])
    active = b < nused_ref[0]

    @pl.when(jnp.logical_and(active, changed))
    def _():
        wgu_bf[...] = wgu_ref[0].astype(BF16)
        wd_bf[...] = wd_ref[0].astype(BF16)

    ff = wd_bf.shape[0]
    d = wgu_bf.shape[0]
    nch = d // LANES
    hm = x_ref.shape[0] // nch // MOE_HALVES
    for h in range(MOE_HALVES):
        @pl.when(jnp.logical_and(active, rows_ref[b] > h * hm))
        def _():
            x = _load_rows(x_ref, hm, nch, h * hm).astype(BF16)
            gu = jnp.dot(x, wgu_bf[...], preferred_element_type=F32) + bgu_ref[0]
            gate = jnp.minimum(gu[:, :ff], SWIGLU_LIMIT)
            up = jnp.clip(gu[:, ff:], -SWIGLU_LIMIT, SWIGLU_LIMIT)
            glu = gate * _sigmoid(SWIGLU_ALPHA * gate)
            hid = ((up + 1.0) * glu).astype(BF16)
            y = jnp.dot(hid, wd_bf[...], preferred_element_type=F32) + bd_ref[0]
            _store_rows(o_ref, y, h * hm)


def _moe(block_expert, nused, block_rows, xs, w_gu, b_gu, w_down, b_down):
    ne, d, ff2 = w_gu.shape
    nch = d // LANES
    n_slots = xs.shape[0] // nch
    ff = w_down.shape[1]
    bm = MOE_BLOCK

    def blk(b, nu):
        return jnp.minimum(b, nu[0] - 1)

    def spec(shape, index):
        return pl.BlockSpec(shape, lambda b, be, nu, rows: index(b, be, nu))

    grid_spec = pltpu.PrefetchScalarGridSpec(
        num_scalar_prefetch=3,
        grid=(n_slots // bm,),
        in_specs=[spec((bm * nch, LANES), lambda b, be, nu: (blk(b, nu), 0)),
                  spec((1, d, ff2), lambda b, be, nu: (be[blk(b, nu)], 0, 0)),
                  spec((1, 1, ff2), lambda b, be, nu: (be[blk(b, nu)], 0, 0)),
                  spec((1, ff, d), lambda b, be, nu: (be[blk(b, nu)], 0, 0)),
                  spec((1, 1, d), lambda b, be, nu: (be[blk(b, nu)], 0, 0))],
        out_specs=spec((bm * nch, LANES), lambda b, be, nu: (blk(b, nu), 0)),
        scratch_shapes=[pltpu.VMEM((d, ff2), BF16), pltpu.VMEM((ff, d), BF16)],
    )
    return pl.pallas_call(
        _moe_kernel,
        grid_spec=grid_spec,
        out_shape=jax.ShapeDtypeStruct(xs.shape, F32),
        compiler_params=_cparams(("arbitrary",)),
        name="moe",
    )(block_expert, nused, block_rows, xs, w_gu, b_gu.reshape(ne, 1, ff2), w_down,
      b_down.reshape(ne, 1, d))


def _final_kernel(cnt_ref, loc_ref, glob_ref, x1_ref, lslot_ref, w_ref, g2_ref, scale_ref,
                  shift_ref, ys_hbm, o_ref, buf, sems, *, n_experts, nch):
    i = pl.program_id(0)
    n = pl.num_programs(0)
    tm = x1_ref.shape[0]
    nslots = tm * TOP_K
    slot = i % 2
    args = (cnt_ref, loc_ref, glob_ref)

    @pl.when(i == 0)
    def _():
        _segment_copies(*args, 0, n_experts, buf.at[0], ys_hbm, sems.at[0], nch, False)

    @pl.when(i + 1 < n)
    def _():
        _segment_copies(*args, i + 1, n_experts, buf.at[1 - slot], ys_hbm, sems.at[1 - slot], nch,
                        False)

    _wait_segments(buf.at[slot], ys_hbm, sems.at[slot], False)

    lane_slot = lax.broadcasted_iota(jnp.int32, (tm, nslots), 1)
    pw = jnp.zeros((tm, nslots), F32)
    for k in range(TOP_K):
        pw = jnp.where(lane_slot == lslot_ref[:, k:k + 1], w_ref[:, k:k + 1], pw)
    pw_hi = pw.astype(BF16)
    pw_lo = (pw - pw_hi.astype(F32)).astype(BF16)
    y = _load_rows(buf, nslots, nch, 0, (slot,))
    y_hi = y.astype(BF16)
    y_lo = (y - y_hi.astype(F32)).astype(BF16)
    f = (jnp.dot(pw_hi, y_hi, preferred_element_type=F32)
         + jnp.dot(pw_hi, y_lo, preferred_element_type=F32)
         + jnp.dot(pw_lo, y_hi, preferred_element_type=F32))
    x2 = x1_ref[...] + g2_ref[0] * f
    o_ref[...] = _modulated_norm(x2, scale_ref[0], shift_ref[0])


def _final(cnt_flat, loc_flat, glob_flat, n_experts, x1, ys, lslot, top_w, g2, scale, shift, seq):
    t, d = x1.shape
    nch = d // LANES
    tm = ROUTE_TILE
    tiles_per_seq = seq // tm

    def per_seq():
        return pl.BlockSpec((1, 1, d), lambda i, *_: (i // tiles_per_seq, 0, 0))

    grid_spec = pltpu.PrefetchScalarGridSpec(
        num_scalar_prefetch=3,
        grid=(t // tm,),
        in_specs=[pl.BlockSpec((tm, d), lambda i, *_: (i, 0)),
                  pl.BlockSpec((tm, LANES), lambda i, *_: (i, 0)),
                  pl.BlockSpec((tm, LANES), lambda i, *_: (i, 0)),
                  per_seq(), per_seq(), per_seq(),
                  pl.BlockSpec(memory_space=pl.ANY)],
        out_specs=pl.BlockSpec((tm, d), lambda i, *_: (i, 0)),
        scratch_shapes=[pltpu.VMEM((2, TOP_K * tm * nch, LANES), F32),
                        pltpu.SemaphoreType.DMA((2,))],
    )
    return pl.pallas_call(
        functools.partial(_final_kernel, n_experts=n_experts, nch=nch),
        grid_spec=grid_spec,
        out_shape=jax.ShapeDtypeStruct((t, d), F32),
        compiler_params=_cparams(("arbitrary",)),
        name="final",
    )(cnt_flat, loc_flat, glob_flat, x1, lslot, top_w, g2, scale, shift, ys)


def _pad_cols(w, n):
    return jnp.pad(w, ((0, 0), (0, n - w.shape[1])))


def kernel(x, c, ada_w, ada_b, norm_mix_g, w_in, conf_dw_w, conf_dw_b, conf_ln_g, conf_ln_b,
           conf_out_w, conf_out_b, ssm_conv_w, ssm_conv_b, dt_bias_f, dt_bias_b, a_log_f,
           a_log_b, ssm_d, ssm_norm_g, ssm_out_w, w_o, norm_ffn_g, router_w, router_b, w_gu,
           b_gu, w_down, b_down, final_ada_w, final_ada_b, final_norm_g):
    bsz, seq, d = x.shape
    depth = ada_w.shape[0]
    t = bsz * seq
    nheads = a_log_f.shape[1]
    inner = nheads * SSM_HEAD_DIM
    gn = SSM_GROUPS * SSM_STATE
    conf = conf_dw_w.shape[2]
    n_experts = router_w.shape[2]
    assert 2 * nheads <= LANES and n_experts <= LANES

    c_pad = jnp.zeros((SUBLANES, d), F32).at[:bsz].set(c)
    fin = _ada(c_pad, final_ada_w, final_ada_b)[:bsz]
    xf = x.reshape(t, d)

    sizes = [("conf_a", conf), ("conf_g", conf), ("z", inner), ("xs", inner), ("bm", gn),
             ("cm", gn), ("gate_conf", d), ("gate_ssm", d), ("dt", LANES)]
    cols, off = {}, 0
    for name, n in sizes:
        cols[name] = off
        off += n
    src = {}
    o = 0
    for name, n in [("conf_a", conf), ("conf_g", conf), ("z", inner), ("xs", inner), ("bm", gn),
                    ("cm", gn), ("dtf", nheads), ("dtb", nheads), ("gate_conf", d),
                    ("gate_ssm", d)]:
        src[name] = (o, o + n)
        o += n

    head_of_col = jnp.arange(inner, dtype=jnp.int32) // SSM_HEAD_DIM
    lanes = jnp.arange(LANES, dtype=jnp.int32)[:, None]
    expand_f = (lanes == head_of_col[None, :]).astype(BF16)
    expand_b = (lanes == head_of_col[None, :] + nheads).astype(BF16)

    for l in range(depth):
        ada = _ada(c_pad, ada_w[l], ada_b[l])[:bsz]
        sh1, sc1, g1, sh2, sc2, g2 = [a.reshape(bsz, 1, d) for a in jnp.split(ada, 6, axis=-1)]
        scale1 = norm_mix_g[l][None, None, :] * (1.0 + sc1)
        scale2 = norm_ffn_g[l][None, None, :] * (1.0 + sc2)

        wl = w_in[l]
        w_dt = _pad_cols(jnp.concatenate([wl[:, slice(*src["dtf"])], wl[:, slice(*src["dtb"])]],
                                         axis=1), LANES)
        w_perm = jnp.concatenate(
            [wl[:, slice(*src[n])] for n in ("conf_a", "conf_g", "z", "xs", "bm", "cm",
                                             "gate_conf", "gate_ssm")] + [w_dt],
            axis=1).astype(BF16)
        proj = _inproj(xf, scale1, sh1, w_perm, seq, tm=256, col_splits=1)

        hc = _cconv(proj, conf_dw_w[l], conf_dw_b[l], conf_ln_g[l], conf_ln_b[l], seq, tm=256)
        xbc = _sconv(proj, cols["xs"], ssm_conv_w[l], ssm_conv_b[l], seq, tm=256)

        dt_bias = _pad_cols(jnp.concatenate([dt_bias_f[l], dt_bias_b[l]])[None, :], LANES)
        a_log = _pad_cols(jnp.concatenate([a_log_f[l], a_log_b[l]])[None, :], LANES)
        dt_blk = cols["dt"] // LANES
        y_f, y_b = _ssd(xbc, proj, dt_blk, dt_bias, a_log, expand_f, expand_b, bsz, seq)

        dexp = jnp.repeat(ssm_d[l], SSM_HEAD_DIM)[None, :]
        wr = _pad_cols(router_w[l], LANES)
        wr_hi = wr.astype(BF16)
        wr = jnp.concatenate([wr_hi, (wr - wr_hi.astype(F32)).astype(BF16)], axis=1)
        br = _pad_cols(router_b[l][None, :], LANES)
        x1, v, logits = _merge(
            y_f, y_b, xbc, proj, hc, xf, cols, dexp, ssm_norm_g[l][None, :],
            ssm_out_w[l].astype(BF16), conf_out_w[l].astype(BF16), conf_out_b[l][None, :],
            w_o[l].astype(BF16), g1, scale2, sh2, wr, br, seq, tm=256)

        top_i, top_w, lslot, lslot_t, tile_cnt = _route(logits, n_experts)
        cnt2 = tile_cnt[:, 0, :n_experts]
        counts = jnp.sum(cnt2, axis=0)
        bm = MOE_BLOCK
        n_blocks = (t * TOP_K) // bm + n_experts
        padded = ((counts + bm - 1) // bm) * bm
        pad_end = jnp.cumsum(padded)
        pad_start = pad_end - padded
        loc2 = jnp.cumsum(cnt2, axis=1) - cnt2
        glob2 = pad_start[None, :] + jnp.cumsum(cnt2, axis=0) - cnt2
        tables = _run_piece_tables(cnt2, loc2, glob2)
        block_first = jnp.arange(n_blocks, dtype=jnp.int32) * bm
        block_expert = jnp.minimum(
            jnp.sum((pad_end[None, :] <= block_first[:, None]).astype(jnp.int32), axis=1),
            n_experts - 1).astype(jnp.int32)
        nused = (pad_end[-1] // bm).astype(jnp.int32).reshape(1)
        xs = _dispatch(*tables, (pad_start + counts).astype(jnp.int32),
                       pad_end.astype(jnp.int32), lslot_t, v, n_blocks * bm)
        block_rows = jnp.clip(counts[block_expert] - (block_first - pad_start[block_expert]), 0,
                              bm).astype(jnp.int32)
        ys = _moe(block_expert, nused, block_rows, xs, w_gu[l], b_gu[l], w_down[l], b_down[l])

        if l == depth - 1:
            sh_f, sc_f = [a.reshape(bsz, 1, d) for a in jnp.split(fin, 2, axis=-1)]
            scale_f = final_norm_g[None, None, :] * (1.0 + sc_f)
            xf = _final(*tables, n_experts, x1, ys, lslot, top_w, g2, scale_f, sh_f, seq)
        else:
            raise NotImplementedError("depth > 1 is not wired")
    return xf.reshape(bsz, seq, d)
```

```python
import functools

import jax
import jax.numpy as jnp
from jax import lax
from jax.experimental import pallas as pl
from jax.experimental.pallas import tpu as pltpu

F32 = jnp.float32
BF16 = jnp.bfloat16
HIGHEST = lax.Precision.HIGHEST

EPS = 1e-6
LOG2E = 1.4426950408889634
CONF_KERNEL = 31
SSM_CONV = 5
SSM_HEAD_DIM = 64
SSM_GROUPS = 4
SSM_STATE = 128
SSM_CHUNK = 128
TOP_K = 4
SWIGLU_ALPHA = 1.702
SWIGLU_LIMIT = 7.0

LANES = 128
SUBLANES = 8
VMEM_LIMIT = 56 * 1024 * 1024

MOE_BLOCK = 512
MOE_HALVES = 2


def _cparams(sem):
    return pltpu.CompilerParams(dimension_semantics=sem, vmem_limit_bytes=VMEM_LIMIT)


def _store_rows(ref, val, row0=0):
    n, d = val.shape
    nch = d // LANES
    for s in range(nch):
        ref[pl.ds(row0 * nch + s, n, stride=nch), :] = val[:, s * LANES:(s + 1) * LANES]


def _load_rows(ref, n, nch, row0=0, lead=()):
    return jnp.concatenate(
        [ref[lead + (pl.ds(row0 * nch + s, n, stride=nch), slice(None))] for s in range(nch)],
        axis=-1)


def _row(ref, r, nch, lead=()):
    return ref.at[lead + (pl.ds(pl.multiple_of(r * nch, nch), nch), slice(None))]


def _sigmoid(x):
    return 1.0 / (1.0 + jnp.exp(-x))


def _silu(x):
    return x * _sigmoid(x)


def _ada_kernel(c_ref, w_ref, b_ref, o_ref):
    c = c_ref[...]
    o_ref[...] = jnp.dot(_silu(c), w_ref[...], precision=HIGHEST,
                         preferred_element_type=F32) + b_ref[...]


def _ada(c_pad, w, b):
    d, n = w.shape
    tn = 1024
    return pl.pallas_call(
        _ada_kernel,
        grid=(n // tn,),
        in_specs=[pl.BlockSpec((SUBLANES, d), lambda j: (0, 0)),
                  pl.BlockSpec((d, tn), lambda j: (0, j)),
                  pl.BlockSpec((1, tn), lambda j: (0, j))],
        out_specs=pl.BlockSpec((SUBLANES, tn), lambda j: (0, j)),
        out_shape=jax.ShapeDtypeStruct((SUBLANES, n), F32),
        compiler_params=_cparams(("parallel",)),
        name="ada",
    )(c_pad, w, b.reshape(1, n))


def _modulated_norm(x, scale, shift):
    ms = jnp.mean(x * x, axis=-1, keepdims=True)
    return (x * lax.rsqrt(ms + EPS)) * scale + shift


def _inproj_kernel(x_ref, scale_ref, shift_ref, w_ref, o_ref, *, col_chunk):
    u = _modulated_norm(x_ref[...], scale_ref[0], shift_ref[0]).astype(BF16)
    ncols = o_ref.shape[1]
    for c0 in range(0, ncols, col_chunk):
        c1 = min(c0 + col_chunk, ncols)
        o_ref[:, c0:c1] = jnp.dot(u, w_ref[:, c0:c1], preferred_element_type=F32)


def _inproj(x, scale, shift, w_bf, seq, tm, col_splits):
    t, d = x.shape
    ncols = w_bf.shape[1]
    nw = ncols // col_splits
    tiles_per_seq = seq // tm
    return pl.pallas_call(
        functools.partial(_inproj_kernel, col_chunk=1024),
        grid=(col_splits, t // tm),
        in_specs=[pl.BlockSpec((tm, d), lambda j, i: (i, 0)),
                  pl.BlockSpec((1, 1, d), lambda j, i: (i // tiles_per_seq, 0, 0)),
                  pl.BlockSpec((1, 1, d), lambda j, i: (i // tiles_per_seq, 0, 0)),
                  pl.BlockSpec((d, nw), lambda j, i: (0, j), pipeline_mode=pl.Buffered(1))],
        out_specs=pl.BlockSpec((tm, nw), lambda j, i: (i, j)),
        out_shape=jax.ShapeDtypeStruct((t, ncols), F32),
        compiler_params=_cparams(("parallel", "parallel")),
        name="inproj",
    )(x, scale, shift, w_bf)


def _fill_shifted(hbuf_ref, sh_ref, shifts, rows):
    for n, r in enumerate(shifts):
        sh_ref[n] = hbuf_ref[r:r + rows, :]


def _tap_plan(ktaps, halo):
    offs = [k + halo - ktaps // 2 for k in range(ktaps)]
    shifts = sorted({o % SUBLANES for o in offs} - {0})
    plan = [(shifts.index(o % SUBLANES) if o % SUBLANES else -1, (o // SUBLANES) * SUBLANES)
            for o in offs]
    return shifts, plan, max(o // SUBLANES for o in offs) * SUBLANES


def _conv_rows(hbuf_ref, sh_ref, w8_ref, plan, r0, rc):
    groups = rc // SUBLANES
    accs = [None] * groups
    for k, (n, q8) in enumerate(plan):
        w = w8_ref[k * SUBLANES:(k + 1) * SUBLANES, :]
        for j in range(groups):
            start = pl.multiple_of(r0 + q8 + j * SUBLANES, SUBLANES)
            rows = pl.ds(start, SUBLANES)
            src = hbuf_ref[rows, :] if n < 0 else sh_ref[n, rows, :]
            term = src * w
            accs[j] = term if accs[j] is None else accs[j] + term
    return jnp.concatenate(accs, axis=0)


def _cconv_kernel(a_ref, g_ref, ap_ref, gp_ref, an_ref, gn_ref, w_ref, b_ref, lg_ref, lb_ref,
                  o_ref, hbuf_ref, sh_ref, cbuf_ref, *, tiles_per_seq, halo, shifts, plan, rc,
                  ln_rows):
    i = pl.program_id(0)
    tm = a_ref.shape[0]
    first = (i % tiles_per_seq) == 0
    last = (i % tiles_per_seq) == tiles_per_seq - 1
    glu_p = ap_ref[...] * _sigmoid(gp_ref[...])
    glu_n = an_ref[...] * _sigmoid(gn_ref[...])
    hbuf_ref[0:halo, :] = jnp.where(first, 0.0, glu_p)
    hbuf_ref[halo:halo + tm, :] = a_ref[...] * _sigmoid(g_ref[...])
    hbuf_ref[halo + tm:2 * halo + tm, :] = jnp.where(last, 0.0, glu_n)
    _fill_shifted(hbuf_ref, sh_ref, shifts, sh_ref.shape[1])

    def body(j, carry):
        r0 = j * rc
        h = _conv_rows(hbuf_ref, sh_ref, w_ref, plan, r0, rc) + b_ref[...]
        cbuf_ref[pl.ds(pl.multiple_of(r0, rc), rc), :] = h
        return carry

    lax.fori_loop(0, tm // rc, body, 0)

    for c0 in range(0, tm, ln_rows):
        h = cbuf_ref[c0:c0 + ln_rows, :]
        mu = jnp.mean(h, axis=-1, keepdims=True)
        hc = h - mu
        var = jnp.mean(hc * hc, axis=-1, keepdims=True)
        y = hc * lax.rsqrt(var + EPS) * lg_ref[...] + lb_ref[...]
        o_ref[c0:c0 + ln_rows, :] = _silu(y).astype(o_ref.dtype)


def _cconv(proj, w, b, ln_g, ln_b, seq, tm):
    t = proj.shape[0]
    ch = w.shape[1]
    halo = 16
    rc = 16
    shifts, plan, qmax = _tap_plan(CONF_KERNEL, halo)
    tiles_per_seq = seq // tm
    hb = tm // halo
    nhb = t // halo
    cb = 1

    def prev_map(col):
        return lambda i: (jnp.maximum(i * hb - 1, 0), col)

    def next_map(col):
        return lambda i: (jnp.minimum((i + 1) * hb, nhb - 1), col)

    kern = functools.partial(_cconv_kernel, tiles_per_seq=tiles_per_seq, halo=halo,
                             shifts=shifts, plan=plan, rc=rc, ln_rows=64)
    return pl.pallas_call(
        kern,
        grid=(t // tm,),
        in_specs=[pl.BlockSpec((tm, ch), lambda i: (i, 0)),
                  pl.BlockSpec((tm, ch), lambda i: (i, cb)),
                  pl.BlockSpec((halo, ch), prev_map(0)),
                  pl.BlockSpec((halo, ch), prev_map(cb)),
                  pl.BlockSpec((halo, ch), next_map(0)),
                  pl.BlockSpec((halo, ch), next_map(cb)),
                  pl.BlockSpec((CONF_KERNEL * SUBLANES, ch), lambda i: (0, 0)),
                  pl.BlockSpec((1, ch), lambda i: (0, 0)),
                  pl.BlockSpec((1, ch), lambda i: (0, 0)),
                  pl.BlockSpec((1, ch), lambda i: (0, 0))],
        out_specs=pl.BlockSpec((tm, ch), lambda i: (i, 0)),
        out_shape=jax.ShapeDtypeStruct((t, ch), BF16),
        scratch_shapes=[pltpu.VMEM((tm + 2 * halo, ch), F32),
                        pltpu.VMEM((len(shifts), tm + qmax, ch), F32),
                        pltpu.VMEM((tm, ch), F32)],
        compiler_params=_cparams(("parallel",)),
        name="cconv",
    )(proj, proj, proj, proj, proj, proj, jnp.repeat(w, SUBLANES, axis=0), b.reshape(1, ch),
      ln_g.reshape(1, ch),
      ln_b.reshape(1, ch))


def _sconv_kernel(x_ref, xp_ref, xn_ref, w_ref, b_ref, o_ref, hbuf_ref, sh_ref, *,
                  tiles_per_seq, halo, shifts, plan, rc):
    i = pl.program_id(0)
    tm = x_ref.shape[0]
    first = (i % tiles_per_seq) == 0
    last = (i % tiles_per_seq) == tiles_per_seq - 1
    hbuf_ref[0:halo, :] = jnp.where(first, 0.0, xp_ref[...])
    hbuf_ref[halo:halo + tm, :] = x_ref[...]
    hbuf_ref[halo + tm:2 * halo + tm, :] = jnp.where(last, 0.0, xn_ref[...])
    _fill_shifted(hbuf_ref, sh_ref, shifts, sh_ref.shape[1])

    def body(j, carry):
        r0 = j * rc
        h = _conv_rows(hbuf_ref, sh_ref, w_ref, plan, r0, rc) + b_ref[...]
        o_ref[pl.ds(pl.multiple_of(r0, rc), rc), :] = _silu(h)
        return carry

    lax.fori_loop(0, tm // rc, body, 0)


def _sconv(proj, col0, w, b, seq, tm):
    t = proj.shape[0]
    ch = w.shape[1]
    cw = 1024
    halo = SUBLANES
    rc = 16
    shifts, plan, qmax = _tap_plan(SSM_CONV, halo)
    tiles_per_seq = seq // tm
    hb = tm // halo
    nhb = t // halo
    c0 = col0 // cw
    kern = functools.partial(_sconv_kernel, tiles_per_seq=tiles_per_seq, halo=halo,
                             shifts=shifts, plan=plan, rc=rc)
    return pl.pallas_call(
        kern,
        grid=(t // tm, ch // cw),
        in_specs=[pl.BlockSpec((tm, cw), lambda i, j: (i, c0 + j)),
                  pl.BlockSpec((halo, cw), lambda i, j: (jnp.maximum(i * hb - 1, 0), c0 + j)),
                  pl.BlockSpec((halo, cw),
                               lambda i, j: (jnp.minimum((i + 1) * hb, nhb - 1), c0 + j)),
                  pl.BlockSpec((SSM_CONV * SUBLANES, cw), lambda i, j: (0, j)),
                  pl.BlockSpec((1, cw), lambda i, j: (0, j))],
        out_specs=pl.BlockSpec((tm, cw), lambda i, j: (i, j)),
        out_shape=jax.ShapeDtypeStruct((t, ch), F32),
        scratch_shapes=[pltpu.VMEM((tm + 2 * halo, cw), F32),
                        pltpu.VMEM((len(shifts), tm + qmax, cw), F32)],
        compiler_params=_cparams(("parallel", "parallel")),
        name="sconv",
    )(proj, proj, proj, jnp.repeat(w, SUBLANES, axis=0), b.reshape(1, ch))


def _split3(f):
    hi = f.astype(BF16)
    r1 = f - hi.astype(F32)
    mid = r1.astype(BF16)
    lo = (r1 - mid.astype(F32)).astype(BF16)
    return hi, mid, lo


def _expand_heads(f, e):
    hi, mid, lo = _split3(f)
    return (jnp.dot(hi, e, preferred_element_type=F32)
            + jnp.dot(mid, e, preferred_element_type=F32)
            + jnp.dot(lo, e, preferred_element_type=F32))


def _ssd_both_kernel(xf_ref, bf_ref, cf_ref, dtf_ref, xb_ref, bb_ref, cb_ref, dtb_ref, bias_ref,
                     alog_ref, ef_ref, eb_ref, of_ref, ob_ref, *scratch):
    nscr = len(scratch) // 2
    fwd, bwd = scratch[:nscr], scratch[nscr:]
    nheads = xf_ref.shape[1] // SSM_HEAD_DIM

    @pl.when(pl.program_id(1) == 0)
    def _():
        fwd[0][...] = jnp.zeros_like(fwd[0])
        bwd[0][...] = jnp.zeros_like(bwd[0])

    _ssd_direction(xf_ref, bf_ref, cf_ref, dtf_ref, bias_ref, alog_ref, ef_ref, of_ref, *fwd,
                   reverse=False, lane0=0)
    _ssd_direction(xb_ref, bb_ref, cb_ref, dtb_ref, bias_ref, alog_ref, eb_ref, ob_ref, *bwd,
                   reverse=True, lane0=nheads)


def _ssd_direction(x_ref, b_ref, c_ref, dt_ref, dtb_ref, alog_ref, e_ref, o_ref, state_ref,
                   acs_s, acst_s, dtt_s, wstt_s, carry_s, cb_s, bgt_s, *, reverse, lane0):
    q = x_ref.shape[0]
    nheads = x_ref.shape[1] // SSM_HEAD_DIM
    hpg = nheads // SSM_GROUPS

    row = lax.broadcasted_iota(jnp.int32, (q, q), 0)
    col = lax.broadcasted_iota(jnp.int32, (q, q), 1)
    keep = (col >= row) if reverse else (col <= row)

    z = dt_ref[...] + dtb_ref[...]
    dt = jnp.maximum(z, 0.0) + jnp.log(1.0 + jnp.exp(-jnp.abs(z)))
    a = dt * (-jnp.exp(alog_ref[...]))
    acs = jnp.dot(keep.astype(F32), a, precision=HIGHEST, preferred_element_type=F32)
    tot = acs[0:1, :] if reverse else acs[q - 1:q, :]
    acs2 = acs * LOG2E
    acs_s[...] = acs2
    acst_s[...] = acs2.T
    dtt_s[...] = dt.T
    wstt_s[...] = (dt * jnp.exp(tot - acs)).T
    carry_s[...] = _expand_heads(jnp.broadcast_to(jnp.exp(tot), (SUBLANES, LANES)), e_ref[...])

    for g in range(SSM_GROUPS):
        gsl = slice(g * SSM_STATE, (g + 1) * SSM_STATE)
        cb_s[g] = lax.dot_general(c_ref[:, gsl].astype(BF16), b_ref[:, gsl].astype(BF16),
                                  (((1,), (1,)), ((), ())), preferred_element_type=F32)
        bgt_s[g] = b_ref[:, gsl].T
        for pr in range(hpg // 2):
            ps = slice((g * hpg + 2 * pr) * SSM_HEAD_DIM, (g * hpg + 2 * pr + 2) * SSM_HEAD_DIM)
            xp = x_ref[:, ps].astype(BF16)
            rhs = jnp.concatenate([xp, state_ref[:, ps].astype(BF16)], axis=0)
            for sub in range(2):
                hl = lane0 + g * hpg + 2 * pr + sub
                half = slice(sub * SSM_HEAD_DIM, (sub + 1) * SSM_HEAD_DIM)
                hs = slice(ps.start + half.start, ps.start + half.stop)
                colb = jnp.broadcast_to(acs_s[:, hl:hl + 1], (q, q))
                seg = colb - acst_s[hl:hl + 1, :]
                lmat = jnp.exp2(jnp.where(keep, seg, -jnp.inf))
                m = (cb_s[g] * lmat * dtt_s[hl:hl + 1, :]).astype(BF16)
                cexp = (c_ref[:, gsl] * jnp.exp2(colb)).astype(BF16)
                y = jnp.dot(jnp.concatenate([m, cexp], axis=1), rhs,
                            preferred_element_type=F32)
                o_ref[:, hs] = y[:, half].astype(o_ref.dtype)
                bw = (bgt_s[g] * wstt_s[hl:hl + 1, :]).astype(BF16)
                upd = jnp.dot(bw, xp, preferred_element_type=F32)
                state_ref[:, hs] = state_ref[:, hs] * carry_s[0:1, hs] + upd[:, half]


def _ssd(xbc, proj, dt_col_block, dt_bias, a_log, expand_f, expand_b, bsz, seq):
    t = xbc.shape[0]
    q = SSM_CHUNK
    nc = seq // q
    inner = expand_f.shape[1]
    gn = SSM_GROUPS * SSM_STATE
    b_blk = inner // gn
    c_blk = b_blk + 1

    def fwd(b, c):
        return b * nc + c

    def bwd(b, c):
        return b * nc + nc - 1 - c

    def chunk_specs(tok):
        return [pl.BlockSpec((q, inner), lambda b, c: (tok(b, c), 0)),
                pl.BlockSpec((q, gn), lambda b, c: (tok(b, c), b_blk)),
                pl.BlockSpec((q, gn), lambda b, c: (tok(b, c), c_blk)),
                pl.BlockSpec((q, LANES), lambda b, c: (tok(b, c), dt_col_block))]

    def const(shape):
        return pl.BlockSpec(shape, lambda b, c: (0, 0))

    direction_scratch = [pltpu.VMEM((SSM_STATE, inner), F32),
                         pltpu.VMEM((q, LANES), F32), pltpu.VMEM((LANES, q), F32),
                         pltpu.VMEM((LANES, q), F32), pltpu.VMEM((LANES, q), F32),
                         pltpu.VMEM((SUBLANES, inner), F32),
                         pltpu.VMEM((SSM_GROUPS, q, q), F32),
                         pltpu.VMEM((SSM_GROUPS, SSM_STATE, q), F32)]
    return pl.pallas_call(
        _ssd_both_kernel,
        grid=(bsz, nc),
        in_specs=chunk_specs(fwd) + chunk_specs(bwd) + [
            const((1, LANES)), const((1, LANES)), const((LANES, inner)), const((LANES, inner))],
        out_specs=[pl.BlockSpec((q, inner), lambda b, c: (fwd(b, c), 0)),
                   pl.BlockSpec((q, inner), lambda b, c: (bwd(b, c), 0))],
        out_shape=[jax.ShapeDtypeStruct((t, inner), BF16), jax.ShapeDtypeStruct((t, inner), BF16)],
        scratch_shapes=direction_scratch + direction_scratch,
        compiler_params=_cparams(("arbitrary", "arbitrary")),
        name="ssd",
    )(xbc, xbc, xbc, proj, xbc, xbc, xbc, proj, dt_bias, a_log, expand_f, expand_b)


def _merge_kernel(yf_ref, yb_ref, xc_ref, z_ref, hc_ref, gc_ref, gs_ref, x_ref,
                  dexp_ref, ng_ref, wso_ref, wco_ref, bco_ref, wo_ref, g1_ref,
                  scale_ref, shift_ref, wr_ref, br_ref,
                  x1_ref, v_ref, lg_ref, *, ngroups, halves):
    tm, inner = yf_ref.shape
    gw = inner // ngroups
    hm = tm // halves
    for h in range(halves):
        r = slice(h * hm, (h + 1) * hm)
        y = ((yf_ref[r, :].astype(F32) + yb_ref[r, :].astype(F32) + xc_ref[r, :] * dexp_ref[...])
             * _silu(z_ref[r, :]))
        parts = []
        for g in range(ngroups):
            yg = y[:, g * gw:(g + 1) * gw]
            ms = jnp.mean(yg * yg, axis=-1, keepdims=True)
            parts.append((yg * lax.rsqrt(ms + EPS)) * ng_ref[:, g * gw:(g + 1) * gw])
        ysn = jnp.concatenate(parts, axis=-1).astype(BF16)
        y_ssm = jnp.dot(ysn, wso_ref[...], preferred_element_type=F32)
        y_conf = jnp.dot(hc_ref[r, :], wco_ref[...], preferred_element_type=F32) + bco_ref[...]
        merged = _sigmoid(gc_ref[r, :]) * y_conf + _sigmoid(gs_ref[r, :]) * y_ssm
        o = jnp.dot(merged.astype(BF16), wo_ref[...], preferred_element_type=F32)
        x1 = x_ref[r, :] + g1_ref[0] * o
        x1_ref[r, :] = x1
        v = _modulated_norm(x1, scale_ref[0], shift_ref[0])
        v_ref[r, :] = v
        v_hi = v.astype(BF16)
        v_lo = (v - v_hi.astype(F32)).astype(BF16)
        p = (jnp.dot(v_hi, wr_ref[...], preferred_element_type=F32)
             + jnp.dot(v_lo, wr_ref[...], preferred_element_type=F32))
        lg_ref[r, :] = p[:, :LANES] + p[:, LANES:] + br_ref[...]


def _merge(yf, yb, xbc, proj, hc, x, cols, dexp, norm_g, wso, wco, bco, wo, g1, scale, shift,
           wr, br, seq, tm):
    t, d = x.shape
    inner = yf.shape[1]
    tiles_per_seq = seq // tm
    z_blk = cols["z"] // inner
    gc_blk = cols["gate_conf"] // d
    gs_blk = cols["gate_ssm"] // d

    def const(shape):
        return pl.BlockSpec(shape, lambda i: tuple(0 for _ in shape),
                            pipeline_mode=pl.Buffered(1))

    def per_seq():
        return pl.BlockSpec((1, 1, d), lambda i: (i // tiles_per_seq, 0, 0))

    kern = functools.partial(_merge_kernel, ngroups=SSM_GROUPS, halves=2)
    return pl.pallas_call(
        kern,
        grid=(t // tm,),
        in_specs=[pl.BlockSpec((tm, inner), lambda i: (i, 0)),
                  pl.BlockSpec((tm, inner), lambda i: (i, 0)),
                  pl.BlockSpec((tm, inner), lambda i: (i, 0)),
                  pl.BlockSpec((tm, inner), lambda i: (i, z_blk)),
                  pl.BlockSpec((tm, d), lambda i: (i, 0)),
                  pl.BlockSpec((tm, d), lambda i: (i, gc_blk)),
                  pl.BlockSpec((tm, d), lambda i: (i, gs_blk)),
                  pl.BlockSpec((tm, d), lambda i: (i, 0)),
                  const((1, inner)), const((1, inner)),
                  const((inner, d)), const((d, d)), const((1, d)), const((d, d)),
                  per_seq(), per_seq(), per_seq(),
                  const((d, 2 * LANES)), const((1, LANES))],
        out_specs=[pl.BlockSpec((tm, d), lambda i: (i, 0)),
                   pl.BlockSpec((tm, d), lambda i: (i, 0)),
                   pl.BlockSpec((tm, LANES), lambda i: (i, 0))],
        out_shape=[jax.ShapeDtypeStruct((t, d), F32),
                   jax.ShapeDtypeStruct((t, d), F32),
                   jax.ShapeDtypeStruct((t, LANES), F32)],
        compiler_params=_cparams(("parallel",)),
        name="merge",
    )(yf, yb, xbc, proj, hc, proj, proj, x, dexp, norm_g, wso, wco, bco, wo, g1, scale, shift,
      wr, br)


ROUTE_TILE = 256


def _route_kernel(lg_ref, idx_ref, w_ref, lslot_ref, lslot_t_ref, cnt_ref, *, n_experts):
    tm = lg_ref.shape[0]
    lane = lax.broadcasted_iota(jnp.int32, (tm, LANES), 1)
    lg = jnp.where(lane < n_experts, lg_ref[...], -jnp.inf)
    sel = jnp.zeros((tm, LANES), F32)
    vals, idxs = [], []
    for _ in range(TOP_K):
        m = jnp.max(lg, axis=-1, keepdims=True)
        ix = jnp.min(jnp.where(lg == m, lane, LANES), axis=-1, keepdims=True)
        hit = lane == ix
        sel = jnp.where(hit, 1.0, sel)
        lg = jnp.where(hit, -jnp.inf, lg)
        vals.append(m)
        idxs.append(ix)
    ex = [jnp.exp(v - vals[0]) for v in vals]
    den = ex[0] + ex[1] + ex[2] + ex[3]
    row = lax.broadcasted_iota(jnp.int32, (tm, tm), 0)
    col = lax.broadcasted_iota(jnp.int32, (tm, tm), 1)
    rank = jnp.dot((col < row).astype(BF16), sel.astype(BF16), preferred_element_type=F32)
    cnt = jnp.sum(sel, axis=0, keepdims=True)
    erow = lax.broadcasted_iota(jnp.int32, (LANES, LANES), 0)
    ecol = lax.broadcasted_iota(jnp.int32, (LANES, LANES), 1)
    off = jnp.dot(jnp.broadcast_to(cnt, (SUBLANES, LANES)).astype(BF16),
                  (erow < ecol).astype(BF16), preferred_element_type=F32)[0:1, :]
    slot_of = rank + off
    idx_out = jnp.zeros((tm, LANES), jnp.int32)
    w_out = jnp.zeros((tm, LANES), F32)
    slot_out = jnp.zeros((tm, LANES), F32)
    for k in range(TOP_K):
        sk = jnp.sum(jnp.where(lane == idxs[k], slot_of, 0.0), axis=-1, keepdims=True)
        idx_out = jnp.where(lane == k, idxs[k], idx_out)
        w_out = jnp.where(lane == k, ex[k] / den, w_out)
        slot_out = jnp.where(lane == k, sk, slot_out)
    idx_ref[...] = idx_out
    w_ref[...] = w_out
    lslot_ref[...] = slot_out.astype(jnp.int32)
    lslot_t_ref[...] = slot_out.T[0:SUBLANES, :].astype(jnp.int32)
    cnt_ref[0] = jnp.broadcast_to(cnt, (SUBLANES, LANES)).astype(jnp.int32)


def _route(logits, n_experts):
    t = logits.shape[0]
    tm = ROUTE_TILE
    nt = t // tm
    kern = functools.partial(_route_kernel, n_experts=n_experts)
    return pl.pallas_call(
        kern,
        grid=(nt,),
        in_specs=[pl.BlockSpec((tm, LANES), lambda i: (i, 0))],
        out_specs=[pl.BlockSpec((tm, LANES), lambda i: (i, 0)),
                   pl.BlockSpec((tm, LANES), lambda i: (i, 0)),
                   pl.BlockSpec((tm, LANES), lambda i: (i, 0)),
                   pl.BlockSpec((SUBLANES, tm), lambda i: (i, 0)),
                   pl.BlockSpec((1, SUBLANES, LANES), lambda i: (i, 0, 0))],
        out_shape=[jax.ShapeDtypeStruct((t, LANES), jnp.int32),
                   jax.ShapeDtypeStruct((t, LANES), F32),
                   jax.ShapeDtypeStruct((t, LANES), jnp.int32),
                   jax.ShapeDtypeStruct((nt * SUBLANES, tm), jnp.int32),
                   jax.ShapeDtypeStruct((nt, SUBLANES, LANES), jnp.int32)],
        compiler_params=_cparams(("parallel",)),
        name="route",
    )(logits)


def _row_copy(src, src_row, dst, dst_row, sem, nch):
    return pltpu.make_async_copy(_row(src, src_row, nch), _row(dst, dst_row, nch), sem)


RUN_BITS = ROUTE_TILE.bit_length()


def _excl_cumsum(x, axis):
    n = x.shape[axis]
    earlier = jnp.arange(n)[:, None] > jnp.arange(n)[None, :]
    xm = jnp.moveaxis(x, axis, -1)
    out = jnp.sum(jnp.where(earlier, xm[..., None, :], 0), axis=-1)
    return jnp.moveaxis(out, -1, axis)


def _run_piece_tables(cnt2, loc2, glob2):
    nt, ne = cnt2.shape
    bits = jnp.arange(RUN_BITS, dtype=jnp.int32)[:, None, None]
    n = cnt2.T[None, :, :]
    valid = ((n >> bits) & 1) == 1
    done = (n >> (bits + 1)) << (bits + 1)
    lower = jnp.arange(ne)[:, None] > jnp.arange(ne)[None, :]
    pos = jnp.sum(jnp.where(lower[None, :, :, None], valid[:, None, :, :], False), axis=2,
                  dtype=jnp.int32)
    j = jnp.tile(jnp.arange(ne, dtype=jnp.int32), nt)
    hit = jnp.repeat(valid, ne, axis=-1) & (jnp.repeat(pos, ne, axis=-1) == j)

    def compact(start2):
        val = jnp.repeat(start2.T[None, :, :] + done, ne, axis=-1)
        out = jnp.sum(jnp.where(hit, val, 0), axis=1)
        return out.reshape(RUN_BITS, nt, ne).transpose(1, 0, 2).reshape(-1).astype(jnp.int32)

    npieces = jnp.sum(valid.astype(jnp.int32), axis=1).T.reshape(-1).astype(jnp.int32)
    return npieces, compact(loc2), compact(glob2)


def _segment_copies(np_ref, loc_ref, glob_ref, tile, n_experts, local_buf, hbm, sem, nch, to_hbm):
    for bit in range(RUN_BITS):
        rows = (1 << bit) * nch
        base = tile * RUN_BITS + bit

        def body(j, c, rows=rows, base=base):
            lo = loc_ref[base * n_experts + j]
            go = glob_ref[base * n_experts + j]
            loc = local_buf.at[pl.ds(pl.multiple_of(lo * nch, nch), rows), :]
            glb = hbm.at[pl.ds(pl.multiple_of(go * nch, nch), rows), :]
            if to_hbm:
                pltpu.make_async_copy(loc, glb, sem).start()
            else:
                pltpu.make_async_copy(glb, loc, sem).start()
            return c

        lax.fori_loop(0, np_ref[base], body, 0)


def _wait_segments(local_buf, hbm, sem, to_hbm):
    whole = hbm.at[pl.ds(0, local_buf.shape[0]), :]
    if to_hbm:
        pltpu.make_async_copy(local_buf, whole, sem).wait()
    else:
        pltpu.make_async_copy(whole, local_buf, sem).wait()


def _dispatch_kernel(cnt_ref, loc_ref, glob_ref, fill_ref, end_ref, lslot_t_ref, v_ref, xs_hbm,
                     buf, sems, fsem, *, n_experts, nch):
    i = pl.program_id(0)
    n = pl.num_programs(0)
    tm = v_ref.shape[0]
    nslots = tm * TOP_K
    slot = i % 2
    srow = lax.broadcasted_iota(jnp.int32, (nslots, tm), 0)
    hit = srow == lslot_t_ref[0:1, :]
    for k in range(1, TOP_K):
        hit = jnp.logical_or(hit, srow == lslot_t_ref[k:k + 1, :])
    p = jnp.where(hit, 1.0, 0.0).astype(BF16)
    v = v_ref[...]
    v_hi = v.astype(BF16)
    v_lo = (v - v_hi.astype(F32)).astype(BF16)
    xl = (jnp.dot(p, v_hi, preferred_element_type=F32)
          + jnp.dot(p, v_lo, preferred_element_type=F32))
    for s_ in range(nch):
        buf[slot, pl.ds(s_, nslots, stride=nch), :] = xl[:, s_ * LANES:(s_ + 1) * LANES]

    @pl.when(i == 0)
    def _():
        def fill(s, c):
            _row_copy(buf.at[0], 0, xs_hbm, s, fsem, nch).start()
            return c

        def drain(s, c):
            _row_copy(buf.at[0], 0, xs_hbm, s, fsem, nch).wait()
            return c

        group = 8
        for e0 in range(0, n_experts, group):
            for e in range(e0, min(e0 + group, n_experts)):
                lax.fori_loop(fill_ref[e], end_ref[e], fill, 0)
            for e in range(e0, min(e0 + group, n_experts)):
                lax.fori_loop(fill_ref[e], end_ref[e], drain, 0)

    args = (cnt_ref, loc_ref, glob_ref)
    _segment_copies(*args, i, n_experts, buf.at[slot], xs_hbm, sems.at[slot], nch, True)

    @pl.when(i > 0)
    def _():
        _wait_segments(buf.at[1 - slot], xs_hbm, sems.at[1 - slot], True)

    @pl.when(i == n - 1)
    def _():
        _wait_segments(buf.at[slot], xs_hbm, sems.at[slot], True)


def _dispatch(cnt_flat, loc_flat, glob_flat, fill_start, pad_end, lslot_t, v, n_slots):
    t, d = v.shape
    nch = d // LANES
    tm = ROUTE_TILE
    n_experts = fill_start.shape[0]
    kern = functools.partial(_dispatch_kernel, n_experts=n_experts, nch=nch)
    grid_spec = pltpu.PrefetchScalarGridSpec(
        num_scalar_prefetch=5,
        grid=(t // tm,),
        in_specs=[pl.BlockSpec((SUBLANES, tm), lambda i, *_: (i, 0)),
                  pl.BlockSpec((tm, d), lambda i, *_: (i, 0))],
        out_specs=pl.BlockSpec(memory_space=pl.ANY),
        scratch_shapes=[pltpu.VMEM((2, tm * TOP_K * nch, LANES), F32),
                        pltpu.SemaphoreType.DMA((2,)), pltpu.SemaphoreType.DMA],
    )
    return pl.pallas_call(
        kern,
        grid_spec=grid_spec,
        out_shape=jax.ShapeDtypeStruct((n_slots * nch, LANES), v.dtype),
        compiler_params=_cparams(("arbitrary",)),
        name="dispatch",
    )(cnt_flat, loc_flat, glob_flat, fill_start, pad_end, lslot_t, v)


def _moe_kernel(be_ref, nused_ref, x_ref, wgu_ref, bgu_ref, wd_ref, bd_ref, o_ref,
                wgu_bf, wd_bf):
    b = pl.program_id(0)
    changed = jnp.logical_or(b == 0, be_ref[b] != be_ref[jnp.maximum(b - 1, 0)])
    active = b < nused_ref[0]

    @pl.when(jnp.logical_and(active, changed))
    def _():
        wgu_bf[...] = wgu_ref[0].astype(BF16)
        wd_bf[...] = wd_ref[0].astype(BF16)

    @pl.when(active)
    def _():
        ff = wd_bf.shape[0]
        d = wgu_bf.shape[0]
        nch = d // LANES
        hm = x_ref.shape[0] // nch // MOE_HALVES
        for h in range(MOE_HALVES):
            x = _load_rows(x_ref, hm, nch, h * hm).astype(BF16)
            gu = jnp.dot(x, wgu_bf[...], preferred_element_type=F32) + bgu_ref[0]
            gate = jnp.minimum(gu[:, :ff], SWIGLU_LIMIT)
            up = jnp.clip(gu[:, ff:], -SWIGLU_LIMIT, SWIGLU_LIMIT)
            glu = gate * _sigmoid(SWIGLU_ALPHA * gate)
            hid = ((up + 1.0) * glu).astype(BF16)
            y = jnp.dot(hid, wd_bf[...], preferred_element_type=F32) + bd_ref[0]
            _store_rows(o_ref, y, h * hm)


def _moe(block_expert, nused, xs, w_gu, b_gu, w_down, b_down):
    ne, d, ff2 = w_gu.shape
    nch = d // LANES
    n_slots = xs.shape[0] // nch
    ff = w_down.shape[1]
    bm = MOE_BLOCK

    def blk(b, nu):
        return jnp.minimum(b, nu[0] - 1)

    def spec(shape, index):
        return pl.BlockSpec(shape, index)

    grid_spec = pltpu.PrefetchScalarGridSpec(
        num_scalar_prefetch=2,
        grid=(n_slots // bm,),
        in_specs=[spec((bm * nch, LANES), lambda b, be, nu: (blk(b, nu), 0)),
                  spec((1, d, ff2), lambda b, be, nu: (be[blk(b, nu)], 0, 0)),
                  spec((1, 1, ff2), lambda b, be, nu: (be[blk(b, nu)], 0, 0)),
                  spec((1, ff, d), lambda b, be, nu: (be[blk(b, nu)], 0, 0)),
                  spec((1, 1, d), lambda b, be, nu: (be[blk(b, nu)], 0, 0))],
        out_specs=spec((bm * nch, LANES), lambda b, be, nu: (blk(b, nu), 0)),
        scratch_shapes=[pltpu.VMEM((d, ff2), BF16), pltpu.VMEM((ff, d), BF16)],
    )
    return pl.pallas_call(
        _moe_kernel,
        grid_spec=grid_spec,
        out_shape=jax.ShapeDtypeStruct(xs.shape, F32),
        compiler_params=_cparams(("arbitrary",)),
        name="moe",
    )(block_expert, nused, xs, w_gu, b_gu.reshape(ne, 1, ff2), w_down, b_down.reshape(ne, 1, d))


def _final_kernel(cnt_ref, loc_ref, glob_ref, x1_ref, lslot_ref, w_ref, g2_ref, scale_ref,
                  shift_ref, ys_hbm, o_ref, buf, sems, *, n_experts, nch):
    i = pl.program_id(0)
    n = pl.num_programs(0)
    tm = x1_ref.shape[0]
    nslots = tm * TOP_K
    slot = i % 2
    args = (cnt_ref, loc_ref, glob_ref)

    @pl.when(i == 0)
    def _():
        _segment_copies(*args, 0, n_experts, buf.at[0], ys_hbm, sems.at[0], nch, False)

    @pl.when(i + 1 < n)
    def _():
        _segment_copies(*args, i + 1, n_experts, buf.at[1 - slot], ys_hbm, sems.at[1 - slot], nch,
                        False)

    _wait_segments(buf.at[slot], ys_hbm, sems.at[slot], False)

    lane_slot = lax.broadcasted_iota(jnp.int32, (tm, nslots), 1)
    pw = jnp.zeros((tm, nslots), F32)
    for k in range(TOP_K):
        pw = jnp.where(lane_slot == lslot_ref[:, k:k + 1], w_ref[:, k:k + 1], pw)
    pw_hi = pw.astype(BF16)
    pw_lo = (pw - pw_hi.astype(F32)).astype(BF16)
    y = _load_rows(buf, nslots, nch, 0, (slot,))
    y_hi = y.astype(BF16)
    y_lo = (y - y_hi.astype(F32)).astype(BF16)
    f = (jnp.dot(pw_hi, y_hi, preferred_element_type=F32)
         + jnp.dot(pw_hi, y_lo, preferred_element_type=F32)
         + jnp.dot(pw_lo, y_hi, preferred_element_type=F32))
    x2 = x1_ref[...] + g2_ref[0] * f
    o_ref[...] = _modulated_norm(x2, scale_ref[0], shift_ref[0])


def _final(cnt_flat, loc_flat, glob_flat, n_experts, x1, ys, lslot, top_w, g2, scale, shift, seq):
    t, d = x1.shape
    nch = d // LANES
    tm = ROUTE_TILE
    tiles_per_seq = seq // tm

    def per_seq():
        return pl.BlockSpec((1, 1, d), lambda i, *_: (i // tiles_per_seq, 0, 0))

    grid_spec = pltpu.PrefetchScalarGridSpec(
        num_scalar_prefetch=3,
        grid=(t // tm,),
        in_specs=[pl.BlockSpec((tm, d), lambda i, *_: (i, 0)),
                  pl.BlockSpec((tm, LANES), lambda i, *_: (i, 0)),
                  pl.BlockSpec((tm, LANES), lambda i, *_: (i, 0)),
                  per_seq(), per_seq(), per_seq(),
                  pl.BlockSpec(memory_space=pl.ANY)],
        out_specs=pl.BlockSpec((tm, d), lambda i, *_: (i, 0)),
        scratch_shapes=[pltpu.VMEM((2, TOP_K * tm * nch, LANES), F32),
                        pltpu.SemaphoreType.DMA((2,))],
    )
    return pl.pallas_call(
        functools.partial(_final_kernel, n_experts=n_experts, nch=nch),
        grid_spec=grid_spec,
        out_shape=jax.ShapeDtypeStruct((t, d), F32),
        compiler_params=_cparams(("arbitrary",)),
        name="final",
    )(cnt_flat, loc_flat, glob_flat, x1, lslot, top_w, g2, scale, shift, ys)


def _pad_cols(w, n):
    return jnp.pad(w, ((0, 0), (0, n - w.shape[1])))


def kernel(x, c, ada_w, ada_b, norm_mix_g, w_in, conf_dw_w, conf_dw_b, conf_ln_g, conf_ln_b,
           conf_out_w, conf_out_b, ssm_conv_w, ssm_conv_b, dt_bias_f, dt_bias_b, a_log_f,
           a_log_b, ssm_d, ssm_norm_g, ssm_out_w, w_o, norm_ffn_g, router_w, router_b, w_gu,
           b_gu, w_down, b_down, final_ada_w, final_ada_b, final_norm_g):
    bsz, seq, d = x.shape
    depth = ada_w.shape[0]
    t = bsz * seq
    nheads = a_log_f.shape[1]
    inner = nheads * SSM_HEAD_DIM
    gn = SSM_GROUPS * SSM_STATE
    conf = conf_dw_w.shape[2]
    n_experts = router_w.shape[2]
    assert 2 * nheads <= LANES and n_experts <= LANES

    c_pad = jnp.zeros((SUBLANES, d), F32).at[:bsz].set(c)
    fin = _ada(c_pad, final_ada_w, final_ada_b)[:bsz]
    xf = x.reshape(t, d)

    sizes = [("conf_a", conf), ("conf_g", conf), ("z", inner), ("xs", inner), ("bm", gn),
             ("cm", gn), ("gate_conf", d), ("gate_ssm", d), ("dt", LANES)]
    cols, off = {}, 0
    for name, n in sizes:
        cols[name] = off
        off += n
    src = {}
    o = 0
    for name, n in [("conf_a", conf), ("conf_g", conf), ("z", inner), ("xs", inner), ("bm", gn),
                    ("cm", gn), ("dtf", nheads), ("dtb", nheads), ("gate_conf", d),
                    ("gate_ssm", d)]:
        src[name] = (o, o + n)
        o += n

    head_of_col = jnp.arange(inner, dtype=jnp.int32) // SSM_HEAD_DIM
    lanes = jnp.arange(LANES, dtype=jnp.int32)[:, None]
    expand_f = (lanes == head_of_col[None, :]).astype(BF16)
    expand_b = (lanes == head_of_col[None, :] + nheads).astype(BF16)

    for l in range(depth):
        ada = _ada(c_pad, ada_w[l], ada_b[l])[:bsz]
        sh1, sc1, g1, sh2, sc2, g2 = [a.reshape(bsz, 1, d) for a in jnp.split(ada, 6, axis=-1)]
        scale1 = norm_mix_g[l][None, None, :] * (1.0 + sc1)
        scale2 = norm_ffn_g[l][None, None, :] * (1.0 + sc2)

        wl = w_in[l]
        w_dt = _pad_cols(jnp.concatenate([wl[:, slice(*src["dtf"])], wl[:, slice(*src["dtb"])]],
                                         axis=1), LANES)
        w_perm = jnp.concatenate(
            [wl[:, slice(*src[n])] for n in ("conf_a", "conf_g", "z", "xs", "bm", "cm",
                                             "gate_conf", "gate_ssm")] + [w_dt],
            axis=1).astype(BF16)
        proj = _inproj(xf, scale1, sh1, w_perm, seq, tm=256, col_splits=1)

        hc = _cconv(proj, conf_dw_w[l], conf_dw_b[l], conf_ln_g[l], conf_ln_b[l], seq, tm=256)
        xbc = _sconv(proj, cols["xs"], ssm_conv_w[l], ssm_conv_b[l], seq, tm=256)

        dt_bias = _pad_cols(jnp.concatenate([dt_bias_f[l], dt_bias_b[l]])[None, :], LANES)
        a_log = _pad_cols(jnp.concatenate([a_log_f[l], a_log_b[l]])[None, :], LANES)
        dt_blk = cols["dt"] // LANES
        y_f, y_b = _ssd(xbc, proj, dt_blk, dt_bias, a_log, expand_f, expand_b, bsz, seq)

        dexp = jnp.repeat(ssm_d[l], SSM_HEAD_DIM)[None, :]
        wr = _pad_cols(router_w[l], LANES)
        wr_hi = wr.astype(BF16)
        wr = jnp.concatenate([wr_hi, (wr - wr_hi.astype(F32)).astype(BF16)], axis=1)
        br = _pad_cols(router_b[l][None, :], LANES)
        x1, v, logits = _merge(
            y_f, y_b, xbc, proj, hc, xf, cols, dexp, ssm_norm_g[l][None, :],
            ssm_out_w[l].astype(BF16), conf_out_w[l].astype(BF16), conf_out_b[l][None, :],
            w_o[l].astype(BF16), g1, scale2, sh2, wr, br, seq, tm=256)

        top_i, top_w, lslot, lslot_t, tile_cnt = _route(logits, n_experts)
        cnt2 = tile_cnt[:, 0, :n_experts]
        counts = jnp.sum(cnt2, axis=0)
        bm = MOE_BLOCK
        n_blocks = (t * TOP_K) // bm + n_experts
        padded = ((counts + bm - 1) // bm) * bm
        pad_start = _excl_cumsum(padded, axis=0)
        pad_end = pad_start + padded
        loc2 = _excl_cumsum(cnt2, axis=1)
        glob2 = pad_start[None, :] + _excl_cumsum(cnt2, axis=0)
        tables = _run_piece_tables(cnt2, loc2, glob2)
        block_first = jnp.arange(n_blocks, dtype=jnp.int32) * bm
        block_expert = jnp.minimum(
            jnp.sum((pad_end[None, :] <= block_first[:, None]).astype(jnp.int32), axis=1),
            n_experts - 1).astype(jnp.int32)
        nused = (pad_end[-1] // bm).astype(jnp.int32).reshape(1)
        xs = _dispatch(*tables, (pad_start + counts).astype(jnp.int32),
                       pad_end.astype(jnp.int32), lslot_t, v, n_blocks * bm)
        ys = _moe(block_expert, nused, xs, w_gu[l], b_gu[l], w_down[l], b_down[l])

        if l == depth - 1:
            sh_f, sc_f = [a.reshape(bsz, 1, d) for a in jnp.split(fin, 2, axis=-1)]
            scale_f = final_norm_g[None, None, :] * (1.0 + sc_f)
            xf = _final(*tables, n_experts, x1, ys, lslot, top_w, g2, scale_f, sh_f, seq)
        else:
            raise NotImplementedError("depth > 1 is not wired")
    return xf.reshape(bsz, seq, d)
```

```python
import functools

import jax
import jax.numpy as jnp
from jax import lax
from jax.experimental import pallas as pl
from jax.experimental.pallas import tpu as pltpu

F32 = jnp.float32
BF16 = jnp.bfloat16
HIGHEST = lax.Precision.HIGHEST

EPS = 1e-6
LOG2E = 1.4426950408889634
CONF_KERNEL = 31
SSM_CONV = 5
SSM_HEAD_DIM = 64
SSM_GROUPS = 4
SSM_STATE = 128
SSM_CHUNK = 128
TOP_K = 4
SWIGLU_ALPHA = 1.702
SWIGLU_LIMIT = 7.0

LANES = 128
SUBLANES = 8
VMEM_LIMIT = 56 * 1024 * 1024

MOE_BLOCK = 512
MOE_HALVES = 2


def _cparams(sem):
    return pltpu.CompilerParams(dimension_semantics=sem, vmem_limit_bytes=VMEM_LIMIT)


def _store_rows(ref, val, row0=0):
    n, d = val.shape
    nch = d // LANES
    for s in range(nch):
        ref[pl.ds(row0 * nch + s, n, stride=nch), :] = val[:, s * LANES:(s + 1) * LANES]


def _load_rows(ref, n, nch, row0=0, lead=()):
    return jnp.concatenate(
        [ref[lead + (pl.ds(row0 * nch + s, n, stride=nch), slice(None))] for s in range(nch)],
        axis=-1)


def _row(ref, r, nch, lead=()):
    return ref.at[lead + (pl.ds(pl.multiple_of(r * nch, nch), nch), slice(None))]


def _sigmoid(x):
    return 1.0 / (1.0 + jnp.exp(-x))


def _silu(x):
    return x * _sigmoid(x)


def _ada_kernel(c_ref, w_ref, b_ref, o_ref):
    c = c_ref[...]
    o_ref[...] = jnp.dot(_silu(c), w_ref[...], precision=HIGHEST,
                         preferred_element_type=F32) + b_ref[...]


def _ada(c_pad, w, b):
    d, n = w.shape
    tn = 1024
    return pl.pallas_call(
        _ada_kernel,
        grid=(n // tn,),
        in_specs=[pl.BlockSpec((SUBLANES, d), lambda j: (0, 0)),
                  pl.BlockSpec((d, tn), lambda j: (0, j)),
                  pl.BlockSpec((1, tn), lambda j: (0, j))],
        out_specs=pl.BlockSpec((SUBLANES, tn), lambda j: (0, j)),
        out_shape=jax.ShapeDtypeStruct((SUBLANES, n), F32),
        compiler_params=_cparams(("parallel",)),
        name="ada",
    )(c_pad, w, b.reshape(1, n))


def _modulated_norm(x, scale, shift):
    ms = jnp.mean(x * x, axis=-1, keepdims=True)
    return (x * lax.rsqrt(ms + EPS)) * scale + shift


def _inproj_kernel(x_ref, scale_ref, shift_ref, w_ref, o_ref, *, col_chunk):
    u = _modulated_norm(x_ref[...], scale_ref[0], shift_ref[0]).astype(BF16)
    ncols = o_ref.shape[1]
    for c0 in range(0, ncols, col_chunk):
        c1 = min(c0 + col_chunk, ncols)
        o_ref[:, c0:c1] = jnp.dot(u, w_ref[:, c0:c1], preferred_element_type=F32)


def _inproj(x, scale, shift, w_bf, seq, tm, col_splits):
    t, d = x.shape
    ncols = w_bf.shape[1]
    nw = ncols // col_splits
    tiles_per_seq = seq // tm
    return pl.pallas_call(
        functools.partial(_inproj_kernel, col_chunk=1024),
        grid=(col_splits, t // tm),
        in_specs=[pl.BlockSpec((tm, d), lambda j, i: (i, 0)),
                  pl.BlockSpec((1, 1, d), lambda j, i: (i // tiles_per_seq, 0, 0)),
                  pl.BlockSpec((1, 1, d), lambda j, i: (i // tiles_per_seq, 0, 0)),
                  pl.BlockSpec((d, nw), lambda j, i: (0, j), pipeline_mode=pl.Buffered(1))],
        out_specs=pl.BlockSpec((tm, nw), lambda j, i: (i, j)),
        out_shape=jax.ShapeDtypeStruct((t, ncols), F32),
        compiler_params=_cparams(("parallel", "parallel")),
        name="inproj",
    )(x, scale, shift, w_bf)


def _fill_shifted(hbuf_ref, sh_ref, shifts, rows):
    for n, r in enumerate(shifts):
        sh_ref[n] = hbuf_ref[r:r + rows, :]


def _tap_plan(ktaps, halo):
    offs = [k + halo - ktaps // 2 for k in range(ktaps)]
    shifts = sorted({o % SUBLANES for o in offs} - {0})
    plan = [(shifts.index(o % SUBLANES) if o % SUBLANES else -1, (o // SUBLANES) * SUBLANES)
            for o in offs]
    return shifts, plan, max(o // SUBLANES for o in offs) * SUBLANES


def _conv_rows(hbuf_ref, sh_ref, w8_ref, plan, r0, rc):
    groups = rc // SUBLANES
    accs = [None] * groups
    for k, (n, q8) in enumerate(plan):
        w = w8_ref[k * SUBLANES:(k + 1) * SUBLANES, :]
        for j in range(groups):
            start = pl.multiple_of(r0 + q8 + j * SUBLANES, SUBLANES)
            rows = pl.ds(start, SUBLANES)
            src = hbuf_ref[rows, :] if n < 0 else sh_ref[n, rows, :]
            term = src * w
            accs[j] = term if accs[j] is None else accs[j] + term
    return jnp.concatenate(accs, axis=0)


def _cconv_kernel(a_ref, g_ref, ap_ref, gp_ref, an_ref, gn_ref, w_ref, b_ref, lg_ref, lb_ref,
                  o_ref, hbuf_ref, sh_ref, cbuf_ref, *, tiles_per_seq, halo, shifts, plan, rc,
                  ln_rows):
    i = pl.program_id(0)
    tm = a_ref.shape[0]
    first = (i % tiles_per_seq) == 0
    last = (i % tiles_per_seq) == tiles_per_seq - 1
    glu_p = ap_ref[...] * _sigmoid(gp_ref[...])
    glu_n = an_ref[...] * _sigmoid(gn_ref[...])
    hbuf_ref[0:halo, :] = jnp.where(first, 0.0, glu_p)
    hbuf_ref[halo:halo + tm, :] = a_ref[...] * _sigmoid(g_ref[...])
    hbuf_ref[halo + tm:2 * halo + tm, :] = jnp.where(last, 0.0, glu_n)
    _fill_shifted(hbuf_ref, sh_ref, shifts, sh_ref.shape[1])

    def body(j, carry):
        r0 = j * rc
        h = _conv_rows(hbuf_ref, sh_ref, w_ref, plan, r0, rc) + b_ref[...]
        cbuf_ref[pl.ds(pl.multiple_of(r0, rc), rc), :] = h
        return carry

    lax.fori_loop(0, tm // rc, body, 0)

    for c0 in range(0, tm, ln_rows):
        h = cbuf_ref[c0:c0 + ln_rows, :]
        mu = jnp.mean(h, axis=-1, keepdims=True)
        hc = h - mu
        var = jnp.mean(hc * hc, axis=-1, keepdims=True)
        y = hc * lax.rsqrt(var + EPS) * lg_ref[...] + lb_ref[...]
        o_ref[c0:c0 + ln_rows, :] = _silu(y).astype(o_ref.dtype)


def _cconv(proj, w, b, ln_g, ln_b, seq, tm):
    t = proj.shape[0]
    ch = w.shape[1]
    halo = 16
    rc = 16
    shifts, plan, qmax = _tap_plan(CONF_KERNEL, halo)
    tiles_per_seq = seq // tm
    hb = tm // halo
    nhb = t // halo
    cb = 1

    def prev_map(col):
        return lambda i: (jnp.maximum(i * hb - 1, 0), col)

    def next_map(col):
        return lambda i: (jnp.minimum((i + 1) * hb, nhb - 1), col)

    kern = functools.partial(_cconv_kernel, tiles_per_seq=tiles_per_seq, halo=halo,
                             shifts=shifts, plan=plan, rc=rc, ln_rows=64)
    return pl.pallas_call(
        kern,
        grid=(t // tm,),
        in_specs=[pl.BlockSpec((tm, ch), lambda i: (i, 0)),
                  pl.BlockSpec((tm, ch), lambda i: (i, cb)),
                  pl.BlockSpec((halo, ch), prev_map(0)),
                  pl.BlockSpec((halo, ch), prev_map(cb)),
                  pl.BlockSpec((halo, ch), next_map(0)),
                  pl.BlockSpec((halo, ch), next_map(cb)),
                  pl.BlockSpec((CONF_KERNEL * SUBLANES, ch), lambda i: (0, 0)),
                  pl.BlockSpec((1, ch), lambda i: (0, 0)),
                  pl.BlockSpec((1, ch), lambda i: (0, 0)),
                  pl.BlockSpec((1, ch), lambda i: (0, 0))],
        out_specs=pl.BlockSpec((tm, ch), lambda i: (i, 0)),
        out_shape=jax.ShapeDtypeStruct((t, ch), BF16),
        scratch_shapes=[pltpu.VMEM((tm + 2 * halo, ch), F32),
                        pltpu.VMEM((len(shifts), tm + qmax, ch), F32),
                        pltpu.VMEM((tm, ch), F32)],
        compiler_params=_cparams(("parallel",)),
        name="cconv",
    )(proj, proj, proj, proj, proj, proj, jnp.repeat(w, SUBLANES, axis=0), b.reshape(1, ch),
      ln_g.reshape(1, ch),
      ln_b.reshape(1, ch))


def _sconv_kernel(x_ref, xp_ref, xn_ref, w_ref, b_ref, o_ref, hbuf_ref, sh_ref, *,
                  tiles_per_seq, halo, shifts, plan, rc):
    i = pl.program_id(0)
    tm = x_ref.shape[0]
    first = (i % tiles_per_seq) == 0
    last = (i % tiles_per_seq) == tiles_per_seq - 1
    hbuf_ref[0:halo, :] = jnp.where(first, 0.0, xp_ref[...])
    hbuf_ref[halo:halo + tm, :] = x_ref[...]
    hbuf_ref[halo + tm:2 * halo + tm, :] = jnp.where(last, 0.0, xn_ref[...])
    _fill_shifted(hbuf_ref, sh_ref, shifts, sh_ref.shape[1])

    def body(j, carry):
        r0 = j * rc
        h = _conv_rows(hbuf_ref, sh_ref, w_ref, plan, r0, rc) + b_ref[...]
        o_ref[pl.ds(pl.multiple_of(r0, rc), rc), :] = _silu(h)
        return carry

    lax.fori_loop(0, tm // rc, body, 0)


def _sconv(proj, col0, w, b, seq, tm):
    t = proj.shape[0]
    ch = w.shape[1]
    cw = 1024
    halo = SUBLANES
    rc = 16
    shifts, plan, qmax = _tap_plan(SSM_CONV, halo)
    tiles_per_seq = seq // tm
    hb = tm // halo
    nhb = t // halo
    c0 = col0 // cw
    kern = functools.partial(_sconv_kernel, tiles_per_seq=tiles_per_seq, halo=halo,
                             shifts=shifts, plan=plan, rc=rc)
    return pl.pallas_call(
        kern,
        grid=(t // tm, ch // cw),
        in_specs=[pl.BlockSpec((tm, cw), lambda i, j: (i, c0 + j)),
                  pl.BlockSpec((halo, cw), lambda i, j: (jnp.maximum(i * hb - 1, 0), c0 + j)),
                  pl.BlockSpec((halo, cw),
                               lambda i, j: (jnp.minimum((i + 1) * hb, nhb - 1), c0 + j)),
                  pl.BlockSpec((SSM_CONV * SUBLANES, cw), lambda i, j: (0, j)),
                  pl.BlockSpec((1, cw), lambda i, j: (0, j))],
        out_specs=pl.BlockSpec((tm, cw), lambda i, j: (i, j)),
        out_shape=jax.ShapeDtypeStruct((t, ch), F32),
        scratch_shapes=[pltpu.VMEM((tm + 2 * halo, cw), F32),
                        pltpu.VMEM((len(shifts), tm + qmax, cw), F32)],
        compiler_params=_cparams(("parallel", "parallel")),
        name="sconv",
    )(proj, proj, proj, jnp.repeat(w, SUBLANES, axis=0), b.reshape(1, ch))


def _split3(f):
    hi = f.astype(BF16)
    r1 = f - hi.astype(F32)
    mid = r1.astype(BF16)
    lo = (r1 - mid.astype(F32)).astype(BF16)
    return hi, mid, lo


def _expand_heads(f, e):
    hi, mid, lo = _split3(f)
    return (jnp.dot(hi, e, preferred_element_type=F32)
            + jnp.dot(mid, e, preferred_element_type=F32)
            + jnp.dot(lo, e, preferred_element_type=F32))


def _ssd_both_kernel(xf_ref, bf_ref, cf_ref, dtf_ref, xb_ref, bb_ref, cb_ref, dtb_ref, bias_ref,
                     alog_ref, ef_ref, eb_ref, of_ref, ob_ref, *scratch):
    nscr = len(scratch) // 2
    fwd, bwd = scratch[:nscr], scratch[nscr:]
    nheads = xf_ref.shape[1] // SSM_HEAD_DIM

    @pl.when(pl.program_id(1) == 0)
    def _():
        fwd[0][...] = jnp.zeros_like(fwd[0])
        bwd[0][...] = jnp.zeros_like(bwd[0])

    _ssd_direction(xf_ref, bf_ref, cf_ref, dtf_ref, bias_ref, alog_ref, ef_ref, of_ref, *fwd,
                   reverse=False, lane0=0)
    _ssd_direction(xb_ref, bb_ref, cb_ref, dtb_ref, bias_ref, alog_ref, eb_ref, ob_ref, *bwd,
                   reverse=True, lane0=nheads)


def _ssd_direction(x_ref, b_ref, c_ref, dt_ref, dtb_ref, alog_ref, e_ref, o_ref, state_ref,
                   acs_s, acst_s, dtt_s, wstt_s, carry_s, cb_s, bgt_s, *, reverse, lane0):
    q = x_ref.shape[0]
    nheads = x_ref.shape[1] // SSM_HEAD_DIM
    hpg = nheads // SSM_GROUPS

    row = lax.broadcasted_iota(jnp.int32, (q, q), 0)
    col = lax.broadcasted_iota(jnp.int32, (q, q), 1)
    keep = (col >= row) if reverse else (col <= row)

    z = dt_ref[...] + dtb_ref[...]
    dt = jnp.maximum(z, 0.0) + jnp.log(1.0 + jnp.exp(-jnp.abs(z)))
    a = dt * (-jnp.exp(alog_ref[...]))
    acs = jnp.dot(keep.astype(F32), a, precision=HIGHEST, preferred_element_type=F32)
    tot = acs[0:1, :] if reverse else acs[q - 1:q, :]
    acs2 = acs * LOG2E
    acs_s[...] = acs2
    acst_s[...] = acs2.T
    dtt_s[...] = dt.T
    wstt_s[...] = (dt * jnp.exp(tot - acs)).T
    carry_s[...] = _expand_heads(jnp.broadcast_to(jnp.exp(tot), (SUBLANES, LANES)), e_ref[...])

    for g in range(SSM_GROUPS):
        gsl = slice(g * SSM_STATE, (g + 1) * SSM_STATE)
        cb_s[g] = lax.dot_general(c_ref[:, gsl].astype(BF16), b_ref[:, gsl].astype(BF16),
                                  (((1,), (1,)), ((), ())), preferred_element_type=F32)
        bgt_s[g] = b_ref[:, gsl].T
        for pr in range(hpg // 2):
            ps = slice((g * hpg + 2 * pr) * SSM_HEAD_DIM, (g * hpg + 2 * pr + 2) * SSM_HEAD_DIM)
            xp = x_ref[:, ps].astype(BF16)
            rhs = jnp.concatenate([xp, state_ref[:, ps].astype(BF16)], axis=0)
            for sub in range(2):
                hl = lane0 + g * hpg + 2 * pr + sub
                half = slice(sub * SSM_HEAD_DIM, (sub + 1) * SSM_HEAD_DIM)
                hs = slice(ps.start + half.start, ps.start + half.stop)
                colb = jnp.broadcast_to(acs_s[:, hl:hl + 1], (q, q))
                seg = colb - acst_s[hl:hl + 1, :]
                lmat = jnp.exp2(jnp.where(keep, seg, -jnp.inf))
                m = (cb_s[g] * lmat * dtt_s[hl:hl + 1, :]).astype(BF16)
                cexp = (c_ref[:, gsl] * jnp.exp2(colb)).astype(BF16)
                y = jnp.dot(jnp.concatenate([m, cexp], axis=1), rhs,
                            preferred_element_type=F32)
                o_ref[:, hs] = y[:, half].astype(o_ref.dtype)
                bw = (bgt_s[g] * wstt_s[hl:hl + 1, :]).astype(BF16)
                upd = jnp.dot(bw, xp, preferred_element_type=F32)
                state_ref[:, hs] = state_ref[:, hs] * carry_s[0:1, hs] + upd[:, half]


def _ssd(xbc, proj, dt_col_block, dt_bias, a_log, expand_f, expand_b, bsz, seq):
    t = xbc.shape[0]
    q = SSM_CHUNK
    nc = seq // q
    inner = expand_f.shape[1]
    gn = SSM_GROUPS * SSM_STATE
    b_blk = inner // gn
    c_blk = b_blk + 1

    def fwd(b, c):
        return b * nc + c

    def bwd(b, c):
        return b * nc + nc - 1 - c

    def chunk_specs(tok):
        return [pl.BlockSpec((q, inner), lambda b, c: (tok(b, c), 0)),
                pl.BlockSpec((q, gn), lambda b, c: (tok(b, c), b_blk)),
                pl.BlockSpec((q, gn), lambda b, c: (tok(b, c), c_blk)),
                pl.BlockSpec((q, LANES), lambda b, c: (tok(b, c), dt_col_block))]

    def const(shape):
        return pl.BlockSpec(shape, lambda b, c: (0, 0))

    direction_scratch = [pltpu.VMEM((SSM_STATE, inner), F32),
                         pltpu.VMEM((q, LANES), F32), pltpu.VMEM((LANES, q), F32),
                         pltpu.VMEM((LANES, q), F32), pltpu.VMEM((LANES, q), F32),
                         pltpu.VMEM((SUBLANES, inner), F32),
                         pltpu.VMEM((SSM_GROUPS, q, q), F32),
                         pltpu.VMEM((SSM_GROUPS, SSM_STATE, q), F32)]
    return pl.pallas_call(
        _ssd_both_kernel,
        grid=(bsz, nc),
        in_specs=chunk_specs(fwd) + chunk_specs(bwd) + [
            const((1, LANES)), const((1, LANES)), const((LANES, inner)), const((LANES, inner))],
        out_specs=[pl.BlockSpec((q, inner), lambda b, c: (fwd(b, c), 0)),
                   pl.BlockSpec((q, inner), lambda b, c: (bwd(b, c), 0))],
        out_shape=[jax.ShapeDtypeStruct((t, inner), BF16), jax.ShapeDtypeStruct((t, inner), BF16)],
        scratch_shapes=direction_scratch + direction_scratch,
        compiler_params=_cparams(("arbitrary", "arbitrary")),
        name="ssd",
    )(xbc, xbc, xbc, proj, xbc, xbc, xbc, proj, dt_bias, a_log, expand_f, expand_b)


def _merge_kernel(yf_ref, yb_ref, xc_ref, z_ref, hc_ref, gc_ref, gs_ref, x_ref,
                  dexp_ref, ng_ref, wso_ref, wco_ref, bco_ref, wo_ref, g1_ref,
                  scale_ref, shift_ref, wr_ref, br_ref,
                  x1_ref, v_ref, lg_ref, *, ngroups, halves):
    tm, inner = yf_ref.shape
    gw = inner // ngroups
    hm = tm // halves
    for h in range(halves):
        r = slice(h * hm, (h + 1) * hm)
        y = ((yf_ref[r, :].astype(F32) + yb_ref[r, :].astype(F32) + xc_ref[r, :] * dexp_ref[...])
             * _silu(z_ref[r, :]))
        parts = []
        for g in range(ngroups):
            yg = y[:, g * gw:(g + 1) * gw]
            ms = jnp.mean(yg * yg, axis=-1, keepdims=True)
            parts.append((yg * lax.rsqrt(ms + EPS)) * ng_ref[:, g * gw:(g + 1) * gw])
        ysn = jnp.concatenate(parts, axis=-1).astype(BF16)
        y_ssm = jnp.dot(ysn, wso_ref[...], preferred_element_type=F32)
        y_conf = jnp.dot(hc_ref[r, :], wco_ref[...], preferred_element_type=F32) + bco_ref[...]
        merged = _sigmoid(gc_ref[r, :]) * y_conf + _sigmoid(gs_ref[r, :]) * y_ssm
        o = jnp.dot(merged.astype(BF16), wo_ref[...], preferred_element_type=F32)
        x1 = x_ref[r, :] + g1_ref[0] * o
        x1_ref[r, :] = x1
        v = _modulated_norm(x1, scale_ref[0], shift_ref[0])
        v_ref[r, :] = v
        v_hi = v.astype(BF16)
        v_lo = (v - v_hi.astype(F32)).astype(BF16)
        p = (jnp.dot(v_hi, wr_ref[...], preferred_element_type=F32)
             + jnp.dot(v_lo, wr_ref[...], preferred_element_type=F32))
        lg_ref[r, :] = p[:, :LANES] + p[:, LANES:] + br_ref[...]


def _merge(yf, yb, xbc, proj, hc, x, cols, dexp, norm_g, wso, wco, bco, wo, g1, scale, shift,
           wr, br, seq, tm):
    t, d = x.shape
    inner = yf.shape[1]
    tiles_per_seq = seq // tm
    z_blk = cols["z"] // inner
    gc_blk = cols["gate_conf"] // d
    gs_blk = cols["gate_ssm"] // d

    def const(shape):
        return pl.BlockSpec(shape, lambda i: tuple(0 for _ in shape),
                            pipeline_mode=pl.Buffered(1))

    def per_seq():
        return pl.BlockSpec((1, 1, d), lambda i: (i // tiles_per_seq, 0, 0))

    kern = functools.partial(_merge_kernel, ngroups=SSM_GROUPS, halves=2)
    return pl.pallas_call(
        kern,
        grid=(t // tm,),
        in_specs=[pl.BlockSpec((tm, inner), lambda i: (i, 0)),
                  pl.BlockSpec((tm, inner), lambda i: (i, 0)),
                  pl.BlockSpec((tm, inner), lambda i: (i, 0)),
                  pl.BlockSpec((tm, inner), lambda i: (i, z_blk)),
                  pl.BlockSpec((tm, d), lambda i: (i, 0)),
                  pl.BlockSpec((tm, d), lambda i: (i, gc_blk)),
                  pl.BlockSpec((tm, d), lambda i: (i, gs_blk)),
                  pl.BlockSpec((tm, d), lambda i: (i, 0)),
                  const((1, inner)), const((1, inner)),
                  const((inner, d)), const((d, d)), const((1, d)), const((d, d)),
                  per_seq(), per_seq(), per_seq(),
                  const((d, 2 * LANES)), const((1, LANES))],
        out_specs=[pl.BlockSpec((tm, d), lambda i: (i, 0)),
                   pl.BlockSpec((tm, d), lambda i: (i, 0)),
                   pl.BlockSpec((tm, LANES), lambda i: (i, 0))],
        out_shape=[jax.ShapeDtypeStruct((t, d), F32),
                   jax.ShapeDtypeStruct((t, d), F32),
                   jax.ShapeDtypeStruct((t, LANES), F32)],
        compiler_params=_cparams(("parallel",)),
        name="merge",
    )(yf, yb, xbc, proj, hc, proj, proj, x, dexp, norm_g, wso, wco, bco, wo, g1, scale, shift,
      wr, br)


ROUTE_TILE = 256


def _route_kernel(lg_ref, w_ref, lslot_ref, lslot_t_ref, cnt_ref, *, n_experts):
    tm = lg_ref.shape[0]
    lg = lg_ref[...].T[0:n_experts, :]
    erow = lax.broadcasted_iota(jnp.int32, (n_experts, tm), 0)
    sel = jnp.zeros((n_experts, tm), F32)
    vals, idxs = [], []
    for _ in range(TOP_K):
        m = jnp.max(lg, axis=0, keepdims=True)
        ix = jnp.min(jnp.where(lg == m, erow, n_experts), axis=0, keepdims=True)
        hit = erow == ix
        sel = jnp.where(hit, 1.0, sel)
        lg = jnp.where(hit, -jnp.inf, lg)
        vals.append(m)
        idxs.append(ix)
    ex = [jnp.exp(v - vals[0]) for v in vals]
    den = ex[0] + ex[1] + ex[2] + ex[3]
    r = lax.broadcasted_iota(jnp.int32, (tm, tm), 0)
    c = lax.broadcasted_iota(jnp.int32, (tm, tm), 1)
    rank = jnp.dot(sel.astype(BF16), (r < c).astype(BF16), preferred_element_type=F32)
    cnt = jnp.broadcast_to(jnp.sum(sel, axis=1, keepdims=True), (n_experts, LANES))
    er = lax.broadcasted_iota(jnp.int32, (n_experts, n_experts), 0)
    ec = lax.broadcasted_iota(jnp.int32, (n_experts, n_experts), 1)
    off = jnp.dot((ec < er).astype(BF16), cnt.astype(BF16), preferred_element_type=F32)
    slot_of = rank + off[:, 0:1]
    row8 = lax.broadcasted_iota(jnp.int32, (LANES, tm), 0)
    slot_t = jnp.zeros((LANES, tm), F32)
    w_t = jnp.zeros((LANES, tm), F32)
    for k in range(TOP_K):
        sk = jnp.sum(jnp.where(erow == idxs[k], slot_of, 0.0), axis=0, keepdims=True)
        slot_t = jnp.where(row8 == k, sk, slot_t)
        w_t = jnp.where(row8 == k, ex[k] / den, w_t)
    w_ref[...] = w_t.T
    lslot_ref[...] = slot_t.T.astype(jnp.int32)
    lslot_t_ref[...] = slot_t[0:SUBLANES, :].astype(jnp.int32)
    cnt_pad = jnp.concatenate([cnt, jnp.zeros((LANES - n_experts, LANES), F32)], axis=0)
    cnt_ref[0] = cnt_pad.T[0:SUBLANES, :].astype(jnp.int32)


def _route(logits, n_experts):
    t = logits.shape[0]
    tm = ROUTE_TILE
    nt = t // tm
    kern = functools.partial(_route_kernel, n_experts=n_experts)
    return pl.pallas_call(
        kern,
        grid=(nt,),
        in_specs=[pl.BlockSpec((tm, LANES), lambda i: (i, 0))],
        out_specs=[pl.BlockSpec((tm, LANES), lambda i: (i, 0)),
                   pl.BlockSpec((tm, LANES), lambda i: (i, 0)),
                   pl.BlockSpec((SUBLANES, tm), lambda i: (i, 0)),
                   pl.BlockSpec((1, SUBLANES, LANES), lambda i: (i, 0, 0))],
        out_shape=[jax.ShapeDtypeStruct((t, LANES), F32),
                   jax.ShapeDtypeStruct((t, LANES), jnp.int32),
                   jax.ShapeDtypeStruct((nt * SUBLANES, tm), jnp.int32),
                   jax.ShapeDtypeStruct((nt, SUBLANES, LANES), jnp.int32)],
        compiler_params=_cparams(("parallel",)),
        name="route",
    )(logits)


def _row_copy(src, src_row, dst, dst_row, sem, nch):
    return pltpu.make_async_copy(_row(src, src_row, nch), _row(dst, dst_row, nch), sem)


RUN_BITS = ROUTE_TILE.bit_length()


def _excl_cumsum(x, axis):
    n = x.shape[axis]
    earlier = jnp.arange(n)[:, None] > jnp.arange(n)[None, :]
    xm = jnp.moveaxis(x, axis, -1)
    out = jnp.sum(jnp.where(earlier, xm[..., None, :], 0), axis=-1)
    return jnp.moveaxis(out, -1, axis)


def _run_piece_tables(cnt2, loc2, glob2):
    nt, ne = cnt2.shape
    bits = jnp.arange(RUN_BITS, dtype=jnp.int32)[:, None, None]
    n = cnt2.T[None, :, :]
    valid = ((n >> bits) & 1) == 1
    done = (n >> (bits + 1)) << (bits + 1)
    lower = jnp.arange(ne)[:, None] > jnp.arange(ne)[None, :]
    pos = jnp.sum(jnp.where(lower[None, :, :, None], valid[:, None, :, :], False), axis=2,
                  dtype=jnp.int32)
    j = jnp.tile(jnp.arange(ne, dtype=jnp.int32), nt)
    hit = jnp.repeat(valid, ne, axis=-1) & (jnp.repeat(pos, ne, axis=-1) == j)

    def compact(start2):
        val = jnp.repeat(start2.T[None, :, :] + done, ne, axis=-1)
        out = jnp.sum(jnp.where(hit, val, 0), axis=1)
        return out.reshape(RUN_BITS, nt, ne).transpose(1, 0, 2).reshape(-1).astype(jnp.int32)

    npieces = jnp.sum(valid.astype(jnp.int32), axis=1).T.reshape(-1).astype(jnp.int32)
    return npieces, compact(loc2), compact(glob2)


def _segment_copies(np_ref, loc_ref, glob_ref, tile, n_experts, local_buf, hbm, sem, nch, to_hbm):
    for bit in range(RUN_BITS):
        rows = (1 << bit) * nch
        base = tile * RUN_BITS + bit

        def body(j, c, rows=rows, base=base):
            lo = loc_ref[base * n_experts + j]
            go = glob_ref[base * n_experts + j]
            loc = local_buf.at[pl.ds(pl.multiple_of(lo * nch, nch), rows), :]
            glb = hbm.at[pl.ds(pl.multiple_of(go * nch, nch), rows), :]
            if to_hbm:
                pltpu.make_async_copy(loc, glb, sem).start()
            else:
                pltpu.make_async_copy(glb, loc, sem).start()
            return c

        lax.fori_loop(0, np_ref[base], body, 0)


def _wait_segments(local_buf, hbm, sem, to_hbm):
    whole = hbm.at[pl.ds(0, local_buf.shape[0]), :]
    if to_hbm:
        pltpu.make_async_copy(local_buf, whole, sem).wait()
    else:
        pltpu.make_async_copy(whole, local_buf, sem).wait()


def _dispatch_kernel(cnt_ref, loc_ref, glob_ref, fill_ref, end_ref, lslot_t_ref, v_ref, xs_hbm,
                     buf, sems, fsem, *, n_experts, nch):
    i = pl.program_id(0)
    n = pl.num_programs(0)
    tm = v_ref.shape[0]
    nslots = tm * TOP_K
    slot = i % 2
    srow = lax.broadcasted_iota(jnp.int32, (nslots, tm), 0)
    hit = srow == lslot_t_ref[0:1, :]
    for k in range(1, TOP_K):
        hit = jnp.logical_or(hit, srow == lslot_t_ref[k:k + 1, :])
    p = jnp.where(hit, 1.0, 0.0).astype(BF16)
    v = v_ref[...]
    v_hi = v.astype(BF16)
    v_lo = (v - v_hi.astype(F32)).astype(BF16)
    xl = (jnp.dot(p, v_hi, preferred_element_type=F32)
          + jnp.dot(p, v_lo, preferred_element_type=F32))
    for s_ in range(nch):
        buf[slot, pl.ds(s_, nslots, stride=nch), :] = xl[:, s_ * LANES:(s_ + 1) * LANES]

    @pl.when(i == 0)
    def _():
        def fill(s, c):
            _row_copy(buf.at[0], 0, xs_hbm, s, fsem, nch).start()
            return c

        def drain(s, c):
            _row_copy(buf.at[0], 0, xs_hbm, s, fsem, nch).wait()
            return c

        group = 8
        for e0 in range(0, n_experts, group):
            for e in range(e0, min(e0 + group, n_experts)):
                lax.fori_loop(fill_ref[e], end_ref[e], fill, 0)
            for e in range(e0, min(e0 + group, n_experts)):
                lax.fori_loop(fill_ref[e], end_ref[e], drain, 0)

    args = (cnt_ref, loc_ref, glob_ref)
    _segment_copies(*args, i, n_experts, buf.at[slot], xs_hbm, sems.at[slot], nch, True)

    @pl.when(i > 0)
    def _():
        _wait_segments(buf.at[1 - slot], xs_hbm, sems.at[1 - slot], True)

    @pl.when(i == n - 1)
    def _():
        _wait_segments(buf.at[slot], xs_hbm, sems.at[slot], True)


def _dispatch(cnt_flat, loc_flat, glob_flat, fill_start, pad_end, lslot_t, v, n_slots):
    t, d = v.shape
    nch = d // LANES
    tm = ROUTE_TILE
    n_experts = fill_start.shape[0]
    kern = functools.partial(_dispatch_kernel, n_experts=n_experts, nch=nch)
    grid_spec = pltpu.PrefetchScalarGridSpec(
        num_scalar_prefetch=5,
        grid=(t // tm,),
        in_specs=[pl.BlockSpec((SUBLANES, tm), lambda i, *_: (i, 0)),
                  pl.BlockSpec((tm, d), lambda i, *_: (i, 0))],
        out_specs=pl.BlockSpec(memory_space=pl.ANY),
        scratch_shapes=[pltpu.VMEM((2, tm * TOP_K * nch, LANES), F32),
                        pltpu.SemaphoreType.DMA((2,)), pltpu.SemaphoreType.DMA],
    )
    return pl.pallas_call(
        kern,
        grid_spec=grid_spec,
        out_shape=jax.ShapeDtypeStruct((n_slots * nch, LANES), v.dtype),
        compiler_params=_cparams(("arbitrary",)),
        name="dispatch",
    )(cnt_flat, loc_flat, glob_flat, fill_start, pad_end, lslot_t, v)


def _moe_kernel(be_ref, nused_ref, x_ref, wgu_ref, bgu_ref, wd_ref, bd_ref, o_ref,
                wgu_bf, wd_bf):
    b = pl.program_id(0)
    changed = jnp.logical_or(b == 0, be_ref[b] != be_ref[jnp.maximum(b - 1, 0)])
    active = b < nused_ref[0]

    @pl.when(jnp.logical_and(active, changed))
    def _():
        wgu_bf[...] = wgu_ref[0].astype(BF16)
        wd_bf[...] = wd_ref[0].astype(BF16)

    @pl.when(active)
    def _():
        ff = wd_bf.shape[0]
        d = wgu_bf.shape[0]
        nch = d // LANES
        hm = x_ref.shape[0] // nch // MOE_HALVES
        for h in range(MOE_HALVES):
            x = _load_rows(x_ref, hm, nch, h * hm).astype(BF16)
            gu = jnp.dot(x, wgu_bf[...], preferred_element_type=F32) + bgu_ref[0]
            gate = jnp.minimum(gu[:, :ff], SWIGLU_LIMIT)
            up = jnp.clip(gu[:, ff:], -SWIGLU_LIMIT, SWIGLU_LIMIT)
            glu = gate * _sigmoid(SWIGLU_ALPHA * gate)
            hid = ((up + 1.0) * glu).astype(BF16)
            y = jnp.dot(hid, wd_bf[...], preferred_element_type=F32) + bd_ref[0]
            _store_rows(o_ref, y, h * hm)


def _moe(block_expert, nused, xs, w_gu, b_gu, w_down, b_down):
    ne, d, ff2 = w_gu.shape
    nch = d // LANES
    n_slots = xs.shape[0] // nch
    ff = w_down.shape[1]
    bm = MOE_BLOCK

    def blk(b, nu):
        return jnp.minimum(b, nu[0] - 1)

    def spec(shape, index):
        return pl.BlockSpec(shape, index)

    grid_spec = pltpu.PrefetchScalarGridSpec(
        num_scalar_prefetch=2,
        grid=(n_slots // bm,),
        in_specs=[spec((bm * nch, LANES), lambda b, be, nu: (blk(b, nu), 0)),
                  spec((1, d, ff2), lambda b, be, nu: (be[blk(b, nu)], 0, 0)),
                  spec((1, 1, ff2), lambda b, be, nu: (be[blk(b, nu)], 0, 0)),
                  spec((1, ff, d), lambda b, be, nu: (be[blk(b, nu)], 0, 0)),
                  spec((1, 1, d), lambda b, be, nu: (be[blk(b, nu)], 0, 0))],
        out_specs=spec((bm * nch, LANES), lambda b, be, nu: (blk(b, nu), 0)),
        scratch_shapes=[pltpu.VMEM((d, ff2), BF16), pltpu.VMEM((ff, d), BF16)],
    )
    return pl.pallas_call(
        _moe_kernel,
        grid_spec=grid_spec,
        out_shape=jax.ShapeDtypeStruct(xs.shape, F32),
        compiler_params=_cparams(("arbitrary",)),
        name="moe",
    )(block_expert, nused, xs, w_gu, b_gu.reshape(ne, 1, ff2), w_down, b_down.reshape(ne, 1, d))


def _final_kernel(cnt_ref, loc_ref, glob_ref, x1_ref, lslot_ref, w_ref, g2_ref, scale_ref,
                  shift_ref, ys_hbm, o_ref, buf, sems, *, n_experts, nch):
    i = pl.program_id(0)
    n = pl.num_programs(0)
    tm = x1_ref.shape[0]
    nslots = tm * TOP_K
    slot = i % 2
    args = (cnt_ref, loc_ref, glob_ref)

    @pl.when(i == 0)
    def _():
        _segment_copies(*args, 0, n_experts, buf.at[0], ys_hbm, sems.at[0], nch, False)

    @pl.when(i + 1 < n)
    def _():
        _segment_copies(*args, i + 1, n_experts, buf.at[1 - slot], ys_hbm, sems.at[1 - slot], nch,
                        False)

    _wait_segments(buf.at[slot], ys_hbm, sems.at[slot], False)

    lane_slot = lax.broadcasted_iota(jnp.int32, (tm, nslots), 1)
    pw = jnp.zeros((tm, nslots), F32)
    for k in range(TOP_K):
        pw = jnp.where(lane_slot == lslot_ref[:, k:k + 1], w_ref[:, k:k + 1], pw)
    pw_hi = pw.astype(BF16)
    pw_lo = (pw - pw_hi.astype(F32)).astype(BF16)
    y = _load_rows(buf, nslots, nch, 0, (slot,))
    y_hi = y.astype(BF16)
    y_lo = (y - y_hi.astype(F32)).astype(BF16)
    f = (jnp.dot(pw_hi, y_hi, preferred_element_type=F32)
         + jnp.dot(pw_hi, y_lo, preferred_element_type=F32)
         + jnp.dot(pw_lo, y_hi, preferred_element_type=F32))
    x2 = x1_ref[...] + g2_ref[0] * f
    o_ref[...] = _modulated_norm(x2, scale_ref[0], shift_ref[0])


def _final(cnt_flat, loc_flat, glob_flat, n_experts, x1, ys, lslot, top_w, g2, scale, shift, seq):
    t, d = x1.shape
    nch = d // LANES
    tm = ROUTE_TILE
    tiles_per_seq = seq // tm

    def per_seq():
        return pl.BlockSpec((1, 1, d), lambda i, *_: (i // tiles_per_seq, 0, 0))

    grid_spec = pltpu.PrefetchScalarGridSpec(
        num_scalar_prefetch=3,
        grid=(t // tm,),
        in_specs=[pl.BlockSpec((tm, d), lambda i, *_: (i, 0)),
                  pl.BlockSpec((tm, LANES), lambda i, *_: (i, 0)),
                  pl.BlockSpec((tm, LANES), lambda i, *_: (i, 0)),
                  per_seq(), per_seq(), per_seq(),
                  pl.BlockSpec(memory_space=pl.ANY)],
        out_specs=pl.BlockSpec((tm, d), lambda i, *_: (i, 0)),
        scratch_shapes=[pltpu.VMEM((2, TOP_K * tm * nch, LANES), F32),
                        pltpu.SemaphoreType.DMA((2,))],
    )
    return pl.pallas_call(
        functools.partial(_final_kernel, n_experts=n_experts, nch=nch),
        grid_spec=grid_spec,
        out_shape=jax.ShapeDtypeStruct((t, d), F32),
        compiler_params=_cparams(("arbitrary",)),
        name="final",
    )(cnt_flat, loc_flat, glob_flat, x1, lslot, top_w, g2, scale, shift, ys)


def _pad_cols(w, n):
    return jnp.pad(w, ((0, 0), (0, n - w.shape[1])))


def kernel(x, c, ada_w, ada_b, norm_mix_g, w_in, conf_dw_w, conf_dw_b, conf_ln_g, conf_ln_b,
           conf_out_w, conf_out_b, ssm_conv_w, ssm_conv_b, dt_bias_f, dt_bias_b, a_log_f,
           a_log_b, ssm_d, ssm_norm_g, ssm_out_w, w_o, norm_ffn_g, router_w, router_b, w_gu,
           b_gu, w_down, b_down, final_ada_w, final_ada_b, final_norm_g):
    bsz, seq, d = x.shape
    depth = ada_w.shape[0]
    t = bsz * seq
    nheads = a_log_f.shape[1]
    inner = nheads * SSM_HEAD_DIM
    gn = SSM_GROUPS * SSM_STATE
    conf = conf_dw_w.shape[2]
    n_experts = router_w.shape[2]
    assert 2 * nheads <= LANES and n_experts <= LANES

    c_pad = jnp.zeros((SUBLANES, d), F32).at[:bsz].set(c)
    fin = _ada(c_pad, final_ada_w, final_ada_b)[:bsz]
    xf = x.reshape(t, d)

    sizes = [("conf_a", conf), ("conf_g", conf), ("z", inner), ("xs", inner), ("bm", gn),
             ("cm", gn), ("gate_conf", d), ("gate_ssm", d), ("dt", LANES)]
    cols, off = {}, 0
    for name, n in sizes:
        cols[name] = off
        off += n
    src = {}
    o = 0
    for name, n in [("conf_a", conf), ("conf_g", conf), ("z", inner), ("xs", inner), ("bm", gn),
                    ("cm", gn), ("dtf", nheads), ("dtb", nheads), ("gate_conf", d),
                    ("gate_ssm", d)]:
        src[name] = (o, o + n)
        o += n

    head_of_col = jnp.arange(inner, dtype=jnp.int32) // SSM_HEAD_DIM
    lanes = jnp.arange(LANES, dtype=jnp.int32)[:, None]
    expand_f = (lanes == head_of_col[None, :]).astype(BF16)
    expand_b = (lanes == head_of_col[None, :] + nheads).astype(BF16)

    for l in range(depth):
        ada = _ada(c_pad, ada_w[l], ada_b[l])[:bsz]
        sh1, sc1, g1, sh2, sc2, g2 = [a.reshape(bsz, 1, d) for a in jnp.split(ada, 6, axis=-1)]
        scale1 = norm_mix_g[l][None, None, :] * (1.0 + sc1)
        scale2 = norm_ffn_g[l][None, None, :] * (1.0 + sc2)

        wl = w_in[l]
        w_dt = _pad_cols(jnp.concatenate([wl[:, slice(*src["dtf"])], wl[:, slice(*src["dtb"])]],
                                         axis=1), LANES)
        w_perm = jnp.concatenate(
            [wl[:, slice(*src[n])] for n in ("conf_a", "conf_g", "z", "xs", "bm", "cm",
                                             "gate_conf", "gate_ssm")] + [w_dt],
            axis=1).astype(BF16)
        proj = _inproj(xf, scale1, sh1, w_perm, seq, tm=256, col_splits=1)

        hc = _cconv(proj, conf_dw_w[l], conf_dw_b[l], conf_ln_g[l], conf_ln_b[l], seq, tm=512)
        xbc = _sconv(proj, cols["xs"], ssm_conv_w[l], ssm_conv_b[l], seq, tm=512)

        dt_bias = _pad_cols(jnp.concatenate([dt_bias_f[l], dt_bias_b[l]])[None, :], LANES)
        a_log = _pad_cols(jnp.concatenate([a_log_f[l], a_log_b[l]])[None, :], LANES)
        dt_blk = cols["dt"] // LANES
        y_f, y_b = _ssd(xbc, proj, dt_blk, dt_bias, a_log, expand_f, expand_b, bsz, seq)

        dexp = jnp.repeat(ssm_d[l], SSM_HEAD_DIM)[None, :]
        wr = _pad_cols(router_w[l], LANES)
        wr_hi = wr.astype(BF16)
        wr = jnp.concatenate([wr_hi, (wr - wr_hi.astype(F32)).astype(BF16)], axis=1)
        br = _pad_cols(router_b[l][None, :], LANES)
        x1, v, logits = _merge(
            y_f, y_b, xbc, proj, hc, xf, cols, dexp, ssm_norm_g[l][None, :],
            ssm_out_w[l].astype(BF16), conf_out_w[l].astype(BF16), conf_out_b[l][None, :],
            w_o[l].astype(BF16), g1, scale2, sh2, wr, br, seq, tm=256)

        top_w, lslot, lslot_t, tile_cnt = _route(logits, n_experts)
        cnt2 = tile_cnt[:, 0, :n_experts]
        counts = jnp.sum(cnt2, axis=0)
        bm = MOE_BLOCK
        n_blocks = (t * TOP_K) // bm + n_experts
        padded = ((counts + bm - 1) // bm) * bm
        pad_start = _excl_cumsum(padded, axis=0)
        pad_end = pad_start + padded
        loc2 = _excl_cumsum(cnt2, axis=1)
        glob2 = pad_start[None, :] + _excl_cumsum(cnt2, axis=0)
        tables = _run_piece_tables(cnt2, loc2, glob2)
        block_first = jnp.arange(n_blocks, dtype=jnp.int32) * bm
        block_expert = jnp.minimum(
            jnp.sum((pad_end[None, :] <= block_first[:, None]).astype(jnp.int32), axis=1),
            n_experts - 1).astype(jnp.int32)
        nused = (pad_end[-1] // bm).astype(jnp.int32).reshape(1)
        xs = _dispatch(*tables, (pad_start + counts).astype(jnp.int32),
                       pad_end.astype(jnp.int32), lslot_t, v, n_blocks * bm)
        ys = _moe(block_expert, nused, xs, w_gu[l], b_gu[l], w_down[l], b_down[l])

        if l == depth - 1:
            sh_f, sc_f = [a.reshape(bsz, 1, d) for a in jnp.split(fin, 2, axis=-1)]
            scale_f = final_norm_g[None, None, :] * (1.0 + sc_f)
            xf = _final(*tables, n_experts, x1, ys, lslot, top_w, g2, scale_f, sh_f, seq)
        else:
            raise NotImplementedError("depth > 1 is not wired")
    return xf.reshape(bsz, seq, d)
```

```python
import functools

import jax
import jax.numpy as jnp
from jax import lax
from jax.experimental import pallas as pl
from jax.experimental.pallas import tpu as pltpu

F32 = jnp.float32
BF16 = jnp.bfloat16
HIGHEST = lax.Precision.HIGHEST

EPS = 1e-6
LOG2E = 1.4426950408889634
CONF_KERNEL = 31
SSM_CONV = 5
SSM_HEAD_DIM = 64
SSM_GROUPS = 4
SSM_STATE = 128
SSM_CHUNK = 128
TOP_K = 4
SWIGLU_ALPHA = 1.702
SWIGLU_LIMIT = 7.0

LANES = 128
SUBLANES = 8
VMEM_LIMIT = 56 * 1024 * 1024

MOE_BLOCK = 512
MOE_HALVES = 2


def _cparams(sem):
    return pltpu.CompilerParams(dimension_semantics=sem, vmem_limit_bytes=VMEM_LIMIT)


def _store_rows(ref, val, row0=0):
    n, d = val.shape
    nch = d // LANES
    for s in range(nch):
        ref[pl.ds(row0 * nch + s, n, stride=nch), :] = val[:, s * LANES:(s + 1) * LANES]


def _load_rows(ref, n, nch, row0=0, lead=()):
    return jnp.concatenate(
        [ref[lead + (pl.ds(row0 * nch + s, n, stride=nch), slice(None))] for s in range(nch)],
        axis=-1)


def _row(ref, r, nch, lead=()):
    return ref.at[lead + (pl.ds(pl.multiple_of(r * nch, nch), nch), slice(None))]


def _sigmoid(x):
    return 1.0 / (1.0 + jnp.exp(-x))


def _silu(x):
    return x * _sigmoid(x)


def _ada_kernel(c_ref, w_ref, b_ref, o_ref):
    c = c_ref[...]
    o_ref[...] = jnp.dot(_silu(c), w_ref[...], precision=HIGHEST,
                         preferred_element_type=F32) + b_ref[...]


def _ada(c_pad, w, b):
    d, n = w.shape
    tn = 1024
    return pl.pallas_call(
        _ada_kernel,
        grid=(n // tn,),
        in_specs=[pl.BlockSpec((SUBLANES, d), lambda j: (0, 0)),
                  pl.BlockSpec((d, tn), lambda j: (0, j)),
                  pl.BlockSpec((1, tn), lambda j: (0, j))],
        out_specs=pl.BlockSpec((SUBLANES, tn), lambda j: (0, j)),
        out_shape=jax.ShapeDtypeStruct((SUBLANES, n), F32),
        compiler_params=_cparams(("parallel",)),
        name="ada",
    )(c_pad, w, b.reshape(1, n))


def _modulated_norm(x, scale, shift):
    ms = jnp.mean(x * x, axis=-1, keepdims=True)
    return (x * lax.rsqrt(ms + EPS)) * scale + shift


def _inproj_kernel(x_ref, scale_ref, shift_ref, w_ref, o_ref, *, col_chunk):
    u = _modulated_norm(x_ref[...], scale_ref[0], shift_ref[0]).astype(BF16)
    ncols = o_ref.shape[1]
    for c0 in range(0, ncols, col_chunk):
        c1 = min(c0 + col_chunk, ncols)
        o_ref[:, c0:c1] = jnp.dot(u, w_ref[:, c0:c1], preferred_element_type=F32)


def _inproj(x, scale, shift, w_bf, seq, tm, col_splits):
    t, d = x.shape
    ncols = w_bf.shape[1]
    nw = ncols // col_splits
    tiles_per_seq = seq // tm
    return pl.pallas_call(
        functools.partial(_inproj_kernel, col_chunk=1024),
        grid=(col_splits, t // tm),
        in_specs=[pl.BlockSpec((tm, d), lambda j, i: (i, 0)),
                  pl.BlockSpec((1, 1, d), lambda j, i: (i // tiles_per_seq, 0, 0)),
                  pl.BlockSpec((1, 1, d), lambda j, i: (i // tiles_per_seq, 0, 0)),
                  pl.BlockSpec((d, nw), lambda j, i: (0, j), pipeline_mode=pl.Buffered(1))],
        out_specs=pl.BlockSpec((tm, nw), lambda j, i: (i, j)),
        out_shape=jax.ShapeDtypeStruct((t, ncols), F32),
        compiler_params=_cparams(("parallel", "parallel")),
        name="inproj",
    )(x, scale, shift, w_bf)


def _fill_shifted(hbuf_ref, sh_ref, shifts, rows):
    for n, r in enumerate(shifts):
        sh_ref[n] = hbuf_ref[r:r + rows, :]


def _tap_plan(ktaps, halo):
    offs = [k + halo - ktaps // 2 for k in range(ktaps)]
    shifts = sorted({o % SUBLANES for o in offs} - {0})
    plan = [(shifts.index(o % SUBLANES) if o % SUBLANES else -1, (o // SUBLANES) * SUBLANES)
            for o in offs]
    return shifts, plan, max(o // SUBLANES for o in offs) * SUBLANES


def _conv_rows(hbuf_ref, sh_ref, w8_ref, plan, r0, rc):
    groups = rc // SUBLANES
    accs = [None] * groups
    for k, (n, q8) in enumerate(plan):
        w = w8_ref[k * SUBLANES:(k + 1) * SUBLANES, :]
        for j in range(groups):
            start = pl.multiple_of(r0 + q8 + j * SUBLANES, SUBLANES)
            rows = pl.ds(start, SUBLANES)
            src = hbuf_ref[rows, :] if n < 0 else sh_ref[n, rows, :]
            term = src * w
            accs[j] = term if accs[j] is None else accs[j] + term
    return jnp.concatenate(accs, axis=0)


def _cconv_kernel(a_ref, g_ref, ap_ref, gp_ref, an_ref, gn_ref, w_ref, b_ref, lg_ref, lb_ref,
                  o_ref, hbuf_ref, sh_ref, cbuf_ref, *, tiles_per_seq, halo, shifts, plan, rc,
                  ln_rows):
    i = pl.program_id(0)
    tm = a_ref.shape[0]
    first = (i % tiles_per_seq) == 0
    last = (i % tiles_per_seq) == tiles_per_seq - 1
    glu_p = ap_ref[...] * _sigmoid(gp_ref[...])
    glu_n = an_ref[...] * _sigmoid(gn_ref[...])
    hbuf_ref[0:halo, :] = jnp.where(first, 0.0, glu_p)
    hbuf_ref[halo:halo + tm, :] = a_ref[...] * _sigmoid(g_ref[...])
    hbuf_ref[halo + tm:2 * halo + tm, :] = jnp.where(last, 0.0, glu_n)
    _fill_shifted(hbuf_ref, sh_ref, shifts, sh_ref.shape[1])

    def body(j, carry):
        r0 = j * rc
        h = _conv_rows(hbuf_ref, sh_ref, w_ref, plan, r0, rc) + b_ref[...]
        cbuf_ref[pl.ds(pl.multiple_of(r0, rc), rc), :] = h
        return carry

    lax.fori_loop(0, tm // rc, body, 0)

    for c0 in range(0, tm, ln_rows):
        h = cbuf_ref[c0:c0 + ln_rows, :]
        mu = jnp.mean(h, axis=-1, keepdims=True)
        hc = h - mu
        var = jnp.mean(hc * hc, axis=-1, keepdims=True)
        y = hc * lax.rsqrt(var + EPS) * lg_ref[...] + lb_ref[...]
        o_ref[c0:c0 + ln_rows, :] = _silu(y).astype(o_ref.dtype)


def _cconv(proj, w, b, ln_g, ln_b, seq, tm):
    t = proj.shape[0]
    ch = w.shape[1]
    halo = 16
    rc = 32
    shifts, plan, qmax = _tap_plan(CONF_KERNEL, halo)
    tiles_per_seq = seq // tm
    hb = tm // halo
    nhb = t // halo
    cb = 1

    def prev_map(col):
        return lambda i: (jnp.maximum(i * hb - 1, 0), col)

    def next_map(col):
        return lambda i: (jnp.minimum((i + 1) * hb, nhb - 1), col)

    kern = functools.partial(_cconv_kernel, tiles_per_seq=tiles_per_seq, halo=halo,
                             shifts=shifts, plan=plan, rc=rc, ln_rows=64)
    return pl.pallas_call(
        kern,
        grid=(t // tm,),
        in_specs=[pl.BlockSpec((tm, ch), lambda i: (i, 0)),
                  pl.BlockSpec((tm, ch), lambda i: (i, cb)),
                  pl.BlockSpec((halo, ch), prev_map(0)),
                  pl.BlockSpec((halo, ch), prev_map(cb)),
                  pl.BlockSpec((halo, ch), next_map(0)),
                  pl.BlockSpec((halo, ch), next_map(cb)),
                  pl.BlockSpec((CONF_KERNEL * SUBLANES, ch), lambda i: (0, 0)),
                  pl.BlockSpec((1, ch), lambda i: (0, 0)),
                  pl.BlockSpec((1, ch), lambda i: (0, 0)),
                  pl.BlockSpec((1, ch), lambda i: (0, 0))],
        out_specs=pl.BlockSpec((tm, ch), lambda i: (i, 0)),
        out_shape=jax.ShapeDtypeStruct((t, ch), BF16),
        scratch_shapes=[pltpu.VMEM((tm + 2 * halo, ch), F32),
                        pltpu.VMEM((len(shifts), tm + qmax, ch), F32),
                        pltpu.VMEM((tm, ch), F32)],
        compiler_params=_cparams(("parallel",)),
        name="cconv",
    )(proj, proj, proj, proj, proj, proj, jnp.repeat(w, SUBLANES, axis=0), b.reshape(1, ch),
      ln_g.reshape(1, ch),
      ln_b.reshape(1, ch))


def _sconv_kernel(x_ref, xp_ref, xn_ref, w_ref, b_ref, o_ref, hbuf_ref, sh_ref, *,
                  tiles_per_seq, halo, shifts, plan, rc):
    i = pl.program_id(0)
    tm = x_ref.shape[0]
    first = (i % tiles_per_seq) == 0
    last = (i % tiles_per_seq) == tiles_per_seq - 1
    hbuf_ref[0:halo, :] = jnp.where(first, 0.0, xp_ref[...])
    hbuf_ref[halo:halo + tm, :] = x_ref[...]
    hbuf_ref[halo + tm:2 * halo + tm, :] = jnp.where(last, 0.0, xn_ref[...])
    _fill_shifted(hbuf_ref, sh_ref, shifts, sh_ref.shape[1])

    def body(j, carry):
        r0 = j * rc
        h = _conv_rows(hbuf_ref, sh_ref, w_ref, plan, r0, rc) + b_ref[...]
        o_ref[pl.ds(pl.multiple_of(r0, rc), rc), :] = _silu(h)
        return carry

    lax.fori_loop(0, tm // rc, body, 0)


def _sconv(proj, col0, w, b, seq, tm):
    t = proj.shape[0]
    ch = w.shape[1]
    cw = 1024
    halo = SUBLANES
    rc = 16
    shifts, plan, qmax = _tap_plan(SSM_CONV, halo)
    tiles_per_seq = seq // tm
    hb = tm // halo
    nhb = t // halo
    c0 = col0 // cw
    kern = functools.partial(_sconv_kernel, tiles_per_seq=tiles_per_seq, halo=halo,
                             shifts=shifts, plan=plan, rc=rc)
    return pl.pallas_call(
        kern,
        grid=(t // tm, ch // cw),
        in_specs=[pl.BlockSpec((tm, cw), lambda i, j: (i, c0 + j)),
                  pl.BlockSpec((halo, cw), lambda i, j: (jnp.maximum(i * hb - 1, 0), c0 + j)),
                  pl.BlockSpec((halo, cw),
                               lambda i, j: (jnp.minimum((i + 1) * hb, nhb - 1), c0 + j)),
                  pl.BlockSpec((SSM_CONV * SUBLANES, cw), lambda i, j: (0, j)),
                  pl.BlockSpec((1, cw), lambda i, j: (0, j))],
        out_specs=pl.BlockSpec((tm, cw), lambda i, j: (i, j)),
        out_shape=jax.ShapeDtypeStruct((t, ch), F32),
        scratch_shapes=[pltpu.VMEM((tm + 2 * halo, cw), F32),
                        pltpu.VMEM((len(shifts), tm + qmax, cw), F32)],
        compiler_params=_cparams(("parallel", "parallel")),
        name="sconv",
    )(proj, proj, proj, jnp.repeat(w, SUBLANES, axis=0), b.reshape(1, ch))


def _split3(f):
    hi = f.astype(BF16)
    r1 = f - hi.astype(F32)
    mid = r1.astype(BF16)
    lo = (r1 - mid.astype(F32)).astype(BF16)
    return hi, mid, lo


def _expand_heads(f, e):
    hi, mid, lo = _split3(f)
    return (jnp.dot(hi, e, preferred_element_type=F32)
            + jnp.dot(mid, e, preferred_element_type=F32)
            + jnp.dot(lo, e, preferred_element_type=F32))


def _ssd_both_kernel(xf_ref, bf_ref, cf_ref, dtf_ref, xb_ref, bb_ref, cb_ref, dtb_ref, bias_ref,
                     alog_ref, ef_ref, eb_ref, of_ref, ob_ref, *scratch):
    nscr = len(scratch) // 2
    fwd, bwd = scratch[:nscr], scratch[nscr:]
    nheads = xf_ref.shape[1] // SSM_HEAD_DIM

    @pl.when(pl.program_id(1) == 0)
    def _():
        fwd[0][...] = jnp.zeros_like(fwd[0])
        bwd[0][...] = jnp.zeros_like(bwd[0])

    _ssd_direction(xf_ref, bf_ref, cf_ref, dtf_ref, bias_ref, alog_ref, ef_ref, of_ref, *fwd,
                   reverse=False, lane0=0)
    _ssd_direction(xb_ref, bb_ref, cb_ref, dtb_ref, bias_ref, alog_ref, eb_ref, ob_ref, *bwd,
                   reverse=True, lane0=nheads)


def _ssd_direction(x_ref, b_ref, c_ref, dt_ref, dtb_ref, alog_ref, e_ref, o_ref, state_ref,
                   acs_s, acst_s, dtt_s, wstt_s, carry_s, cb_s, bgt_s, *, reverse, lane0):
    q = x_ref.shape[0]
    nheads = x_ref.shape[1] // SSM_HEAD_DIM
    hpg = nheads // SSM_GROUPS

    row = lax.broadcasted_iota(jnp.int32, (q, q), 0)
    col = lax.broadcasted_iota(jnp.int32, (q, q), 1)
    keep = (col >= row) if reverse else (col <= row)

    z = dt_ref[...] + dtb_ref[...]
    dt = jnp.maximum(z, 0.0) + jnp.log(1.0 + jnp.exp(-jnp.abs(z)))
    a = dt * (-jnp.exp(alog_ref[...]))
    acs = jnp.dot(keep.astype(F32), a, precision=HIGHEST, preferred_element_type=F32)
    tot = acs[0:1, :] if reverse else acs[q - 1:q, :]
    acs2 = acs * LOG2E
    acs_s[...] = acs2
    acst_s[...] = acs2.T
    dtt_s[...] = dt.T
    wstt_s[...] = (dt * jnp.exp(tot - acs)).T
    carry_s[...] = _expand_heads(jnp.broadcast_to(jnp.exp(tot), (SUBLANES, LANES)), e_ref[...])

    for g in range(SSM_GROUPS):
        gsl = slice(g * SSM_STATE, (g + 1) * SSM_STATE)
        cb_s[g] = lax.dot_general(c_ref[:, gsl].astype(BF16), b_ref[:, gsl].astype(BF16),
                                  (((1,), (1,)), ((), ())), preferred_element_type=F32)
        bgt_s[g] = b_ref[:, gsl].T
        for pr in range(hpg // 2):
            ps = slice((g * hpg + 2 * pr) * SSM_HEAD_DIM, (g * hpg + 2 * pr + 2) * SSM_HEAD_DIM)
            xp = x_ref[:, ps].astype(BF16)
            rhs = jnp.concatenate([xp, state_ref[:, ps].astype(BF16)], axis=0)
            for sub in range(2):
                hl = lane0 + g * hpg + 2 * pr + sub
                half = slice(sub * SSM_HEAD_DIM, (sub + 1) * SSM_HEAD_DIM)
                hs = slice(ps.start + half.start, ps.start + half.stop)
                colb = jnp.broadcast_to(acs_s[:, hl:hl + 1], (q, q))
                seg = colb - acst_s[hl:hl + 1, :]
                lmat = jnp.exp2(jnp.where(keep, seg, -jnp.inf))
                m = (cb_s[g] * lmat * dtt_s[hl:hl + 1, :]).astype(BF16)
                cexp = (c_ref[:, gsl] * jnp.exp2(colb)).astype(BF16)
                y = jnp.dot(jnp.concatenate([m, cexp], axis=1), rhs,
                            preferred_element_type=F32)
                o_ref[:, hs] = y[:, half].astype(o_ref.dtype)
                bw = (bgt_s[g] * wstt_s[hl:hl + 1, :]).astype(BF16)
                upd = jnp.dot(bw, xp, preferred_element_type=F32)
                state_ref[:, hs] = state_ref[:, hs] * carry_s[0:1, hs] + upd[:, half]


def _ssd(xbc, proj, dt_col_block, dt_bias, a_log, expand_f, expand_b, bsz, seq):
    t = xbc.shape[0]
    q = SSM_CHUNK
    nc = seq // q
    inner = expand_f.shape[1]
    gn = SSM_GROUPS * SSM_STATE
    b_blk = inner // gn
    c_blk = b_blk + 1

    def fwd(b, c):
        return b * nc + c

    def bwd(b, c):
        return b * nc + nc - 1 - c

    def chunk_specs(tok):
        return [pl.BlockSpec((q, inner), lambda b, c: (tok(b, c), 0)),
                pl.BlockSpec((q, gn), lambda b, c: (tok(b, c), b_blk)),
                pl.BlockSpec((q, gn), lambda b, c: (tok(b, c), c_blk)),
                pl.BlockSpec((q, LANES), lambda b, c: (tok(b, c), dt_col_block))]

    def const(shape):
        return pl.BlockSpec(shape, lambda b, c: (0, 0))

    direction_scratch = [pltpu.VMEM((SSM_STATE, inner), F32),
                         pltpu.VMEM((q, LANES), F32), pltpu.VMEM((LANES, q), F32),
                         pltpu.VMEM((LANES, q), F32), pltpu.VMEM((LANES, q), F32),
                         pltpu.VMEM((SUBLANES, inner), F32),
                         pltpu.VMEM((SSM_GROUPS, q, q), F32),
                         pltpu.VMEM((SSM_GROUPS, SSM_STATE, q), F32)]
    return pl.pallas_call(
        _ssd_both_kernel,
        grid=(bsz, nc),
        in_specs=chunk_specs(fwd) + chunk_specs(bwd) + [
            const((1, LANES)), const((1, LANES)), const((LANES, inner)), const((LANES, inner))],
        out_specs=[pl.BlockSpec((q, inner), lambda b, c: (fwd(b, c), 0)),
                   pl.BlockSpec((q, inner), lambda b, c: (bwd(b, c), 0))],
        out_shape=[jax.ShapeDtypeStruct((t, inner), BF16), jax.ShapeDtypeStruct((t, inner), BF16)],
        scratch_shapes=direction_scratch + direction_scratch,
        compiler_params=_cparams(("arbitrary", "arbitrary")),
        name="ssd",
    )(xbc, xbc, xbc, proj, xbc, xbc, xbc, proj, dt_bias, a_log, expand_f, expand_b)


def _merge_kernel(yf_ref, yb_ref, xc_ref, z_ref, hc_ref, gc_ref, gs_ref, x_ref,
                  dexp_ref, ng_ref, wso_ref, wco_ref, bco_ref, wo_ref, g1_ref,
                  scale_ref, shift_ref, wr_ref, br_ref,
                  x1_ref, v_ref, lg_ref, *, ngroups, halves):
    tm, inner = yf_ref.shape
    gw = inner // ngroups
    hm = tm // halves
    for h in range(halves):
        r = slice(h * hm, (h + 1) * hm)
        y = ((yf_ref[r, :].astype(F32) + yb_ref[r, :].astype(F32) + xc_ref[r, :] * dexp_ref[...])
             * _silu(z_ref[r, :]))
        parts = []
        for g in range(ngroups):
            yg = y[:, g * gw:(g + 1) * gw]
            ms = jnp.mean(yg * yg, axis=-1, keepdims=True)
            parts.append((yg * lax.rsqrt(ms + EPS)) * ng_ref[:, g * gw:(g + 1) * gw])
        ysn = jnp.concatenate(parts, axis=-1).astype(BF16)
        y_ssm = jnp.dot(ysn, wso_ref[...], preferred_element_type=F32)
        y_conf = jnp.dot(hc_ref[r, :], wco_ref[...], preferred_element_type=F32) + bco_ref[...]
        merged = _sigmoid(gc_ref[r, :]) * y_conf + _sigmoid(gs_ref[r, :]) * y_ssm
        o = jnp.dot(merged.astype(BF16), wo_ref[...], preferred_element_type=F32)
        x1 = x_ref[r, :] + g1_ref[0] * o
        x1_ref[r, :] = x1
        v = _modulated_norm(x1, scale_ref[0], shift_ref[0])
        v_ref[r, :] = v
        v_hi = v.astype(BF16)
        v_lo = (v - v_hi.astype(F32)).astype(BF16)
        p = (jnp.dot(v_hi, wr_ref[...], preferred_element_type=F32)
             + jnp.dot(v_lo, wr_ref[...], preferred_element_type=F32))
        lg_ref[r, :] = p[:, :LANES] + p[:, LANES:] + br_ref[...]


def _merge(yf, yb, xbc, proj, hc, x, cols, dexp, norm_g, wso, wco, bco, wo, g1, scale, shift,
           wr, br, seq, tm):
    t, d = x.shape
    inner = yf.shape[1]
    tiles_per_seq = seq // tm
    z_blk = cols["z"] // inner
    gc_blk = cols["gate_conf"] // d
    gs_blk = cols["gate_ssm"] // d

    def const(shape):
        return pl.BlockSpec(shape, lambda i: tuple(0 for _ in shape),
                            pipeline_mode=pl.Buffered(1))

    def per_seq():
        return pl.BlockSpec((1, 1, d), lambda i: (i // tiles_per_seq, 0, 0))

    kern = functools.partial(_merge_kernel, ngroups=SSM_GROUPS, halves=2)
    return pl.pallas_call(
        kern,
        grid=(t // tm,),
        in_specs=[pl.BlockSpec((tm, inner), lambda i: (i, 0)),
                  pl.BlockSpec((tm, inner), lambda i: (i, 0)),
                  pl.BlockSpec((tm, inner), lambda i: (i, 0)),
                  pl.BlockSpec((tm, inner), lambda i: (i, z_blk)),
                  pl.BlockSpec((tm, d), lambda i: (i, 0)),
                  pl.BlockSpec((tm, d), lambda i: (i, gc_blk)),
                  pl.BlockSpec((tm, d), lambda i: (i, gs_blk)),
                  pl.BlockSpec((tm, d), lambda i: (i, 0)),
                  const((1, inner)), const((1, inner)),
                  const((inner, d)), const((d, d)), const((1, d)), const((d, d)),
                  per_seq(), per_seq(), per_seq(),
                  const((d, 2 * LANES)), const((1, LANES))],
        out_specs=[pl.BlockSpec((tm, d), lambda i: (i, 0)),
                   pl.BlockSpec((tm, d), lambda i: (i, 0)),
                   pl.BlockSpec((tm, LANES), lambda i: (i, 0))],
        out_shape=[jax.ShapeDtypeStruct((t, d), F32),
                   jax.ShapeDtypeStruct((t, d), F32),
                   jax.ShapeDtypeStruct((t, LANES), F32)],
        compiler_params=_cparams(("parallel",)),
        name="merge",
    )(yf, yb, xbc, proj, hc, proj, proj, x, dexp, norm_g, wso, wco, bco, wo, g1, scale, shift,
      wr, br)


ROUTE_TILE = 256


def _route_kernel(lg_ref, w_ref, lslot_ref, lslot_t_ref, cnt_ref, *, n_experts):
    tm = lg_ref.shape[0]
    lg = lg_ref[...].T[0:n_experts, :]
    erow = lax.broadcasted_iota(jnp.int32, (n_experts, tm), 0)
    sel = jnp.zeros((n_experts, tm), F32)
    vals, idxs = [], []
    for _ in range(TOP_K):
        m = jnp.max(lg, axis=0, keepdims=True)
        ix = jnp.min(jnp.where(lg == m, erow, n_experts), axis=0, keepdims=True)
        hit = erow == ix
        sel = jnp.where(hit, 1.0, sel)
        lg = jnp.where(hit, -jnp.inf, lg)
        vals.append(m)
        idxs.append(ix)
    ex = [jnp.exp(v - vals[0]) for v in vals]
    den = ex[0] + ex[1] + ex[2] + ex[3]
    r = lax.broadcasted_iota(jnp.int32, (tm, tm), 0)
    c = lax.broadcasted_iota(jnp.int32, (tm, tm), 1)
    rank = jnp.dot(sel.astype(BF16), (r < c).astype(BF16), preferred_element_type=F32)
    cnt = jnp.broadcast_to(jnp.sum(sel, axis=1, keepdims=True), (n_experts, LANES))
    er = lax.broadcasted_iota(jnp.int32, (n_experts, n_experts), 0)
    ec = lax.broadcasted_iota(jnp.int32, (n_experts, n_experts), 1)
    off = jnp.dot((ec < er).astype(BF16), cnt.astype(BF16), preferred_element_type=F32)
    slot_of = rank + off[:, 0:1]
    row8 = lax.broadcasted_iota(jnp.int32, (LANES, tm), 0)
    slot_t = jnp.zeros((LANES, tm), F32)
    w_t = jnp.zeros((LANES, tm), F32)
    for k in range(TOP_K):
        sk = jnp.sum(jnp.where(erow == idxs[k], slot_of, 0.0), axis=0, keepdims=True)
        slot_t = jnp.where(row8 == k, sk, slot_t)
        w_t = jnp.where(row8 == k, ex[k] / den, w_t)
    w_ref[...] = w_t.T
    lslot_ref[...] = slot_t.T.astype(jnp.int32)
    lslot_t_ref[...] = slot_t[0:SUBLANES, :].astype(jnp.int32)
    cnt_pad = jnp.concatenate([cnt, jnp.zeros((LANES - n_experts, LANES), F32)], axis=0)
    cnt_ref[0] = cnt_pad.T[0:SUBLANES, :].astype(jnp.int32)


def _route(logits, n_experts):
    t = logits.shape[0]
    tm = ROUTE_TILE
    nt = t // tm
    kern = functools.partial(_route_kernel, n_experts=n_experts)
    return pl.pallas_call(
        kern,
        grid=(nt,),
        in_specs=[pl.BlockSpec((tm, LANES), lambda i: (i, 0))],
        out_specs=[pl.BlockSpec((tm, LANES), lambda i: (i, 0)),
                   pl.BlockSpec((tm, LANES), lambda i: (i, 0)),
                   pl.BlockSpec((SUBLANES, tm), lambda i: (i, 0)),
                   pl.BlockSpec((1, SUBLANES, LANES), lambda i: (i, 0, 0))],
        out_shape=[jax.ShapeDtypeStruct((t, LANES), F32),
                   jax.ShapeDtypeStruct((t, LANES), jnp.int32),
                   jax.ShapeDtypeStruct((nt * SUBLANES, tm), jnp.int32),
                   jax.ShapeDtypeStruct((nt, SUBLANES, LANES), jnp.int32)],
        compiler_params=_cparams(("parallel",)),
        name="route",
    )(logits)


def _row_copy(src, src_row, dst, dst_row, sem, nch):
    return pltpu.make_async_copy(_row(src, src_row, nch), _row(dst, dst_row, nch), sem)


RUN_BITS = ROUTE_TILE.bit_length()


def _excl_cumsum(x, axis):
    n = x.shape[axis]
    earlier = jnp.arange(n)[:, None] > jnp.arange(n)[None, :]
    xm = jnp.moveaxis(x, axis, -1)
    out = jnp.sum(jnp.where(earlier, xm[..., None, :], 0), axis=-1)
    return jnp.moveaxis(out, -1, axis)


def _run_piece_tables(cnt2, loc2, glob2):
    nt, ne = cnt2.shape
    bits = jnp.arange(RUN_BITS, dtype=jnp.int32)[:, None, None]
    n = cnt2.T[None, :, :]
    valid = ((n >> bits) & 1) == 1
    done = (n >> (bits + 1)) << (bits + 1)
    lower = jnp.arange(ne)[:, None] > jnp.arange(ne)[None, :]
    pos = jnp.sum(jnp.where(lower[None, :, :, None], valid[:, None, :, :], False), axis=2,
                  dtype=jnp.int32)
    j = jnp.tile(jnp.arange(ne, dtype=jnp.int32), nt)
    hit = jnp.repeat(valid, ne, axis=-1) & (jnp.repeat(pos, ne, axis=-1) == j)

    def compact(start2):
        val = jnp.repeat(start2.T[None, :, :] + done, ne, axis=-1)
        out = jnp.sum(jnp.where(hit, val, 0), axis=1)
        return out.reshape(RUN_BITS, nt, ne).transpose(1, 0, 2).reshape(-1).astype(jnp.int32)

    npieces = jnp.sum(valid.astype(jnp.int32), axis=1).T.reshape(-1).astype(jnp.int32)
    return npieces, compact(loc2), compact(glob2)


def _segment_copies(np_ref, loc_ref, glob_ref, tile, n_experts, local_buf, hbm, sem, nch, to_hbm):
    for bit in range(RUN_BITS):
        rows = (1 << bit) * nch
        base = tile * RUN_BITS + bit

        def body(j, c, rows=rows, base=base):
            lo = loc_ref[base * n_experts + j]
            go = glob_ref[base * n_experts + j]
            loc = local_buf.at[pl.ds(pl.multiple_of(lo * nch, nch), rows), :]
            glb = hbm.at[pl.ds(pl.multiple_of(go * nch, nch), rows), :]
            if to_hbm:
                pltpu.make_async_copy(loc, glb, sem).start()
            else:
                pltpu.make_async_copy(glb, loc, sem).start()
            return c

        lax.fori_loop(0, np_ref[base], body, 0)


def _wait_segments(local_buf, hbm, sem, to_hbm):
    whole = hbm.at[pl.ds(0, local_buf.shape[0]), :]
    if to_hbm:
        pltpu.make_async_copy(local_buf, whole, sem).wait()
    else:
        pltpu.make_async_copy(whole, local_buf, sem).wait()


def _dispatch_kernel(cnt_ref, loc_ref, glob_ref, fill_ref, end_ref, lslot_t_ref, v_ref, xs_hbm,
                     buf, sems, fsem, *, n_experts, nch):
    i = pl.program_id(0)
    n = pl.num_programs(0)
    tm = v_ref.shape[0]
    nslots = tm * TOP_K
    slot = i % 2
    srow = lax.broadcasted_iota(jnp.int32, (nslots, tm), 0)
    hit = srow == lslot_t_ref[0:1, :]
    for k in range(1, TOP_K):
        hit = jnp.logical_or(hit, srow == lslot_t_ref[k:k + 1, :])
    p = jnp.where(hit, 1.0, 0.0).astype(BF16)
    v = v_ref[...]
    v_hi = v.astype(BF16)
    v_lo = (v - v_hi.astype(F32)).astype(BF16)
    xl = (jnp.dot(p, v_hi, preferred_element_type=F32)
          + jnp.dot(p, v_lo, preferred_element_type=F32))
    for s_ in range(nch):
        buf[slot, pl.ds(s_, nslots, stride=nch), :] = xl[:, s_ * LANES:(s_ + 1) * LANES]

    @pl.when(i == 0)
    def _():
        def fill(s, c):
            _row_copy(buf.at[0], 0, xs_hbm, s, fsem, nch).start()
            return c

        def drain(s, c):
            _row_copy(buf.at[0], 0, xs_hbm, s, fsem, nch).wait()
            return c

        group = 8
        for e0 in range(0, n_experts, group):
            for e in range(e0, min(e0 + group, n_experts)):
                lax.fori_loop(fill_ref[e], end_ref[e], fill, 0)
            for e in range(e0, min(e0 + group, n_experts)):
                lax.fori_loop(fill_ref[e], end_ref[e], drain, 0)

    args = (cnt_ref, loc_ref, glob_ref)
    _segment_copies(*args, i, n_experts, buf.at[slot], xs_hbm, sems.at[slot], nch, True)

    @pl.when(i > 0)
    def _():
        _wait_segments(buf.at[1 - slot], xs_hbm, sems.at[1 - slot], True)

    @pl.when(i == n - 1)
    def _():
        _wait_segments(buf.at[slot], xs_hbm, sems.at[slot], True)


def _dispatch(cnt_flat, loc_flat, glob_flat, fill_start, pad_end, lslot_t, v, n_slots):
    t, d = v.shape
    nch = d // LANES
    tm = ROUTE_TILE
    n_experts = fill_start.shape[0]
    kern = functools.partial(_dispatch_kernel, n_experts=n_experts, nch=nch)
    grid_spec = pltpu.PrefetchScalarGridSpec(
        num_scalar_prefetch=5,
        grid=(t // tm,),
        in_specs=[pl.BlockSpec((SUBLANES, tm), lambda i, *_: (i, 0)),
                  pl.BlockSpec((tm, d), lambda i, *_: (i, 0))],
        out_specs=pl.BlockSpec(memory_space=pl.ANY),
        scratch_shapes=[pltpu.VMEM((2, tm * TOP_K * nch, LANES), F32),
                        pltpu.SemaphoreType.DMA((2,)), pltpu.SemaphoreType.DMA],
    )
    return pl.pallas_call(
        kern,
        grid_spec=grid_spec,
        out_shape=jax.ShapeDtypeStruct((n_slots * nch, LANES), v.dtype),
        compiler_params=_cparams(("arbitrary",)),
        name="dispatch",
    )(cnt_flat, loc_flat, glob_flat, fill_start, pad_end, lslot_t, v)


def _moe_kernel(be_ref, nused_ref, x_ref, wgu_ref, bgu_ref, wd_ref, bd_ref, o_ref,
                wgu_bf, wd_bf):
    b = pl.program_id(0)
    changed = jnp.logical_or(b == 0, be_ref[b] != be_ref[jnp.maximum(b - 1, 0)])
    active = b < nused_ref[0]

    @pl.when(jnp.logical_and(active, changed))
    def _():
        wgu_bf[...] = wgu_ref[0].astype(BF16)
        wd_bf[...] = wd_ref[0].astype(BF16)

    @pl.when(active)
    def _():
        ff = wd_bf.shape[0]
        d = wgu_bf.shape[0]
        nch = d // LANES
        hm = x_ref.shape[0] // nch // MOE_HALVES
        for h in range(MOE_HALVES):
            x = _load_rows(x_ref, hm, nch, h * hm).astype(BF16)
            gu = jnp.dot(x, wgu_bf[...], preferred_element_type=F32) + bgu_ref[0]
            gate = jnp.minimum(gu[:, :ff], SWIGLU_LIMIT)
            up = jnp.clip(gu[:, ff:], -SWIGLU_LIMIT, SWIGLU_LIMIT)
            glu = gate * _sigmoid(SWIGLU_ALPHA * gate)
            hid = ((up + 1.0) * glu).astype(BF16)
            y = jnp.dot(hid, wd_bf[...], preferred_element_type=F32) + bd_ref[0]
            _store_rows(o_ref, y, h * hm)


def _moe(block_expert, nused, xs, w_gu, b_gu, w_down, b_down):
    ne, d, ff2 = w_gu.shape
    nch = d // LANES
    n_slots = xs.shape[0] // nch
    ff = w_down.shape[1]
    bm = MOE_BLOCK

    def blk(b, nu):
        return jnp.minimum(b, nu[0] - 1)

    def spec(shape, index):
        return pl.BlockSpec(shape, index)

    grid_spec = pltpu.PrefetchScalarGridSpec(
        num_scalar_prefetch=2,
        grid=(n_slots // bm,),
        in_specs=[spec((bm * nch, LANES), lambda b, be, nu: (blk(b, nu), 0)),
                  spec((1, d, ff2), lambda b, be, nu: (be[blk(b, nu)], 0, 0)),
                  spec((1, 1, ff2), lambda b, be, nu: (be[blk(b, nu)], 0, 0)),
                  spec((1, ff, d), lambda b, be, nu: (be[blk(b, nu)], 0, 0)),
                  spec((1, 1, d), lambda b, be, nu: (be[blk(b, nu)], 0, 0))],
        out_specs=spec((bm * nch, LANES), lambda b, be, nu: (blk(b, nu), 0)),
        scratch_shapes=[pltpu.VMEM((d, ff2), BF16), pltpu.VMEM((ff, d), BF16)],
    )
    return pl.pallas_call(
        _moe_kernel,
        grid_spec=grid_spec,
        out_shape=jax.ShapeDtypeStruct(xs.shape, F32),
        compiler_params=_cparams(("arbitrary",)),
        name="moe",
    )(block_expert, nused, xs, w_gu, b_gu.reshape(ne, 1, ff2), w_down, b_down.reshape(ne, 1, d))


def _final_kernel(cnt_ref, loc_ref, glob_ref, x1_ref, lslot_ref, w_ref, g2_ref, scale_ref,
                  shift_ref, ys_hbm, o_ref, buf, sems, *, n_experts, nch):
    i = pl.program_id(0)
    n = pl.num_programs(0)
    tm = x1_ref.shape[0]
    nslots = tm * TOP_K
    slot = i % 2
    args = (cnt_ref, loc_ref, glob_ref)

    @pl.when(i == 0)
    def _():
        _segment_copies(*args, 0, n_experts, buf.at[0], ys_hbm, sems.at[0], nch, False)

    @pl.when(i + 1 < n)
    def _():
        _segment_copies(*args, i + 1, n_experts, buf.at[1 - slot], ys_hbm, sems.at[1 - slot], nch,
                        False)

    _wait_segments(buf.at[slot], ys_hbm, sems.at[slot], False)

    lane_slot = lax.broadcasted_iota(jnp.int32, (tm, nslots), 1)
    pw = jnp.zeros((tm, nslots), F32)
    for k in range(TOP_K):
        pw = jnp.where(lane_slot == lslot_ref[:, k:k + 1], w_ref[:, k:k + 1], pw)
    pw_hi = pw.astype(BF16)
    pw_lo = (pw - pw_hi.astype(F32)).astype(BF16)
    y = _load_rows(buf, nslots, nch, 0, (slot,))
    y_hi = y.astype(BF16)
    y_lo = (y - y_hi.astype(F32)).astype(BF16)
    f = (jnp.dot(pw_hi, y_hi, preferred_element_type=F32)
         + jnp.dot(pw_hi, y_lo, preferred_element_type=F32)
         + jnp.dot(pw_lo, y_hi, preferred_element_type=F32))
    x2 = x1_ref[...] + g2_ref[0] * f
    o_ref[...] = _modulated_norm(x2, scale_ref[0], shift_ref[0])


def _final(cnt_flat, loc_flat, glob_flat, n_experts, x1, ys, lslot, top_w, g2, scale, shift, seq):
    t, d = x1.shape
    nch = d // LANES
    tm = ROUTE_TILE
    tiles_per_seq = seq // tm

    def per_seq():
        return pl.BlockSpec((1, 1, d), lambda i, *_: (i // tiles_per_seq, 0, 0))

    grid_spec = pltpu.PrefetchScalarGridSpec(
        num_scalar_prefetch=3,
        grid=(t // tm,),
        in_specs=[pl.BlockSpec((tm, d), lambda i, *_: (i, 0)),
                  pl.BlockSpec((tm, LANES), lambda i, *_: (i, 0)),
                  pl.BlockSpec((tm, LANES), lambda i, *_: (i, 0)),
                  per_seq(), per_seq(), per_seq(),
                  pl.BlockSpec(memory_space=pl.ANY)],
        out_specs=pl.BlockSpec((tm, d), lambda i, *_: (i, 0)),
        scratch_shapes=[pltpu.VMEM((2, TOP_K * tm * nch, LANES), F32),
                        pltpu.SemaphoreType.DMA((2,))],
    )
    return pl.pallas_call(
        functools.partial(_final_kernel, n_experts=n_experts, nch=nch),
        grid_spec=grid_spec,
        out_shape=jax.ShapeDtypeStruct((t, d), F32),
        compiler_params=_cparams(("arbitrary",)),
        name="final",
    )(cnt_flat, loc_flat, glob_flat, x1, lslot, top_w, g2, scale, shift, ys)


def _pad_cols(w, n):
    return jnp.pad(w, ((0, 0), (0, n - w.shape[1])))


def kernel(x, c, ada_w, ada_b, norm_mix_g, w_in, conf_dw_w, conf_dw_b, conf_ln_g, conf_ln_b,
           conf_out_w, conf_out_b, ssm_conv_w, ssm_conv_b, dt_bias_f, dt_bias_b, a_log_f,
           a_log_b, ssm_d, ssm_norm_g, ssm_out_w, w_o, norm_ffn_g, router_w, router_b, w_gu,
           b_gu, w_down, b_down, final_ada_w, final_ada_b, final_norm_g):
    bsz, seq, d = x.shape
    depth = ada_w.shape[0]
    t = bsz * seq
    nheads = a_log_f.shape[1]
    inner = nheads * SSM_HEAD_DIM
    gn = SSM_GROUPS * SSM_STATE
    conf = conf_dw_w.shape[2]
    n_experts = router_w.shape[2]
    assert 2 * nheads <= LANES and n_experts <= LANES

    c_pad = jnp.zeros((SUBLANES, d), F32).at[:bsz].set(c)
    fin = _ada(c_pad, final_ada_w, final_ada_b)[:bsz]
    xf = x.reshape(t, d)

    sizes = [("conf_a", conf), ("conf_g", conf), ("z", inner), ("xs", inner), ("bm", gn),
             ("cm", gn), ("gate_conf", d), ("gate_ssm", d), ("dt", LANES)]
    cols, off = {}, 0
    for name, n in sizes:
        cols[name] = off
        off += n
    src = {}
    o = 0
    for name, n in [("conf_a", conf), ("conf_g", conf), ("z", inner), ("xs", inner), ("bm", gn),
                    ("cm", gn), ("dtf", nheads), ("dtb", nheads), ("gate_conf", d),
                    ("gate_ssm", d)]:
        src[name] = (o, o + n)
        o += n

    head_of_col = jnp.arange(inner, dtype=jnp.int32) // SSM_HEAD_DIM
    lanes = jnp.arange(LANES, dtype=jnp.int32)[:, None]
    expand_f = (lanes == head_of_col[None, :]).astype(BF16)
    expand_b = (lanes == head_of_col[None, :] + nheads).astype(BF16)

    for l in range(depth):
        ada = _ada(c_pad, ada_w[l], ada_b[l])[:bsz]
        sh1, sc1, g1, sh2, sc2, g2 = [a.reshape(bsz, 1, d) for a in jnp.split(ada, 6, axis=-1)]
        scale1 = norm_mix_g[l][None, None, :] * (1.0 + sc1)
        scale2 = norm_ffn_g[l][None, None, :] * (1.0 + sc2)

        wl = w_in[l]
        w_dt = _pad_cols(jnp.concatenate([wl[:, slice(*src["dtf"])], wl[:, slice(*src["dtb"])]],
                                         axis=1), LANES)
        w_perm = jnp.concatenate(
            [wl[:, slice(*src[n])] for n in ("conf_a", "conf_g", "z", "xs", "bm", "cm",
                                             "gate_conf", "gate_ssm")] + [w_dt],
            axis=1).astype(BF16)
        proj = _inproj(xf, scale1, sh1, w_perm, seq, tm=256, col_splits=1)

        hc = _cconv(proj, conf_dw_w[l], conf_dw_b[l], conf_ln_g[l], conf_ln_b[l], seq, tm=512)
        xbc = _sconv(proj, cols["xs"], ssm_conv_w[l], ssm_conv_b[l], seq, tm=1024)

        dt_bias = _pad_cols(jnp.concatenate([dt_bias_f[l], dt_bias_b[l]])[None, :], LANES)
        a_log = _pad_cols(jnp.concatenate([a_log_f[l], a_log_b[l]])[None, :], LANES)
        dt_blk = cols["dt"] // LANES
        y_f, y_b = _ssd(xbc, proj, dt_blk, dt_bias, a_log, expand_f, expand_b, bsz, seq)

        dexp = jnp.repeat(ssm_d[l], SSM_HEAD_DIM)[None, :]
        wr = _pad_cols(router_w[l], LANES)
        wr_hi = wr.astype(BF16)
        wr = jnp.concatenate([wr_hi, (wr - wr_hi.astype(F32)).astype(BF16)], axis=1)
        br = _pad_cols(router_b[l][None, :], LANES)
        x1, v, logits = _merge(
            y_f, y_b, xbc, proj, hc, xf, cols, dexp, ssm_norm_g[l][None, :],
            ssm_out_w[l].astype(BF16), conf_out_w[l].astype(BF16), conf_out_b[l][None, :],
            w_o[l].astype(BF16), g1, scale2, sh2, wr, br, seq, tm=256)

        top_w, lslot, lslot_t, tile_cnt = _route(logits, n_experts)
        cnt2 = tile_cnt[:, 0, :n_experts]
        counts = jnp.sum(cnt2, axis=0)
        bm = MOE_BLOCK
        n_blocks = (t * TOP_K) // bm + n_experts
        padded = ((counts + bm - 1) // bm) * bm
        pad_start = _excl_cumsum(padded, axis=0)
        pad_end = pad_start + padded
        loc2 = _excl_cumsum(cnt2, axis=1)
        glob2 = pad_start[None, :] + _excl_cumsum(cnt2, axis=0)
        tables = _run_piece_tables(cnt2, loc2, glob2)
        block_first = jnp.arange(n_blocks, dtype=jnp.int32) * bm
        block_expert = jnp.minimum(
            jnp.sum((pad_end[None, :] <= block_first[:, None]).astype(jnp.int32), axis=1),
            n_experts - 1).astype(jnp.int32)
        nused = (pad_end[-1] // bm).astype(jnp.int32).reshape(1)
        xs = _dispatch(*tables, (pad_start + counts).astype(jnp.int32),
                       pad_end.astype(jnp.int32), lslot_t, v, n_blocks * bm)
        ys = _moe(block_expert, nused, xs, w_gu[l], b_gu[l], w_down[l], b_down[l])

        if l == depth - 1:
            sh_f, sc_f = [a.reshape(bsz, 1, d) for a in jnp.split(fin, 2, axis=-1)]
            scale_f = final_norm_g[None, None, :] * (1.0 + sc_f)
            xf = _final(*tables, n_experts, x1, ys, lslot, top_w, g2, scale_f, sh_f, seq)
        else:
            raise NotImplementedError("depth > 1 is not wired")
    return xf.reshape(bsz, seq, d)
```

```python
import functools

import jax
import jax.numpy as jnp
from jax import lax
from jax.experimental import pallas as pl
from jax.experimental.pallas import tpu as pltpu

F32 = jnp.float32
BF16 = jnp.bfloat16
HIGHEST = lax.Precision.HIGHEST

EPS = 1e-6
LOG2E = 1.4426950408889634
CONF_KERNEL = 31
SSM_CONV = 5
SSM_HEAD_DIM = 64
SSM_GROUPS = 4
SSM_STATE = 128
SSM_CHUNK = 128
TOP_K = 4
SWIGLU_ALPHA = 1.702
SWIGLU_LIMIT = 7.0

LANES = 128
SUBLANES = 8
VMEM_LIMIT = 56 * 1024 * 1024

INPROJ_TILE = 256
CCONV_TILE = 512
SCONV_TILE = 1024
MERGE_TILE = 256
MOE_BLOCK = 512
MOE_HALVES = 2


def _cparams(sem):
    return pltpu.CompilerParams(dimension_semantics=sem, vmem_limit_bytes=VMEM_LIMIT)


def _store_rows(ref, val, row0=0):
    n, d = val.shape
    nch = d // LANES
    for s in range(nch):
        ref[pl.ds(row0 * nch + s, n, stride=nch), :] = val[:, s * LANES:(s + 1) * LANES]


def _load_rows(ref, n, nch, row0=0, lead=()):
    return jnp.concatenate(
        [ref[lead + (pl.ds(row0 * nch + s, n, stride=nch), slice(None))] for s in range(nch)],
        axis=-1)


def _row(ref, r, nch, lead=()):
    return ref.at[lead + (pl.ds(pl.multiple_of(r * nch, nch), nch), slice(None))]


def _sigmoid(x):
    return 1.0 / (1.0 + jnp.exp(-x))


def _silu(x):
    return x * _sigmoid(x)


def _ada_kernel(c_ref, w_ref, b_ref, o_ref):
    c = c_ref[...]
    o_ref[...] = jnp.dot(_silu(c), w_ref[...], precision=HIGHEST,
                         preferred_element_type=F32) + b_ref[...]


def _ada(c_pad, w, b):
    d, n = w.shape
    tn = 1024
    return pl.pallas_call(
        _ada_kernel,
        grid=(n // tn,),
        in_specs=[pl.BlockSpec((SUBLANES, d), lambda j: (0, 0)),
                  pl.BlockSpec((d, tn), lambda j: (0, j)),
                  pl.BlockSpec((1, tn), lambda j: (0, j))],
        out_specs=pl.BlockSpec((SUBLANES, tn), lambda j: (0, j)),
        out_shape=jax.ShapeDtypeStruct((SUBLANES, n), F32),
        compiler_params=_cparams(("parallel",)),
        name="ada",
    )(c_pad, w, b.reshape(1, n))


def _modulated_norm(x, scale, shift):
    ms = jnp.mean(x * x, axis=-1, keepdims=True)
    return (x * lax.rsqrt(ms + EPS)) * scale + shift


def _inproj_kernel(x_ref, scale_ref, shift_ref, w_ref, o_ref, g_ref, *, col_chunk):
    u = _modulated_norm(x_ref[...], scale_ref[0], shift_ref[0]).astype(BF16)
    n32 = o_ref.shape[1]
    for out_ref, base in ((o_ref, 0), (g_ref, n32)):
        ncols = out_ref.shape[1]
        for c0 in range(0, ncols, col_chunk):
            c1 = min(c0 + col_chunk, ncols)
            out_ref[:, c0:c1] = jnp.dot(u, w_ref[:, base + c0:base + c1],
                                        preferred_element_type=F32).astype(out_ref.dtype)


def _inproj(x, scale, shift, w_bf, n32, seq, tm):
    t, d = x.shape
    ncols = w_bf.shape[1]
    tiles_per_seq = seq // tm
    return pl.pallas_call(
        functools.partial(_inproj_kernel, col_chunk=1024),
        grid=(t // tm,),
        in_specs=[pl.BlockSpec((tm, d), lambda i: (i, 0)),
                  pl.BlockSpec((1, 1, d), lambda i: (i // tiles_per_seq, 0, 0)),
                  pl.BlockSpec((1, 1, d), lambda i: (i // tiles_per_seq, 0, 0)),
                  pl.BlockSpec((d, ncols), lambda i: (0, 0), pipeline_mode=pl.Buffered(1))],
        out_specs=[pl.BlockSpec((tm, n32), lambda i: (i, 0)),
                   pl.BlockSpec((tm, ncols - n32), lambda i: (i, 0))],
        out_shape=[jax.ShapeDtypeStruct((t, n32), F32),
                   jax.ShapeDtypeStruct((t, ncols - n32), BF16)],
        compiler_params=_cparams(("parallel",)),
        name="inproj",
    )(x, scale, shift, w_bf)


def _fill_shifted(hbuf_ref, sh_ref, shifts, rows):
    for n, r in enumerate(shifts):
        sh_ref[n] = hbuf_ref[r:r + rows, :]


def _tap_plan(ktaps, halo):
    offs = [k + halo - ktaps // 2 for k in range(ktaps)]
    shifts = sorted({o % SUBLANES for o in offs} - {0})
    plan = [(shifts.index(o % SUBLANES) if o % SUBLANES else -1, (o // SUBLANES) * SUBLANES)
            for o in offs]
    return shifts, plan, max(o // SUBLANES for o in offs) * SUBLANES


def _conv_rows(hbuf_ref, sh_ref, w8_ref, plan, r0, rc):
    groups = rc // SUBLANES
    accs = [None] * groups
    for k, (n, q8) in enumerate(plan):
        w = w8_ref[k * SUBLANES:(k + 1) * SUBLANES, :]
        for j in range(groups):
            start = pl.multiple_of(r0 + q8 + j * SUBLANES, SUBLANES)
            rows = pl.ds(start, SUBLANES)
            src = hbuf_ref[rows, :] if n < 0 else sh_ref[n, rows, :]
            term = src * w
            accs[j] = term if accs[j] is None else accs[j] + term
    return jnp.concatenate(accs, axis=0)


def _cconv_kernel(a_ref, g_ref, ap_ref, gp_ref, an_ref, gn_ref, w_ref, b_ref, lg_ref, lb_ref,
                  o_ref, hbuf_ref, sh_ref, cbuf_ref, *, tiles_per_seq, halo, shifts, plan, rc,
                  ln_rows):
    i = pl.program_id(0)
    tm = a_ref.shape[0]
    first = (i % tiles_per_seq) == 0
    last = (i % tiles_per_seq) == tiles_per_seq - 1
    glu_p = ap_ref[...] * _sigmoid(gp_ref[...])
    glu_n = an_ref[...] * _sigmoid(gn_ref[...])
    hbuf_ref[0:halo, :] = jnp.where(first, 0.0, glu_p)
    hbuf_ref[halo:halo + tm, :] = a_ref[...] * _sigmoid(g_ref[...])
    hbuf_ref[halo + tm:2 * halo + tm, :] = jnp.where(last, 0.0, glu_n)
    _fill_shifted(hbuf_ref, sh_ref, shifts, sh_ref.shape[1])

    def body(j, carry):
        r0 = j * rc
        h = _conv_rows(hbuf_ref, sh_ref, w_ref, plan, r0, rc) + b_ref[...]
        cbuf_ref[pl.ds(pl.multiple_of(r0, rc), rc), :] = h
        return carry

    lax.fori_loop(0, tm // rc, body, 0)

    for c0 in range(0, tm, ln_rows):
        h = cbuf_ref[c0:c0 + ln_rows, :]
        mu = jnp.mean(h, axis=-1, keepdims=True)
        hc = h - mu
        var = jnp.mean(hc * hc, axis=-1, keepdims=True)
        y = hc * lax.rsqrt(var + EPS) * lg_ref[...] + lb_ref[...]
        o_ref[c0:c0 + ln_rows, :] = _silu(y).astype(o_ref.dtype)


def _cconv(proj, w, b, ln_g, ln_b, seq, tm):
    t = proj.shape[0]
    ch = w.shape[1]
    halo = 16
    rc = 32
    shifts, plan, qmax = _tap_plan(CONF_KERNEL, halo)
    tiles_per_seq = seq // tm
    hb = tm // halo
    nhb = t // halo
    cb = 1

    def prev_map(col):
        return lambda i: (jnp.maximum(i * hb - 1, 0), col)

    def next_map(col):
        return lambda i: (jnp.minimum((i + 1) * hb, nhb - 1), col)

    kern = functools.partial(_cconv_kernel, tiles_per_seq=tiles_per_seq, halo=halo,
                             shifts=shifts, plan=plan, rc=rc, ln_rows=64)
    return pl.pallas_call(
        kern,
        grid=(t // tm,),
        in_specs=[pl.BlockSpec((tm, ch), lambda i: (i, 0)),
                  pl.BlockSpec((tm, ch), lambda i: (i, cb)),
                  pl.BlockSpec((halo, ch), prev_map(0)),
                  pl.BlockSpec((halo, ch), prev_map(cb)),
                  pl.BlockSpec((halo, ch), next_map(0)),
                  pl.BlockSpec((halo, ch), next_map(cb)),
                  pl.BlockSpec((CONF_KERNEL * SUBLANES, ch), lambda i: (0, 0)),
                  pl.BlockSpec((1, ch), lambda i: (0, 0)),
                  pl.BlockSpec((1, ch), lambda i: (0, 0)),
                  pl.BlockSpec((1, ch), lambda i: (0, 0))],
        out_specs=pl.BlockSpec((tm, ch), lambda i: (i, 0)),
        out_shape=jax.ShapeDtypeStruct((t, ch), BF16),
        scratch_shapes=[pltpu.VMEM((tm + 2 * halo, ch), F32),
                        pltpu.VMEM((len(shifts), tm + qmax, ch), F32),
                        pltpu.VMEM((tm, ch), F32)],
        compiler_params=_cparams(("parallel",)),
        name="cconv",
    )(proj, proj, proj, proj, proj, proj, jnp.repeat(w, SUBLANES, axis=0), b.reshape(1, ch),
      ln_g.reshape(1, ch),
      ln_b.reshape(1, ch))


def _sconv_kernel(x_ref, xp_ref, xn_ref, w_ref, b_ref, o_ref, hbuf_ref, sh_ref, *,
                  tiles_per_seq, halo, shifts, plan, rc):
    i = pl.program_id(0)
    tm = x_ref.shape[0]
    first = (i % tiles_per_seq) == 0
    last = (i % tiles_per_seq) == tiles_per_seq - 1
    hbuf_ref[0:halo, :] = jnp.where(first, 0.0, xp_ref[...])
    hbuf_ref[halo:halo + tm, :] = x_ref[...]
    hbuf_ref[halo + tm:2 * halo + tm, :] = jnp.where(last, 0.0, xn_ref[...])
    _fill_shifted(hbuf_ref, sh_ref, shifts, sh_ref.shape[1])

    def body(j, carry):
        r0 = j * rc
        h = _conv_rows(hbuf_ref, sh_ref, w_ref, plan, r0, rc) + b_ref[...]
        o_ref[pl.ds(pl.multiple_of(r0, rc), rc), :] = _silu(h)
        return carry

    lax.fori_loop(0, tm // rc, body, 0)


def _sconv(proj, col0, w, b, seq, tm):
    t = proj.shape[0]
    ch = w.shape[1]
    cw = 1024
    halo = SUBLANES
    rc = 16
    shifts, plan, qmax = _tap_plan(SSM_CONV, halo)
    tiles_per_seq = seq // tm
    hb = tm // halo
    nhb = t // halo
    c0 = col0 // cw
    kern = functools.partial(_sconv_kernel, tiles_per_seq=tiles_per_seq, halo=halo,
                             shifts=shifts, plan=plan, rc=rc)
    return pl.pallas_call(
        kern,
        grid=(t // tm, ch // cw),
        in_specs=[pl.BlockSpec((tm, cw), lambda i, j: (i, c0 + j)),
                  pl.BlockSpec((halo, cw), lambda i, j: (jnp.maximum(i * hb - 1, 0), c0 + j)),
                  pl.BlockSpec((halo, cw),
                               lambda i, j: (jnp.minimum((i + 1) * hb, nhb - 1), c0 + j)),
                  pl.BlockSpec((SSM_CONV * SUBLANES, cw), lambda i, j: (0, j)),
                  pl.BlockSpec((1, cw), lambda i, j: (0, j))],
        out_specs=pl.BlockSpec((tm, cw), lambda i, j: (i, j)),
        out_shape=jax.ShapeDtypeStruct((t, ch), F32),
        scratch_shapes=[pltpu.VMEM((tm + 2 * halo, cw), F32),
                        pltpu.VMEM((len(shifts), tm + qmax, cw), F32)],
        compiler_params=_cparams(("parallel", "parallel")),
        name="sconv",
    )(proj, proj, proj, jnp.repeat(w, SUBLANES, axis=0), b.reshape(1, ch))


def _split3(f):
    hi = f.astype(BF16)
    r1 = f - hi.astype(F32)
    mid = r1.astype(BF16)
    lo = (r1 - mid.astype(F32)).astype(BF16)
    return hi, mid, lo


def _expand_heads(f, e):
    hi, mid, lo = _split3(f)
    return (jnp.dot(hi, e, preferred_element_type=F32)
            + jnp.dot(mid, e, preferred_element_type=F32)
            + jnp.dot(lo, e, preferred_element_type=F32))


def _ssd_both_kernel(xf_ref, bf_ref, cf_ref, dtf_ref, xb_ref, bb_ref, cb_ref, dtb_ref, bias_ref,
                     alog_ref, ef_ref, eb_ref, of_ref, ob_ref, *scratch):
    nscr = len(scratch) // 2
    fwd, bwd = scratch[:nscr], scratch[nscr:]
    nheads = xf_ref.shape[1] // SSM_HEAD_DIM

    @pl.when(pl.program_id(1) == 0)
    def _():
        fwd[0][...] = jnp.zeros_like(fwd[0])
        bwd[0][...] = jnp.zeros_like(bwd[0])

    _ssd_direction(xf_ref, bf_ref, cf_ref, dtf_ref, bias_ref, alog_ref, ef_ref, of_ref, *fwd,
                   reverse=False, lane0=0)
    _ssd_direction(xb_ref, bb_ref, cb_ref, dtb_ref, bias_ref, alog_ref, eb_ref, ob_ref, *bwd,
                   reverse=True, lane0=nheads)


def _ssd_direction(x_ref, b_ref, c_ref, dt_ref, dtb_ref, alog_ref, e_ref, o_ref, state_ref,
                   acs_s, acst_s, dtt_s, wstt_s, carry_s, cb_s, bgt_s, *, reverse, lane0):
    q = x_ref.shape[0]
    nheads = x_ref.shape[1] // SSM_HEAD_DIM
    hpg = nheads // SSM_GROUPS

    row = lax.broadcasted_iota(jnp.int32, (q, q), 0)
    col = lax.broadcasted_iota(jnp.int32, (q, q), 1)
    keep = (col >= row) if reverse else (col <= row)

    z = dt_ref[...] + dtb_ref[...]
    dt = jnp.maximum(z, 0.0) + jnp.log(1.0 + jnp.exp(-jnp.abs(z)))
    a = dt * (-jnp.exp(alog_ref[...]))
    acs = jnp.dot(keep.astype(F32), a, precision=HIGHEST, preferred_element_type=F32)
    tot = acs[0:1, :] if reverse else acs[q - 1:q, :]
    acs2 = acs * LOG2E
    acs_s[...] = acs2
    acst_s[...] = acs2.T
    dtt_s[...] = dt.T
    wstt_s[...] = (dt * jnp.exp(tot - acs)).T
    carry_s[...] = _expand_heads(jnp.broadcast_to(jnp.exp(tot), (SUBLANES, LANES)), e_ref[...])

    for g in range(SSM_GROUPS):
        gsl = slice(g * SSM_STATE, (g + 1) * SSM_STATE)
        cb_s[g] = lax.dot_general(c_ref[:, gsl].astype(BF16), b_ref[:, gsl].astype(BF16),
                                  (((1,), (1,)), ((), ())), preferred_element_type=F32)
        bgt_s[g] = b_ref[:, gsl].T
        for pr in range(hpg // 2):
            ps = slice((g * hpg + 2 * pr) * SSM_HEAD_DIM, (g * hpg + 2 * pr + 2) * SSM_HEAD_DIM)
            xp = x_ref[:, ps].astype(BF16)
            rhs = jnp.concatenate([xp, state_ref[:, ps].astype(BF16)], axis=0)
            for sub in range(2):
                hl = lane0 + g * hpg + 2 * pr + sub
                half = slice(sub * SSM_HEAD_DIM, (sub + 1) * SSM_HEAD_DIM)
                hs = slice(ps.start + half.start, ps.start + half.stop)
                colb = jnp.broadcast_to(acs_s[:, hl:hl + 1], (q, q))
                seg = colb - acst_s[hl:hl + 1, :]
                lmat = jnp.exp2(jnp.where(keep, seg, -jnp.inf))
                m = (cb_s[g] * lmat * dtt_s[hl:hl + 1, :]).astype(BF16)
                cexp = (c_ref[:, gsl] * jnp.exp2(colb)).astype(BF16)
                y = jnp.dot(jnp.concatenate([m, cexp], axis=1), rhs,
                            preferred_element_type=F32)
                o_ref[:, hs] = y[:, half].astype(o_ref.dtype)
                bw = (bgt_s[g] * wstt_s[hl:hl + 1, :]).astype(BF16)
                upd = jnp.dot(bw, xp, preferred_element_type=F32)
                state_ref[:, hs] = state_ref[:, hs] * carry_s[0:1, hs] + upd[:, half]


def _ssd(xbc, proj, dt_col_block, dt_bias, a_log, expand_f, expand_b, bsz, seq):
    t = xbc.shape[0]
    q = SSM_CHUNK
    nc = seq // q
    inner = expand_f.shape[1]
    gn = SSM_GROUPS * SSM_STATE
    b_blk = inner // gn
    c_blk = b_blk + 1

    def fwd(b, c):
        return b * nc + c

    def bwd(b, c):
        return b * nc + nc - 1 - c

    def chunk_specs(tok):
        return [pl.BlockSpec((q, inner), lambda b, c: (tok(b, c), 0)),
                pl.BlockSpec((q, gn), lambda b, c: (tok(b, c), b_blk)),
                pl.BlockSpec((q, gn), lambda b, c: (tok(b, c), c_blk)),
                pl.BlockSpec((q, LANES), lambda b, c: (tok(b, c), dt_col_block))]

    def const(shape):
        return pl.BlockSpec(shape, lambda b, c: (0, 0))

    direction_scratch = [pltpu.VMEM((SSM_STATE, inner), F32),
                         pltpu.VMEM((q, LANES), F32), pltpu.VMEM((LANES, q), F32),
                         pltpu.VMEM((LANES, q), F32), pltpu.VMEM((LANES, q), F32),
                         pltpu.VMEM((SUBLANES, inner), F32),
                         pltpu.VMEM((SSM_GROUPS, q, q), F32),
                         pltpu.VMEM((SSM_GROUPS, SSM_STATE, q), F32)]
    return pl.pallas_call(
        _ssd_both_kernel,
        grid=(bsz, nc),
        in_specs=chunk_specs(fwd) + chunk_specs(bwd) + [
            const((1, LANES)), const((1, LANES)), const((LANES, inner)), const((LANES, inner))],
        out_specs=[pl.BlockSpec((q, inner), lambda b, c: (fwd(b, c), 0)),
                   pl.BlockSpec((q, inner), lambda b, c: (bwd(b, c), 0))],
        out_shape=[jax.ShapeDtypeStruct((t, inner), BF16), jax.ShapeDtypeStruct((t, inner), BF16)],
        scratch_shapes=direction_scratch + direction_scratch,
        compiler_params=_cparams(("arbitrary", "arbitrary")),
        name="ssd",
    )(xbc, xbc, xbc, proj, xbc, xbc, xbc, proj, dt_bias, a_log, expand_f, expand_b)


def _merge_kernel(yf_ref, yb_ref, xc_ref, z_ref, hc_ref, gc_ref, gs_ref, x_ref,
                  dexp_ref, ng_ref, wso_ref, wco_ref, bco_ref, wo_ref, g1_ref,
                  scale_ref, shift_ref, wr_ref, br_ref,
                  x1_ref, v_ref, lg_ref, *, ngroups, halves):
    tm, inner = yf_ref.shape
    gw = inner // ngroups
    hm = tm // halves
    for h in range(halves):
        r = slice(h * hm, (h + 1) * hm)
        y = ((yf_ref[r, :].astype(F32) + yb_ref[r, :].astype(F32) + xc_ref[r, :] * dexp_ref[...])
             * _silu(z_ref[r, :].astype(F32)))
        parts = []
        for g in range(ngroups):
            yg = y[:, g * gw:(g + 1) * gw]
            ms = jnp.mean(yg * yg, axis=-1, keepdims=True)
            parts.append((yg * lax.rsqrt(ms + EPS)) * ng_ref[:, g * gw:(g + 1) * gw])
        ysn = jnp.concatenate(parts, axis=-1).astype(BF16)
        y_ssm = jnp.dot(ysn, wso_ref[...], preferred_element_type=F32)
        y_conf = jnp.dot(hc_ref[r, :], wco_ref[...], preferred_element_type=F32) + bco_ref[...]
        merged = (_sigmoid(gc_ref[r, :].astype(F32)) * y_conf
                  + _sigmoid(gs_ref[r, :].astype(F32)) * y_ssm)
        o = jnp.dot(merged.astype(BF16), wo_ref[...], preferred_element_type=F32)
        x1 = x_ref[r, :] + g1_ref[0] * o
        x1_ref[r, :] = x1
        v = _modulated_norm(x1, scale_ref[0], shift_ref[0])
        v_ref[r, :] = v
        v_hi = v.astype(BF16)
        v_lo = (v - v_hi.astype(F32)).astype(BF16)
        p = (jnp.dot(v_hi, wr_ref[...], preferred_element_type=F32)
             + jnp.dot(v_lo, wr_ref[...], preferred_element_type=F32))
        lg_ref[r, :] = p[:, :LANES] + p[:, LANES:] + br_ref[...]


def _merge(yf, yb, xbc, gates, hc, x, gcols, dexp, norm_g, wso, wco, bco, wo, g1, scale, shift,
           wr, br, seq, tm):
    t, d = x.shape
    inner = yf.shape[1]
    tiles_per_seq = seq // tm
    z_blk = gcols["z"] // inner
    gc_blk = gcols["gate_conf"] // d
    gs_blk = gcols["gate_ssm"] // d

    def const(shape):
        return pl.BlockSpec(shape, lambda i: tuple(0 for _ in shape),
                            pipeline_mode=pl.Buffered(1))

    def per_seq():
        return pl.BlockSpec((1, 1, d), lambda i: (i // tiles_per_seq, 0, 0))

    kern = functools.partial(_merge_kernel, ngroups=SSM_GROUPS, halves=2)
    return pl.pallas_call(
        kern,
        grid=(t // tm,),
        in_specs=[pl.BlockSpec((tm, inner), lambda i: (i, 0)),
                  pl.BlockSpec((tm, inner), lambda i: (i, 0)),
                  pl.BlockSpec((tm, inner), lambda i: (i, 0)),
                  pl.BlockSpec((tm, inner), lambda i: (i, z_blk)),
                  pl.BlockSpec((tm, d), lambda i: (i, 0)),
                  pl.BlockSpec((tm, d), lambda i: (i, gc_blk)),
                  pl.BlockSpec((tm, d), lambda i: (i, gs_blk)),
                  pl.BlockSpec((tm, d), lambda i: (i, 0)),
                  const((1, inner)), const((1, inner)),
                  const((inner, d)), const((d, d)), const((1, d)), const((d, d)),
                  per_seq(), per_seq(), per_seq(),
                  const((d, 2 * LANES)), const((1, LANES))],
        out_specs=[pl.BlockSpec((tm, d), lambda i: (i, 0)),
                   pl.BlockSpec((tm, d), lambda i: (i, 0)),
                   pl.BlockSpec((tm, LANES), lambda i: (i, 0))],
        out_shape=[jax.ShapeDtypeStruct((t, d), F32),
                   jax.ShapeDtypeStruct((t, d), F32),
                   jax.ShapeDtypeStruct((t, LANES), F32)],
        compiler_params=_cparams(("parallel",)),
        name="merge",
    )(yf, yb, xbc, gates, hc, gates, gates, x, dexp, norm_g, wso, wco, bco, wo, g1, scale, shift,
      wr, br)


ROUTE_TILE = 256


def _route_kernel(lg_ref, w_ref, lslot_ref, lslot_t_ref, cnt_ref, *, n_experts):
    tm = lg_ref.shape[0]
    lg = lg_ref[...].T[0:n_experts, :]
    erow = lax.broadcasted_iota(jnp.int32, (n_experts, tm), 0)
    sel = jnp.zeros((n_experts, tm), F32)
    vals, idxs = [], []
    for _ in range(TOP_K):
        m = jnp.max(lg, axis=0, keepdims=True)
        ix = jnp.min(jnp.where(lg == m, erow, n_experts), axis=0, keepdims=True)
        hit = erow == ix
        sel = jnp.where(hit, 1.0, sel)
        lg = jnp.where(hit, -jnp.inf, lg)
        vals.append(m)
        idxs.append(ix)
    ex = [jnp.exp(v - vals[0]) for v in vals]
    den = ex[0] + ex[1] + ex[2] + ex[3]
    r = lax.broadcasted_iota(jnp.int32, (tm, tm), 0)
    c = lax.broadcasted_iota(jnp.int32, (tm, tm), 1)
    rank = jnp.dot(sel.astype(BF16), (r < c).astype(BF16), preferred_element_type=F32)
    cnt = jnp.broadcast_to(jnp.sum(sel, axis=1, keepdims=True), (n_experts, LANES))
    er = lax.broadcasted_iota(jnp.int32, (n_experts, n_experts), 0)
    ec = lax.broadcasted_iota(jnp.int32, (n_experts, n_experts), 1)
    off = jnp.dot((ec < er).astype(BF16), cnt.astype(BF16), preferred_element_type=F32)
    slot_of = rank + off[:, 0:1]
    row8 = lax.broadcasted_iota(jnp.int32, (LANES, tm), 0)
    slot_t = jnp.zeros((LANES, tm), F32)
    w_t = jnp.zeros((LANES, tm), F32)
    for k in range(TOP_K):
        sk = jnp.sum(jnp.where(erow == idxs[k], slot_of, 0.0), axis=0, keepdims=True)
        slot_t = jnp.where(row8 == k, sk, slot_t)
        w_t = jnp.where(row8 == k, ex[k] / den, w_t)
    w_ref[...] = w_t.T
    lslot_ref[...] = slot_t.T.astype(jnp.int32)
    lslot_t_ref[...] = slot_t[0:SUBLANES, :].astype(jnp.int32)
    cnt_pad = jnp.concatenate([cnt, jnp.zeros((LANES - n_experts, LANES), F32)], axis=0)
    cnt_ref[0] = cnt_pad.T[0:SUBLANES, :].astype(jnp.int32)


def _route(logits, n_experts):
    t = logits.shape[0]
    tm = ROUTE_TILE
    nt = t // tm
    kern = functools.partial(_route_kernel, n_experts=n_experts)
    return pl.pallas_call(
        kern,
        grid=(nt,),
        in_specs=[pl.BlockSpec((tm, LANES), lambda i: (i, 0))],
        out_specs=[pl.BlockSpec((tm, LANES), lambda i: (i, 0)),
                   pl.BlockSpec((tm, LANES), lambda i: (i, 0)),
                   pl.BlockSpec((SUBLANES, tm), lambda i: (i, 0)),
                   pl.BlockSpec((1, SUBLANES, LANES), lambda i: (i, 0, 0))],
        out_shape=[jax.ShapeDtypeStruct((t, LANES), F32),
                   jax.ShapeDtypeStruct((t, LANES), jnp.int32),
                   jax.ShapeDtypeStruct((nt * SUBLANES, tm), jnp.int32),
                   jax.ShapeDtypeStruct((nt, SUBLANES, LANES), jnp.int32)],
        compiler_params=_cparams(("parallel",)),
        name="route",
    )(logits)


def _row_copy(src, src_row, dst, dst_row, sem, nch):
    return pltpu.make_async_copy(_row(src, src_row, nch), _row(dst, dst_row, nch), sem)


RUN_BITS = ROUTE_TILE.bit_length()


def _excl_cumsum(x, axis):
    n = x.shape[axis]
    earlier = jnp.arange(n)[:, None] > jnp.arange(n)[None, :]
    xm = jnp.moveaxis(x, axis, -1)
    out = jnp.sum(jnp.where(earlier, xm[..., None, :], 0), axis=-1)
    return jnp.moveaxis(out, -1, axis)


def _run_piece_tables(cnt2, loc2, glob2):
    nt, ne = cnt2.shape
    bits = jnp.arange(RUN_BITS, dtype=jnp.int32)[:, None, None]
    n = cnt2.T[None, :, :]
    valid = ((n >> bits) & 1) == 1
    done = (n >> (bits + 1)) << (bits + 1)
    lower = jnp.arange(ne)[:, None] > jnp.arange(ne)[None, :]
    pos = jnp.sum(jnp.where(lower[None, :, :, None], valid[:, None, :, :], False), axis=2,
                  dtype=jnp.int32)
    j = jnp.tile(jnp.arange(ne, dtype=jnp.int32), nt)
    hit = jnp.repeat(valid, ne, axis=-1) & (jnp.repeat(pos, ne, axis=-1) == j)

    def compact(start2):
        val = jnp.repeat(start2.T[None, :, :] + done, ne, axis=-1)
        out = jnp.sum(jnp.where(hit, val, 0), axis=1)
        return out.reshape(RUN_BITS, nt, ne).transpose(1, 0, 2).reshape(-1).astype(jnp.int32)

    npieces = jnp.sum(valid.astype(jnp.int32), axis=1).T.reshape(-1).astype(jnp.int32)
    return npieces, compact(loc2), compact(glob2)


def _segment_copies(np_ref, loc_ref, glob_ref, tile, n_experts, local_buf, hbm, sem, nch, to_hbm):
    for bit in range(RUN_BITS):
        rows = (1 << bit) * nch
        base = tile * RUN_BITS + bit

        def body(j, c, rows=rows, base=base):
            lo = loc_ref[base * n_experts + j]
            go = glob_ref[base * n_experts + j]
            loc = local_buf.at[pl.ds(pl.multiple_of(lo * nch, nch), rows), :]
            glb = hbm.at[pl.ds(pl.multiple_of(go * nch, nch), rows), :]
            if to_hbm:
                pltpu.make_async_copy(loc, glb, sem).start()
            else:
                pltpu.make_async_copy(glb, loc, sem).start()
            return c

        lax.fori_loop(0, np_ref[base], body, 0)


def _wait_segments(local_buf, hbm, sem, to_hbm):
    whole = hbm.at[pl.ds(0, local_buf.shape[0]), :]
    if to_hbm:
        pltpu.make_async_copy(local_buf, whole, sem).wait()
    else:
        pltpu.make_async_copy(whole, local_buf, sem).wait()


def _dispatch_kernel(cnt_ref, loc_ref, glob_ref, fill_ref, end_ref, lslot_t_ref, v_ref, xs_hbm,
                     buf, sems, fsem, *, n_experts, nch):
    i = pl.program_id(0)
    n = pl.num_programs(0)
    tm = v_ref.shape[0]
    nslots = tm * TOP_K
    slot = i % 2
    srow = lax.broadcasted_iota(jnp.int32, (nslots, tm), 0)
    hit = srow == lslot_t_ref[0:1, :]
    for k in range(1, TOP_K):
        hit = jnp.logical_or(hit, srow == lslot_t_ref[k:k + 1, :])
    p = jnp.where(hit, 1.0, 0.0).astype(BF16)
    v = v_ref[...]
    v_hi = v.astype(BF16)
    v_lo = (v - v_hi.astype(F32)).astype(BF16)
    xl = (jnp.dot(p, v_hi, preferred_element_type=F32)
          + jnp.dot(p, v_lo, preferred_element_type=F32))
    for s_ in range(nch):
        buf[slot, pl.ds(s_, nslots, stride=nch), :] = xl[:, s_ * LANES:(s_ + 1) * LANES]

    @pl.when(i == 0)
    def _():
        def fill(s, c):
            _row_copy(buf.at[0], 0, xs_hbm, s, fsem, nch).start()
            return c

        def drain(s, c):
            _row_copy(buf.at[0], 0, xs_hbm, s, fsem, nch).wait()
            return c

        group = 8
        for e0 in range(0, n_experts, group):
            for e in range(e0, min(e0 + group, n_experts)):
                lax.fori_loop(fill_ref[e], end_ref[e], fill, 0)
            for e in range(e0, min(e0 + group, n_experts)):
                lax.fori_loop(fill_ref[e], end_ref[e], drain, 0)

    args = (cnt_ref, loc_ref, glob_ref)
    _segment_copies(*args, i, n_experts, buf.at[slot], xs_hbm, sems.at[slot], nch, True)

    @pl.when(i > 0)
    def _():
        _wait_segments(buf.at[1 - slot], xs_hbm, sems.at[1 - slot], True)

    @pl.when(i == n - 1)
    def _():
        _wait_segments(buf.at[slot], xs_hbm, sems.at[slot], True)


def _dispatch(cnt_flat, loc_flat, glob_flat, fill_start, pad_end, lslot_t, v, n_slots):
    t, d = v.shape
    nch = d // LANES
    tm = ROUTE_TILE
    n_experts = fill_start.shape[0]
    kern = functools.partial(_dispatch_kernel, n_experts=n_experts, nch=nch)
    grid_spec = pltpu.PrefetchScalarGridSpec(
        num_scalar_prefetch=5,
        grid=(t // tm,),
        in_specs=[pl.BlockSpec((SUBLANES, tm), lambda i, *_: (i, 0)),
                  pl.BlockSpec((tm, d), lambda i, *_: (i, 0))],
        out_specs=pl.BlockSpec(memory_space=pl.ANY),
        scratch_shapes=[pltpu.VMEM((2, tm * TOP_K * nch, LANES), F32),
                        pltpu.SemaphoreType.DMA((2,)), pltpu.SemaphoreType.DMA],
    )
    return pl.pallas_call(
        kern,
        grid_spec=grid_spec,
        out_shape=jax.ShapeDtypeStruct((n_slots * nch, LANES), v.dtype),
        compiler_params=_cparams(("arbitrary",)),
        name="dispatch",
    )(cnt_flat, loc_flat, glob_flat, fill_start, pad_end, lslot_t, v)


def _moe_kernel(be_ref, nused_ref, x_ref, wgu_ref, bgu_ref, wd_ref, bd_ref, o_ref,
                wgu_bf, wd_bf):
    b = pl.program_id(0)
    changed = jnp.logical_or(b == 0, be_ref[b] != be_ref[jnp.maximum(b - 1, 0)])
    active = b < nused_ref[0]

    @pl.when(jnp.logical_and(active, changed))
    def _():
        wgu_bf[...] = wgu_ref[0].astype(BF16)
        wd_bf[...] = wd_ref[0].astype(BF16)

    @pl.when(active)
    def _():
        ff = wd_bf.shape[0]
        d = wgu_bf.shape[0]
        nch = d // LANES
        hm = x_ref.shape[0] // nch // MOE_HALVES
        for h in range(MOE_HALVES):
            x = _load_rows(x_ref, hm, nch, h * hm).astype(BF16)
            gu = jnp.dot(x, wgu_bf[...], preferred_element_type=F32) + bgu_ref[0]
            gate = jnp.minimum(gu[:, :ff], SWIGLU_LIMIT)
            up = jnp.clip(gu[:, ff:], -SWIGLU_LIMIT, SWIGLU_LIMIT)
            glu = gate * _sigmoid(SWIGLU_ALPHA * gate)
            hid = ((up + 1.0) * glu).astype(BF16)
            y = jnp.dot(hid, wd_bf[...], preferred_element_type=F32) + bd_ref[0]
            _store_rows(o_ref, y, h * hm)


def _moe(block_expert, nused, xs, w_gu, b_gu, w_down, b_down):
    ne, d, ff2 = w_gu.shape
    nch = d // LANES
    n_slots = xs.shape[0] // nch
    ff = w_down.shape[1]
    bm = MOE_BLOCK

    def blk(b, nu):
        return jnp.minimum(b, nu[0] - 1)

    def rows(b, be, nu):
        return (blk(b, nu), 0)

    def expert(b, be, nu):
        return (be[blk(b, nu)], 0, 0)

    grid_spec = pltpu.PrefetchScalarGridSpec(
        num_scalar_prefetch=2,
        grid=(n_slots // bm,),
        in_specs=[pl.BlockSpec((bm * nch, LANES), rows),
                  pl.BlockSpec((1, d, ff2), expert),
                  pl.BlockSpec((1, 1, ff2), expert),
                  pl.BlockSpec((1, ff, d), expert),
                  pl.BlockSpec((1, 1, d), expert)],
        out_specs=pl.BlockSpec((bm * nch, LANES), rows),
        scratch_shapes=[pltpu.VMEM((d, ff2), BF16), pltpu.VMEM((ff, d), BF16)],
    )
    return pl.pallas_call(
        _moe_kernel,
        grid_spec=grid_spec,
        out_shape=jax.ShapeDtypeStruct(xs.shape, F32),
        compiler_params=_cparams(("arbitrary",)),
        name="moe",
    )(block_expert, nused, xs, w_gu, b_gu.reshape(ne, 1, ff2), w_down, b_down.reshape(ne, 1, d))


def _final_kernel(cnt_ref, loc_ref, glob_ref, x1_ref, lslot_ref, w_ref, g2_ref, scale_ref,
                  shift_ref, ys_hbm, o_ref, buf, sems, *, n_experts, nch):
    i = pl.program_id(0)
    n = pl.num_programs(0)
    tm = x1_ref.shape[0]
    nslots = tm * TOP_K
    slot = i % 2
    args = (cnt_ref, loc_ref, glob_ref)

    @pl.when(i == 0)
    def _():
        _segment_copies(*args, 0, n_experts, buf.at[0], ys_hbm, sems.at[0], nch, False)

    @pl.when(i + 1 < n)
    def _():
        _segment_copies(*args, i + 1, n_experts, buf.at[1 - slot], ys_hbm, sems.at[1 - slot], nch,
                        False)

    _wait_segments(buf.at[slot], ys_hbm, sems.at[slot], False)

    lane_slot = lax.broadcasted_iota(jnp.int32, (tm, nslots), 1)
    pw = jnp.zeros((tm, nslots), F32)
    for k in range(TOP_K):
        pw = jnp.where(lane_slot == lslot_ref[:, k:k + 1], w_ref[:, k:k + 1], pw)
    pw_hi = pw.astype(BF16)
    pw_lo = (pw - pw_hi.astype(F32)).astype(BF16)
    y = _load_rows(buf, nslots, nch, 0, (slot,))
    y_hi = y.astype(BF16)
    y_lo = (y - y_hi.astype(F32)).astype(BF16)
    f = (jnp.dot(pw_hi, y_hi, preferred_element_type=F32)
         + jnp.dot(pw_hi, y_lo, preferred_element_type=F32)
         + jnp.dot(pw_lo, y_hi, preferred_element_type=F32))
    x2 = x1_ref[...] + g2_ref[0] * f
    o_ref[...] = _modulated_norm(x2, scale_ref[0], shift_ref[0])


def _final(cnt_flat, loc_flat, glob_flat, n_experts, x1, ys, lslot, top_w, g2, scale, shift, seq):
    t, d = x1.shape
    nch = d // LANES
    tm = ROUTE_TILE
    tiles_per_seq = seq // tm

    def per_seq():
        return pl.BlockSpec((1, 1, d), lambda i, *_: (i // tiles_per_seq, 0, 0))

    grid_spec = pltpu.PrefetchScalarGridSpec(
        num_scalar_prefetch=3,
        grid=(t // tm,),
        in_specs=[pl.BlockSpec((tm, d), lambda i, *_: (i, 0)),
                  pl.BlockSpec((tm, LANES), lambda i, *_: (i, 0)),
                  pl.BlockSpec((tm, LANES), lambda i, *_: (i, 0)),
                  per_seq(), per_seq(), per_seq(),
                  pl.BlockSpec(memory_space=pl.ANY)],
        out_specs=pl.BlockSpec((tm, d), lambda i, *_: (i, 0)),
        scratch_shapes=[pltpu.VMEM((2, TOP_K * tm * nch, LANES), F32),
                        pltpu.SemaphoreType.DMA((2,))],
    )
    return pl.pallas_call(
        functools.partial(_final_kernel, n_experts=n_experts, nch=nch),
        grid_spec=grid_spec,
        out_shape=jax.ShapeDtypeStruct((t, d), F32),
        compiler_params=_cparams(("arbitrary",)),
        name="final",
    )(cnt_flat, loc_flat, glob_flat, x1, lslot, top_w, g2, scale, shift, ys)


def _pad_cols(w, n):
    return jnp.pad(w, ((0, 0), (0, n - w.shape[1])))


def kernel(x, c, ada_w, ada_b, norm_mix_g, w_in, conf_dw_w, conf_dw_b, conf_ln_g, conf_ln_b,
           conf_out_w, conf_out_b, ssm_conv_w, ssm_conv_b, dt_bias_f, dt_bias_b, a_log_f,
           a_log_b, ssm_d, ssm_norm_g, ssm_out_w, w_o, norm_ffn_g, router_w, router_b, w_gu,
           b_gu, w_down, b_down, final_ada_w, final_ada_b, final_norm_g):
    bsz, seq, d = x.shape
    depth = ada_w.shape[0]
    t = bsz * seq
    nheads = a_log_f.shape[1]
    inner = nheads * SSM_HEAD_DIM
    gn = SSM_GROUPS * SSM_STATE
    conf = conf_dw_w.shape[2]
    n_experts = router_w.shape[2]
    assert 2 * nheads <= LANES and n_experts <= LANES

    c_pad = jnp.zeros((SUBLANES, d), F32).at[:bsz].set(c)
    fin = _ada(c_pad, final_ada_w, final_ada_b)[:bsz]
    xf = x.reshape(t, d)

    sizes = [("conf_a", conf), ("conf_g", conf), ("xs", inner), ("bm", gn), ("cm", gn),
             ("dt", LANES)]
    gsizes = [("z", inner), ("gate_conf", d), ("gate_ssm", d)]
    cols, off = {}, 0
    for name, n in sizes:
        cols[name] = off
        off += n
    n32 = off
    gcols, off = {}, 0
    for name, n in gsizes:
        gcols[name] = off
        off += n
    src = {}
    o = 0
    for name, n in [("conf_a", conf), ("conf_g", conf), ("z", inner), ("xs", inner), ("bm", gn),
                    ("cm", gn), ("dtf", nheads), ("dtb", nheads), ("gate_conf", d),
                    ("gate_ssm", d)]:
        src[name] = (o, o + n)
        o += n

    head_of_col = jnp.arange(inner, dtype=jnp.int32) // SSM_HEAD_DIM
    lanes = jnp.arange(LANES, dtype=jnp.int32)[:, None]
    expand_f = (lanes == head_of_col[None, :]).astype(BF16)
    expand_b = (lanes == head_of_col[None, :] + nheads).astype(BF16)

    for l in range(depth):
        ada = _ada(c_pad, ada_w[l], ada_b[l])[:bsz]
        sh1, sc1, g1, sh2, sc2, g2 = [a.reshape(bsz, 1, d) for a in jnp.split(ada, 6, axis=-1)]
        scale1 = norm_mix_g[l][None, None, :] * (1.0 + sc1)
        scale2 = norm_ffn_g[l][None, None, :] * (1.0 + sc2)

        wl = w_in[l]
        w_dt = _pad_cols(jnp.concatenate([wl[:, slice(*src["dtf"])], wl[:, slice(*src["dtb"])]],
                                         axis=1), LANES)
        w_perm = jnp.concatenate(
            [wl[:, slice(*src[n])] for n in ("conf_a", "conf_g", "xs", "bm", "cm")] + [w_dt]
            + [wl[:, slice(*src[n])] for n in ("z", "gate_conf", "gate_ssm")],
            axis=1).astype(BF16)
        proj, gates = _inproj(xf, scale1, sh1, w_perm, n32, seq, tm=INPROJ_TILE)

        hc = _cconv(proj, conf_dw_w[l], conf_dw_b[l], conf_ln_g[l], conf_ln_b[l], seq,
                    tm=CCONV_TILE)
        xbc = _sconv(proj, cols["xs"], ssm_conv_w[l], ssm_conv_b[l], seq, tm=SCONV_TILE)

        dt_bias = _pad_cols(jnp.concatenate([dt_bias_f[l], dt_bias_b[l]])[None, :], LANES)
        a_log = _pad_cols(jnp.concatenate([a_log_f[l], a_log_b[l]])[None, :], LANES)
        dt_blk = cols["dt"] // LANES
        y_f, y_b = _ssd(xbc, proj, dt_blk, dt_bias, a_log, expand_f, expand_b, bsz, seq)

        dexp = jnp.repeat(ssm_d[l], SSM_HEAD_DIM)[None, :]
        wr = _pad_cols(router_w[l], LANES)
        wr_hi = wr.astype(BF16)
        wr = jnp.concatenate([wr_hi, (wr - wr_hi.astype(F32)).astype(BF16)], axis=1)
        br = _pad_cols(router_b[l][None, :], LANES)
        x1, v, logits = _merge(
            y_f, y_b, xbc, gates, hc, xf, gcols, dexp, ssm_norm_g[l][None, :],
            ssm_out_w[l].astype(BF16), conf_out_w[l].astype(BF16), conf_out_b[l][None, :],
            w_o[l].astype(BF16), g1, scale2, sh2, wr, br, seq, tm=MERGE_TILE)

        top_w, lslot, lslot_t, tile_cnt = _route(logits, n_experts)
        cnt2 = tile_cnt[:, 0, :n_experts]
        counts = jnp.sum(cnt2, axis=0)
        bm = MOE_BLOCK
        n_blocks = (t * TOP_K) // bm + n_experts
        padded = ((counts + bm - 1) // bm) * bm
        pad_start = _excl_cumsum(padded, axis=0)
        pad_end = pad_start + padded
        loc2 = _excl_cumsum(cnt2, axis=1)
        glob2 = pad_start[None, :] + _excl_cumsum(cnt2, axis=0)
        tables = _run_piece_tables(cnt2, loc2, glob2)
        block_first = jnp.arange(n_blocks, dtype=jnp.int32) * bm
        block_expert = jnp.minimum(
            jnp.sum((pad_end[None, :] <= block_first[:, None]).astype(jnp.int32), axis=1),
            n_experts - 1).astype(jnp.int32)
        nused = (pad_end[-1] // bm).astype(jnp.int32).reshape(1)
        xs = _dispatch(*tables, (pad_start + counts).astype(jnp.int32),
                       pad_end.astype(jnp.int32), lslot_t, v, n_blocks * bm)
        ys = _moe(block_expert, nused, xs, w_gu[l], b_gu[l], w_down[l], b_down[l])

        if l == depth - 1:
            sh_f, sc_f = [a.reshape(bsz, 1, d) for a in jnp.split(fin, 2, axis=-1)]
            scale_f = final_norm_g[None, None, :] * (1.0 + sc_f)
            xf = _final(*tables, n_experts, x1, ys, lslot, top_w, g2, scale_f, sh_f, seq)
        else:
            raise NotImplementedError("depth > 1 is not wired")
    return xf.reshape(bsz, seq, d)
```

```python
import functools

import jax
import jax.numpy as jnp
from jax import lax
from jax.experimental import pallas as pl
from jax.experimental.pallas import tpu as pltpu

F32 = jnp.float32
BF16 = jnp.bfloat16
HIGHEST = lax.Precision.HIGHEST

EPS = 1e-6
LOG2E = 1.4426950408889634
CONF_KERNEL = 31
SSM_CONV = 5
SSM_HEAD_DIM = 64
SSM_GROUPS = 4
SSM_STATE = 128
SSM_CHUNK = 128
TOP_K = 4
SWIGLU_ALPHA = 1.702
SWIGLU_LIMIT = 7.0

LANES = 128
SUBLANES = 8
VMEM_LIMIT = 56 * 1024 * 1024

INPROJ_TILE = 256
CCONV_TILE = 512
SCONV_TILE = 1024
MERGE_TILE = 256
MOE_BLOCK = 512
MOE_HALVES = 2


def _cparams(sem):
    return pltpu.CompilerParams(dimension_semantics=sem, vmem_limit_bytes=VMEM_LIMIT)


def _store_rows(ref, val, row0=0):
    n, d = val.shape
    nch = d // LANES
    for s in range(nch):
        ref[pl.ds(row0 * nch + s, n, stride=nch), :] = val[:, s * LANES:(s + 1) * LANES]


def _load_rows(ref, n, nch, row0=0, lead=()):
    return jnp.concatenate(
        [ref[lead + (pl.ds(row0 * nch + s, n, stride=nch), slice(None))] for s in range(nch)],
        axis=-1)


def _row(ref, r, nch, lead=()):
    return ref.at[lead + (pl.ds(pl.multiple_of(r * nch, nch), nch), slice(None))]


def _sigmoid(x):
    return 1.0 / (1.0 + jnp.exp(-x))


def _silu(x):
    return x * _sigmoid(x)


def _ada_kernel(c_ref, w_ref, b_ref, o_ref):
    c = c_ref[...]
    o_ref[...] = jnp.dot(_silu(c), w_ref[...], precision=HIGHEST,
                         preferred_element_type=F32) + b_ref[...]


def _ada(c_pad, w, b):
    d, n = w.shape
    tn = 1024
    return pl.pallas_call(
        _ada_kernel,
        grid=(n // tn,),
        in_specs=[pl.BlockSpec((SUBLANES, d), lambda j: (0, 0)),
                  pl.BlockSpec((d, tn), lambda j: (0, j)),
                  pl.BlockSpec((1, tn), lambda j: (0, j))],
        out_specs=pl.BlockSpec((SUBLANES, tn), lambda j: (0, j)),
        out_shape=jax.ShapeDtypeStruct((SUBLANES, n), F32),
        compiler_params=_cparams(("parallel",)),
        name="ada",
    )(c_pad, w, b.reshape(1, n))


def _modulated_norm(x, scale, shift):
    ms = jnp.mean(x * x, axis=-1, keepdims=True)
    return (x * lax.rsqrt(ms + EPS)) * scale + shift


def _inproj_kernel(x_ref, scale_ref, shift_ref, w_ref, o_ref, *, col_chunk):
    u = _modulated_norm(x_ref[...], scale_ref[0], shift_ref[0]).astype(BF16)
    ncols = o_ref.shape[1]
    for c0 in range(0, ncols, col_chunk):
        c1 = min(c0 + col_chunk, ncols)
        o_ref[:, c0:c1] = jnp.dot(u, w_ref[:, c0:c1], preferred_element_type=F32)


def _inproj(x, scale, shift, w_bf, seq, tm):
    t, d = x.shape
    ncols = w_bf.shape[1]
    tiles_per_seq = seq // tm
    return pl.pallas_call(
        functools.partial(_inproj_kernel, col_chunk=1024),
        grid=(t // tm,),
        in_specs=[pl.BlockSpec((tm, d), lambda i: (i, 0)),
                  pl.BlockSpec((1, 1, d), lambda i: (i // tiles_per_seq, 0, 0)),
                  pl.BlockSpec((1, 1, d), lambda i: (i // tiles_per_seq, 0, 0)),
                  pl.BlockSpec((d, ncols), lambda i: (0, 0), pipeline_mode=pl.Buffered(1))],
        out_specs=pl.BlockSpec((tm, ncols), lambda i: (i, 0)),
        out_shape=jax.ShapeDtypeStruct((t, ncols), F32),
        compiler_params=_cparams(("parallel",)),
        name="inproj",
    )(x, scale, shift, w_bf)


def _fill_shifted(hbuf_ref, sh_ref, shifts, rows):
    for n, r in enumerate(shifts):
        sh_ref[n] = hbuf_ref[r:r + rows, :]


def _tap_plan(ktaps, halo):
    offs = [k + halo - ktaps // 2 for k in range(ktaps)]
    shifts = sorted({o % SUBLANES for o in offs} - {0})
    plan = [(shifts.index(o % SUBLANES) if o % SUBLANES else -1, (o // SUBLANES) * SUBLANES)
            for o in offs]
    return shifts, plan, max(o // SUBLANES for o in offs) * SUBLANES


def _conv_rows(hbuf_ref, sh_ref, w8_ref, plan, r0, rc):
    groups = rc // SUBLANES
    accs = [None] * groups
    for k, (n, q8) in enumerate(plan):
        w = w8_ref[k * SUBLANES:(k + 1) * SUBLANES, :]
        for j in range(groups):
            start = pl.multiple_of(r0 + q8 + j * SUBLANES, SUBLANES)
            rows = pl.ds(start, SUBLANES)
            src = hbuf_ref[rows, :] if n < 0 else sh_ref[n, rows, :]
            term = src * w
            accs[j] = term if accs[j] is None else accs[j] + term
    return jnp.concatenate(accs, axis=0)


def _cconv_kernel(a_ref, g_ref, ap_ref, gp_ref, an_ref, gn_ref, w_ref, b_ref, lg_ref, lb_ref,
                  o_ref, hbuf_ref, sh_ref, cbuf_ref, *, tiles_per_seq, halo, shifts, plan, rc,
                  ln_rows):
    i = pl.program_id(0)
    tm = a_ref.shape[0]
    first = (i % tiles_per_seq) == 0
    last = (i % tiles_per_seq) == tiles_per_seq - 1
    glu_p = ap_ref[...] * _sigmoid(gp_ref[...])
    glu_n = an_ref[...] * _sigmoid(gn_ref[...])
    hbuf_ref[0:halo, :] = jnp.where(first, 0.0, glu_p)
    hbuf_ref[halo:halo + tm, :] = a_ref[...] * _sigmoid(g_ref[...])
    hbuf_ref[halo + tm:2 * halo + tm, :] = jnp.where(last, 0.0, glu_n)
    _fill_shifted(hbuf_ref, sh_ref, shifts, sh_ref.shape[1])

    def body(j, carry):
        r0 = j * rc
        h = _conv_rows(hbuf_ref, sh_ref, w_ref, plan, r0, rc) + b_ref[...]
        cbuf_ref[pl.ds(pl.multiple_of(r0, rc), rc), :] = h
        return carry

    lax.fori_loop(0, tm // rc, body, 0)

    for c0 in range(0, tm, ln_rows):
        h = cbuf_ref[c0:c0 + ln_rows, :]
        mu = jnp.mean(h, axis=-1, keepdims=True)
        hc = h - mu
        var = jnp.mean(hc * hc, axis=-1, keepdims=True)
        y = hc * lax.rsqrt(var + EPS) * lg_ref[...] + lb_ref[...]
        o_ref[c0:c0 + ln_rows, :] = _silu(y).astype(o_ref.dtype)


def _cconv(proj, w, b, ln_g, ln_b, seq, tm):
    t = proj.shape[0]
    ch = w.shape[1]
    halo = 16
    rc = 32
    shifts, plan, qmax = _tap_plan(CONF_KERNEL, halo)
    tiles_per_seq = seq // tm
    hb = tm // halo
    nhb = t // halo
    cb = 1

    def prev_map(col):
        return lambda i: (jnp.maximum(i * hb - 1, 0), col)

    def next_map(col):
        return lambda i: (jnp.minimum((i + 1) * hb, nhb - 1), col)

    kern = functools.partial(_cconv_kernel, tiles_per_seq=tiles_per_seq, halo=halo,
                             shifts=shifts, plan=plan, rc=rc, ln_rows=64)
    return pl.pallas_call(
        kern,
        grid=(t // tm,),
        in_specs=[pl.BlockSpec((tm, ch), lambda i: (i, 0)),
                  pl.BlockSpec((tm, ch), lambda i: (i, cb)),
                  pl.BlockSpec((halo, ch), prev_map(0)),
                  pl.BlockSpec((halo, ch), prev_map(cb)),
                  pl.BlockSpec((halo, ch), next_map(0)),
                  pl.BlockSpec((halo, ch), next_map(cb)),
                  pl.BlockSpec((CONF_KERNEL * SUBLANES, ch), lambda i: (0, 0)),
                  pl.BlockSpec((1, ch), lambda i: (0, 0)),
                  pl.BlockSpec((1, ch), lambda i: (0, 0)),
                  pl.BlockSpec((1, ch), lambda i: (0, 0))],
        out_specs=pl.BlockSpec((tm, ch), lambda i: (i, 0)),
        out_shape=jax.ShapeDtypeStruct((t, ch), BF16),
        scratch_shapes=[pltpu.VMEM((tm + 2 * halo, ch), F32),
                        pltpu.VMEM((len(shifts), tm + qmax, ch), F32),
                        pltpu.VMEM((tm, ch), F32)],
        compiler_params=_cparams(("parallel",)),
        name="cconv",
    )(proj, proj, proj, proj, proj, proj, jnp.repeat(w, SUBLANES, axis=0), b.reshape(1, ch),
      ln_g.reshape(1, ch),
      ln_b.reshape(1, ch))


def _sconv_kernel(x_ref, xp_ref, xn_ref, w_ref, b_ref, o_ref, hbuf_ref, sh_ref, *,
                  tiles_per_seq, halo, shifts, plan, rc):
    i = pl.program_id(0)
    tm = x_ref.shape[0]
    first = (i % tiles_per_seq) == 0
    last = (i % tiles_per_seq) == tiles_per_seq - 1
    hbuf_ref[0:halo, :] = jnp.where(first, 0.0, xp_ref[...])
    hbuf_ref[halo:halo + tm, :] = x_ref[...]
    hbuf_ref[halo + tm:2 * halo + tm, :] = jnp.where(last, 0.0, xn_ref[...])
    _fill_shifted(hbuf_ref, sh_ref, shifts, sh_ref.shape[1])

    def body(j, carry):
        r0 = j * rc
        h = _conv_rows(hbuf_ref, sh_ref, w_ref, plan, r0, rc) + b_ref[...]
        o_ref[pl.ds(pl.multiple_of(r0, rc), rc), :] = _silu(h)
        return carry

    lax.fori_loop(0, tm // rc, body, 0)


def _sconv(proj, col0, w, b, seq, tm):
    t = proj.shape[0]
    ch = w.shape[1]
    cw = 1024
    halo = SUBLANES
    rc = 16
    shifts, plan, qmax = _tap_plan(SSM_CONV, halo)
    tiles_per_seq = seq // tm
    hb = tm // halo
    nhb = t // halo
    c0 = col0 // cw
    kern = functools.partial(_sconv_kernel, tiles_per_seq=tiles_per_seq, halo=halo,
                             shifts=shifts, plan=plan, rc=rc)
    return pl.pallas_call(
        kern,
        grid=(t // tm, ch // cw),
        in_specs=[pl.BlockSpec((tm, cw), lambda i, j: (i, c0 + j)),
                  pl.BlockSpec((halo, cw), lambda i, j: (jnp.maximum(i * hb - 1, 0), c0 + j)),
                  pl.BlockSpec((halo, cw),
                               lambda i, j: (jnp.minimum((i + 1) * hb, nhb - 1), c0 + j)),
                  pl.BlockSpec((SSM_CONV * SUBLANES, cw), lambda i, j: (0, j)),
                  pl.BlockSpec((1, cw), lambda i, j: (0, j))],
        out_specs=pl.BlockSpec((tm, cw), lambda i, j: (i, j)),
        out_shape=jax.ShapeDtypeStruct((t, ch), F32),
        scratch_shapes=[pltpu.VMEM((tm + 2 * halo, cw), F32),
                        pltpu.VMEM((len(shifts), tm + qmax, cw), F32)],
        compiler_params=_cparams(("parallel", "parallel")),
        name="sconv",
    )(proj, proj, proj, jnp.repeat(w, SUBLANES, axis=0), b.reshape(1, ch))


def _split3(f):
    hi = f.astype(BF16)
    r1 = f - hi.astype(F32)
    mid = r1.astype(BF16)
    lo = (r1 - mid.astype(F32)).astype(BF16)
    return hi, mid, lo


def _expand_heads(f, e):
    hi, mid, lo = _split3(f)
    return (jnp.dot(hi, e, preferred_element_type=F32)
            + jnp.dot(mid, e, preferred_element_type=F32)
            + jnp.dot(lo, e, preferred_element_type=F32))


def _ssd_both_kernel(xf_ref, bf_ref, cf_ref, dtf_ref, xb_ref, bb_ref, cb_ref, dtb_ref, bias_ref,
                     alog_ref, ef_ref, eb_ref, of_ref, ob_ref, *scratch):
    nscr = len(scratch) // 2
    fwd, bwd = scratch[:nscr], scratch[nscr:]
    nheads = xf_ref.shape[1] // SSM_HEAD_DIM

    @pl.when(pl.program_id(1) == 0)
    def _():
        fwd[0][...] = jnp.zeros_like(fwd[0])
        bwd[0][...] = jnp.zeros_like(bwd[0])

    _ssd_direction(xf_ref, bf_ref, cf_ref, dtf_ref, bias_ref, alog_ref, ef_ref, of_ref, *fwd,
                   reverse=False, lane0=0)
    _ssd_direction(xb_ref, bb_ref, cb_ref, dtb_ref, bias_ref, alog_ref, eb_ref, ob_ref, *bwd,
                   reverse=True, lane0=nheads)


def _ssd_direction(x_ref, b_ref, c_ref, dt_ref, dtb_ref, alog_ref, e_ref, o_ref, state_ref,
                   acs_s, acst_s, dtt_s, wstt_s, carry_s, cb_s, bgt_s, *, reverse, lane0):
    q = x_ref.shape[0]
    nheads = x_ref.shape[1] // SSM_HEAD_DIM
    hpg = nheads // SSM_GROUPS

    row = lax.broadcasted_iota(jnp.int32, (q, q), 0)
    col = lax.broadcasted_iota(jnp.int32, (q, q), 1)
    keep = (col >= row) if reverse else (col <= row)

    z = dt_ref[...] + dtb_ref[...]
    dt = jnp.maximum(z, 0.0) + jnp.log(1.0 + jnp.exp(-jnp.abs(z)))
    a = dt * (-jnp.exp(alog_ref[...]))
    acs = jnp.dot(keep.astype(F32), a, precision=HIGHEST, preferred_element_type=F32)
    tot = acs[0:1, :] if reverse else acs[q - 1:q, :]
    acs2 = acs * LOG2E
    acs_s[...] = acs2
    acst_s[...] = acs2.T
    dtt_s[...] = dt.T
    wstt_s[...] = (dt * jnp.exp(tot - acs)).T
    carry_s[...] = _expand_heads(jnp.broadcast_to(jnp.exp(tot), (SUBLANES, LANES)), e_ref[...])

    for g in range(SSM_GROUPS):
        gsl = slice(g * SSM_STATE, (g + 1) * SSM_STATE)
        cb_s[g] = lax.dot_general(c_ref[:, gsl].astype(BF16), b_ref[:, gsl].astype(BF16),
                                  (((1,), (1,)), ((), ())), preferred_element_type=F32)
        bgt_s[g] = b_ref[:, gsl].T
        for pr in range(hpg // 2):
            ps = slice((g * hpg + 2 * pr) * SSM_HEAD_DIM, (g * hpg + 2 * pr + 2) * SSM_HEAD_DIM)
            xp = x_ref[:, ps].astype(BF16)
            rhs = jnp.concatenate([xp, state_ref[:, ps].astype(BF16)], axis=0)
            for sub in range(2):
                hl = lane0 + g * hpg + 2 * pr + sub
                half = slice(sub * SSM_HEAD_DIM, (sub + 1) * SSM_HEAD_DIM)
                hs = slice(ps.start + half.start, ps.start + half.stop)
                colb = jnp.broadcast_to(acs_s[:, hl:hl + 1], (q, q))
                seg = colb - acst_s[hl:hl + 1, :]
                lmat = jnp.exp2(jnp.where(keep, seg, -jnp.inf))
                m = (cb_s[g] * lmat * dtt_s[hl:hl + 1, :]).astype(BF16)
                cexp = (c_ref[:, gsl] * jnp.exp2(colb)).astype(BF16)
                y = jnp.dot(jnp.concatenate([m, cexp], axis=1), rhs,
                            preferred_element_type=F32)
                o_ref[:, hs] = y[:, half].astype(o_ref.dtype)
                bw = (bgt_s[g] * wstt_s[hl:hl + 1, :]).astype(BF16)
                upd = jnp.dot(bw, xp, preferred_element_type=F32)
                state_ref[:, hs] = state_ref[:, hs] * carry_s[0:1, hs] + upd[:, half]


def _ssd(xbc, proj, dt_col_block, dt_bias, a_log, expand_f, expand_b, bsz, seq):
    t = xbc.shape[0]
    q = SSM_CHUNK
    nc = seq // q
    inner = expand_f.shape[1]
    gn = SSM_GROUPS * SSM_STATE
    b_blk = inner // gn
    c_blk = b_blk + 1

    def fwd(b, c):
        return b * nc + c

    def bwd(b, c):
        return b * nc + nc - 1 - c

    def chunk_specs(tok):
        return [pl.BlockSpec((q, inner), lambda b, c: (tok(b, c), 0)),
                pl.BlockSpec((q, gn), lambda b, c: (tok(b, c), b_blk)),
                pl.BlockSpec((q, gn), lambda b, c: (tok(b, c), c_blk)),
                pl.BlockSpec((q, LANES), lambda b, c: (tok(b, c), dt_col_block))]

    def const(shape):
        return pl.BlockSpec(shape, lambda b, c: (0, 0))

    direction_scratch = [pltpu.VMEM((SSM_STATE, inner), F32),
                         pltpu.VMEM((q, LANES), F32), pltpu.VMEM((LANES, q), F32),
                         pltpu.VMEM((LANES, q), F32), pltpu.VMEM((LANES, q), F32),
                         pltpu.VMEM((SUBLANES, inner), F32),
                         pltpu.VMEM((SSM_GROUPS, q, q), F32),
                         pltpu.VMEM((SSM_GROUPS, SSM_STATE, q), F32)]
    return pl.pallas_call(
        _ssd_both_kernel,
        grid=(bsz, nc),
        in_specs=chunk_specs(fwd) + chunk_specs(bwd) + [
            const((1, LANES)), const((1, LANES)), const((LANES, inner)), const((LANES, inner))],
        out_specs=[pl.BlockSpec((q, inner), lambda b, c: (fwd(b, c), 0)),
                   pl.BlockSpec((q, inner), lambda b, c: (bwd(b, c), 0))],
        out_shape=[jax.ShapeDtypeStruct((t, inner), BF16), jax.ShapeDtypeStruct((t, inner), BF16)],
        scratch_shapes=direction_scratch + direction_scratch,
        compiler_params=_cparams(("arbitrary", "arbitrary")),
        name="ssd",
    )(xbc, xbc, xbc, proj, xbc, xbc, xbc, proj, dt_bias, a_log, expand_f, expand_b)


def _merge_kernel(yf_ref, yb_ref, xc_ref, z_ref, hc_ref, gc_ref, gs_ref, x_ref,
                  dexp_ref, ng_ref, wso_ref, wco_ref, bco_ref, wo_ref, g1_ref,
                  scale_ref, shift_ref, wr_ref, br_ref,
                  x1_ref, v_ref, lg_ref, *, ngroups, halves):
    tm, inner = yf_ref.shape
    gw = inner // ngroups
    hm = tm // halves
    for h in range(halves):
        r = slice(h * hm, (h + 1) * hm)
        y = ((yf_ref[r, :].astype(F32) + yb_ref[r, :].astype(F32) + xc_ref[r, :] * dexp_ref[...])
             * _silu(z_ref[r, :]))
        parts = []
        for g in range(ngroups):
            yg = y[:, g * gw:(g + 1) * gw]
            ms = jnp.mean(yg * yg, axis=-1, keepdims=True)
            parts.append((yg * lax.rsqrt(ms + EPS)) * ng_ref[:, g * gw:(g + 1) * gw])
        ysn = jnp.concatenate(parts, axis=-1).astype(BF16)
        y_ssm = jnp.dot(ysn, wso_ref[...], preferred_element_type=F32)
        y_conf = jnp.dot(hc_ref[r, :], wco_ref[...], preferred_element_type=F32) + bco_ref[...]
        merged = _sigmoid(gc_ref[r, :]) * y_conf + _sigmoid(gs_ref[r, :]) * y_ssm
        o = jnp.dot(merged.astype(BF16), wo_ref[...], preferred_element_type=F32)
        x1 = x_ref[r, :] + g1_ref[0] * o
        x1_ref[r, :] = x1
        v = _modulated_norm(x1, scale_ref[0], shift_ref[0])
        v_ref[r, :] = v
        v_hi = v.astype(BF16)
        v_lo = (v - v_hi.astype(F32)).astype(BF16)
        p = (jnp.dot(v_hi, wr_ref[...], preferred_element_type=F32)
             + jnp.dot(v_lo, wr_ref[...], preferred_element_type=F32))
        lg_ref[r, :] = p[:, :LANES] + p[:, LANES:] + br_ref[...]


def _merge(yf, yb, xbc, proj, hc, x, cols, dexp, norm_g, wso, wco, bco, wo, g1, scale, shift,
           wr, br, seq, tm):
    t, d = x.shape
    inner = yf.shape[1]
    tiles_per_seq = seq // tm
    z_blk = cols["z"] // inner
    gc_blk = cols["gate_conf"] // d
    gs_blk = cols["gate_ssm"] // d

    def const(shape):
        return pl.BlockSpec(shape, lambda i: tuple(0 for _ in shape),
                            pipeline_mode=pl.Buffered(1))

    def per_seq():
        return pl.BlockSpec((1, 1, d), lambda i: (i // tiles_per_seq, 0, 0))

    kern = functools.partial(_merge_kernel, ngroups=SSM_GROUPS, halves=2)
    return pl.pallas_call(
        kern,
        grid=(t // tm,),
        in_specs=[pl.BlockSpec((tm, inner), lambda i: (i, 0)),
                  pl.BlockSpec((tm, inner), lambda i: (i, 0)),
                  pl.BlockSpec((tm, inner), lambda i: (i, 0)),
                  pl.BlockSpec((tm, inner), lambda i: (i, z_blk)),
                  pl.BlockSpec((tm, d), lambda i: (i, 0)),
                  pl.BlockSpec((tm, d), lambda i: (i, gc_blk)),
                  pl.BlockSpec((tm, d), lambda i: (i, gs_blk)),
                  pl.BlockSpec((tm, d), lambda i: (i, 0)),
                  const((1, inner)), const((1, inner)),
                  const((inner, d)), const((d, d)), const((1, d)), const((d, d)),
                  per_seq(), per_seq(), per_seq(),
                  const((d, 2 * LANES)), const((1, LANES))],
        out_specs=[pl.BlockSpec((tm, d), lambda i: (i, 0)),
                   pl.BlockSpec((tm, d), lambda i: (i, 0)),
                   pl.BlockSpec((tm, LANES), lambda i: (i, 0))],
        out_shape=[jax.ShapeDtypeStruct((t, d), F32),
                   jax.ShapeDtypeStruct((t, d), F32),
                   jax.ShapeDtypeStruct((t, LANES), F32)],
        compiler_params=_cparams(("parallel",)),
        name="merge",
    )(yf, yb, xbc, proj, hc, proj, proj, x, dexp, norm_g, wso, wco, bco, wo, g1, scale, shift,
      wr, br)


ROUTE_TILE = 256


def _route_kernel(lg_ref, w_ref, lslot_ref, lslot_t_ref, cnt_ref, *, n_experts):
    tm = lg_ref.shape[0]
    lg = lg_ref[...].T[0:n_experts, :]
    erow = lax.broadcasted_iota(jnp.int32, (n_experts, tm), 0)
    sel = jnp.zeros((n_experts, tm), F32)
    vals, idxs = [], []
    for _ in range(TOP_K):
        m = jnp.max(lg, axis=0, keepdims=True)
        ix = jnp.min(jnp.where(lg == m, erow, n_experts), axis=0, keepdims=True)
        hit = erow == ix
        sel = jnp.where(hit, 1.0, sel)
        lg = jnp.where(hit, -jnp.inf, lg)
        vals.append(m)
        idxs.append(ix)
    ex = [jnp.exp(v - vals[0]) for v in vals]
    den = ex[0] + ex[1] + ex[2] + ex[3]
    r = lax.broadcasted_iota(jnp.int32, (tm, tm), 0)
    c = lax.broadcasted_iota(jnp.int32, (tm, tm), 1)
    rank = jnp.dot(sel.astype(BF16), (r < c).astype(BF16), preferred_element_type=F32)
    cnt = jnp.broadcast_to(jnp.sum(sel, axis=1, keepdims=True), (n_experts, LANES))
    er = lax.broadcasted_iota(jnp.int32, (n_experts, n_experts), 0)
    ec = lax.broadcasted_iota(jnp.int32, (n_experts, n_experts), 1)
    off = jnp.dot((ec < er).astype(BF16), cnt.astype(BF16), preferred_element_type=F32)
    slot_of = rank + off[:, 0:1]
    row8 = lax.broadcasted_iota(jnp.int32, (LANES, tm), 0)
    slot_t = jnp.zeros((LANES, tm), F32)
    w_t = jnp.zeros((LANES, tm), F32)
    for k in range(TOP_K):
        sk = jnp.sum(jnp.where(erow == idxs[k], slot_of, 0.0), axis=0, keepdims=True)
        slot_t = jnp.where(row8 == k, sk, slot_t)
        w_t = jnp.where(row8 == k, ex[k] / den, w_t)
    w_ref[...] = w_t.T
    lslot_ref[...] = slot_t.T.astype(jnp.int32)
    lslot_t_ref[...] = slot_t[0:SUBLANES, :].astype(jnp.int32)
    cnt_pad = jnp.concatenate([cnt, jnp.zeros((LANES - n_experts, LANES), F32)], axis=0)
    cnt_ref[0] = cnt_pad.T[0:SUBLANES, :].astype(jnp.int32)


def _route(logits, n_experts):
    t = logits.shape[0]
    tm = ROUTE_TILE
    nt = t // tm
    kern = functools.partial(_route_kernel, n_experts=n_experts)
    return pl.pallas_call(
        kern,
        grid=(nt,),
        in_specs=[pl.BlockSpec((tm, LANES), lambda i: (i, 0))],
        out_specs=[pl.BlockSpec((tm, LANES), lambda i: (i, 0)),
                   pl.BlockSpec((tm, LANES), lambda i: (i, 0)),
                   pl.BlockSpec((SUBLANES, tm), lambda i: (i, 0)),
                   pl.BlockSpec((1, SUBLANES, LANES), lambda i: (i, 0, 0))],
        out_shape=[jax.ShapeDtypeStruct((t, LANES), F32),
                   jax.ShapeDtypeStruct((t, LANES), jnp.int32),
                   jax.ShapeDtypeStruct((nt * SUBLANES, tm), jnp.int32),
                   jax.ShapeDtypeStruct((nt, SUBLANES, LANES), jnp.int32)],
        compiler_params=_cparams(("parallel",)),
        name="route",
    )(logits)


def _row_copy(src, src_row, dst, dst_row, sem, nch):
    return pltpu.make_async_copy(_row(src, src_row, nch), _row(dst, dst_row, nch), sem)


RUN_BITS = ROUTE_TILE.bit_length()


def _excl_cumsum(x, axis):
    n = x.shape[axis]
    earlier = jnp.arange(n)[:, None] > jnp.arange(n)[None, :]
    xm = jnp.moveaxis(x, axis, -1)
    out = jnp.sum(jnp.where(earlier, xm[..., None, :], 0), axis=-1)
    return jnp.moveaxis(out, -1, axis)


def _run_piece_tables(cnt2, loc2, glob2):
    nt, ne = cnt2.shape
    bits = jnp.arange(RUN_BITS, dtype=jnp.int32)[:, None, None]
    n = cnt2.T[None, :, :]
    valid = ((n >> bits) & 1) == 1
    done = (n >> (bits + 1)) << (bits + 1)
    lower = jnp.arange(ne)[:, None] > jnp.arange(ne)[None, :]
    pos = jnp.sum(jnp.where(lower[None, :, :, None], valid[:, None, :, :], False), axis=2,
                  dtype=jnp.int32)
    j = jnp.tile(jnp.arange(ne, dtype=jnp.int32), nt)
    hit = jnp.repeat(valid, ne, axis=-1) & (jnp.repeat(pos, ne, axis=-1) == j)

    def compact(start2):
        val = jnp.repeat(start2.T[None, :, :] + done, ne, axis=-1)
        out = jnp.sum(jnp.where(hit, val, 0), axis=1)
        return out.reshape(RUN_BITS, nt, ne).transpose(1, 0, 2).reshape(-1).astype(jnp.int32)

    npieces = jnp.sum(valid.astype(jnp.int32), axis=1).T.reshape(-1).astype(jnp.int32)
    return npieces, compact(loc2), compact(glob2)


def _segment_copies(np_ref, loc_ref, glob_ref, tile, n_experts, local_buf, hbm, sem, nch, to_hbm):
    for bit in range(RUN_BITS):
        rows = (1 << bit) * nch
        base = tile * RUN_BITS + bit

        def body(j, c, rows=rows, base=base):
            lo = loc_ref[base * n_experts + j]
            go = glob_ref[base * n_experts + j]
            loc = local_buf.at[pl.ds(pl.multiple_of(lo * nch, nch), rows), :]
            glb = hbm.at[pl.ds(pl.multiple_of(go * nch, nch), rows), :]
            if to_hbm:
                pltpu.make_async_copy(loc, glb, sem).start()
            else:
                pltpu.make_async_copy(glb, loc, sem).start()
            return c

        lax.fori_loop(0, np_ref[base], body, 0)


def _wait_segments(local_buf, hbm, sem, to_hbm):
    whole = hbm.at[pl.ds(0, local_buf.shape[0]), :]
    if to_hbm:
        pltpu.make_async_copy(local_buf, whole, sem).wait()
    else:
        pltpu.make_async_copy(whole, local_buf, sem).wait()


def _dispatch_kernel(cnt_ref, loc_ref, glob_ref, fill_ref, end_ref, lslot_t_ref, v_ref, xs_hbm,
                     buf, sems, fsem, *, n_experts, nch):
    i = pl.program_id(0)
    n = pl.num_programs(0)
    tm = v_ref.shape[0]
    nslots = tm * TOP_K
    slot = i % 2
    srow = lax.broadcasted_iota(jnp.int32, (nslots, tm), 0)
    hit = srow == lslot_t_ref[0:1, :]
    for k in range(1, TOP_K):
        hit = jnp.logical_or(hit, srow == lslot_t_ref[k:k + 1, :])
    p = jnp.where(hit, 1.0, 0.0).astype(BF16)
    v = v_ref[...]
    v_hi = v.astype(BF16)
    v_lo = (v - v_hi.astype(F32)).astype(BF16)
    xl = (jnp.dot(p, v_hi, preferred_element_type=F32)
          + jnp.dot(p, v_lo, preferred_element_type=F32))
    for s_ in range(nch):
        buf[slot, pl.ds(s_, nslots, stride=nch), :] = xl[:, s_ * LANES:(s_ + 1) * LANES]

    @pl.when(i == 0)
    def _():
        def fill(s, c):
            _row_copy(buf.at[0], 0, xs_hbm, s, fsem, nch).start()
            return c

        def drain(s, c):
            _row_copy(buf.at[0], 0, xs_hbm, s, fsem, nch).wait()
            return c

        group = 8
        for e0 in range(0, n_experts, group):
            for e in range(e0, min(e0 + group, n_experts)):
                lax.fori_loop(fill_ref[e], end_ref[e], fill, 0)
            for e in range(e0, min(e0 + group, n_experts)):
                lax.fori_loop(fill_ref[e], end_ref[e], drain, 0)

    args = (cnt_ref, loc_ref, glob_ref)
    _segment_copies(*args, i, n_experts, buf.at[slot], xs_hbm, sems.at[slot], nch, True)

    @pl.when(i > 0)
    def _():
        _wait_segments(buf.at[1 - slot], xs_hbm, sems.at[1 - slot], True)

    @pl.when(i == n - 1)
    def _():
        _wait_segments(buf.at[slot], xs_hbm, sems.at[slot], True)


def _dispatch(cnt_flat, loc_flat, glob_flat, fill_start, pad_end, lslot_t, v, n_slots):
    t, d = v.shape
    nch = d // LANES
    tm = ROUTE_TILE
    n_experts = fill_start.shape[0]
    kern = functools.partial(_dispatch_kernel, n_experts=n_experts, nch=nch)
    grid_spec = pltpu.PrefetchScalarGridSpec(
        num_scalar_prefetch=5,
        grid=(t // tm,),
        in_specs=[pl.BlockSpec((SUBLANES, tm), lambda i, *_: (i, 0)),
                  pl.BlockSpec((tm, d), lambda i, *_: (i, 0))],
        out_specs=pl.BlockSpec(memory_space=pl.ANY),
        scratch_shapes=[pltpu.VMEM((2, tm * TOP_K * nch, LANES), F32),
                        pltpu.SemaphoreType.DMA((2,)), pltpu.SemaphoreType.DMA],
    )
    return pl.pallas_call(
        kern,
        grid_spec=grid_spec,
        out_shape=jax.ShapeDtypeStruct((n_slots * nch, LANES), v.dtype),
        compiler_params=_cparams(("arbitrary",)),
        name="dispatch",
    )(cnt_flat, loc_flat, glob_flat, fill_start, pad_end, lslot_t, v)


def _moe_kernel(be_ref, nused_ref, x_ref, wgu_ref, bgu_ref, wd_ref, bd_ref, o_ref,
                wgu_bf, wd_bf):
    b = pl.program_id(0)
    changed = jnp.logical_or(b == 0, be_ref[b] != be_ref[jnp.maximum(b - 1, 0)])
    active = b < nused_ref[0]

    @pl.when(jnp.logical_and(active, changed))
    def _():
        wgu_bf[...] = wgu_ref[0].astype(BF16)
        wd_bf[...] = wd_ref[0].astype(BF16)

    @pl.when(active)
    def _():
        ff = wd_bf.shape[0]
        d = wgu_bf.shape[0]
        nch = d // LANES
        hm = x_ref.shape[0] // nch // MOE_HALVES
        for h in range(MOE_HALVES):
            x = _load_rows(x_ref, hm, nch, h * hm).astype(BF16)
            gu = jnp.dot(x, wgu_bf[...], preferred_element_type=F32) + bgu_ref[0]
            gate = jnp.minimum(gu[:, :ff], SWIGLU_LIMIT)
            up = jnp.clip(gu[:, ff:], -SWIGLU_LIMIT, SWIGLU_LIMIT)
            glu = gate * _sigmoid(SWIGLU_ALPHA * gate)
            hid = ((up + 1.0) * glu).astype(BF16)
            y = jnp.dot(hid, wd_bf[...], preferred_element_type=F32) + bd_ref[0]
            _store_rows(o_ref, y, h * hm)


def _moe(block_expert, nused, xs, w_gu, b_gu, w_down, b_down):
    ne, d, ff2 = w_gu.shape
    nch = d // LANES
    n_slots = xs.shape[0] // nch
    ff = w_down.shape[1]
    bm = MOE_BLOCK

    def blk(b, nu):
        return jnp.minimum(b, nu[0] - 1)

    def rows(b, be, nu):
        return (blk(b, nu), 0)

    def expert(b, be, nu):
        return (be[blk(b, nu)], 0, 0)

    grid_spec = pltpu.PrefetchScalarGridSpec(
        num_scalar_prefetch=2,
        grid=(n_slots // bm,),
        in_specs=[pl.BlockSpec((bm * nch, LANES), rows),
                  pl.BlockSpec((1, d, ff2), expert),
                  pl.BlockSpec((1, 1, ff2), expert),
                  pl.BlockSpec((1, ff, d), expert),
                  pl.BlockSpec((1, 1, d), expert)],
        out_specs=pl.BlockSpec((bm * nch, LANES), rows),
        scratch_shapes=[pltpu.VMEM((d, ff2), BF16), pltpu.VMEM((ff, d), BF16)],
    )
    return pl.pallas_call(
        _moe_kernel,
        grid_spec=grid_spec,
        out_shape=jax.ShapeDtypeStruct(xs.shape, F32),
        compiler_params=_cparams(("arbitrary",)),
        name="moe",
    )(block_expert, nused, xs, w_gu, b_gu.reshape(ne, 1, ff2), w_down, b_down.reshape(ne, 1, d))


def _final_kernel(cnt_ref, loc_ref, glob_ref, x1_ref, lslot_ref, w_ref, g2_ref, scale_ref,
                  shift_ref, ys_hbm, o_ref, buf, sems, *, n_experts, nch):
    i = pl.program_id(0)
    n = pl.num_programs(0)
    tm = x1_ref.shape[0]
    nslots = tm * TOP_K
    slot = i % 2
    args = (cnt_ref, loc_ref, glob_ref)

    @pl.when(i == 0)
    def _():
        _segment_copies(*args, 0, n_experts, buf.at[0], ys_hbm, sems.at[0], nch, False)

    @pl.when(i + 1 < n)
    def _():
        _segment_copies(*args, i + 1, n_experts, buf.at[1 - slot], ys_hbm, sems.at[1 - slot], nch,
                        False)

    _wait_segments(buf.at[slot], ys_hbm, sems.at[slot], False)

    lane_slot = lax.broadcasted_iota(jnp.int32, (tm, nslots), 1)
    pw = jnp.zeros((tm, nslots), F32)
    for k in range(TOP_K):
        pw = jnp.where(lane_slot == lslot_ref[:, k:k + 1], w_ref[:, k:k + 1], pw)
    pw_hi = pw.astype(BF16)
    pw_lo = (pw - pw_hi.astype(F32)).astype(BF16)
    y = _load_rows(buf, nslots, nch, 0, (slot,))
    y_hi = y.astype(BF16)
    y_lo = (y - y_hi.astype(F32)).astype(BF16)
    f = (jnp.dot(pw_hi, y_hi, preferred_element_type=F32)
         + jnp.dot(pw_hi, y_lo, preferred_element_type=F32)
         + jnp.dot(pw_lo, y_hi, preferred_element_type=F32))
    x2 = x1_ref[...] + g2_ref[0] * f
    o_ref[...] = _modulated_norm(x2, scale_ref[0], shift_ref[0])


def _final(cnt_flat, loc_flat, glob_flat, n_experts, x1, ys, lslot, top_w, g2, scale, shift, seq):
    t, d = x1.shape
    nch = d // LANES
    tm = ROUTE_TILE
    tiles_per_seq = seq // tm

    def per_seq():
        return pl.BlockSpec((1, 1, d), lambda i, *_: (i // tiles_per_seq, 0, 0))

    grid_spec = pltpu.PrefetchScalarGridSpec(
        num_scalar_prefetch=3,
        grid=(t // tm,),
        in_specs=[pl.BlockSpec((tm, d), lambda i, *_: (i, 0)),
                  pl.BlockSpec((tm, LANES), lambda i, *_: (i, 0)),
                  pl.BlockSpec((tm, LANES), lambda i, *_: (i, 0)),
                  per_seq(), per_seq(), per_seq(),
                  pl.BlockSpec(memory_space=pl.ANY)],
        out_specs=pl.BlockSpec((tm, d), lambda i, *_: (i, 0)),
        scratch_shapes=[pltpu.VMEM((2, TOP_K * tm * nch, LANES), F32),
                        pltpu.SemaphoreType.DMA((2,))],
    )
    return pl.pallas_call(
        functools.partial(_final_kernel, n_experts=n_experts, nch=nch),
        grid_spec=grid_spec,
        out_shape=jax.ShapeDtypeStruct((t, d), F32),
        compiler_params=_cparams(("arbitrary",)),
        name="final",
    )(cnt_flat, loc_flat, glob_flat, x1, lslot, top_w, g2, scale, shift, ys)


def _pad_cols(w, n):
    return jnp.pad(w, ((0, 0), (0, n - w.shape[1])))


def kernel(x, c, ada_w, ada_b, norm_mix_g, w_in, conf_dw_w, conf_dw_b, conf_ln_g, conf_ln_b,
           conf_out_w, conf_out_b, ssm_conv_w, ssm_conv_b, dt_bias_f, dt_bias_b, a_log_f,
           a_log_b, ssm_d, ssm_norm_g, ssm_out_w, w_o, norm_ffn_g, router_w, router_b, w_gu,
           b_gu, w_down, b_down, final_ada_w, final_ada_b, final_norm_g):
    bsz, seq, d = x.shape
    depth = ada_w.shape[0]
    t = bsz * seq
    nheads = a_log_f.shape[1]
    inner = nheads * SSM_HEAD_DIM
    gn = SSM_GROUPS * SSM_STATE
    conf = conf_dw_w.shape[2]
    n_experts = router_w.shape[2]
    assert 2 * nheads <= LANES and n_experts <= LANES

    c_pad = jnp.zeros((SUBLANES, d), F32).at[:bsz].set(c)
    fin = _ada(c_pad, final_ada_w, final_ada_b)[:bsz]
    xf = x.reshape(t, d)

    sizes = [("conf_a", conf), ("conf_g", conf), ("z", inner), ("xs", inner), ("bm", gn),
             ("cm", gn), ("gate_conf", d), ("gate_ssm", d), ("dt", LANES)]
    cols, off = {}, 0
    for name, n in sizes:
        cols[name] = off
        off += n
    src = {}
    o = 0
    for name, n in [("conf_a", conf), ("conf_g", conf), ("z", inner), ("xs", inner), ("bm", gn),
                    ("cm", gn), ("dtf", nheads), ("dtb", nheads), ("gate_conf", d),
                    ("gate_ssm", d)]:
        src[name] = (o, o + n)
        o += n

    head_of_col = jnp.arange(inner, dtype=jnp.int32) // SSM_HEAD_DIM
    lanes = jnp.arange(LANES, dtype=jnp.int32)[:, None]
    expand_f = (lanes == head_of_col[None, :]).astype(BF16)
    expand_b = (lanes == head_of_col[None, :] + nheads).astype(BF16)

    for l in range(depth):
        ada = _ada(c_pad, ada_w[l], ada_b[l])[:bsz]
        sh1, sc1, g1, sh2, sc2, g2 = [a.reshape(bsz, 1, d) for a in jnp.split(ada, 6, axis=-1)]
        scale1 = norm_mix_g[l][None, None, :] * (1.0 + sc1)
        scale2 = norm_ffn_g[l][None, None, :] * (1.0 + sc2)

        wl = w_in[l]
        w_dt = _pad_cols(jnp.concatenate([wl[:, slice(*src["dtf"])], wl[:, slice(*src["dtb"])]],
                                         axis=1), LANES)
        w_perm = jnp.concatenate(
            [wl[:, slice(*src[n])] for n in ("conf_a", "conf_g", "z", "xs", "bm", "cm",
                                             "gate_conf", "gate_ssm")] + [w_dt],
            axis=1).astype(BF16)
        proj = _inproj(xf, scale1, sh1, w_perm, seq, tm=INPROJ_TILE)

        hc = _cconv(proj, conf_dw_w[l], conf_dw_b[l], conf_ln_g[l], conf_ln_b[l], seq,
                    tm=CCONV_TILE)
        xbc = _sconv(proj, cols["xs"], ssm_conv_w[l], ssm_conv_b[l], seq, tm=SCONV_TILE)

        dt_bias = _pad_cols(jnp.concatenate([dt_bias_f[l], dt_bias_b[l]])[None, :], LANES)
        a_log = _pad_cols(jnp.concatenate([a_log_f[l], a_log_b[l]])[None, :], LANES)
        dt_blk = cols["dt"] // LANES
        y_f, y_b = _ssd(xbc, proj, dt_blk, dt_bias, a_log, expand_f, expand_b, bsz, seq)

        dexp = jnp.repeat(ssm_d[l], SSM_HEAD_DIM)[None, :]
        wr = _pad_cols(router_w[l], LANES)
        wr_hi = wr.astype(BF16)
        wr = jnp.concatenate([wr_hi, (wr - wr_hi.astype(F32)).astype(BF16)], axis=1)
        br = _pad_cols(router_b[l][None, :], LANES)
        x1, v, logits = _merge(
            y_f, y_b, xbc, proj, hc, xf, cols, dexp, ssm_norm_g[l][None, :],
            ssm_out_w[l].astype(BF16), conf_out_w[l].astype(BF16), conf_out_b[l][None, :],
            w_o[l].astype(BF16), g1, scale2, sh2, wr, br, seq, tm=MERGE_TILE)

        top_w, lslot, lslot_t, tile_cnt = _route(logits, n_experts)
        cnt2 = tile_cnt[:, 0, :n_experts]
        counts = jnp.sum(cnt2, axis=0)
        bm = MOE_BLOCK
        n_blocks = (t * TOP_K) // bm + n_experts
        padded = ((counts + bm - 1) // bm) * bm
        pad_start = _excl_cumsum(padded, axis=0)
        pad_end = pad_start + padded
        loc2 = _excl_cumsum(cnt2, axis=1)
        glob2 = pad_start[None, :] + _excl_cumsum(cnt2, axis=0)
        tables = _run_piece_tables(cnt2, loc2, glob2)
        block_first = jnp.arange(n_blocks, dtype=jnp.int32) * bm
        block_expert = jnp.minimum(
            jnp.sum((pad_end[None, :] <= block_first[:, None]).astype(jnp.int32), axis=1),
            n_experts - 1).astype(jnp.int32)
        nused = (pad_end[-1] // bm).astype(jnp.int32).reshape(1)
        xs = _dispatch(*tables, (pad_start + counts).astype(jnp.int32),
                       pad_end.astype(jnp.int32), lslot_t, v, n_blocks * bm)
        ys = _moe(block_expert, nused, xs, w_gu[l], b_gu[l], w_down[l], b_down[l])

        if l == depth - 1:
            sh_f, sc_f = [a.reshape(bsz, 1, d) for a in jnp.split(fin, 2, axis=-1)]
            scale_f = final_norm_g[None, None, :] * (1.0 + sc_f)
            xf = _final(*tables, n_experts, x1, ys, lslot, top_w, g2, scale_f, sh_f, seq)
        else:
            raise NotImplementedError("depth > 1 is not wired")
    return xf.reshape(bsz, seq, d)
```

```python
import functools

import jax
import jax.numpy as jnp
from jax import lax
from jax.experimental import pallas as pl
from jax.experimental.pallas import tpu as pltpu

F32 = jnp.float32
BF16 = jnp.bfloat16
HIGHEST = lax.Precision.HIGHEST

EPS = 1e-6
LOG2E = 1.4426950408889634
CONF_KERNEL = 31
SSM_CONV = 5
SSM_HEAD_DIM = 64
SSM_GROUPS = 4
SSM_STATE = 128
SSM_CHUNK = 128
TOP_K = 4
SWIGLU_ALPHA = 1.702
SWIGLU_LIMIT = 7.0

LANES = 128
SUBLANES = 8
VMEM_LIMIT = 56 * 1024 * 1024

INPROJ_TILE = 256
CCONV_TILE = 512
SCONV_TILE = 1024
MERGE_TILE = 256
MOE_BLOCK = 512
MOE_HALVES = 2


def _cparams(sem):
    return pltpu.CompilerParams(dimension_semantics=sem, vmem_limit_bytes=VMEM_LIMIT)


def _store_rows(ref, val, row0=0):
    n, d = val.shape
    nch = d // LANES
    for s in range(nch):
        ref[pl.ds(row0 * nch + s, n, stride=nch), :] = val[:, s * LANES:(s + 1) * LANES]


def _load_rows(ref, n, nch, row0=0, lead=()):
    return jnp.concatenate(
        [ref[lead + (pl.ds(row0 * nch + s, n, stride=nch), slice(None))] for s in range(nch)],
        axis=-1)


def _row(ref, r, nch, lead=()):
    return ref.at[lead + (pl.ds(pl.multiple_of(r * nch, nch), nch), slice(None))]


def _sigmoid(x):
    return 1.0 / (1.0 + jnp.exp(-x))


def _silu(x):
    return x * _sigmoid(x)


def _ada_kernel(c_ref, w_ref, b_ref, o_ref):
    c = c_ref[...]
    o_ref[...] = jnp.dot(_silu(c), w_ref[...], precision=HIGHEST,
                         preferred_element_type=F32) + b_ref[...]


def _ada(c_pad, w, b):
    d, n = w.shape
    tn = 1024
    return pl.pallas_call(
        _ada_kernel,
        grid=(n // tn,),
        in_specs=[pl.BlockSpec((SUBLANES, d), lambda j: (0, 0)),
                  pl.BlockSpec((d, tn), lambda j: (0, j)),
                  pl.BlockSpec((1, tn), lambda j: (0, j))],
        out_specs=pl.BlockSpec((SUBLANES, tn), lambda j: (0, j)),
        out_shape=jax.ShapeDtypeStruct((SUBLANES, n), F32),
        compiler_params=_cparams(("parallel",)),
        name="ada",
    )(c_pad, w, b.reshape(1, n))


def _modulated_norm(x, scale, shift):
    ms = jnp.mean(x * x, axis=-1, keepdims=True)
    return (x * lax.rsqrt(ms + EPS)) * scale + shift


def _inproj_kernel(x_ref, scale_ref, shift_ref, w_ref, o_ref, *, col_chunk):
    u = _modulated_norm(x_ref[...], scale_ref[0], shift_ref[0]).astype(BF16)
    ncols = o_ref.shape[1]
    for c0 in range(0, ncols, col_chunk):
        c1 = min(c0 + col_chunk, ncols)
        o_ref[:, c0:c1] = jnp.dot(u, w_ref[:, c0:c1], preferred_element_type=F32)


def _inproj(x, scale, shift, w_bf, seq, tm):
    t, d = x.shape
    ncols = w_bf.shape[1]
    tiles_per_seq = seq // tm
    return pl.pallas_call(
        functools.partial(_inproj_kernel, col_chunk=1024),
        grid=(t // tm,),
        in_specs=[pl.BlockSpec((tm, d), lambda i: (i, 0)),
                  pl.BlockSpec((1, 1, d), lambda i: (i // tiles_per_seq, 0, 0)),
                  pl.BlockSpec((1, 1, d), lambda i: (i // tiles_per_seq, 0, 0)),
                  pl.BlockSpec((d, ncols), lambda i: (0, 0), pipeline_mode=pl.Buffered(1))],
        out_specs=pl.BlockSpec((tm, ncols), lambda i: (i, 0)),
        out_shape=jax.ShapeDtypeStruct((t, ncols), F32),
        compiler_params=_cparams(("parallel",)),
        name="inproj",
    )(x, scale, shift, w_bf)


def _fill_shifted(hbuf_ref, sh_ref, shifts, rows):
    for n, r in enumerate(shifts):
        sh_ref[n] = hbuf_ref[r:r + rows, :]


def _tap_plan(ktaps, halo):
    offs = [k + halo - ktaps // 2 for k in range(ktaps)]
    shifts = sorted({o % SUBLANES for o in offs} - {0})
    plan = [(shifts.index(o % SUBLANES) if o % SUBLANES else -1, (o // SUBLANES) * SUBLANES)
            for o in offs]
    return shifts, plan, max(o // SUBLANES for o in offs) * SUBLANES


def _conv_rows(hbuf_ref, sh_ref, w8_ref, plan, r0, rc):
    groups = rc // SUBLANES
    accs = [None] * groups
    for k, (n, q8) in enumerate(plan):
        w = w8_ref[k * SUBLANES:(k + 1) * SUBLANES, :]
        for j in range(groups):
            start = pl.multiple_of(r0 + q8 + j * SUBLANES, SUBLANES)
            rows = pl.ds(start, SUBLANES)
            src = hbuf_ref[rows, :] if n < 0 else sh_ref[n, rows, :]
            term = src * w
            accs[j] = term if accs[j] is None else accs[j] + term
    return jnp.concatenate(accs, axis=0)


def _cconv_kernel(a_ref, g_ref, ap_ref, gp_ref, an_ref, gn_ref, w_ref, b_ref, lg_ref, lb_ref,
                  o_ref, hbuf_ref, sh_ref, cbuf_ref, *, tiles_per_seq, halo, shifts, plan, rc,
                  ln_rows):
    i = pl.program_id(0)
    tm = a_ref.shape[0]
    first = (i % tiles_per_seq) == 0
    last = (i % tiles_per_seq) == tiles_per_seq - 1
    glu_p = ap_ref[...] * _sigmoid(gp_ref[...])
    glu_n = an_ref[...] * _sigmoid(gn_ref[...])
    hbuf_ref[0:halo, :] = jnp.where(first, 0.0, glu_p)
    hbuf_ref[halo:halo + tm, :] = a_ref[...] * _sigmoid(g_ref[...])
    hbuf_ref[halo + tm:2 * halo + tm, :] = jnp.where(last, 0.0, glu_n)
    _fill_shifted(hbuf_ref, sh_ref, shifts, sh_ref.shape[1])

    def body(j, carry):
        r0 = j * rc
        h = _conv_rows(hbuf_ref, sh_ref, w_ref, plan, r0, rc) + b_ref[...]
        cbuf_ref[pl.ds(pl.multiple_of(r0, rc), rc), :] = h
        return carry

    lax.fori_loop(0, tm // rc, body, 0)

    for c0 in range(0, tm, ln_rows):
        h = cbuf_ref[c0:c0 + ln_rows, :]
        mu = jnp.mean(h, axis=-1, keepdims=True)
        hc = h - mu
        var = jnp.mean(hc * hc, axis=-1, keepdims=True)
        y = hc * lax.rsqrt(var + EPS) * lg_ref[...] + lb_ref[...]
        o_ref[c0:c0 + ln_rows, :] = _silu(y).astype(o_ref.dtype)


def _cconv(proj, w, b, ln_g, ln_b, seq, tm):
    t = proj.shape[0]
    ch = w.shape[1]
    halo = 16
    rc = 32
    shifts, plan, qmax = _tap_plan(CONF_KERNEL, halo)
    tiles_per_seq = seq // tm
    hb = tm // halo
    nhb = t // halo
    cb = 1

    def prev_map(col):
        return lambda i: (jnp.maximum(i * hb - 1, 0), col)

    def next_map(col):
        return lambda i: (jnp.minimum((i + 1) * hb, nhb - 1), col)

    kern = functools.partial(_cconv_kernel, tiles_per_seq=tiles_per_seq, halo=halo,
                             shifts=shifts, plan=plan, rc=rc, ln_rows=64)
    return pl.pallas_call(
        kern,
        grid=(t // tm,),
        in_specs=[pl.BlockSpec((tm, ch), lambda i: (i, 0)),
                  pl.BlockSpec((tm, ch), lambda i: (i, cb)),
                  pl.BlockSpec((halo, ch), prev_map(0)),
                  pl.BlockSpec((halo, ch), prev_map(cb)),
                  pl.BlockSpec((halo, ch), next_map(0)),
                  pl.BlockSpec((halo, ch), next_map(cb)),
                  pl.BlockSpec((CONF_KERNEL * SUBLANES, ch), lambda i: (0, 0)),
                  pl.BlockSpec((1, ch), lambda i: (0, 0)),
                  pl.BlockSpec((1, ch), lambda i: (0, 0)),
                  pl.BlockSpec((1, ch), lambda i: (0, 0))],
        out_specs=pl.BlockSpec((tm, ch), lambda i: (i, 0)),
        out_shape=jax.ShapeDtypeStruct((t, ch), BF16),
        scratch_shapes=[pltpu.VMEM((tm + 2 * halo, ch), F32),
                        pltpu.VMEM((len(shifts), tm + qmax, ch), F32),
                        pltpu.VMEM((tm, ch), F32)],
        compiler_params=_cparams(("parallel",)),
        name="cconv",
    )(proj, proj, proj, proj, proj, proj, jnp.repeat(w, SUBLANES, axis=0), b.reshape(1, ch),
      ln_g.reshape(1, ch),
      ln_b.reshape(1, ch))


def _sconv_kernel(x_ref, xp_ref, xn_ref, w_ref, b_ref, o_ref, hbuf_ref, sh_ref, *,
                  tiles_per_seq, halo, shifts, plan, rc):
    i = pl.program_id(0)
    tm = x_ref.shape[0]
    first = (i % tiles_per_seq) == 0
    last = (i % tiles_per_seq) == tiles_per_seq - 1
    hbuf_ref[0:halo, :] = jnp.where(first, 0.0, xp_ref[...])
    hbuf_ref[halo:halo + tm, :] = x_ref[...]
    hbuf_ref[halo + tm:2 * halo + tm, :] = jnp.where(last, 0.0, xn_ref[...])
    _fill_shifted(hbuf_ref, sh_ref, shifts, sh_ref.shape[1])

    def body(j, carry):
        r0 = j * rc
        h = _conv_rows(hbuf_ref, sh_ref, w_ref, plan, r0, rc) + b_ref[...]
        o_ref[pl.ds(pl.multiple_of(r0, rc), rc), :] = _silu(h)
        return carry

    lax.fori_loop(0, tm // rc, body, 0)


def _sconv(proj, col0, w, b, seq, tm):
    t = proj.shape[0]
    ch = w.shape[1]
    cw = 1024
    halo = SUBLANES
    rc = 32
    shifts, plan, qmax = _tap_plan(SSM_CONV, halo)
    tiles_per_seq = seq // tm
    hb = tm // halo
    nhb = t // halo
    c0 = col0 // cw
    kern = functools.partial(_sconv_kernel, tiles_per_seq=tiles_per_seq, halo=halo,
                             shifts=shifts, plan=plan, rc=rc)
    return pl.pallas_call(
        kern,
        grid=(t // tm, ch // cw),
        in_specs=[pl.BlockSpec((tm, cw), lambda i, j: (i, c0 + j)),
                  pl.BlockSpec((halo, cw), lambda i, j: (jnp.maximum(i * hb - 1, 0), c0 + j)),
                  pl.BlockSpec((halo, cw),
                               lambda i, j: (jnp.minimum((i + 1) * hb, nhb - 1), c0 + j)),
                  pl.BlockSpec((SSM_CONV * SUBLANES, cw), lambda i, j: (0, j)),
                  pl.BlockSpec((1, cw), lambda i, j: (0, j))],
        out_specs=pl.BlockSpec((tm, cw), lambda i, j: (i, j)),
        out_shape=jax.ShapeDtypeStruct((t, ch), F32),
        scratch_shapes=[pltpu.VMEM((tm + 2 * halo, cw), F32),
                        pltpu.VMEM((len(shifts), tm + qmax, cw), F32)],
        compiler_params=_cparams(("parallel", "parallel")),
        name="sconv",
    )(proj, proj, proj, jnp.repeat(w, SUBLANES, axis=0), b.reshape(1, ch))


def _split3(f):
    hi = f.astype(BF16)
    r1 = f - hi.astype(F32)
    mid = r1.astype(BF16)
    lo = (r1 - mid.astype(F32)).astype(BF16)
    return hi, mid, lo


def _expand_heads(f, e):
    hi, mid, lo = _split3(f)
    return (jnp.dot(hi, e, preferred_element_type=F32)
            + jnp.dot(mid, e, preferred_element_type=F32)
            + jnp.dot(lo, e, preferred_element_type=F32))


def _ssd_both_kernel(xf_ref, bf_ref, cf_ref, dtf_ref, xb_ref, bb_ref, cb_ref, dtb_ref, bias_ref,
                     alog_ref, ef_ref, eb_ref, of_ref, ob_ref, *scratch):
    nscr = len(scratch) // 2
    fwd, bwd = scratch[:nscr], scratch[nscr:]
    nheads = xf_ref.shape[1] // SSM_HEAD_DIM

    @pl.when(pl.program_id(1) == 0)
    def _():
        fwd[0][...] = jnp.zeros_like(fwd[0])
        bwd[0][...] = jnp.zeros_like(bwd[0])

    _ssd_direction(xf_ref, bf_ref, cf_ref, dtf_ref, bias_ref, alog_ref, ef_ref, of_ref, *fwd,
                   reverse=False, lane0=0)
    _ssd_direction(xb_ref, bb_ref, cb_ref, dtb_ref, bias_ref, alog_ref, eb_ref, ob_ref, *bwd,
                   reverse=True, lane0=nheads)


def _ssd_direction(x_ref, b_ref, c_ref, dt_ref, dtb_ref, alog_ref, e_ref, o_ref, state_ref,
                   acs_s, acst_s, dtt_s, wstt_s, carry_s, cb_s, bgt_s, *, reverse, lane0):
    q = x_ref.shape[0]
    nheads = x_ref.shape[1] // SSM_HEAD_DIM
    hpg = nheads // SSM_GROUPS

    row = lax.broadcasted_iota(jnp.int32, (q, q), 0)
    col = lax.broadcasted_iota(jnp.int32, (q, q), 1)
    keep = (col >= row) if reverse else (col <= row)

    z = dt_ref[...] + dtb_ref[...]
    dt = jnp.maximum(z, 0.0) + jnp.log(1.0 + jnp.exp(-jnp.abs(z)))
    a = dt * (-jnp.exp(alog_ref[...]))
    acs = jnp.dot(keep.astype(F32), a, precision=HIGHEST, preferred_element_type=F32)
    tot = acs[0:1, :] if reverse else acs[q - 1:q, :]
    acs2 = acs * LOG2E
    acs_s[...] = acs2
    acst_s[...] = acs2.T
    dtt_s[...] = dt.T
    wstt_s[...] = (dt * jnp.exp(tot - acs)).T
    carry_s[...] = _expand_heads(jnp.broadcast_to(jnp.exp(tot), (SUBLANES, LANES)), e_ref[...])

    for g in range(SSM_GROUPS):
        gsl = slice(g * SSM_STATE, (g + 1) * SSM_STATE)
        cb_s[g] = lax.dot_general(c_ref[:, gsl].astype(BF16), b_ref[:, gsl].astype(BF16),
                                  (((1,), (1,)), ((), ())), preferred_element_type=F32)
        bgt_s[g] = b_ref[:, gsl].T
        for pr in range(hpg // 2):
            ps = slice((g * hpg + 2 * pr) * SSM_HEAD_DIM, (g * hpg + 2 * pr + 2) * SSM_HEAD_DIM)
            xp = x_ref[:, ps].astype(BF16)
            rhs = jnp.concatenate([xp, state_ref[:, ps].astype(BF16)], axis=0)
            for sub in range(2):
                hl = lane0 + g * hpg + 2 * pr + sub
                half = slice(sub * SSM_HEAD_DIM, (sub + 1) * SSM_HEAD_DIM)
                hs = slice(ps.start + half.start, ps.start + half.stop)
                colb = jnp.broadcast_to(acs_s[:, hl:hl + 1], (q, q))
                seg = colb - acst_s[hl:hl + 1, :]
                lmat = jnp.exp2(jnp.where(keep, seg, -jnp.inf))
                m = (cb_s[g] * lmat * dtt_s[hl:hl + 1, :]).astype(BF16)
                cexp = (c_ref[:, gsl] * jnp.exp2(colb)).astype(BF16)
                y = jnp.dot(jnp.concatenate([m, cexp], axis=1), rhs,
                            preferred_element_type=F32)
                o_ref[:, hs] = y[:, half].astype(o_ref.dtype)
                bw = (bgt_s[g] * wstt_s[hl:hl + 1, :]).astype(BF16)
                upd = jnp.dot(bw, xp, preferred_element_type=F32)
                state_ref[:, hs] = state_ref[:, hs] * carry_s[0:1, hs] + upd[:, half]


def _ssd(xbc, proj, dt_col_block, dt_bias, a_log, expand_f, expand_b, bsz, seq):
    t = xbc.shape[0]
    q = SSM_CHUNK
    nc = seq // q
    inner = expand_f.shape[1]
    gn = SSM_GROUPS * SSM_STATE
    b_blk = inner // gn
    c_blk = b_blk + 1

    def fwd(b, c):
        return b * nc + c

    def bwd(b, c):
        return b * nc + nc - 1 - c

    def chunk_specs(tok):
        return [pl.BlockSpec((q, inner), lambda b, c: (tok(b, c), 0)),
                pl.BlockSpec((q, gn), lambda b, c: (tok(b, c), b_blk)),
                pl.BlockSpec((q, gn), lambda b, c: (tok(b, c), c_blk)),
                pl.BlockSpec((q, LANES), lambda b, c: (tok(b, c), dt_col_block))]

    def const(shape):
        return pl.BlockSpec(shape, lambda b, c: (0, 0))

    direction_scratch = [pltpu.VMEM((SSM_STATE, inner), F32),
                         pltpu.VMEM((q, LANES), F32), pltpu.VMEM((LANES, q), F32),
                         pltpu.VMEM((LANES, q), F32), pltpu.VMEM((LANES, q), F32),
                         pltpu.VMEM((SUBLANES, inner), F32),
                         pltpu.VMEM((SSM_GROUPS, q, q), F32),
                         pltpu.VMEM((SSM_GROUPS, SSM_STATE, q), F32)]
    return pl.pallas_call(
        _ssd_both_kernel,
        grid=(bsz, nc),
        in_specs=chunk_specs(fwd) + chunk_specs(bwd) + [
            const((1, LANES)), const((1, LANES)), const((LANES, inner)), const((LANES, inner))],
        out_specs=[pl.BlockSpec((q, inner), lambda b, c: (fwd(b, c), 0)),
                   pl.BlockSpec((q, inner), lambda b, c: (bwd(b, c), 0))],
        out_shape=[jax.ShapeDtypeStruct((t, inner), BF16), jax.ShapeDtypeStruct((t, inner), BF16)],
        scratch_shapes=direction_scratch + direction_scratch,
        compiler_params=_cparams(("arbitrary", "arbitrary")),
        name="ssd",
    )(xbc, xbc, xbc, proj, xbc, xbc, xbc, proj, dt_bias, a_log, expand_f, expand_b)


def _merge_kernel(yf_ref, yb_ref, xc_ref, z_ref, hc_ref, gc_ref, gs_ref, x_ref,
                  dexp_ref, ng_ref, wso_ref, wco_ref, bco_ref, wo_ref, g1_ref,
                  scale_ref, shift_ref, wr_ref, br_ref,
                  x1_ref, v_ref, lg_ref, *, ngroups, halves):
    tm, inner = yf_ref.shape
    gw = inner // ngroups
    hm = tm // halves
    for h in range(halves):
        r = slice(h * hm, (h + 1) * hm)
        y = ((yf_ref[r, :].astype(F32) + yb_ref[r, :].astype(F32) + xc_ref[r, :] * dexp_ref[...])
             * _silu(z_ref[r, :]))
        parts = []
        for g in range(ngroups):
            yg = y[:, g * gw:(g + 1) * gw]
            ms = jnp.mean(yg * yg, axis=-1, keepdims=True)
            parts.append((yg * lax.rsqrt(ms + EPS)) * ng_ref[:, g * gw:(g + 1) * gw])
        ysn = jnp.concatenate(parts, axis=-1).astype(BF16)
        y_ssm = jnp.dot(ysn, wso_ref[...], preferred_element_type=F32)
        y_conf = jnp.dot(hc_ref[r, :], wco_ref[...], preferred_element_type=F32) + bco_ref[...]
        merged = _sigmoid(gc_ref[r, :]) * y_conf + _sigmoid(gs_ref[r, :]) * y_ssm
        o = jnp.dot(merged.astype(BF16), wo_ref[...], preferred_element_type=F32)
        x1 = x_ref[r, :] + g1_ref[0] * o
        x1_ref[r, :] = x1
        v = _modulated_norm(x1, scale_ref[0], shift_ref[0])
        v_ref[r, :] = v
        v_hi = v.astype(BF16)
        v_lo = (v - v_hi.astype(F32)).astype(BF16)
        p = (jnp.dot(v_hi, wr_ref[...], preferred_element_type=F32)
             + jnp.dot(v_lo, wr_ref[...], preferred_element_type=F32))
        lg_ref[r, :] = p[:, :LANES] + p[:, LANES:] + br_ref[...]


def _merge(yf, yb, xbc, proj, hc, x, cols, dexp, norm_g, wso, wco, bco, wo, g1, scale, shift,
           wr, br, seq, tm):
    t, d = x.shape
    inner = yf.shape[1]
    tiles_per_seq = seq // tm
    z_blk = cols["z"] // inner
    gc_blk = cols["gate_conf"] // d
    gs_blk = cols["gate_ssm"] // d

    def const(shape):
        return pl.BlockSpec(shape, lambda i: tuple(0 for _ in shape),
                            pipeline_mode=pl.Buffered(1))

    def per_seq():
        return pl.BlockSpec((1, 1, d), lambda i: (i // tiles_per_seq, 0, 0))

    kern = functools.partial(_merge_kernel, ngroups=SSM_GROUPS, halves=2)
    return pl.pallas_call(
        kern,
        grid=(t // tm,),
        in_specs=[pl.BlockSpec((tm, inner), lambda i: (i, 0)),
                  pl.BlockSpec((tm, inner), lambda i: (i, 0)),
                  pl.BlockSpec((tm, inner), lambda i: (i, 0)),
                  pl.BlockSpec((tm, inner), lambda i: (i, z_blk)),
                  pl.BlockSpec((tm, d), lambda i: (i, 0)),
                  pl.BlockSpec((tm, d), lambda i: (i, gc_blk)),
                  pl.BlockSpec((tm, d), lambda i: (i, gs_blk)),
                  pl.BlockSpec((tm, d), lambda i: (i, 0)),
                  const((1, inner)), const((1, inner)),
                  const((inner, d)), const((d, d)), const((1, d)), const((d, d)),
                  per_seq(), per_seq(), per_seq(),
                  const((d, 2 * LANES)), const((1, LANES))],
        out_specs=[pl.BlockSpec((tm, d), lambda i: (i, 0)),
                   pl.BlockSpec((tm, d), lambda i: (i, 0)),
                   pl.BlockSpec((tm, LANES), lambda i: (i, 0))],
        out_shape=[jax.ShapeDtypeStruct((t, d), F32),
                   jax.ShapeDtypeStruct((t, d), F32),
                   jax.ShapeDtypeStruct((t, LANES), F32)],
        compiler_params=_cparams(("parallel",)),
        name="merge",
    )(yf, yb, xbc, proj, hc, proj, proj, x, dexp, norm_g, wso, wco, bco, wo, g1, scale, shift,
      wr, br)


ROUTE_TILE = 256


def _route_kernel(lg_ref, w_ref, lslot_ref, lslot_t_ref, cnt_ref, *, n_experts):
    tm = lg_ref.shape[0]
    lg = lg_ref[...].T[0:n_experts, :]
    erow = lax.broadcasted_iota(jnp.int32, (n_experts, tm), 0)
    sel = jnp.zeros((n_experts, tm), F32)
    vals, idxs = [], []
    for _ in range(TOP_K):
        m = jnp.max(lg, axis=0, keepdims=True)
        ix = jnp.min(jnp.where(lg == m, erow, n_experts), axis=0, keepdims=True)
        hit = erow == ix
        sel = jnp.where(hit, 1.0, sel)
        lg = jnp.where(hit, -jnp.inf, lg)
        vals.append(m)
        idxs.append(ix)
    ex = [jnp.exp(v - vals[0]) for v in vals]
    den = ex[0] + ex[1] + ex[2] + ex[3]
    r = lax.broadcasted_iota(jnp.int32, (tm, tm), 0)
    c = lax.broadcasted_iota(jnp.int32, (tm, tm), 1)
    rank = jnp.dot(sel.astype(BF16), (r < c).astype(BF16), preferred_element_type=F32)
    cnt = jnp.broadcast_to(jnp.sum(sel, axis=1, keepdims=True), (n_experts, LANES))
    er = lax.broadcasted_iota(jnp.int32, (n_experts, n_experts), 0)
    ec = lax.broadcasted_iota(jnp.int32, (n_experts, n_experts), 1)
    off = jnp.dot((ec < er).astype(BF16), cnt.astype(BF16), preferred_element_type=F32)
    slot_of = rank + off[:, 0:1]
    row8 = lax.broadcasted_iota(jnp.int32, (LANES, tm), 0)
    slot_t = jnp.zeros((LANES, tm), F32)
    w_t = jnp.zeros((LANES, tm), F32)
    for k in range(TOP_K):
        sk = jnp.sum(jnp.where(erow == idxs[k], slot_of, 0.0), axis=0, keepdims=True)
        slot_t = jnp.where(row8 == k, sk, slot_t)
        w_t = jnp.where(row8 == k, ex[k] / den, w_t)
    w_ref[...] = w_t.T
    lslot_ref[...] = slot_t.T.astype(jnp.int32)
    lslot_t_ref[...] = slot_t[0:SUBLANES, :].astype(jnp.int32)
    cnt_pad = jnp.concatenate([cnt, jnp.zeros((LANES - n_experts, LANES), F32)], axis=0)
    cnt_ref[0] = cnt_pad.T[0:SUBLANES, :].astype(jnp.int32)


def _route(logits, n_experts):
    t = logits.shape[0]
    tm = ROUTE_TILE
    nt = t // tm
    kern = functools.partial(_route_kernel, n_experts=n_experts)
    return pl.pallas_call(
        kern,
        grid=(nt,),
        in_specs=[pl.BlockSpec((tm, LANES), lambda i: (i, 0))],
        out_specs=[pl.BlockSpec((tm, LANES), lambda i: (i, 0)),
                   pl.BlockSpec((tm, LANES), lambda i: (i, 0)),
                   pl.BlockSpec((SUBLANES, tm), lambda i: (i, 0)),
                   pl.BlockSpec((1, SUBLANES, LANES), lambda i: (i, 0, 0))],
        out_shape=[jax.ShapeDtypeStruct((t, LANES), F32),
                   jax.ShapeDtypeStruct((t, LANES), jnp.int32),
                   jax.ShapeDtypeStruct((nt * SUBLANES, tm), jnp.int32),
                   jax.ShapeDtypeStruct((nt, SUBLANES, LANES), jnp.int32)],
        compiler_params=_cparams(("parallel",)),
        name="route",
    )(logits)


def _row_copy(src, src_row, dst, dst_row, sem, nch):
    return pltpu.make_async_copy(_row(src, src_row, nch), _row(dst, dst_row, nch), sem)


RUN_BITS = ROUTE_TILE.bit_length()


def _excl_cumsum(x, axis):
    n = x.shape[axis]
    earlier = jnp.arange(n)[:, None] > jnp.arange(n)[None, :]
    xm = jnp.moveaxis(x, axis, -1)
    out = jnp.sum(jnp.where(earlier, xm[..., None, :], 0), axis=-1)
    return jnp.moveaxis(out, -1, axis)


def _run_piece_tables(cnt2, loc2, glob2):
    nt, ne = cnt2.shape
    bits = jnp.arange(RUN_BITS, dtype=jnp.int32)[:, None, None]
    n = cnt2.T[None, :, :]
    valid = ((n >> bits) & 1) == 1
    done = (n >> (bits + 1)) << (bits + 1)
    lower = jnp.arange(ne)[:, None] > jnp.arange(ne)[None, :]
    pos = jnp.sum(jnp.where(lower[None, :, :, None], valid[:, None, :, :], False), axis=2,
                  dtype=jnp.int32)
    j = jnp.tile(jnp.arange(ne, dtype=jnp.int32), nt)
    hit = jnp.repeat(valid, ne, axis=-1) & (jnp.repeat(pos, ne, axis=-1) == j)

    def compact(start2):
        val = jnp.repeat(start2.T[None, :, :] + done, ne, axis=-1)
        out = jnp.sum(jnp.where(hit, val, 0), axis=1)
        return out.reshape(RUN_BITS, nt, ne).transpose(1, 0, 2).reshape(-1).astype(jnp.int32)

    npieces = jnp.sum(valid.astype(jnp.int32), axis=1).T.reshape(-1).astype(jnp.int32)
    return npieces, compact(loc2), compact(glob2)


def _segment_copies(np_ref, loc_ref, glob_ref, tile, n_experts, local_buf, hbm, sem, nch, to_hbm):
    for bit in range(RUN_BITS):
        rows = (1 << bit) * nch
        base = tile * RUN_BITS + bit

        def body(j, c, rows=rows, base=base):
            lo = loc_ref[base * n_experts + j]
            go = glob_ref[base * n_experts + j]
            loc = local_buf.at[pl.ds(pl.multiple_of(lo * nch, nch), rows), :]
            glb = hbm.at[pl.ds(pl.multiple_of(go * nch, nch), rows), :]
            if to_hbm:
                pltpu.make_async_copy(loc, glb, sem).start()
            else:
                pltpu.make_async_copy(glb, loc, sem).start()
            return c

        lax.fori_loop(0, np_ref[base], body, 0)


def _wait_segments(local_buf, hbm, sem, to_hbm):
    whole = hbm.at[pl.ds(0, local_buf.shape[0]), :]
    if to_hbm:
        pltpu.make_async_copy(local_buf, whole, sem).wait()
    else:
        pltpu.make_async_copy(whole, local_buf, sem).wait()


def _dispatch_kernel(cnt_ref, loc_ref, glob_ref, fill_ref, end_ref, lslot_t_ref, v_ref, xs_hbm,
                     buf, sems, fsem, *, n_experts, nch):
    i = pl.program_id(0)
    n = pl.num_programs(0)
    tm = v_ref.shape[0]
    nslots = tm * TOP_K
    slot = i % 2
    srow = lax.broadcasted_iota(jnp.int32, (nslots, tm), 0)
    hit = srow == lslot_t_ref[0:1, :]
    for k in range(1, TOP_K):
        hit = jnp.logical_or(hit, srow == lslot_t_ref[k:k + 1, :])
    p = jnp.where(hit, 1.0, 0.0).astype(BF16)
    v = v_ref[...]
    v_hi = v.astype(BF16)
    v_lo = (v - v_hi.astype(F32)).astype(BF16)
    xl = (jnp.dot(p, v_hi, preferred_element_type=F32)
          + jnp.dot(p, v_lo, preferred_element_type=F32))
    for s_ in range(nch):
        buf[slot, pl.ds(s_, nslots, stride=nch), :] = xl[:, s_ * LANES:(s_ + 1) * LANES]

    @pl.when(i == 0)
    def _():
        def fill(s, c):
            _row_copy(buf.at[0], 0, xs_hbm, s, fsem, nch).start()
            return c

        def drain(s, c):
            _row_copy(buf.at[0], 0, xs_hbm, s, fsem, nch).wait()
            return c

        group = 8
        for e0 in range(0, n_experts, group):
            for e in range(e0, min(e0 + group, n_experts)):
                lax.fori_loop(fill_ref[e], end_ref[e], fill, 0)
            for e in range(e0, min(e0 + group, n_experts)):
                lax.fori_loop(fill_ref[e], end_ref[e], drain, 0)

    args = (cnt_ref, loc_ref, glob_ref)
    _segment_copies(*args, i, n_experts, buf.at[slot], xs_hbm, sems.at[slot], nch, True)

    @pl.when(i > 0)
    def _():
        _wait_segments(buf.at[1 - slot], xs_hbm, sems.at[1 - slot], True)

    @pl.when(i == n - 1)
    def _():
        _wait_segments(buf.at[slot], xs_hbm, sems.at[slot], True)


def _dispatch(cnt_flat, loc_flat, glob_flat, fill_start, pad_end, lslot_t, v, n_slots):
    t, d = v.shape
    nch = d // LANES
    tm = ROUTE_TILE
    n_experts = fill_start.shape[0]
    kern = functools.partial(_dispatch_kernel, n_experts=n_experts, nch=nch)
    grid_spec = pltpu.PrefetchScalarGridSpec(
        num_scalar_prefetch=5,
        grid=(t // tm,),
        in_specs=[pl.BlockSpec((SUBLANES, tm), lambda i, *_: (i, 0)),
                  pl.BlockSpec((tm, d), lambda i, *_: (i, 0))],
        out_specs=pl.BlockSpec(memory_space=pl.ANY),
        scratch_shapes=[pltpu.VMEM((2, tm * TOP_K * nch, LANES), F32),
                        pltpu.SemaphoreType.DMA((2,)), pltpu.SemaphoreType.DMA],
    )
    return pl.pallas_call(
        kern,
        grid_spec=grid_spec,
        out_shape=jax.ShapeDtypeStruct((n_slots * nch, LANES), v.dtype),
        compiler_params=_cparams(("arbitrary",)),
        name="dispatch",
    )(cnt_flat, loc_flat, glob_flat, fill_start, pad_end, lslot_t, v)


def _moe_kernel(be_ref, nused_ref, x_ref, wgu_ref, bgu_ref, wd_ref, bd_ref, o_ref,
                wgu_bf, wd_bf):
    b = pl.program_id(0)
    changed = jnp.logical_or(b == 0, be_ref[b] != be_ref[jnp.maximum(b - 1, 0)])
    active = b < nused_ref[0]

    @pl.when(jnp.logical_and(active, changed))
    def _():
        wgu_bf[...] = wgu_ref[0].astype(BF16)
        wd_bf[...] = wd_ref[0].astype(BF16)

    @pl.when(active)
    def _():
        ff = wd_bf.shape[0]
        d = wgu_bf.shape[0]
        nch = d // LANES
        hm = x_ref.shape[0] // nch // MOE_HALVES
        for h in range(MOE_HALVES):
            x = _load_rows(x_ref, hm, nch, h * hm).astype(BF16)
            gu = jnp.dot(x, wgu_bf[...], preferred_element_type=F32) + bgu_ref[0]
            gate = jnp.minimum(gu[:, :ff], SWIGLU_LIMIT)
            up = jnp.clip(gu[:, ff:], -SWIGLU_LIMIT, SWIGLU_LIMIT)
            glu = gate * _sigmoid(SWIGLU_ALPHA * gate)
            hid = ((up + 1.0) * glu).astype(BF16)
            y = jnp.dot(hid, wd_bf[...], preferred_element_type=F32) + bd_ref[0]
            _store_rows(o_ref, y, h * hm)


def _moe(block_expert, nused, xs, w_gu, b_gu, w_down, b_down):
    ne, d, ff2 = w_gu.shape
    nch = d // LANES
    n_slots = xs.shape[0] // nch
    ff = w_down.shape[1]
    bm = MOE_BLOCK

    def blk(b, nu):
        return jnp.minimum(b, nu[0] - 1)

    def rows(b, be, nu):
        return (blk(b, nu), 0)

    def expert(b, be, nu):
        return (be[blk(b, nu)], 0, 0)

    grid_spec = pltpu.PrefetchScalarGridSpec(
        num_scalar_prefetch=2,
        grid=(n_slots // bm,),
        in_specs=[pl.BlockSpec((bm * nch, LANES), rows),
                  pl.BlockSpec((1, d, ff2), expert),
                  pl.BlockSpec((1, 1, ff2), expert),
                  pl.BlockSpec((1, ff, d), expert),
                  pl.BlockSpec((1, 1, d), expert)],
        out_specs=pl.BlockSpec((bm * nch, LANES), rows),
        scratch_shapes=[pltpu.VMEM((d, ff2), BF16), pltpu.VMEM((ff, d), BF16)],
    )
    return pl.pallas_call(
        _moe_kernel,
        grid_spec=grid_spec,
        out_shape=jax.ShapeDtypeStruct(xs.shape, F32),
        compiler_params=_cparams(("arbitrary",)),
        name="moe",
    )(block_expert, nused, xs, w_gu, b_gu.reshape(ne, 1, ff2), w_down, b_down.reshape(ne, 1, d))


def _final_kernel(cnt_ref, loc_ref, glob_ref, x1_ref, lslot_ref, w_ref, g2_ref, scale_ref,
                  shift_ref, ys_hbm, o_ref, buf, sems, *, n_experts, nch):
    i = pl.program_id(0)
    n = pl.num_programs(0)
    tm = x1_ref.shape[0]
    nslots = tm * TOP_K
    slot = i % 2
    args = (cnt_ref, loc_ref, glob_ref)

    @pl.when(i == 0)
    def _():
        _segment_copies(*args, 0, n_experts, buf.at[0], ys_hbm, sems.at[0], nch, False)

    @pl.when(i + 1 < n)
    def _():
        _segment_copies(*args, i + 1, n_experts, buf.at[1 - slot], ys_hbm, sems.at[1 - slot], nch,
                        False)

    _wait_segments(buf.at[slot], ys_hbm, sems.at[slot], False)

    lane_slot = lax.broadcasted_iota(jnp.int32, (tm, nslots), 1)
    pw = jnp.zeros((tm, nslots), F32)
    for k in range(TOP_K):
        pw = jnp.where(lane_slot == lslot_ref[:, k:k + 1], w_ref[:, k:k + 1], pw)
    pw_hi = pw.astype(BF16)
    pw_lo = (pw - pw_hi.astype(F32)).astype(BF16)
    y = _load_rows(buf, nslots, nch, 0, (slot,))
    y_hi = y.astype(BF16)
    y_lo = (y - y_hi.astype(F32)).astype(BF16)
    f = (jnp.dot(pw_hi, y_hi, preferred_element_type=F32)
         + jnp.dot(pw_hi, y_lo, preferred_element_type=F32)
         + jnp.dot(pw_lo, y_hi, preferred_element_type=F32))
    x2 = x1_ref[...] + g2_ref[0] * f
    o_ref[...] = _modulated_norm(x2, scale_ref[0], shift_ref[0])


def _final(cnt_flat, loc_flat, glob_flat, n_experts, x1, ys, lslot, top_w, g2, scale, shift, seq):
    t, d = x1.shape
    nch = d // LANES
    tm = ROUTE_TILE
    tiles_per_seq = seq // tm

    def per_seq():
        return pl.BlockSpec((1, 1, d), lambda i, *_: (i // tiles_per_seq, 0, 0))

    grid_spec = pltpu.PrefetchScalarGridSpec(
        num_scalar_prefetch=3,
        grid=(t // tm,),
        in_specs=[pl.BlockSpec((tm, d), lambda i, *_: (i, 0)),
                  pl.BlockSpec((tm, LANES), lambda i, *_: (i, 0)),
                  pl.BlockSpec((tm, LANES), lambda i, *_: (i, 0)),
                  per_seq(), per_seq(), per_seq(),
                  pl.BlockSpec(memory_space=pl.ANY)],
        out_specs=pl.BlockSpec((tm, d), lambda i, *_: (i, 0)),
        scratch_shapes=[pltpu.VMEM((2, TOP_K * tm * nch, LANES), F32),
                        pltpu.SemaphoreType.DMA((2,))],
    )
    return pl.pallas_call(
        functools.partial(_final_kernel, n_experts=n_experts, nch=nch),
        grid_spec=grid_spec,
        out_shape=jax.ShapeDtypeStruct((t, d), F32),
        compiler_params=_cparams(("arbitrary",)),
        name="final",
    )(cnt_flat, loc_flat, glob_flat, x1, lslot, top_w, g2, scale, shift, ys)


def _pad_cols(w, n):
    return jnp.pad(w, ((0, 0), (0, n - w.shape[1])))


def kernel(x, c, ada_w, ada_b, norm_mix_g, w_in, conf_dw_w, conf_dw_b, conf_ln_g, conf_ln_b,
           conf_out_w, conf_out_b, ssm_conv_w, ssm_conv_b, dt_bias_f, dt_bias_b, a_log_f,
           a_log_b, ssm_d, ssm_norm_g, ssm_out_w, w_o, norm_ffn_g, router_w, router_b, w_gu,
           b_gu, w_down, b_down, final_ada_w, final_ada_b, final_norm_g):
    bsz, seq, d = x.shape
    depth = ada_w.shape[0]
    t = bsz * seq
    nheads = a_log_f.shape[1]
    inner = nheads * SSM_HEAD_DIM
    gn = SSM_GROUPS * SSM_STATE
    conf = conf_dw_w.shape[2]
    n_experts = router_w.shape[2]
    assert 2 * nheads <= LANES and n_experts <= LANES

    c_pad = jnp.zeros((SUBLANES, d), F32).at[:bsz].set(c)
    fin = _ada(c_pad, final_ada_w, final_ada_b)[:bsz]
    xf = x.reshape(t, d)

    sizes = [("conf_a", conf), ("conf_g", conf), ("z", inner), ("xs", inner), ("bm", gn),
             ("cm", gn), ("gate_conf", d), ("gate_ssm", d), ("dt", LANES)]
    cols, off = {}, 0
    for name, n in sizes:
        cols[name] = off
        off += n
    src = {}
    o = 0
    for name, n in [("conf_a", conf), ("conf_g", conf), ("z", inner), ("xs", inner), ("bm", gn),
                    ("cm", gn), ("dtf", nheads), ("dtb", nheads), ("gate_conf", d),
                    ("gate_ssm", d)]:
        src[name] = (o, o + n)
        o += n

    head_of_col = jnp.arange(inner, dtype=jnp.int32) // SSM_HEAD_DIM
    lanes = jnp.arange(LANES, dtype=jnp.int32)[:, None]
    expand_f = (lanes == head_of_col[None, :]).astype(BF16)
    expand_b = (lanes == head_of_col[None, :] + nheads).astype(BF16)

    for l in range(depth):
        ada = _ada(c_pad, ada_w[l], ada_b[l])[:bsz]
        sh1, sc1, g1, sh2, sc2, g2 = [a.reshape(bsz, 1, d) for a in jnp.split(ada, 6, axis=-1)]
        scale1 = norm_mix_g[l][None, None, :] * (1.0 + sc1)
        scale2 = norm_ffn_g[l][None, None, :] * (1.0 + sc2)

        wl = w_in[l]
        w_dt = _pad_cols(jnp.concatenate([wl[:, slice(*src["dtf"])], wl[:, slice(*src["dtb"])]],
                                         axis=1), LANES)
        w_perm = jnp.concatenate(
            [wl[:, slice(*src[n])] for n in ("conf_a", "conf_g", "z", "xs", "bm", "cm",
                                             "gate_conf", "gate_ssm")] + [w_dt],
            axis=1).astype(BF16)
        proj = _inproj(xf, scale1, sh1, w_perm, seq, tm=INPROJ_TILE)

        hc = _cconv(proj, conf_dw_w[l], conf_dw_b[l], conf_ln_g[l], conf_ln_b[l], seq,
                    tm=CCONV_TILE)
        xbc = _sconv(proj, cols["xs"], ssm_conv_w[l], ssm_conv_b[l], seq, tm=SCONV_TILE)

        dt_bias = _pad_cols(jnp.concatenate([dt_bias_f[l], dt_bias_b[l]])[None, :], LANES)
        a_log = _pad_cols(jnp.concatenate([a_log_f[l], a_log_b[l]])[None, :], LANES)
        dt_blk = cols["dt"] // LANES
        y_f, y_b = _ssd(xbc, proj, dt_blk, dt_bias, a_log, expand_f, expand_b, bsz, seq)

        dexp = jnp.repeat(ssm_d[l], SSM_HEAD_DIM)[None, :]
        wr = _pad_cols(router_w[l], LANES)
        wr_hi = wr.astype(BF16)
        wr = jnp.concatenate([wr_hi, (wr - wr_hi.astype(F32)).astype(BF16)], axis=1)
        br = _pad_cols(router_b[l][None, :], LANES)
        x1, v, logits = _merge(
            y_f, y_b, xbc, proj, hc, xf, cols, dexp, ssm_norm_g[l][None, :],
            ssm_out_w[l].astype(BF16), conf_out_w[l].astype(BF16), conf_out_b[l][None, :],
            w_o[l].astype(BF16), g1, scale2, sh2, wr, br, seq, tm=MERGE_TILE)

        top_w, lslot, lslot_t, tile_cnt = _route(logits, n_experts)
        cnt2 = tile_cnt[:, 0, :n_experts]
        counts = jnp.sum(cnt2, axis=0)
        bm = MOE_BLOCK
        n_blocks = (t * TOP_K) // bm + n_experts
        padded = ((counts + bm - 1) // bm) * bm
        pad_start = _excl_cumsum(padded, axis=0)
        pad_end = pad_start + padded
        loc2 = _excl_cumsum(cnt2, axis=1)
        glob2 = pad_start[None, :] + _excl_cumsum(cnt2, axis=0)
        tables = _run_piece_tables(cnt2, loc2, glob2)
        block_first = jnp.arange(n_blocks, dtype=jnp.int32) * bm
        block_expert = jnp.minimum(
            jnp.sum((pad_end[None, :] <= block_first[:, None]).astype(jnp.int32), axis=1),
            n_experts - 1).astype(jnp.int32)
        nused = (pad_end[-1] // bm).astype(jnp.int32).reshape(1)
        xs = _dispatch(*tables, (pad_start + counts).astype(jnp.int32),
                       pad_end.astype(jnp.int32), lslot_t, v, n_blocks * bm)
        ys = _moe(block_expert, nused, xs, w_gu[l], b_gu[l], w_down[l], b_down[l])

        if l == depth - 1:
            sh_f, sc_f = [a.reshape(bsz, 1, d) for a in jnp.split(fin, 2, axis=-1)]
            scale_f = final_norm_g[None, None, :] * (1.0 + sc_f)
            xf = _final(*tables, n_experts, x1, ys, lslot, top_w, g2, scale_f, sh_f, seq)
        else:
            raise NotImplementedError("depth > 1 is not wired")
    return xf.reshape(bsz, seq, d)
```
